```python
import math
import jax, jax.numpy as jnp
from jax import lax
import numpy as np

D_MODEL = 1024
BATCH = 2
SEQ = 8192
DEPTH = 4
DEC_BATCH = 32
DEC_SEQ = 8
PAST_LEN = 8192
PAGE_SIZE = 128

N_MIXERS = 2
N_NSA = (DEPTH + 1) // 2
N_POOLMIX = DEPTH // 2
HEAD_DIM = 64
N_HEADS = D_MODEL // HEAD_DIM
N_KV_HEADS = 4
GQA_R = N_HEADS // N_KV_HEADS
Q_DIM = N_HEADS * HEAD_DIM
KV_DIM = 6 * N_KV_HEADS * HEAD_DIM
GATE_DIM = 3 * N_HEADS
CMP_LEN = 32
CMP_STRIDE = 16
CMP_HID = 256
SEL_LEN = 64
CMP_PER_SEL = SEL_LEN // CMP_STRIDE
TOP_N = 16
WINDOW = 512
Q_BLOCK = 128
FORCE_SCORE = 1.0e4
REL_BUCKETS = 32
REL_MAX_DIST = 128
POOL_WINDOWS = (2, 4, 8, 16)
POOL_GW = D_MODEL // len(POOL_WINDOWS)
POOL_MAX = max(POOL_WINDOWS)
POOL_HIST = POOL_MAX - 1
D_FF = 2816
CONV_W = 3
PLE_DIM = 256
DEEPNORM_ALPHA = (2.0 * DEPTH) ** 0.25
DEEPNORM_BETA = (8.0 * DEPTH) ** -0.25
LN_EPS = 1e-5

kernel_name = "nsa_pool_hybrid_decoder_step"


def layer_norm(x, g, b):
    xf = x.astype(jnp.float32)
    mu = jnp.mean(xf, axis=-1, keepdims=True)
    var = jnp.mean(jnp.square(xf - mu), axis=-1, keepdims=True)
    return ((xf - mu) * lax.rsqrt(var + LN_EPS) * g.astype(jnp.float32) + b.astype(jnp.float32)).astype(x.dtype)


def masked_softmax(s, mask):
    s = jnp.where(mask, s, -jnp.inf)
    m = jnp.max(s, axis=-1, keepdims=True)
    m = jnp.where(jnp.isfinite(m), m, 0.0)
    e = jnp.where(mask, jnp.exp(s - m), 0.0)
    return e / jnp.maximum(jnp.sum(e, axis=-1, keepdims=True), 1e-30)


def rel_bucket(dist):
    n = jnp.maximum(dist, 0)
    exact = REL_BUCKETS // 2
    nf = jnp.maximum(n, 1).astype(jnp.float32)
    large = exact + (jnp.log(nf / exact) / math.log(REL_MAX_DIST / exact) * (REL_BUCKETS - exact)).astype(jnp.int32)
    return jnp.where(n < exact, n, jnp.minimum(large, REL_BUCKETS - 1))


def nsa_mixer(x, pos0, past_kv, past_win, w_in, cmp_pe, cmp_w1, cmp_b1, cmp_w2, w_out, rel_bias):
    B, T, _ = x.shape
    G, R, hd = N_KV_HEADS, GQA_R, HEAD_DIM
    proj = x @ w_in
    q = (proj[..., :Q_DIM] * (HEAD_DIM ** -0.5)).reshape(B, T, G, R, hd)
    kv = proj[..., Q_DIM:Q_DIM + KV_DIM].reshape(B, T, 6, G, hd)
    gates = jax.nn.sigmoid(proj[..., Q_DIM + KV_DIM:].astype(jnp.float32)).reshape(B, T, G, R, 3)
    new_rows = kv[:, :, :4]
    new_win = kv[:, :, 4:]

    n_past = 0 if past_kv is None else past_kv.shape[1]
    lk0 = n_past + T
    lk = -(-lk0 // SEL_LEN) * SEL_LEN
    parts = ([] if past_kv is None else [past_kv]) + [new_rows]
    if lk > lk0:
        parts.append(jnp.zeros((B, lk - lk0, 4, G, hd), new_rows.dtype))
    full = parts[0] if len(parts) == 1 else jnp.concatenate(parts, axis=1)

    nc = lk // CMP_STRIDE
    chunks = full[:, :, 0:2].reshape(B, nc, CMP_STRIDE, 2, G, hd)
    w1r = cmp_w1.reshape(2, CMP_LEN, hd, CMP_HID)
    h0 = jnp.einsum('bnscgd,csdh->bncgh', chunks, w1r[:, :CMP_STRIDE])
    h1 = jnp.einsum('bnscgd,csdh->bncgh', chunks, w1r[:, CMP_STRIDE:])
    h1 = jnp.concatenate([h1[:, 1:], jnp.zeros_like(h1[:, :1])], axis=1)
    pe_term = jnp.einsum('csd,csdh->ch', cmp_pe, w1r) + cmp_b1
    hid = jax.nn.gelu(h0 + h1 + pe_term[:, None, :])
    cmp = jnp.einsum('bncgh,chd->bncgd', hid, cmp_w2)
    k_cmp, v_cmp = cmp[:, :, 0], cmp[:, :, 1]
    cmp_end = jnp.arange(nc) * CMP_STRIDE + CMP_LEN - 1

    nb = lk // SEL_LEN
    sel_kv = full[:, :, 2:4].reshape(B, nb, SEL_LEN, 2, G, hd).transpose(0, 1, 4, 2, 3, 5)
    n_sel = min(TOP_N, nb)

    win_src = new_win if past_win is None else jnp.concatenate([past_win, new_win], axis=1)
    w0 = pos0 - (win_src.shape[1] - T)
    winp = jnp.pad(win_src, ((0, 0), (WINDOW, 0), (0, 0), (0, 0), (0, 0)))

    rel_f = rel_bias.astype(jnp.float32)
    rel_g = rel_f.reshape(REL_BUCKETS, G, R).transpose(1, 0, 2)
    bi = jnp.arange(B)[:, None, None, None]
    gi = jnp.arange(G)[None, None, :, None]

    def attend(qb, gb, t0):
        Q = qb.shape[1]
        tq = t0 + jnp.arange(Q)
        d_c = tq[:, None] - cmp_end[None, :]
        s_c = jnp.einsum('bqgrd,bngd->bqgrn', qb, k_cmp).astype(jnp.float32)
        b_c = rel_f[rel_bucket(d_c)].reshape(Q, nc, G, R).transpose(0, 2, 3, 1)
        p_c = masked_softmax(s_c + b_c, (d_c >= 0)[:, None, None, :])
        o_c = jnp.einsum('bqgrn,bngd->bqgrd', p_c.astype(v_cmp.dtype), v_cmp)
        grp = p_c.sum(axis=3).reshape(B, Q, G, nb, CMP_PER_SEL)
        imp = grp.sum(-1) + jnp.concatenate([jnp.zeros_like(grp[..., :1, -1]), grp[..., :-1, -1]], axis=-1)
        blk = jnp.arange(nb)[None, :]
        cur = (tq // SEL_LEN)[:, None]
        forced = (blk == 0) | (blk == cur) | (blk == cur - 1)
        future = blk * SEL_LEN > tq[:, None]
        score = jnp.where(future[None, :, None], -1.0, jnp.where(forced[None, :, None], FORCE_SCORE, imp))
        _, idx = lax.top_k(score, n_sel)
        sel = sel_kv[bi, idx, gi].reshape(B, Q, G, n_sel * SEL_LEN, 2, hd)
        kpos = (idx[..., None] * SEL_LEN + jnp.arange(SEL_LEN)).reshape(B, Q, G, n_sel * SEL_LEN)
        d_s = tq[None, :, None, None] - kpos
        s_s = jnp.einsum('bqgrd,bqgkd->bqgrk', qb, sel[..., 0, :]).astype(jnp.float32)
        b_s = jnp.swapaxes(rel_g[gi, rel_bucket(d_s)], -1, -2)
        p_s = masked_softmax(s_s + b_s, (d_s >= 0)[:, :, :, None, :])
        o_s = jnp.einsum('bqgrk,bqgkd->bqgrd', p_s.astype(sel.dtype), sel[..., 1, :])
        wkv = lax.dynamic_slice_in_dim(winp, t0 - w0, WINDOW + Q, axis=1)
        kp = t0 - WINDOW + jnp.arange(WINDOW + Q)
        d_w = tq[:, None] - kp[None, :]
        m_w = (d_w >= 0) & (d_w <= WINDOW) & (kp[None, :] >= 0)
        s_w = jnp.einsum('bqgrd,bkgd->bqgrk', qb, wkv[:, :, 0]).astype(jnp.float32)
        b_w = rel_f[rel_bucket(d_w)].reshape(Q, WINDOW + Q, G, R).transpose(0, 2, 3, 1)
        p_w = masked_softmax(s_w + b_w, m_w[:, None, None, :])
        o_w = jnp.einsum('bqgrk,bkgd->bqgrd', p_w.astype(wkv.dtype), wkv[:, :, 1])
        o = gb[..., 0:1] * o_c + gb[..., 1:2] * o_s + gb[..., 2:3] * o_w
        return o.astype(x.dtype)

    if T % Q_BLOCK == 0:
        nblk = T // Q_BLOCK
        qs = q.reshape(B, nblk, Q_BLOCK, G, R, hd).transpose(1, 0, 2, 3, 4, 5)
        gs = gates.reshape(B, nblk, Q_BLOCK, G, R, 3).transpose(1, 0, 2, 3, 4, 5)
        t0s = pos0 + jnp.arange(nblk, dtype=jnp.int32) * Q_BLOCK
        o = lax.map(lambda a: attend(a[0], a[1], a[2]), (qs, gs, t0s))
        o = o.transpose(1, 0, 2, 3, 4, 5)
    else:
        o = attend(q, gates, pos0)
    out = o.reshape(B, T, Q_DIM) @ w_out
    keep = min(WINDOW, win_src.shape[1])
    return out, new_rows, win_src[:, win_src.shape[1] - keep:]


def pool_mixer(x, hist, pos0, w, b, scale):
    B, T, D = x.shape
    xe = x if hist is None else jnp.concatenate([hist, x], axis=1)
    nh = xe.shape[1] - T
    cs = jnp.pad(jnp.cumsum(xe.astype(jnp.float32), axis=1), ((0, 0), (POOL_MAX, 0), (0, 0)))
    pos = (pos0 + jnp.arange(T))[None, :, None]
    xf = x.astype(jnp.float32)
    diffs = []
    for g_i, win in enumerate(POOL_WINDOWS):
        lo, hi = g_i * POOL_GW, (g_i + 1) * POOL_GW
        a = nh + POOL_MAX
        tot = cs[:, a:a + T, lo:hi] - cs[:, a - win:a - win + T, lo:hi]
        cnt = jnp.minimum(win, pos + 1).astype(jnp.float32)
        diffs.append(tot / cnt - xf[:, :, lo:hi])
    d = jnp.stack(diffs, axis=2).astype(x.dtype)
    y = jnp.einsum('btgc,gcd->btgd', d, w).reshape(B, T, D) + b
    return scale * y, xe[:, xe.shape[1] - POOL_HIST:]


def conv_ffn(x, hist, w_up, conv_w, conv_b, w_down):
    B, T, _ = x.shape
    u = x @ w_up
    ue = u if hist is None else jnp.concatenate([hist, u], axis=1)
    nh = ue.shape[1] - T
    up = jnp.pad(ue, ((0, 0), (CONV_W - 1 - nh, 0), (0, 0)))
    c = conv_b
    for k in range(CONV_W):
        c = c + conv_w[k] * up[:, k:k + T]
    h = jax.nn.gelu(c[..., D_FF:]) * c[..., :D_FF]
    return h @ w_down, ue[:, ue.shape[1] - (CONV_W - 1):]


def setup_inputs(seed: int = 0) -> dict:
    key = jax.random.key(seed)
    ks = jax.random.split(key, 32)
    n_pages = PAST_LEN // PAGE_SIZE
    n_used = DEC_BATCH * n_pages
    n_pool_pages = n_used + n_used // 4
    win_buf = min(WINDOW, PAST_LEN)
    f32 = jnp.float32
    nrm = lambda k, s, sc: jax.random.normal(k, s, f32) * sc
    perm = jax.random.permutation(ks[0], n_pool_pages)
    page_table = perm[:n_used].reshape(DEC_BATCH, n_pages).astype(jnp.int32)
    return {
        "x_prompt": nrm(ks[1], (BATCH, SEQ, D_MODEL), 1.0),
        "x_sample": nrm(ks[2], (DEC_BATCH, DEC_SEQ, D_MODEL), 1.0),
        "cache_nsa": nrm(ks[3], (N_NSA, n_pool_pages, PAGE_SIZE, 4, N_KV_HEADS, HEAD_DIM), 1.0),
        "state_nsa_win": nrm(ks[4], (N_NSA, DEC_BATCH, win_buf, 2, N_KV_HEADS, HEAD_DIM), 1.0),
        "state_pool": nrm(ks[5], (N_POOLMIX, DEC_BATCH, POOL_HIST, D_MODEL), 1.0),
        "state_ffn": nrm(ks[6], (DEPTH, DEC_BATCH, CONV_W - 1, 2 * D_FF), 1.0),
        "page_table": page_table,
        "p_prompt": nrm(ks[7], (DEPTH, BATCH, SEQ, PLE_DIM), 1.0),
        "p_sample": nrm(ks[8], (DEPTH, DEC_BATCH, DEC_SEQ, PLE_DIM), 1.0),
        "rel_bias": nrm(ks[9], (REL_BUCKETS, N_HEADS), 0.5),
        "nsa_w_in": nrm(ks[10], (N_NSA, D_MODEL, Q_DIM + KV_DIM + GATE_DIM), D_MODEL ** -0.5),
        "nsa_cmp_pe": nrm(ks[11], (N_NSA, 2, CMP_LEN, HEAD_DIM), 0.1),
        "nsa_cmp_w1": nrm(ks[12], (N_NSA, 2, CMP_LEN * HEAD_DIM, CMP_HID), (CMP_LEN * HEAD_DIM) ** -0.5),
        "nsa_cmp_b1": nrm(ks[13], (N_NSA, 2, CMP_HID), 0.02),
        "nsa_cmp_w2": nrm(ks[14], (N_NSA, 2, CMP_HID, HEAD_DIM), CMP_HID ** -0.5),
        "nsa_w_out": nrm(ks[15], (N_NSA, Q_DIM, D_MODEL), Q_DIM ** -0.5 * DEEPNORM_BETA),
        "pool_w": nrm(ks[16], (N_POOLMIX, len(POOL_WINDOWS), POOL_GW, POOL_GW), POOL_GW ** -0.5 * DEEPNORM_BETA),
        "pool_b": nrm(ks[17], (N_POOLMIX, D_MODEL), 0.02),
        "pool_scale": 1.0 + nrm(ks[18], (N_POOLMIX, D_MODEL), 0.02),
        "ffn_w_up": nrm(ks[19], (DEPTH, D_MODEL, 2 * D_FF), D_MODEL ** -0.5),
        "ffn_conv_w": nrm(ks[20], (DEPTH, CONV_W, 2 * D_FF), CONV_W ** -0.5),
        "ffn_conv_b": nrm(ks[21], (DEPTH, 2 * D_FF), 0.02),
        "ffn_w_down": nrm(ks[22], (DEPTH, D_FF, D_MODEL), D_FF ** -0.5 * DEEPNORM_BETA),
        "ln_g": 1.0 + nrm(ks[23], (DEPTH, 2, D_MODEL), 0.02),
        "ln_b": nrm(ks[24], (DEPTH, 2, D_MODEL), 0.02),
        "ple_w_proj": nrm(ks[25], (DEPTH, PLE_DIM, D_MODEL), PLE_DIM ** -0.5),
        "ple_w_gate": nrm(ks[26], (DEPTH, D_MODEL, D_MODEL), D_MODEL ** -0.5),
        "ple_b_gate": nrm(ks[27], (DEPTH, D_MODEL), 0.02),
    }


def reference(x_prompt, x_sample, cache_nsa, state_nsa_win, state_pool, state_ffn, page_table,
              p_prompt, p_sample, rel_bias, nsa_w_in, nsa_cmp_pe, nsa_cmp_w1, nsa_cmp_b1, nsa_cmp_w2,
              nsa_w_out, pool_w, pool_b, pool_scale, ffn_w_up, ffn_conv_w, ffn_conv_b, ffn_w_down,
              ln_g, ln_b, ple_w_proj, ple_w_gate, ple_b_gate):
    db = x_sample.shape[0]
    n_pages = page_table.shape[1]
    past_len = n_pages * PAGE_SIZE
    xp, xs = x_prompt, x_sample
    nsa_rp, nsa_rs, nsa_wp, nsa_ws = [], [], [], []
    pool_hp, pool_hs, ffn_hp, ffn_hs = [], [], [], []
    for i in range(DEPTH):
        j = i // N_MIXERS
        if i % N_MIXERS == 0:
            past = cache_nsa[j][page_table].reshape(db, past_len, 4, N_KV_HEADS, HEAD_DIM)
            w = (nsa_w_in[j], nsa_cmp_pe[j], nsa_cmp_w1[j], nsa_cmp_b1[j], nsa_cmp_w2[j], nsa_w_out[j], rel_bias)
            mp, rp, wp = nsa_mixer(xp, 0, None, None, *w)
            ms, rs, ws = nsa_mixer(xs, past_len, past, state_nsa_win[j], *w)
            nsa_rp.append(rp); nsa_rs.append(rs); nsa_wp.append(wp); nsa_ws.append(ws)
        else:
            mp, hp = pool_mixer(xp, None, 0, pool_w[j], pool_b[j], pool_scale[j])
            ms, hs = pool_mixer(xs, state_pool[j], past_len, pool_w[j], pool_b[j], pool_scale[j])
            pool_hp.append(hp); pool_hs.append(hs)
        xp = layer_norm(DEEPNORM_ALPHA * xp + mp, ln_g[i, 0], ln_b[i, 0])
        xs = layer_norm(DEEPNORM_ALPHA * xs + ms, ln_g[i, 0], ln_b[i, 0])
        fp, cp = conv_ffn(xp, None, ffn_w_up[i], ffn_conv_w[i], ffn_conv_b[i], ffn_w_down[i])
        fs, cs = conv_ffn(xs, state_ffn[i], ffn_w_up[i], ffn_conv_w[i], ffn_conv_b[i], ffn_w_down[i])
        ffn_hp.append(cp); ffn_hs.append(cs)
        xp = layer_norm(DEEPNORM_ALPHA * xp + fp, ln_g[i, 1], ln_b[i, 1])
        xs = layer_norm(DEEPNORM_ALPHA * xs + fs, ln_g[i, 1], ln_b[i, 1])
        xp = xp + jax.nn.sigmoid(xp @ ple_w_gate[i] + ple_b_gate[i]) * (p_prompt[i] @ ple_w_proj[i])
        xs = xs + jax.nn.sigmoid(xs @ ple_w_gate[i] + ple_b_gate[i]) * (p_sample[i] @ ple_w_proj[i])
    return (xp, xs,
            jnp.stack(nsa_rp), jnp.stack(nsa_rs), jnp.stack(nsa_wp), jnp.stack(nsa_ws),
            jnp.stack(pool_hp), jnp.stack(pool_hs), jnp.stack(ffn_hp), jnp.stack(ffn_hs))
```

```python
import functools
import math

import numpy as np
import jax
import jax.numpy as jnp
from jax import lax
from jax.experimental import pallas as pl
from jax.experimental.pallas import tpu as pltpu

N_KV_HEADS = 4
GQA_R = 4
HEAD_DIM = 64
CMP_LEN = 32
CMP_STRIDE = 16
CMP_HID = 256
SEL_LEN = 64
CMP_PER_SEL = SEL_LEN // CMP_STRIDE
TOP_N = 16
WINDOW = 512
Q_BLOCK = 128
FORCE_SCORE = 1.0e4
REL_BUCKETS = 32
REL_MAX_DIST = 128
POOL_WINDOWS = (2, 4, 8, 16)
POOL_MAX = max(POOL_WINDOWS)
POOL_HIST = POOL_MAX - 1
CONV_W = 3
LN_EPS = 1e-5
PAGE_SIZE = 128

NEG = -1.0e30
NEG_VALID = -1.0e29
BF16 = jnp.bfloat16
F32 = jnp.float32

VMEM_LIMIT_BYTES = 56 * 1024 * 1024


def _cparams(n_grid):
    return pltpu.CompilerParams(dimension_semantics=("arbitrary",) * n_grid,
                                vmem_limit_bytes=VMEM_LIMIT_BYTES)


def _layer_norm(y, g, b):
    mu = jnp.mean(y, axis=-1, keepdims=True)
    yc = y - mu
    var = jnp.mean(yc * yc, axis=-1, keepdims=True)
    return yc * lax.rsqrt(var + LN_EPS) * g + b


FFN_TN = 256
HALO = 16


def _ffn_up_prompt_kernel(x_ref, xh_ref, w_ref, cw_ref, cb_ref, h_ref, xe_ref, ue_ref, *,
                          tm, d_ff, tiles_per_batch):
    i = pl.program_id(0)
    first = (i % tiles_per_batch) == 0
    halo = xh_ref[...]
    xe_ref[0:HALO, :] = jnp.where(first, jnp.zeros_like(halo), halo).astype(BF16)
    xe_ref[HALO:HALO + tm, :] = x_ref[...].astype(BF16)
    xe = xe_ref[...]
    for j in range(d_ff // FFN_TN):
        c_parts = []
        for half in range(2):
            c0 = half * d_ff + j * FFN_TN
            ue_ref[...] = jnp.dot(xe, w_ref[:, c0:c0 + FFN_TN], preferred_element_type=F32)
            cw = cw_ref[:, c0:c0 + FFN_TN]
            c = cb_ref[:, c0:c0 + FFN_TN]
            for k in range(CONV_W):
                off = HALO - (CONV_W - 1) + k
                c = c + cw[k:k + 1, :] * ue_ref[off:off + tm, :]
            c_parts.append(c)
        h_ref[:, j * FFN_TN:(j + 1) * FFN_TN] = (jax.nn.gelu(c_parts[1]) * c_parts[0]).astype(BF16)


def _ffn_up_prompt(x, w_up, conv_w, conv_b, seq, tm=512):
    m, d = x.shape
    d_ff = w_up.shape[1] // 2
    tpb = seq // tm
    kern = functools.partial(_ffn_up_prompt_kernel, tm=tm, d_ff=d_ff, tiles_per_batch=tpb)
    return pl.pallas_call(
        kern,
        grid=(m // tm,),
        in_specs=[
            pl.BlockSpec((tm, d), lambda i: (i, 0)),
            pl.BlockSpec((HALO, d), lambda i: (jnp.maximum(i * (tm // HALO) - 1, 0), 0)),
            pl.BlockSpec(w_up.shape, lambda i: (0, 0)),
            pl.BlockSpec(conv_w.shape, lambda i: (0, 0)),
            pl.BlockSpec(conv_b.shape, lambda i: (0, 0)),
        ],
        out_specs=pl.BlockSpec((tm, d_ff), lambda i: (i, 0)),
        out_shape=jax.ShapeDtypeStruct((m, d_ff), BF16),
        scratch_shapes=[pltpu.VMEM((HALO + tm, d), BF16), pltpu.VMEM((HALO + tm, FFN_TN), F32)],
        compiler_params=_cparams(1),
        name="ffn_up_prompt",
    )(x, x, w_up, conv_w, conv_b)


def _ffn_gate_sample_kernel(ue_ref, cw_ref, cb_ref, h_ref, *, t, d_ff):
    for j in range(d_ff // FFN_TN):
        c_parts = []
        for half in range(2):
            c0 = half * d_ff + j * FFN_TN
            cw = cw_ref[:, c0:c0 + FFN_TN]
            c = cb_ref[:, c0:c0 + FFN_TN][None]
            for k in range(CONV_W):
                c = c + cw[k:k + 1, :][None] * ue_ref[:, k:k + t, c0:c0 + FFN_TN]
            c_parts.append(c)
        h_ref[:, :, j * FFN_TN:(j + 1) * FFN_TN] = (jax.nn.gelu(c_parts[1]) * c_parts[0]).astype(BF16)


def _ffn_gate_sample(ue, conv_w, conv_b, t):
    b, te, n2 = ue.shape
    d_ff = n2 // 2
    bb = 8
    kern = functools.partial(_ffn_gate_sample_kernel, t=t, d_ff=d_ff)
    return pl.pallas_call(
        kern,
        grid=(b // bb,),
        in_specs=[
            pl.BlockSpec((bb, te, n2), lambda i: (i, 0, 0)),
            pl.BlockSpec(conv_w.shape, lambda i: (0, 0)),
            pl.BlockSpec(conv_b.shape, lambda i: (0, 0)),
        ],
        out_specs=pl.BlockSpec((bb, t, d_ff), lambda i: (i, 0, 0)),
        out_shape=jax.ShapeDtypeStruct((b, t, d_ff), BF16),
        compiler_params=_cparams(1),
        name="ffn_gate_sample",
    )(ue, conv_w, conv_b)


def _matmul_kernel(x_ref, w_ref, o_ref):
    o_ref[...] = jnp.dot(x_ref[...].astype(BF16), w_ref[...], preferred_element_type=F32).astype(o_ref.dtype)


def _matmul(x, w, tm, tn=None, out_dtype=F32):
    m, k = x.shape
    n = w.shape[1]
    tn = n if tn is None else tn
    return pl.pallas_call(
        _matmul_kernel,
        grid=(m // tm, n // tn),
        in_specs=[pl.BlockSpec((tm, k), lambda i, j: (i, 0)),
                  pl.BlockSpec((k, tn), lambda i, j: (0, j))],
        out_specs=pl.BlockSpec((tm, tn), lambda i, j: (i, j)),
        out_shape=jax.ShapeDtypeStruct((m, n), out_dtype),
        compiler_params=_cparams(2),
        name="matmul",
    )(x, w)


def _linear_ln_kernel(a_ref, w_ref, r_ref, g_ref, b_ref, o_ref, *, alpha):
    f = jnp.dot(a_ref[...].astype(BF16), w_ref[...], preferred_element_type=F32)
    o_ref[...] = _layer_norm(alpha * r_ref[...] + f, g_ref[...], b_ref[...])


def _linear_ln(a, w, resid, g, b, alpha, tm):
    m, k = a.shape
    d = w.shape[1]
    kern = functools.partial(_linear_ln_kernel, alpha=alpha)
    return pl.pallas_call(
        kern,
        grid=(m // tm,),
        in_specs=[pl.BlockSpec((tm, k), lambda i: (i, 0)),
                  pl.BlockSpec((k, d), lambda i: (0, 0)),
                  pl.BlockSpec((tm, d), lambda i: (i, 0)),
                  pl.BlockSpec((1, d), lambda i: (0, 0)),
                  pl.BlockSpec((1, d), lambda i: (0, 0))],
        out_specs=pl.BlockSpec((tm, d), lambda i: (i, 0)),
        out_shape=jax.ShapeDtypeStruct((m, d), F32),
        compiler_params=_cparams(1),
        name="linear_ln",
    )(a, w, resid, g, b)


def _ffn_down_kernel(h_ref, wd_ref, r_ref, g_ref, b_ref, p_ref, wg_ref, bg_ref, wp_ref, o_ref, *, alpha):
    f = jnp.dot(h_ref[...], wd_ref[...], preferred_element_type=F32)
    x2 = _layer_norm(alpha * r_ref[...] + f, g_ref[...], b_ref[...])
    gate = jax.nn.sigmoid(jnp.dot(x2.astype(BF16), wg_ref[...], preferred_element_type=F32) + bg_ref[...])
    inj = jnp.dot(p_ref[...].astype(BF16), wp_ref[...], preferred_element_type=F32)
    o_ref[...] = x2 + gate * inj


def _ffn_down_ln_ple(h, w_down, resid, g, b, p, w_gate, b_gate, w_proj, alpha, tm):
    m, k = h.shape
    d = w_down.shape[1]
    pd = p.shape[1]
    kern = functools.partial(_ffn_down_kernel, alpha=alpha)
    full = lambda shape: pl.BlockSpec(shape, lambda i: (0, 0))
    return pl.pallas_call(
        kern,
        grid=(m // tm,),
        in_specs=[pl.BlockSpec((tm, k), lambda i: (i, 0)), full((k, d)),
                  pl.BlockSpec((tm, d), lambda i: (i, 0)), full((1, d)), full((1, d)),
                  pl.BlockSpec((tm, pd), lambda i: (i, 0)), full((d, d)), full((1, d)), full((pd, d))],
        out_specs=pl.BlockSpec((tm, d), lambda i: (i, 0)),
        out_shape=jax.ShapeDtypeStruct((m, d), F32),
        compiler_params=_cparams(1),
        name="ffn_down_ln_ple",
    )(h, w_down, resid, g, b, p, w_gate, b_gate, w_proj)


def _pool_finish(x, diffs, w_ref, pb_ref, ps_ref, g_ref, b_ref, alpha):
    ys = [jnp.dot(dg.astype(BF16), w_ref[gi], preferred_element_type=F32) for gi, dg in enumerate(diffs)]
    y = jnp.concatenate(ys, axis=-1) + pb_ref[...]
    return _layer_norm(alpha * x + ps_ref[...] * y, g_ref[...], b_ref[...])


def _pool_prompt_kernel(x_ref, xh_ref, w_ref, pb_ref, ps_ref, g_ref, b_ref, o_ref, xe_ref, *,
                        tt, gw, alpha):
    i = pl.program_id(1)
    halo = xh_ref[0]
    xe_ref[0:POOL_MAX, :] = jnp.where(i == 0, jnp.zeros_like(halo), halo)
    xe_ref[POOL_MAX:POOL_MAX + tt, :] = x_ref[0]
    pos = i * tt + lax.broadcasted_iota(jnp.int32, (tt, 1), 0)
    x = x_ref[0]
    diffs = []
    for gi, win in enumerate(POOL_WINDOWS):
        lo = gi * gw
        tot = x[:, lo:lo + gw]
        for k in range(1, win):
            tot = tot + xe_ref[POOL_MAX - k:POOL_MAX - k + tt, lo:lo + gw]
        cnt = jnp.minimum(win, pos + 1).astype(F32)
        diffs.append(tot / cnt - x[:, lo:lo + gw])
    o_ref[0] = _pool_finish(x, diffs, w_ref, pb_ref, ps_ref, g_ref, b_ref, alpha)


def _pool_ln_prompt(x, w, pb, ps, g, b, alpha, tt=512):
    bsz, t, d = x.shape
    gw = d // len(POOL_WINDOWS)
    kern = functools.partial(_pool_prompt_kernel, tt=tt, gw=gw, alpha=alpha)
    vec = pl.BlockSpec((1, d), lambda bi, i: (0, 0))
    return pl.pallas_call(
        kern,
        grid=(bsz, t // tt),
        in_specs=[pl.BlockSpec((1, tt, d), lambda bi, i: (bi, i, 0)),
                  pl.BlockSpec((1, POOL_MAX, d), lambda bi, i: (bi, jnp.maximum(i * (tt // POOL_MAX) - 1, 0), 0)),
                  pl.BlockSpec(w.shape, lambda bi, i: (0, 0, 0)), vec, vec, vec, vec],
        out_specs=pl.BlockSpec((1, tt, d), lambda bi, i: (bi, i, 0)),
        out_shape=jax.ShapeDtypeStruct((bsz, t, d), F32),
        scratch_shapes=[pltpu.VMEM((POOL_MAX + tt, d), F32)],
        compiler_params=_cparams(2),
        name="pool_ln_prompt",
    )(x, x, w, pb, ps, g, b)


def _pool_sample_kernel(xe_ref, w_ref, pb_ref, ps_ref, g_ref, b_ref, o_ref, *, t, nh, gw, pos0, alpha):
    bb = xe_ref.shape[0]
    d = xe_ref.shape[2]
    x = xe_ref[:, nh:nh + t, :].reshape(bb * t, d)
    cnt_np = np.minimum(np.array(POOL_WINDOWS)[:, None], pos0 + np.arange(t)[None, :] + 1).astype(np.float32)
    diffs = []
    for gi, win in enumerate(POOL_WINDOWS):
        lo = gi * gw
        tot = xe_ref[:, nh:nh + t, lo:lo + gw]
        for k in range(1, win):
            tot = tot + xe_ref[:, nh - k:nh - k + t, lo:lo + gw]
        if np.all(cnt_np[gi] == win):
            mean = tot / float(win)
        else:
            t_idx = lax.broadcasted_iota(jnp.int32, (1, t, 1), 1)
            mean = tot / jnp.minimum(win, pos0 + t_idx + 1).astype(F32)
        diffs.append(mean.reshape(bb * t, gw) - x[:, lo:lo + gw])
    o_ref[...] = _pool_finish(x, diffs, w_ref, pb_ref, ps_ref, g_ref, b_ref, alpha).reshape(bb, t, d)


def _pool_ln_sample(xe, t, pos0, w, pb, ps, g, b, alpha, bb=8):
    bsz, te, d = xe.shape
    gw = d // len(POOL_WINDOWS)
    kern = functools.partial(_pool_sample_kernel, t=t, nh=te - t, gw=gw, pos0=pos0, alpha=alpha)
    vec = pl.BlockSpec((1, d), lambda i: (0, 0))
    return pl.pallas_call(
        kern,
        grid=(bsz // bb,),
        in_specs=[pl.BlockSpec((bb, te, d), lambda i: (i, 0, 0)),
                  pl.BlockSpec(w.shape, lambda i: (0, 0, 0)), vec, vec, vec, vec],
        out_specs=pl.BlockSpec((bb, t, d), lambda i: (i, 0, 0)),
        out_shape=jax.ShapeDtypeStruct((bsz, t, d), F32),
        compiler_params=_cparams(1),
        name="pool_ln_sample",
    )(xe, w, pb, ps, g, b)


def _nsa_proj_kernel(x_ref, wq_ref, wkv_ref, wg_ref, q_ref, kv_ref, kvb_ref, gt_ref):
    xb = x_ref[...].astype(BF16)
    q_ref[...] = jnp.dot(xb, wq_ref[...], preferred_element_type=F32).astype(BF16)
    kv = jnp.dot(xb, wkv_ref[...], preferred_element_type=F32)
    kv_ref[...] = kv
    kvb_ref[...] = kv.astype(BF16)
    gt_ref[...] = jax.nn.sigmoid(jnp.dot(xb, wg_ref[...], preferred_element_type=F32))


def _nsa_proj(x, wq, wkv, wg, tm):
    m, d = x.shape
    nq, nkv, ng = wq.shape[1], wkv.shape[1], wg.shape[1]
    full = lambda w: pl.BlockSpec(w.shape, lambda i: (0, 0))
    row = lambda n: pl.BlockSpec((tm, n), lambda i: (i, 0))
    return pl.pallas_call(
        _nsa_proj_kernel,
        grid=(m // tm,),
        in_specs=[row(d), full(wq), full(wkv), full(wg)],
        out_specs=[row(nq), row(nkv), row(nkv), row(ng)],
        out_shape=[jax.ShapeDtypeStruct((m, nq), BF16), jax.ShapeDtypeStruct((m, nkv), F32),
                   jax.ShapeDtypeStruct((m, nkv), BF16), jax.ShapeDtypeStruct((m, ng), F32)],
        compiler_params=_cparams(1),
        name="nsa_proj",
    )(x, wq, wkv, wg)


def _nsa_cmp_kernel(*refs, nc, transposed):
    ns = CMP_STRIDE
    x_refs = refs[:ns]
    w1_ref, pe_ref, w2k_ref, w2v_ref, ko_ref, vo_ref = refs[ns:]
    hd = HEAD_DIM
    gw = N_KV_HEADS * hd
    last = lax.broadcasted_iota(jnp.int32, (nc, CMP_HID), 0) == nc - 1
    for c in range(2):
        for g in range(N_KV_HEADS):
            acc = jnp.zeros((nc, 2 * CMP_HID), F32)
            for s in range(ns):
                xs = x_refs[s][0, :, c * gw + g * hd:c * gw + (g + 1) * hd]
                acc = acc + jnp.dot(xs, w1_ref[c, s], preferred_element_type=F32)
            h1 = jnp.where(last, 0.0, pltpu.roll(acc[:, CMP_HID:], nc - 1, 0))
            hid = jax.nn.gelu(acc[:, :CMP_HID] + h1 + pe_ref[c:c + 1, :]).astype(BF16)
            if transposed:
                if c == 0:
                    ko_ref[0, g] = lax.dot_general(w2k_ref[...], hid, (((1,), (1,)), ((), ())),
                                                   preferred_element_type=F32).astype(BF16)
                else:
                    vo_ref[0, g] = jnp.dot(hid, w2v_ref[...], preferred_element_type=F32).astype(BF16)
            else:
                o_ref = ko_ref if c == 0 else vo_ref
                w2 = w2k_ref if c == 0 else w2v_ref
                o_ref[0, :, g * hd:(g + 1) * hd] = jnp.dot(hid, w2[...], preferred_element_type=F32).astype(BF16)


def _nsa_cmp(xv, col_block_stride, w1, pe_term, w2k, w2v, transposed):
    bsz, nc, _ = xv.shape
    g, hd = N_KV_HEADS, HEAD_DIM
    kern = functools.partial(_nsa_cmp_kernel, nc=nc, transposed=transposed)
    x_specs = [pl.BlockSpec((1, nc, 2 * g * hd), functools.partial(lambda b, s: (b, 0, s * col_block_stride), s=s))
               for s in range(CMP_STRIDE)]
    full = lambda w: pl.BlockSpec(w.shape, lambda b: (0,) * w.ndim)
    if transposed:
        out_specs = [pl.BlockSpec((1, g, 128, nc), lambda b: (b, 0, 0, 0)),
                     pl.BlockSpec((1, g, nc, 128), lambda b: (b, 0, 0, 0))]
        out_shape = [jax.ShapeDtypeStruct((bsz, g, 128, nc), BF16), jax.ShapeDtypeStruct((bsz, g, nc, 128), BF16)]
    else:
        out_specs = [pl.BlockSpec((1, nc, g * hd), lambda b: (b, 0, 0))] * 2
        out_shape = [jax.ShapeDtypeStruct((bsz, nc, g * hd), BF16)] * 2
    return pl.pallas_call(
        kern,
        grid=(bsz,),
        in_specs=x_specs + [full(w1), full(pe_term), full(w2k), full(w2v)],
        out_specs=out_specs,
        out_shape=out_shape,
        compiler_params=_cparams(1),
        name="nsa_cmp",
    )(*([xv] * CMP_STRIDE), w1, pe_term, w2k, w2v)


PADL = 512


def _nsa_prep_kernel(kv_ref, kst_ref, kwt_ref, vs_ref, vw_ref, *, nb):
    j = pl.program_id(1)
    tl = PADL
    hd = HEAD_DIM
    gw = N_KV_HEADS * hd

    @pl.when(j == 0)
    def _():
        for g in range(N_KV_HEADS):
            kst_ref[0, g] = (lax.broadcasted_iota(jnp.int32, (128 + nb, tl), 0) == hd).astype(BF16)
            kwt_ref[0, g] = (lax.broadcasted_iota(jnp.int32, (128, tl), 0) == hd).astype(BF16)
            vs_ref[0, g] = jnp.zeros((tl, 128), BF16)
            vw_ref[0, g] = jnp.zeros((tl, 128), BF16)

    @pl.when(j > 0)
    def _():
        kst = kv_ref[0, :, 2 * gw:3 * gw].T
        kwt = kv_ref[0, :, 4 * gw:5 * gw].T
        pos = (j - 1) * tl + lax.broadcasted_iota(jnp.int32, (nb, tl), 1)
        onehot = (pos // SEL_LEN == lax.broadcasted_iota(jnp.int32, (nb, tl), 0)).astype(BF16)
        zeros = jnp.zeros((128 - hd, tl), BF16)
        lane = lax.broadcasted_iota(jnp.int32, (tl, 128), 1)
        tail = (lane == hd).astype(F32)
        for g in range(N_KV_HEADS):
            kst_ref[0, g, 0:hd, :] = kst[g * hd:(g + 1) * hd].astype(BF16)
            kst_ref[0, g, hd:128, :] = zeros
            kst_ref[0, g, 128:128 + nb, :] = onehot
            kwt_ref[0, g, 0:hd, :] = kwt[g * hd:(g + 1) * hd].astype(BF16)
            kwt_ref[0, g, hd:128, :] = zeros
            for stream, ref in ((3, vs_ref), (5, vw_ref)):
                a = kv_ref[0, :, stream * gw + (g // 2) * 128:stream * gw + (g // 2 + 1) * 128]
                if g % 2:
                    a = pltpu.roll(a, hd, 1)
                ref[0, g] = jnp.where(lane < hd, a, tail).astype(BF16)


def _nsa_prep(kv, nb):
    bsz, t, n = kv.shape
    g = N_KV_HEADS
    tl = PADL
    tp = PADL + t
    kern = functools.partial(_nsa_prep_kernel, nb=nb)
    kspec = lambda rows: pl.BlockSpec((1, g, rows, tl), lambda b, j: (b, 0, 0, j))
    vspec = pl.BlockSpec((1, g, tl, 128), lambda b, j: (b, 0, j, 0))
    return pl.pallas_call(
        kern,
        grid=(bsz, tp // tl),
        in_specs=[pl.BlockSpec((1, tl, n), lambda b, j: (b, jnp.maximum(j - 1, 0), 0))],
        out_specs=[kspec(128 + nb), kspec(128), vspec, vspec],
        out_shape=[jax.ShapeDtypeStruct((bsz, g, 128 + nb, tp), BF16), jax.ShapeDtypeStruct((bsz, g, 128, tp), BF16),
                   jax.ShapeDtypeStruct((bsz, g, tp, 128), BF16), jax.ShapeDtypeStruct((bsz, g, tp, 128), BF16)],
        compiler_params=_cparams(2),
        name="nsa_prep",
    )(kv)


def _split3(x):
    a = x.astype(BF16)
    r = x - a.astype(F32)
    b = r.astype(BF16)
    return a, b, (r - b.astype(F32)).astype(BF16)


def _top_blocks(score, blk, n_sel, n_rows):
    sel = jnp.zeros(score.shape, jnp.bool_)
    for _ in range(n_sel):
        m = jnp.max(score, axis=0, keepdims=True)
        first = jnp.min(jnp.where(score == m, blk, n_rows), axis=0, keepdims=True)
        pick = blk == first
        sel = sel | pick
        score = jnp.where(pick, -3.0e38, score)
    return sel


def _nsa_attn_prompt_kernel(q_ref, gt_ref, kct_ref, vc_ref, kst_ref, vs_ref, kwt_ref, vw_ref,
                            wtab_ref, atab_ref, poolt_ref, o_ref, *, nb, nc):
    qi = pl.program_id(2)
    t0 = qi * Q_BLOCK
    r_heads = GQA_R
    hd = HEAD_DIM
    rows = r_heads * Q_BLOCK
    nw = WINDOW + Q_BLOCK
    lane = lax.broadcasted_iota(jnp.int32, (Q_BLOCK, 128), 1)
    q = jnp.concatenate([jnp.where(lane == hd, NEG, q_ref[:, r * 128:(r + 1) * 128]) for r in range(r_heads)],
                        axis=0)

    n_idx = lax.broadcasted_iota(jnp.int32, (48, nc), 1)
    j_idx = lax.broadcasted_iota(jnp.int32, (48, nc), 0)
    rel = n_idx - (qi * (Q_BLOCK // CMP_STRIDE) - 8)
    shift = (((j_idx < 32) & (rel == jnp.where(j_idx < 16, j_idx, j_idx - 16)))
             | ((j_idx == 32) & (rel >= 16))).astype(F32).astype(BF16)
    s_c = (jnp.dot(q, kct_ref[0, 0], preferred_element_type=F32)
           + jnp.dot(atab_ref[0], shift, preferred_element_type=F32))
    m_c = jnp.max(s_c, axis=1, keepdims=True)
    e_c = jnp.where(s_c > NEG_VALID, jnp.exp(s_c - m_c), 0.0)
    p_c = e_c / jnp.maximum(jnp.sum(e_c, axis=1, keepdims=True), 1e-30)
    o_c = jnp.dot(p_c.astype(BF16), vc_ref[0, 0], preferred_element_type=F32)

    grp = p_c[0:Q_BLOCK]
    for r in range(1, r_heads):
        grp = grp + p_c[r * Q_BLOCK:(r + 1) * Q_BLOCK]
    imp_t = jnp.zeros((nb, Q_BLOCK), F32)
    for part in _split3(grp):
        imp_t = imp_t + lax.dot_general(poolt_ref[...], part, (((1,), (1,)), ((), ())),
                                        preferred_element_type=F32)
    blk = lax.broadcasted_iota(jnp.int32, (nb, Q_BLOCK), 0)
    tq = t0 + lax.broadcasted_iota(jnp.int32, (nb, Q_BLOCK), 1)
    cur = tq // SEL_LEN
    forced = (blk == 0) | (blk == cur) | (blk == cur - 1)
    score = jnp.where(blk * SEL_LEN > tq, -1.0, jnp.where(forced, FORCE_SCORE, imp_t))
    sel = _top_blocks(score, blk, min(TOP_N, nb), nb)
    selb = jnp.where(sel, 0.0, NEG).T.astype(BF16)
    q_aug = jnp.concatenate([q, jnp.concatenate([selb] * r_heads, axis=0)], axis=1)

    c_near = pl.multiple_of(t0 + PADL - Q_BLOCK, Q_BLOCK)
    s = (jnp.dot(q_aug, kst_ref[0, 0, :, pl.ds(c_near, 2 * Q_BLOCK)], preferred_element_type=F32)
         + wtab_ref[0, :, nw - 2 * Q_BLOCK:nw])
    m_s = jnp.max(s, axis=1, keepdims=True)
    acc = jnp.dot(jnp.exp(s - m_s).astype(BF16), vs_ref[0, 0, pl.ds(c_near, 2 * Q_BLOCK), :],
                  preferred_element_type=F32)

    def far_step(c0, width, carry):
        m_old, acc_old = carry
        s_f = jnp.dot(q_aug, kst_ref[0, 0, :, pl.ds(c0, width)], preferred_element_type=F32)
        m_new = jnp.maximum(m_old, jnp.max(s_f, axis=1, keepdims=True))
        pv = jnp.dot(jnp.exp(s_f - m_new).astype(BF16), vs_ref[0, 0, pl.ds(c0, width), :],
                     preferred_element_type=F32)
        return m_new, jnp.exp(m_old - m_new) * acc_old + pv

    n_far = jnp.maximum(qi - 1, 0)
    n_wide = n_far // 4
    carry = lax.fori_loop(
        0, n_wide, lambda i, c: far_step(pl.multiple_of(PADL + i * 512, 512), 512, c), (m_s, acc))
    carry = lax.fori_loop(
        0, n_far - 4 * n_wide,
        lambda i, c: far_step(pl.multiple_of(PADL + n_wide * 512 + i * Q_BLOCK, Q_BLOCK), Q_BLOCK, c), carry)
    acc = carry[1]
    o_s = acc[:, :hd] / acc[:, hd:hd + 1]

    c_w = pl.multiple_of(t0 + PADL - WINDOW, Q_BLOCK)
    s_w = jnp.dot(q, kwt_ref[0, 0, :, pl.ds(c_w, nw)], preferred_element_type=F32) + wtab_ref[0]
    m_w = jnp.max(s_w, axis=1, keepdims=True)
    acc_w = jnp.dot(jnp.exp(s_w - m_w).astype(BF16), vw_ref[0, 0, pl.ds(c_w, nw), :], preferred_element_type=F32)
    o_w = acc_w[:, :hd] / acc_w[:, hd:hd + 1]

    gt = gt_ref[0]
    for r in range(r_heads):
        sl = slice(r * Q_BLOCK, (r + 1) * Q_BLOCK)
        o = (gt[:, 3 * r:3 * r + 1] * o_c[sl, :hd] + gt[:, 3 * r + 1:3 * r + 2] * o_s[sl]
             + gt[:, 3 * r + 2:3 * r + 3] * o_w[sl])
        o_ref[:, r * hd:(r + 1) * hd] = o.astype(BF16)


def _nsa_attn_prompt(q_pad, gates_g, kct, vc, kst, vs, kwt, vw, wtab, atab, poolt, bsz, seq):
    g, r_heads, hd = N_KV_HEADS, GQA_R, HEAD_DIM
    nq = seq // Q_BLOCK
    nb, nc = poolt.shape
    tp = PADL + seq
    kern = functools.partial(_nsa_attn_prompt_kernel, nb=nb, nc=nc)
    per_bg = lambda shape: pl.BlockSpec((1, 1) + shape, lambda b, gi, i: (b, gi, 0, 0))
    per_g = lambda shape: pl.BlockSpec((1,) + shape, lambda b, gi, i: (gi, 0, 0))
    return pl.pallas_call(
        kern,
        grid=(bsz, g, nq),
        in_specs=[pl.BlockSpec((Q_BLOCK, r_heads * 128), lambda b, gi, i: (b * nq + i, gi)),
                  pl.BlockSpec((1, Q_BLOCK, 128), lambda b, gi, i: (gi, b * nq + i, 0)),
                  per_bg((128, nc)), per_bg((nc, 128)),
                  per_bg((128 + nb, tp)), per_bg((tp, 128)),
                  per_bg((128, tp)), per_bg((tp, 128)),
                  per_g((r_heads * Q_BLOCK, WINDOW + Q_BLOCK)), per_g((r_heads * Q_BLOCK, 48)),
                  pl.BlockSpec((nb, nc), lambda b, gi, i: (0, 0))],
        out_specs=pl.BlockSpec((Q_BLOCK, r_heads * hd), lambda b, gi, i: (b * nq + i, gi)),
        out_shape=jax.ShapeDtypeStruct((bsz * seq, g * r_heads * hd), BF16),
        compiler_params=_cparams(3),
        name="nsa_attn_prompt",
    )(q_pad, gates_g, kct, vc, kst, vs, kwt, vw, wtab, atab, poolt)


def _rel_bucket_np(dist):
    n = np.maximum(dist, 0)
    exact = REL_BUCKETS // 2
    nf = np.maximum(n, 1).astype(np.float32)
    large = exact + (np.log(nf / np.float32(exact)) / np.float32(math.log(REL_MAX_DIST / exact))
                     * np.float32(REL_BUCKETS - exact)).astype(np.int32)
    return np.where(n < exact, n, np.minimum(large, REL_BUCKETS - 1))


FAR_DIST = Q_BLOCK - CMP_STRIDE + 1
assert np.all(_rel_bucket_np(np.arange(FAR_DIST, 1 << 16)) == REL_BUCKETS - 1)


def _bias_table(rel_bias, dist, valid):
    relc = rel_bias.astype(F32) - rel_bias[REL_BUCKETS - 1].astype(F32)[None]
    tab = relc[_rel_bucket_np(dist)]
    tab = jnp.where(valid[..., None], tab, NEG)
    return jnp.moveaxis(tab, -1, 0)


def _prompt_tables(rel_bias, nb, nc):
    g, r_heads = N_KV_HEADS, GQA_R
    qi = np.arange(Q_BLOCK)[:, None]
    nw = WINDOW + Q_BLOCK
    d_w = qi + WINDOW - np.arange(nw)[None, :]
    wtab = _bias_table(rel_bias, d_w, (d_w >= 0) & (d_w <= WINDOW)).reshape(g, r_heads * Q_BLOCK, nw)
    d_c = qi - CMP_STRIDE * (np.arange(16)[None, :] - 8) - (CMP_LEN - 1)
    valid_c = d_c >= 0
    a = _bias_table(rel_bias, d_c, valid_c).reshape(g, r_heads * Q_BLOCK, 16)
    hi = a.astype(BF16)
    lo = jnp.where(jnp.asarray(np.tile(valid_c, (r_heads, 1)))[None], a - hi.astype(F32), 0.0).astype(BF16)
    fut = jnp.full((g, r_heads * Q_BLOCK, 1), NEG, BF16)
    atab = jnp.concatenate([hi, lo, fut, jnp.zeros((g, r_heads * Q_BLOCK, 15), BF16)], axis=-1)
    jb = np.arange(nb)[:, None]
    n = np.arange(nc)[None, :]
    poolt = jnp.asarray(((n >= CMP_PER_SEL * jb - 1) & (n <= CMP_PER_SEL * jb + CMP_PER_SEL - 1)).astype(np.float32),
                        BF16)
    return wtab, atab, poolt


PAGES_PER_STEP = 4


def _gather_pages_kernel(pt_ref, *refs):
    del pt_ref
    page_refs = refs[:PAGES_PER_STEP]
    cmp_ref, sel_ref = refs[PAGES_PER_STEP:]
    half = cmp_ref.shape[2]
    for k in range(PAGES_PER_STEP):
        rows = slice(k * PAGE_SIZE, (k + 1) * PAGE_SIZE)
        cmp_ref[0, rows, :] = page_refs[k][0, :, :half].astype(BF16)
        sel_ref[0, rows, :] = page_refs[k][0, :, half:].astype(BF16)


def _gather_pages(cache2d, page_table):
    bsz, n_pages = page_table.shape
    width = cache2d.shape[2]
    half = width // 2
    steps = n_pages // PAGES_PER_STEP
    page_spec = lambda k: pl.BlockSpec((1, PAGE_SIZE, width),
                                       lambda b, i, pt: (pt[b, i * PAGES_PER_STEP + k], 0, 0))
    out_spec = pl.BlockSpec((1, PAGES_PER_STEP * PAGE_SIZE, half), lambda b, i, pt: (b, i, 0))
    out = jax.ShapeDtypeStruct((bsz, n_pages * PAGE_SIZE, half), BF16)
    return pl.pallas_call(
        _gather_pages_kernel,
        grid_spec=pltpu.PrefetchScalarGridSpec(
            num_scalar_prefetch=1, grid=(bsz, steps),
            in_specs=[page_spec(k) for k in range(PAGES_PER_STEP)],
            out_specs=[out_spec, out_spec]),
        out_shape=[out, out],
        compiler_params=_cparams(2),
        name="gather_pages",
    )(page_table, *([cache2d] * PAGES_PER_STEP))


def _nsa_attn_sample_kernel(q_ref, gt_ref, kc_ref, vc_ref, past_ref, new_ref, win_ref,
                            tc_ref, tnear_ref, tnew_ref, tw_ref, poolt_ref, e_ref, rsum_ref,
                            o_ref, s_ref, *, t, n_past_blk):
    g, r_heads, hd = N_KV_HEADS, GQA_R, HEAD_DIM
    gw = g * hd
    lanes = g * r_heads * t
    past = past_ref.shape[1]
    dn0 = (((0,), (0,)), ((), ()))
    q_t = q_ref[0]

    s_c = jnp.dot(kc_ref[0], q_t, preferred_element_type=F32) + tc_ref[...]
    m_c = jnp.max(s_c, axis=0, keepdims=True)
    e_c = jnp.where(s_c > NEG_VALID, jnp.exp(s_c - m_c), 0.0)
    p_c = e_c / jnp.maximum(jnp.sum(e_c, axis=0, keepdims=True), 1e-30)
    o_c = lax.dot_general(p_c.astype(BF16), vc_ref[0], dn0, preferred_element_type=F32)

    imp0 = jnp.zeros((n_past_blk, lanes), F32)
    for part in _split3(p_c):
        imp0 = imp0 + jnp.dot(poolt_ref[...], part, preferred_element_type=F32)
    imp = jnp.zeros((n_past_blk, lanes), F32)
    for part in _split3(imp0):
        imp = imp + jnp.dot(part, rsum_ref[...], preferred_element_type=F32)
    blk = lax.broadcasted_iota(jnp.int32, (n_past_blk, lanes), 0)
    forced = (blk == 0) | (blk == n_past_blk - 1)
    sel = _top_blocks(jnp.where(forced, FORCE_SCORE, imp), blk, min(TOP_N - 1, n_past_blk), n_past_blk)
    selb = jnp.where(sel, 0.0, NEG).astype(BF16)

    s_ref[...] = (jnp.dot(past_ref[0, :, :gw], q_t, preferred_element_type=F32)
                  + jnp.dot(e_ref[...], selb, preferred_element_type=F32))
    near = tnear_ref.shape[0]
    s_ref[past - near:past, :] = s_ref[past - near:past, :] + tnear_ref[...]
    new = new_ref[0]
    s_new = jnp.dot(new[:, 2 * gw:3 * gw], q_t, preferred_element_type=F32) + tnew_ref[...]
    m_s = jnp.maximum(jnp.max(s_ref[...], axis=0, keepdims=True), jnp.max(s_new, axis=0, keepdims=True))
    p_s = jnp.exp(s_ref[...] - m_s).astype(BF16)
    p_new = jnp.exp(s_new - m_s).astype(BF16)
    o_s = (lax.dot_general(p_s, past_ref[0, :, gw:], dn0, preferred_element_type=F32)
           + lax.dot_general(p_new, new[:, 3 * gw:4 * gw], dn0, preferred_element_type=F32))
    l_s = (lax.dot_general(p_s, jnp.ones((past, 128), BF16), dn0, preferred_element_type=F32)
           + lax.dot_general(p_new, jnp.ones((t, 128), BF16), dn0, preferred_element_type=F32))

    win = win_ref[0].astype(BF16)
    s_w = jnp.dot(win[:, :gw], q_t, preferred_element_type=F32) + tw_ref[...]
    s_wn = jnp.dot(new[:, 4 * gw:5 * gw], q_t, preferred_element_type=F32) + tnew_ref[...]
    m_w = jnp.maximum(jnp.max(s_w, axis=0, keepdims=True), jnp.max(s_wn, axis=0, keepdims=True))
    p_w = jnp.exp(s_w - m_w).astype(BF16)
    p_wn = jnp.exp(s_wn - m_w).astype(BF16)
    o_w = (lax.dot_general(p_w, win[:, gw:], dn0, preferred_element_type=F32)
           + lax.dot_general(p_wn, new[:, 5 * gw:6 * gw], dn0, preferred_element_type=F32))
    l_w = (lax.dot_general(p_w, jnp.ones((win.shape[0], 128), BF16), dn0, preferred_element_type=F32)
           + lax.dot_general(p_wn, jnp.ones((t, 128), BF16), dn0, preferred_element_type=F32))

    gt = gt_ref[0]
    o = gt[:, 0:1] * o_c + gt[:, 1:2] * (o_s / l_s[:, 0:1]) + gt[:, 2:3] * (o_w / l_w[:, 0:1])
    for gi in range(g):
        for r in range(r_heads):
            h = gi * r_heads + r
            o_ref[0, :, h * hd:(h + 1) * hd] = o[h * t:(h + 1) * t, gi * hd:(gi + 1) * hd].astype(BF16)


def _nsa_attn_sample(q_bd, gates, kc, vc, past_sel, kvb_new, win, tabs, t):
    bsz, past, _ = past_sel.shape
    g, r_heads, hd = N_KV_HEADS, GQA_R, HEAD_DIM
    n_past_blk = past // SEL_LEN
    kern = functools.partial(_nsa_attn_sample_kernel, t=t, n_past_blk=n_past_blk)
    per_b = lambda a: pl.BlockSpec((1,) + a.shape[1:], lambda b: (b, 0, 0))
    full = lambda a: pl.BlockSpec(a.shape, lambda b: (0, 0))
    return pl.pallas_call(
        kern,
        grid=(bsz,),
        in_specs=[per_b(q_bd), per_b(gates), per_b(kc), per_b(vc), per_b(past_sel), per_b(kvb_new), per_b(win)]
                 + [full(a) for a in tabs],
        out_specs=pl.BlockSpec((1, t, g * r_heads * hd), lambda b: (b, 0, 0)),
        out_shape=jax.ShapeDtypeStruct((bsz, t, g * r_heads * hd), BF16),
        scratch_shapes=[pltpu.VMEM((past, g * r_heads * t), F32)],
        compiler_params=_cparams(1),
        name="nsa_attn_sample",
    )(q_bd, gates, kc, vc, past_sel, kvb_new, win, *tabs)


def _sample_tables(rel_bias, pos0, t, nc):
    g, r_heads = N_KV_HEADS, GQA_R
    lanes = g * r_heads * t
    tq = pos0 + np.arange(t)[None, :]

    def table(kpos, extra_valid=True):
        dist = tq - kpos[:, None]
        tab = _bias_table(rel_bias, dist, (dist >= 0) & extra_valid)
        return tab.transpose(1, 0, 2).reshape(kpos.shape[0], lanes)

    tc = table(CMP_STRIDE * np.arange(nc) + CMP_LEN - 1)
    tnear = table(pos0 - Q_BLOCK + np.arange(Q_BLOCK))
    tnew = table(pos0 + np.arange(t))
    kw = pos0 - WINDOW + np.arange(WINDOW)
    tw = table(kw, (tq - kw[:, None]) <= WINDOW)
    n_past_blk = pos0 // SEL_LEN
    jb = np.arange(n_past_blk)[:, None]
    n = np.arange(nc)[None, :]
    poolt = jnp.asarray(((n >= CMP_PER_SEL * jb - 1) & (n <= CMP_PER_SEL * jb + CMP_PER_SEL - 1)).astype(np.float32),
                        BF16)
    e = jnp.asarray((np.arange(pos0)[:, None] // SEL_LEN == np.arange(n_past_blk)[None, :]).astype(np.float32), BF16)
    li = np.arange(lanes)
    same = (li[:, None] // (r_heads * t) == li[None, :] // (r_heads * t)) & (li[:, None] % t == li[None, :] % t)
    rsum = jnp.asarray(same.astype(np.float32), BF16)
    return tc, tnear, tnew, tw, poolt, e, rsum


def _nsa_sample(xs2d, bsz, t, cache2d, page_table, win_state, wts, tabs):
    g, r_heads, hd = N_KV_HEADS, GQA_R, HEAD_DIM
    gw = g * hd
    q, kv, kvb, gt = _nsa_proj(xs2d, wts["wq"], wts["wkv"], wts["wg"], tm=xs2d.shape[0])
    past_cmp, past_sel = _gather_pages(cache2d, page_table)
    past = past_cmp.shape[1]
    nc = past // CMP_STRIDE
    kc, vc = _nsa_cmp(past_cmp.reshape(bsz, nc, CMP_STRIDE * 2 * gw), 1,
                      wts["w1"], wts["pe_term"], wts["w2k"], wts["w2v"], transposed=False)
    q5 = q.reshape(bsz, t, g, r_heads, hd)
    q_bd = jnp.einsum('btgrd,gh->bgdhrt', q5, jnp.eye(g, dtype=q.dtype)).reshape(bsz, gw, g * r_heads * t)
    gates = jnp.pad(gt[:, :3 * g * r_heads].reshape(bsz, t, g * r_heads, 3).transpose(0, 2, 1, 3)
                    .reshape(bsz, g * r_heads * t, 3), ((0, 0), (0, 0), (0, 125)))
    o = _nsa_attn_sample(q_bd, gates, kc, vc, past_sel, kvb.reshape(bsz, t, 6 * gw), win_state, tabs, t)
    return o.reshape(bsz * t, g * r_heads * hd), kv


def _nsa_weights(w_in, cmp_pe, cmp_w1, cmp_b1, cmp_w2):
    d = w_in.shape[0]
    g, r_heads, hd = N_KV_HEADS, GQA_R, HEAD_DIM
    qd, kvd = g * r_heads * hd, 6 * g * hd
    wq = (w_in[:, :qd] * hd ** -0.5).astype(BF16)
    wq_pad = jnp.pad(wq.reshape(d, g * r_heads, hd), ((0, 0), (0, 0), (0, 128 - hd))).reshape(d, g * r_heads * 128)
    wkv = w_in[:, qd:qd + kvd].astype(BF16)
    wg = jnp.pad(w_in[:, qd + kvd:], ((0, 0), (0, 128 - 3 * g * r_heads))).astype(BF16)
    w1r = cmp_w1.reshape(2, CMP_LEN, hd, CMP_HID)
    w1 = jnp.concatenate([w1r[:, :CMP_STRIDE], w1r[:, CMP_STRIDE:]], axis=-1).astype(BF16)
    pe_term = jnp.einsum('csd,csdh->ch', cmp_pe, w1r) + cmp_b1
    return dict(wq=wq, wq_pad=wq_pad, wkv=wkv, wg=wg, w1=w1, pe_term=pe_term,
                w2k=cmp_w2[0].astype(BF16), w2v=cmp_w2[1].astype(BF16))


def _nsa_prompt(x2d, bsz, seq, wts, tables):
    g, r_heads, hd = N_KV_HEADS, GQA_R, HEAD_DIM
    m = bsz * seq
    nc, nb = seq // CMP_STRIDE, seq // SEL_LEN
    q_pad, kv, kvb, gt = _nsa_proj(x2d, wts["wq_pad"], wts["wkv"], wts["wg"], tm=512)
    w2k_t = jnp.pad(wts["w2k"].T, ((0, 128 - hd), (0, 0)))
    w2v_p = jnp.pad(wts["w2v"], ((0, 0), (0, 128 - hd)))
    kct, vc = _nsa_cmp(kvb.reshape(bsz, nc, CMP_STRIDE * kvb.shape[1]), kvb.shape[1] // (2 * g * hd),
                       wts["w1"], wts["pe_term"], w2k_t, w2v_p, transposed=True)
    kst, kwt, vs, vw = _nsa_prep(kv.reshape(bsz, seq, kv.shape[1]), nb)
    gates_g = jnp.pad(gt[:, :3 * g * r_heads].reshape(m, g, 3 * r_heads).transpose(1, 0, 2),
                      ((0, 0), (0, 0), (0, 128 - 3 * r_heads)))
    wtab, atab, poolt = tables
    o = _nsa_attn_prompt(q_pad, gates_g, kct, vc, kst, vs, kwt, vw, wtab, atab, poolt, bsz, seq)
    return o, kv


def kernel(x_prompt, x_sample, cache_nsa, state_nsa_win, state_pool, state_ffn, page_table, p_prompt, p_sample,
           rel_bias, nsa_w_in, nsa_cmp_pe, nsa_cmp_w1, nsa_cmp_b1, nsa_cmp_w2, nsa_w_out, pool_w, pool_b,
           pool_scale, ffn_w_up, ffn_conv_w, ffn_conv_b, ffn_w_down, ln_g, ln_b, ple_w_proj, ple_w_gate,
           ple_b_gate):
    bsz, seq, d = x_prompt.shape
    db, dt, _ = x_sample.shape
    depth = ffn_w_up.shape[0]
    alpha = (2.0 * depth) ** 0.25
    g, hd = N_KV_HEADS, HEAD_DIM
    gw = g * hd
    past_len = page_table.shape[1] * PAGE_SIZE
    mp, ms = bsz * seq, db * dt
    assert g * GQA_R * dt == 128 and past_len % SEL_LEN == 0 and seq % PADL == 0
    assert state_nsa_win.shape[2] == WINDOW and seq >= WINDOW and state_pool.shape[2] == POOL_HIST

    xp = x_prompt.reshape(mp, d)
    xs = x_sample.reshape(ms, d)
    tables_p = _prompt_tables(rel_bias, seq // SEL_LEN, seq // CMP_STRIDE)
    tables_s = _sample_tables(rel_bias, past_len, dt, past_len // CMP_STRIDE)
    nsa_rp, nsa_rs, nsa_wp, nsa_ws, pool_hp, pool_hs, ffn_hp, ffn_hs = ([] for _ in range(8))
    for i in range(depth):
        j = i // 2
        g0, b0, g1, b1 = ln_g[i, 0][None], ln_b[i, 0][None], ln_g[i, 1][None], ln_b[i, 1][None]
        if i % 2 == 0:
            wts = _nsa_weights(nsa_w_in[j], nsa_cmp_pe[j], nsa_cmp_w1[j], nsa_cmp_b1[j], nsa_cmp_w2[j])
            o_p, kv_p = _nsa_prompt(xp, bsz, seq, wts, tables_p)
            o_s, kv_s = _nsa_sample(xs, db, dt, cache_nsa[j].reshape(cache_nsa.shape[1], PAGE_SIZE, 4 * gw),
                                    page_table, state_nsa_win[j].reshape(db, WINDOW, 2 * gw), wts, tables_s)
            w_out = nsa_w_out[j].astype(BF16)
            xp = _linear_ln(o_p, w_out, xp, g0, b0, alpha, tm=512)
            xs = _linear_ln(o_s, w_out, xs, g0, b0, alpha, tm=ms)
            kv_p3 = kv_p.reshape(bsz, seq, 6, g, hd)
            kv_s3 = kv_s.reshape(db, dt, 6, g, hd)
            nsa_rp.append(kv_p3[:, :, :4])
            nsa_rs.append(kv_s3[:, :, :4])
            nsa_wp.append(kv_p3[:, seq - WINDOW:, 4:])
            nsa_ws.append(jnp.concatenate([state_nsa_win[j][:, dt:], kv_s3[:, :, 4:]], axis=1))
        else:
            xp3 = xp.reshape(bsz, seq, d)
            xe = jnp.concatenate([state_pool[j], xs.reshape(db, dt, d)], axis=1)
            pool_hp.append(xp3[:, seq - POOL_HIST:])
            pool_hs.append(xe[:, xe.shape[1] - POOL_HIST:])
            pw = pool_w[j].astype(BF16)
            xp = _pool_ln_prompt(xp3, pw, pool_b[j][None], pool_scale[j][None], g0, b0, alpha).reshape(mp, d)
            xs = _pool_ln_sample(xe, dt, past_len, pw, pool_b[j][None], pool_scale[j][None], g0, b0,
                                 alpha).reshape(ms, d)
        w_up = ffn_w_up[i].astype(BF16)
        w_down = ffn_w_down[i].astype(BF16)
        cw, cb = ffn_conv_w[i], ffn_conv_b[i][None]
        h_p = _ffn_up_prompt(xp, w_up, cw, cb, seq)
        tail = xp.reshape(bsz, seq, d)[:, seq - 8:].reshape(bsz * 8, d)
        u_tail = _matmul(tail, w_up, tm=bsz * 8).reshape(bsz, 8, -1)
        ffn_hp.append(u_tail[:, 8 - (CONV_W - 1):])
        u_s = _matmul(xs, w_up, tm=ms).reshape(db, dt, -1)
        ue = jnp.concatenate([state_ffn[i], u_s], axis=1)
        ffn_hs.append(ue[:, ue.shape[1] - (CONV_W - 1):])
        h_s = _ffn_gate_sample(ue, cw, cb, dt).reshape(ms, -1)
        ple = (ple_w_gate[i].astype(BF16), ple_b_gate[i][None], ple_w_proj[i].astype(BF16))
        xp = _ffn_down_ln_ple(h_p, w_down, xp, g1, b1, p_prompt[i].reshape(mp, -1), *ple, alpha, 512)
        xs = _ffn_down_ln_ple(h_s, w_down, xs, g1, b1, p_sample[i].reshape(ms, -1), *ple, alpha, ms)
    return (xp.reshape(bsz, seq, d), xs.reshape(db, dt, d),
            jnp.stack(nsa_rp), jnp.stack(nsa_rs), jnp.stack(nsa_wp), jnp.stack(nsa_ws),
            jnp.stack(pool_hp), jnp.stack(pool_hs), jnp.stack(ffn_hp), jnp.stack(ffn_hs))
```

```python
import functools
import math

import numpy as np
import jax
import jax.numpy as jnp
from jax import lax
from jax.experimental import pallas as pl
from jax.experimental.pallas import tpu as pltpu

N_KV_HEADS = 4
GQA_R = 4
HEAD_DIM = 64
CMP_LEN = 32
CMP_STRIDE = 16
CMP_HID = 256
CMP_PACK = 4
SEL_LEN = 64
CMP_PER_SEL = SEL_LEN // CMP_STRIDE
TOP_N = 16
WINDOW = 512
Q_BLOCK = 128
FORCE_SCORE = 1.0e4
REL_BUCKETS = 32
REL_MAX_DIST = 128
POOL_WINDOWS = (2, 4, 8, 16)
POOL_MAX = max(POOL_WINDOWS)
POOL_HIST = POOL_MAX - 1
CONV_W = 3
LN_EPS = 1e-5
PAGE_SIZE = 128

NEG = -1.0e30
NEG_VALID = -1.0e29
BF16 = jnp.bfloat16
F32 = jnp.float32

VMEM_LIMIT_BYTES = 56 * 1024 * 1024


def _cparams(n_grid):
    return pltpu.CompilerParams(dimension_semantics=("arbitrary",) * n_grid,
                                vmem_limit_bytes=VMEM_LIMIT_BYTES)


def _layer_norm(y, g, b):
    mu = jnp.mean(y, axis=-1, keepdims=True)
    yc = y - mu
    var = jnp.mean(yc * yc, axis=-1, keepdims=True)
    return yc * lax.rsqrt(var + LN_EPS) * g + b


FFN_TN = 256
HALO = 16


def _ffn_up_prompt_kernel(x_ref, xh_ref, w_ref, cw_ref, cb_ref, h_ref, xe_ref, ue_ref, *,
                          tm, d_ff, tiles_per_batch):
    i = pl.program_id(0)
    first = (i % tiles_per_batch) == 0
    halo = xh_ref[...]
    xe_ref[0:HALO, :] = jnp.where(first, jnp.zeros_like(halo), halo).astype(BF16)
    xe_ref[HALO:HALO + tm, :] = x_ref[...].astype(BF16)
    xe = xe_ref[...]
    for j in range(d_ff // FFN_TN):
        c_parts = []
        for half in range(2):
            c0 = half * d_ff + j * FFN_TN
            ue_ref[...] = jnp.dot(xe, w_ref[:, c0:c0 + FFN_TN], preferred_element_type=F32)
            cw = cw_ref[:, c0:c0 + FFN_TN]
            c = cb_ref[:, c0:c0 + FFN_TN]
            for k in range(CONV_W):
                off = HALO - (CONV_W - 1) + k
                c = c + cw[k:k + 1, :] * ue_ref[off:off + tm, :]
            c_parts.append(c)
        h_ref[:, j * FFN_TN:(j + 1) * FFN_TN] = (jax.nn.gelu(c_parts[1]) * c_parts[0]).astype(BF16)


def _ffn_up_prompt(x, w_up, conv_w, conv_b, seq, tm=512):
    m, d = x.shape
    d_ff = w_up.shape[1] // 2
    tpb = seq // tm
    kern = functools.partial(_ffn_up_prompt_kernel, tm=tm, d_ff=d_ff, tiles_per_batch=tpb)
    return pl.pallas_call(
        kern,
        grid=(m // tm,),
        in_specs=[
            pl.BlockSpec((tm, d), lambda i: (i, 0)),
            pl.BlockSpec((HALO, d), lambda i: (jnp.maximum(i * (tm // HALO) - 1, 0), 0)),
            pl.BlockSpec(w_up.shape, lambda i: (0, 0)),
            pl.BlockSpec(conv_w.shape, lambda i: (0, 0)),
            pl.BlockSpec(conv_b.shape, lambda i: (0, 0)),
        ],
        out_specs=pl.BlockSpec((tm, d_ff), lambda i: (i, 0)),
        out_shape=jax.ShapeDtypeStruct((m, d_ff), BF16),
        scratch_shapes=[pltpu.VMEM((HALO + tm, d), BF16), pltpu.VMEM((HALO + tm, FFN_TN), F32)],
        compiler_params=_cparams(1),
        name="ffn_up_prompt",
    )(x, x, w_up, conv_w, conv_b)


def _ffn_gate_sample_kernel(ue_ref, cw_ref, cb_ref, h_ref, *, t, d_ff):
    for j in range(d_ff // FFN_TN):
        c_parts = []
        for half in range(2):
            c0 = half * d_ff + j * FFN_TN
            cw = cw_ref[:, c0:c0 + FFN_TN]
            c = cb_ref[:, c0:c0 + FFN_TN][None]
            for k in range(CONV_W):
                c = c + cw[k:k + 1, :][None] * ue_ref[:, k:k + t, c0:c0 + FFN_TN]
            c_parts.append(c)
        h_ref[:, :, j * FFN_TN:(j + 1) * FFN_TN] = (jax.nn.gelu(c_parts[1]) * c_parts[0]).astype(BF16)


def _ffn_gate_sample(ue, conv_w, conv_b, t):
    b, te, n2 = ue.shape
    d_ff = n2 // 2
    bb = 8
    kern = functools.partial(_ffn_gate_sample_kernel, t=t, d_ff=d_ff)
    return pl.pallas_call(
        kern,
        grid=(b // bb,),
        in_specs=[
            pl.BlockSpec((bb, te, n2), lambda i: (i, 0, 0)),
            pl.BlockSpec(conv_w.shape, lambda i: (0, 0)),
            pl.BlockSpec(conv_b.shape, lambda i: (0, 0)),
        ],
        out_specs=pl.BlockSpec((bb, t, d_ff), lambda i: (i, 0, 0)),
        out_shape=jax.ShapeDtypeStruct((b, t, d_ff), BF16),
        compiler_params=_cparams(1),
        name="ffn_gate_sample",
    )(ue, conv_w, conv_b)


def _matmul_kernel(x_ref, w_ref, o_ref):
    o_ref[...] = jnp.dot(x_ref[...].astype(BF16), w_ref[...], preferred_element_type=F32).astype(o_ref.dtype)


def _matmul(x, w, tm, tn=None, out_dtype=F32):
    m, k = x.shape
    n = w.shape[1]
    tn = n if tn is None else tn
    return pl.pallas_call(
        _matmul_kernel,
        grid=(m // tm, n // tn),
        in_specs=[pl.BlockSpec((tm, k), lambda i, j: (i, 0)),
                  pl.BlockSpec((k, tn), lambda i, j: (0, j))],
        out_specs=pl.BlockSpec((tm, tn), lambda i, j: (i, j)),
        out_shape=jax.ShapeDtypeStruct((m, n), out_dtype),
        compiler_params=_cparams(2),
        name="matmul",
    )(x, w)


def _linear_ln_kernel(a_ref, w_ref, r_ref, g_ref, b_ref, o_ref, *, alpha):
    f = jnp.dot(a_ref[...].astype(BF16), w_ref[...], preferred_element_type=F32)
    o_ref[...] = _layer_norm(alpha * r_ref[...] + f, g_ref[...], b_ref[...])


def _linear_ln(a, w, resid, g, b, alpha, tm):
    m, k = a.shape
    d = w.shape[1]
    kern = functools.partial(_linear_ln_kernel, alpha=alpha)
    return pl.pallas_call(
        kern,
        grid=(m // tm,),
        in_specs=[pl.BlockSpec((tm, k), lambda i: (i, 0)),
                  pl.BlockSpec((k, d), lambda i: (0, 0)),
                  pl.BlockSpec((tm, d), lambda i: (i, 0)),
                  pl.BlockSpec((1, d), lambda i: (0, 0)),
                  pl.BlockSpec((1, d), lambda i: (0, 0))],
        out_specs=pl.BlockSpec((tm, d), lambda i: (i, 0)),
        out_shape=jax.ShapeDtypeStruct((m, d), F32),
        compiler_params=_cparams(1),
        name="linear_ln",
    )(a, w, resid, g, b)


def _ffn_down_kernel(h_ref, wd_ref, r_ref, g_ref, b_ref, p_ref, wg_ref, bg_ref, wp_ref, o_ref, *, alpha):
    f = jnp.dot(h_ref[...], wd_ref[...], preferred_element_type=F32)
    x2 = _layer_norm(alpha * r_ref[...] + f, g_ref[...], b_ref[...])
    gate = jax.nn.sigmoid(jnp.dot(x2.astype(BF16), wg_ref[...], preferred_element_type=F32) + bg_ref[...])
    inj = jnp.dot(p_ref[...].astype(BF16), wp_ref[...], preferred_element_type=F32)
    o_ref[...] = x2 + gate * inj


def _ffn_down_ln_ple(h, w_down, resid, g, b, p, w_gate, b_gate, w_proj, alpha, tm):
    m, k = h.shape
    d = w_down.shape[1]
    pd = p.shape[1]
    kern = functools.partial(_ffn_down_kernel, alpha=alpha)
    full = lambda shape: pl.BlockSpec(shape, lambda i: (0, 0))
    return pl.pallas_call(
        kern,
        grid=(m // tm,),
        in_specs=[pl.BlockSpec((tm, k), lambda i: (i, 0)), full((k, d)),
                  pl.BlockSpec((tm, d), lambda i: (i, 0)), full((1, d)), full((1, d)),
                  pl.BlockSpec((tm, pd), lambda i: (i, 0)), full((d, d)), full((1, d)), full((pd, d))],
        out_specs=pl.BlockSpec((tm, d), lambda i: (i, 0)),
        out_shape=jax.ShapeDtypeStruct((m, d), F32),
        compiler_params=_cparams(1),
        name="ffn_down_ln_ple",
    )(h, w_down, resid, g, b, p, w_gate, b_gate, w_proj)


def _pool_finish(x, diffs, w_ref, pb_ref, ps_ref, g_ref, b_ref, alpha):
    ys = [jnp.dot(dg.astype(BF16), w_ref[gi], preferred_element_type=F32) for gi, dg in enumerate(diffs)]
    y = jnp.concatenate(ys, axis=-1) + pb_ref[...]
    return _layer_norm(alpha * x + ps_ref[...] * y, g_ref[...], b_ref[...])


def _pool_prompt_kernel(x_ref, xh_ref, w_ref, pb_ref, ps_ref, g_ref, b_ref, o_ref, xe_ref, *,
                        tt, gw, alpha):
    i = pl.program_id(1)
    halo = xh_ref[0]
    xe_ref[0:POOL_MAX, :] = jnp.where(i == 0, jnp.zeros_like(halo), halo)
    xe_ref[POOL_MAX:POOL_MAX + tt, :] = x_ref[0]
    pos = i * tt + lax.broadcasted_iota(jnp.int32, (tt, 1), 0)
    x = x_ref[0]
    diffs = []
    for gi, win in enumerate(POOL_WINDOWS):
        lo = gi * gw
        tot = x[:, lo:lo + gw]
        for k in range(1, win):
            tot = tot + xe_ref[POOL_MAX - k:POOL_MAX - k + tt, lo:lo + gw]
        cnt = jnp.minimum(win, pos + 1).astype(F32)
        diffs.append(tot / cnt - x[:, lo:lo + gw])
    o_ref[0] = _pool_finish(x, diffs, w_ref, pb_ref, ps_ref, g_ref, b_ref, alpha)


def _pool_ln_prompt(x, w, pb, ps, g, b, alpha, tt=512):
    bsz, t, d = x.shape
    gw = d // len(POOL_WINDOWS)
    kern = functools.partial(_pool_prompt_kernel, tt=tt, gw=gw, alpha=alpha)
    vec = pl.BlockSpec((1, d), lambda bi, i: (0, 0))
    return pl.pallas_call(
        kern,
        grid=(bsz, t // tt),
        in_specs=[pl.BlockSpec((1, tt, d), lambda bi, i: (bi, i, 0)),
                  pl.BlockSpec((1, POOL_MAX, d), lambda bi, i: (bi, jnp.maximum(i * (tt // POOL_MAX) - 1, 0), 0)),
                  pl.BlockSpec(w.shape, lambda bi, i: (0, 0, 0)), vec, vec, vec, vec],
        out_specs=pl.BlockSpec((1, tt, d), lambda bi, i: (bi, i, 0)),
        out_shape=jax.ShapeDtypeStruct((bsz, t, d), F32),
        scratch_shapes=[pltpu.VMEM((POOL_MAX + tt, d), F32)],
        compiler_params=_cparams(2),
        name="pool_ln_prompt",
    )(x, x, w, pb, ps, g, b)


def _pool_sample_kernel(xe_ref, w_ref, pb_ref, ps_ref, g_ref, b_ref, o_ref, *, t, nh, gw, pos0, alpha):
    bb = xe_ref.shape[0]
    d = xe_ref.shape[2]
    x = xe_ref[:, nh:nh + t, :].reshape(bb * t, d)
    cnt_np = np.minimum(np.array(POOL_WINDOWS)[:, None], pos0 + np.arange(t)[None, :] + 1).astype(np.float32)
    diffs = []
    for gi, win in enumerate(POOL_WINDOWS):
        lo = gi * gw
        tot = xe_ref[:, nh:nh + t, lo:lo + gw]
        for k in range(1, win):
            tot = tot + xe_ref[:, nh - k:nh - k + t, lo:lo + gw]
        if np.all(cnt_np[gi] == win):
            mean = tot / float(win)
        else:
            t_idx = lax.broadcasted_iota(jnp.int32, (1, t, 1), 1)
            mean = tot / jnp.minimum(win, pos0 + t_idx + 1).astype(F32)
        diffs.append(mean.reshape(bb * t, gw) - x[:, lo:lo + gw])
    o_ref[...] = _pool_finish(x, diffs, w_ref, pb_ref, ps_ref, g_ref, b_ref, alpha).reshape(bb, t, d)


def _pool_ln_sample(xe, t, pos0, w, pb, ps, g, b, alpha, bb=8):
    bsz, te, d = xe.shape
    gw = d // len(POOL_WINDOWS)
    kern = functools.partial(_pool_sample_kernel, t=t, nh=te - t, gw=gw, pos0=pos0, alpha=alpha)
    vec = pl.BlockSpec((1, d), lambda i: (0, 0))
    return pl.pallas_call(
        kern,
        grid=(bsz // bb,),
        in_specs=[pl.BlockSpec((bb, te, d), lambda i: (i, 0, 0)),
                  pl.BlockSpec(w.shape, lambda i: (0, 0, 0)), vec, vec, vec, vec],
        out_specs=pl.BlockSpec((bb, t, d), lambda i: (i, 0, 0)),
        out_shape=jax.ShapeDtypeStruct((bsz, t, d), F32),
        compiler_params=_cparams(1),
        name="pool_ln_sample",
    )(xe, w, pb, ps, g, b)


def _nsa_proj_kernel(x_ref, wq_ref, wkv_ref, wg_ref, q_ref, kv_ref, kvb_ref, gt_ref):
    xb = x_ref[...].astype(BF16)
    q_ref[...] = jnp.dot(xb, wq_ref[...], preferred_element_type=F32).astype(BF16)
    kv = jnp.dot(xb, wkv_ref[...], preferred_element_type=F32)
    kv_ref[...] = kv
    kvb_ref[...] = kv.astype(BF16)
    gt_ref[...] = jax.nn.sigmoid(jnp.dot(xb, wg_ref[...], preferred_element_type=F32))


def _nsa_proj(x, wq, wkv, wg, tm):
    m, d = x.shape
    nq, nkv, ng = wq.shape[1], wkv.shape[1], wg.shape[1]
    full = lambda w: pl.BlockSpec(w.shape, lambda i: (0, 0))
    row = lambda n: pl.BlockSpec((tm, n), lambda i: (i, 0))
    return pl.pallas_call(
        _nsa_proj_kernel,
        grid=(m // tm,),
        in_specs=[row(d), full(wq), full(wkv), full(wg)],
        out_specs=[row(nq), row(nkv), row(nkv), row(ng)],
        out_shape=[jax.ShapeDtypeStruct((m, nq), BF16), jax.ShapeDtypeStruct((m, nkv), F32),
                   jax.ShapeDtypeStruct((m, nkv), BF16), jax.ShapeDtypeStruct((m, ng), F32)],
        compiler_params=_cparams(1),
        name="nsa_proj",
    )(x, wq, wkv, wg)


def _nsa_cmp_kernel(*refs, nc, transposed):
    ns = CMP_STRIDE
    x_refs = refs[:ns]
    w1_ref, pe_ref, w2k_ref, w2v_ref, ko_ref, vo_ref = refs[ns:]
    hd = HEAD_DIM
    gw = N_KV_HEADS * hd
    last = lax.broadcasted_iota(jnp.int32, (nc, CMP_HID), 0) == nc - 1
    nt = (((1,), (1,)), ((), ()))
    for c in range(2):
        for g in range(N_KV_HEADS):
            acc = jnp.zeros((nc, 2 * CMP_HID), F32)
            lo = c * gw + g * hd
            for k in range(ns // CMP_PACK):
                xs = jnp.concatenate([x_refs[k * CMP_PACK + i][0, :, lo:lo + hd] for i in range(CMP_PACK)], axis=1)
                acc = acc + jnp.dot(xs, w1_ref[c, k], preferred_element_type=F32)
            h1 = jnp.where(last, 0.0, pltpu.roll(acc[:, CMP_HID:], nc - 1, 0))
            hid = jax.nn.gelu(acc[:, :CMP_HID] + h1 + pe_ref[c:c + 1, :]).astype(BF16)
            if c == 0 and transposed:
                ko_ref[0, g] = lax.dot_general(w2k_ref[...], hid, nt, preferred_element_type=F32).astype(BF16)
            elif c == 0:
                ko_ref[0, g * hd:(g + 1) * hd, :] = lax.dot_general(w2k_ref[...], hid, nt,
                                                                    preferred_element_type=F32).astype(BF16)
            elif transposed:
                vo_ref[0, g] = jnp.dot(hid, w2v_ref[...], preferred_element_type=F32).astype(BF16)
            else:
                vo_ref[0, :, g * hd:(g + 1) * hd] = jnp.dot(hid, w2v_ref[...], preferred_element_type=F32).astype(BF16)


def _nsa_cmp(xv, col_block_stride, w1, pe_term, w2k, w2v, transposed):
    bsz, nc, _ = xv.shape
    g, hd = N_KV_HEADS, HEAD_DIM
    kern = functools.partial(_nsa_cmp_kernel, nc=nc, transposed=transposed)
    x_specs = [pl.BlockSpec((1, nc, 2 * g * hd), functools.partial(lambda b, s: (b, 0, s * col_block_stride), s=s))
               for s in range(CMP_STRIDE)]
    full = lambda w: pl.BlockSpec(w.shape, lambda b: (0,) * w.ndim)
    if transposed:
        out_specs = [pl.BlockSpec((1, g, 128, nc), lambda b: (b, 0, 0, 0)),
                     pl.BlockSpec((1, g, nc, 128), lambda b: (b, 0, 0, 0))]
        out_shape = [jax.ShapeDtypeStruct((bsz, g, 128, nc), BF16), jax.ShapeDtypeStruct((bsz, g, nc, 128), BF16)]
    else:
        out_specs = [pl.BlockSpec((1, g * hd, nc), lambda b: (b, 0, 0)), pl.BlockSpec((1, nc, g * hd), lambda b: (b, 0, 0))]
        out_shape = [jax.ShapeDtypeStruct((bsz, g * hd, nc), BF16), jax.ShapeDtypeStruct((bsz, nc, g * hd), BF16)]
    return pl.pallas_call(
        kern,
        grid=(bsz,),
        in_specs=x_specs + [full(w1), full(pe_term), full(w2k), full(w2v)],
        out_specs=out_specs,
        out_shape=out_shape,
        compiler_params=_cparams(1),
        name="nsa_cmp",
    )(*([xv] * CMP_STRIDE), w1, pe_term, w2k, w2v)


PADL = 512


def _nsa_prep_kernel(kv_ref, kst_ref, kwt_ref, vs_ref, vw_ref, *, nb):
    j = pl.program_id(1)
    tl = PADL
    hd = HEAD_DIM
    gw = N_KV_HEADS * hd

    @pl.when(j == 0)
    def _():
        for g in range(N_KV_HEADS):
            kst_ref[0, g] = (lax.broadcasted_iota(jnp.int32, (128 + nb, tl), 0) == hd).astype(BF16)
            kwt_ref[0, g] = (lax.broadcasted_iota(jnp.int32, (128, tl), 0) == hd).astype(BF16)
            vs_ref[0, g] = jnp.zeros((tl, 128), BF16)
            vw_ref[0, g] = jnp.zeros((tl, 128), BF16)

    @pl.when(j > 0)
    def _():
        kst = kv_ref[0, :, 2 * gw:3 * gw].T
        kwt = kv_ref[0, :, 4 * gw:5 * gw].T
        pos = (j - 1) * tl + lax.broadcasted_iota(jnp.int32, (nb, tl), 1)
        onehot = (pos // SEL_LEN == lax.broadcasted_iota(jnp.int32, (nb, tl), 0)).astype(BF16)
        zeros = jnp.zeros((128 - hd, tl), BF16)
        lane = lax.broadcasted_iota(jnp.int32, (tl, 128), 1)
        tail = (lane == hd).astype(F32)
        for g in range(N_KV_HEADS):
            kst_ref[0, g, 0:hd, :] = kst[g * hd:(g + 1) * hd].astype(BF16)
            kst_ref[0, g, hd:128, :] = zeros
            kst_ref[0, g, 128:128 + nb, :] = onehot
            kwt_ref[0, g, 0:hd, :] = kwt[g * hd:(g + 1) * hd].astype(BF16)
            kwt_ref[0, g, hd:128, :] = zeros
            for stream, ref in ((3, vs_ref), (5, vw_ref)):
                a = kv_ref[0, :, stream * gw + (g // 2) * 128:stream * gw + (g // 2 + 1) * 128]
                if g % 2:
                    a = pltpu.roll(a, hd, 1)
                ref[0, g] = jnp.where(lane < hd, a, tail).astype(BF16)


def _nsa_prep(kv, nb):
    bsz, t, n = kv.shape
    g = N_KV_HEADS
    tl = PADL
    tp = PADL + t
    kern = functools.partial(_nsa_prep_kernel, nb=nb)
    kspec = lambda rows: pl.BlockSpec((1, g, rows, tl), lambda b, j: (b, 0, 0, j))
    vspec = pl.BlockSpec((1, g, tl, 128), lambda b, j: (b, 0, j, 0))
    return pl.pallas_call(
        kern,
        grid=(bsz, tp // tl),
        in_specs=[pl.BlockSpec((1, tl, n), lambda b, j: (b, jnp.maximum(j - 1, 0), 0))],
        out_specs=[kspec(128 + nb), kspec(128), vspec, vspec],
        out_shape=[jax.ShapeDtypeStruct((bsz, g, 128 + nb, tp), BF16), jax.ShapeDtypeStruct((bsz, g, 128, tp), BF16),
                   jax.ShapeDtypeStruct((bsz, g, tp, 128), BF16), jax.ShapeDtypeStruct((bsz, g, tp, 128), BF16)],
        compiler_params=_cparams(2),
        name="nsa_prep",
    )(kv)


def _split3(x):
    a = x.astype(BF16)
    r = x - a.astype(F32)
    b = r.astype(BF16)
    return a, b, (r - b.astype(F32)).astype(BF16)


def _top_blocks(score, blk, n_sel, n_rows):
    sel = jnp.zeros(score.shape, jnp.bool_)
    for _ in range(n_sel):
        m = jnp.max(score, axis=0, keepdims=True)
        first = jnp.min(jnp.where(score == m, blk, n_rows), axis=0, keepdims=True)
        pick = blk == first
        sel = sel | pick
        score = jnp.where(pick, -3.0e38, score)
    return sel


def _nsa_attn_prompt_kernel(q_ref, gt_ref, kct_ref, vc_ref, kst_ref, vs_ref, kwt_ref, vw_ref,
                            wtab_ref, atab_ref, poolt_ref, o_ref, *, nb, nc):
    qi = pl.program_id(2)
    t0 = qi * Q_BLOCK
    r_heads = GQA_R
    hd = HEAD_DIM
    rows = r_heads * Q_BLOCK
    nw = WINDOW + Q_BLOCK
    lane = lax.broadcasted_iota(jnp.int32, (Q_BLOCK, 128), 1)
    q = jnp.concatenate([jnp.where(lane == hd, NEG, q_ref[:, r * 128:(r + 1) * 128]) for r in range(r_heads)],
                        axis=0)

    n_idx = lax.broadcasted_iota(jnp.int32, (48, nc), 1)
    j_idx = lax.broadcasted_iota(jnp.int32, (48, nc), 0)
    rel = n_idx - (qi * (Q_BLOCK // CMP_STRIDE) - 8)
    shift = (((j_idx < 32) & (rel == jnp.where(j_idx < 16, j_idx, j_idx - 16)))
             | ((j_idx == 32) & (rel >= 16))).astype(F32).astype(BF16)
    s_c = (jnp.dot(q, kct_ref[0, 0], preferred_element_type=F32)
           + jnp.dot(atab_ref[0], shift, preferred_element_type=F32))
    m_c = jnp.max(s_c, axis=1, keepdims=True)
    e_c = jnp.where(s_c > NEG_VALID, jnp.exp(s_c - m_c), 0.0)
    p_c = e_c / jnp.maximum(jnp.sum(e_c, axis=1, keepdims=True), 1e-30)
    o_c = jnp.dot(p_c.astype(BF16), vc_ref[0, 0], preferred_element_type=F32)

    grp = p_c[0:Q_BLOCK]
    for r in range(1, r_heads):
        grp = grp + p_c[r * Q_BLOCK:(r + 1) * Q_BLOCK]
    imp_t = jnp.zeros((nb, Q_BLOCK), F32)
    for part in _split3(grp):
        imp_t = imp_t + lax.dot_general(poolt_ref[...], part, (((1,), (1,)), ((), ())),
                                        preferred_element_type=F32)
    blk = lax.broadcasted_iota(jnp.int32, (nb, Q_BLOCK), 0)
    tq = t0 + lax.broadcasted_iota(jnp.int32, (nb, Q_BLOCK), 1)
    cur = tq // SEL_LEN
    forced = (blk == 0) | (blk == cur) | (blk == cur - 1)
    score = jnp.where(blk * SEL_LEN > tq, -1.0, jnp.where(forced, FORCE_SCORE, imp_t))
    sel = _top_blocks(score, blk, min(TOP_N, nb), nb)
    selb = jnp.where(sel, 0.0, NEG).T.astype(BF16)
    q_aug = jnp.concatenate([q, jnp.concatenate([selb] * r_heads, axis=0)], axis=1)

    c_near = pl.multiple_of(t0 + PADL - Q_BLOCK, Q_BLOCK)
    s = (jnp.dot(q_aug, kst_ref[0, 0, :, pl.ds(c_near, 2 * Q_BLOCK)], preferred_element_type=F32)
         + wtab_ref[0, :, nw - 2 * Q_BLOCK:nw])
    m_s = jnp.max(s, axis=1, keepdims=True)
    acc = jnp.dot(jnp.exp(s - m_s).astype(BF16), vs_ref[0, 0, pl.ds(c_near, 2 * Q_BLOCK), :],
                  preferred_element_type=F32)

    def far_step(c0, width, carry):
        m_old, acc_old = carry
        s_f = jnp.dot(q_aug, kst_ref[0, 0, :, pl.ds(c0, width)], preferred_element_type=F32)
        m_new = jnp.maximum(m_old, jnp.max(s_f, axis=1, keepdims=True))
        pv = jnp.dot(jnp.exp(s_f - m_new).astype(BF16), vs_ref[0, 0, pl.ds(c0, width), :],
                     preferred_element_type=F32)
        return m_new, jnp.exp(m_old - m_new) * acc_old + pv

    n_far = jnp.maximum(qi - 1, 0)
    n_wide = n_far // 4
    carry = lax.fori_loop(
        0, n_wide, lambda i, c: far_step(pl.multiple_of(PADL + i * 512, 512), 512, c), (m_s, acc))
    carry = lax.fori_loop(
        0, n_far - 4 * n_wide,
        lambda i, c: far_step(pl.multiple_of(PADL + n_wide * 512 + i * Q_BLOCK, Q_BLOCK), Q_BLOCK, c), carry)
    acc = carry[1]
    o_s = acc[:, :hd] / acc[:, hd:hd + 1]

    c_w = pl.multiple_of(t0 + PADL - WINDOW, Q_BLOCK)
    s_w = jnp.dot(q, kwt_ref[0, 0, :, pl.ds(c_w, nw)], preferred_element_type=F32) + wtab_ref[0]
    m_w = jnp.max(s_w, axis=1, keepdims=True)
    acc_w = jnp.dot(jnp.exp(s_w - m_w).astype(BF16), vw_ref[0, 0, pl.ds(c_w, nw), :], preferred_element_type=F32)
    o_w = acc_w[:, :hd] / acc_w[:, hd:hd + 1]

    gt = gt_ref[0]
    for r in range(r_heads):
        sl = slice(r * Q_BLOCK, (r + 1) * Q_BLOCK)
        o = (gt[:, 3 * r:3 * r + 1] * o_c[sl, :hd] + gt[:, 3 * r + 1:3 * r + 2] * o_s[sl]
             + gt[:, 3 * r + 2:3 * r + 3] * o_w[sl])
        o_ref[:, r * hd:(r + 1) * hd] = o.astype(BF16)


def _nsa_attn_prompt(q_pad, gates_g, kct, vc, kst, vs, kwt, vw, wtab, atab, poolt, bsz, seq):
    g, r_heads, hd = N_KV_HEADS, GQA_R, HEAD_DIM
    nq = seq // Q_BLOCK
    nb, nc = poolt.shape
    tp = PADL + seq
    kern = functools.partial(_nsa_attn_prompt_kernel, nb=nb, nc=nc)
    per_bg = lambda shape: pl.BlockSpec((1, 1) + shape, lambda b, gi, i: (b, gi, 0, 0))
    per_g = lambda shape: pl.BlockSpec((1,) + shape, lambda b, gi, i: (gi, 0, 0))
    return pl.pallas_call(
        kern,
        grid=(bsz, g, nq),
        in_specs=[pl.BlockSpec((Q_BLOCK, r_heads * 128), lambda b, gi, i: (b * nq + i, gi)),
                  pl.BlockSpec((1, Q_BLOCK, 128), lambda b, gi, i: (gi, b * nq + i, 0)),
                  per_bg((128, nc)), per_bg((nc, 128)),
                  per_bg((128 + nb, tp)), per_bg((tp, 128)),
                  per_bg((128, tp)), per_bg((tp, 128)),
                  per_g((r_heads * Q_BLOCK, WINDOW + Q_BLOCK)), per_g((r_heads * Q_BLOCK, 48)),
                  pl.BlockSpec((nb, nc), lambda b, gi, i: (0, 0))],
        out_specs=pl.BlockSpec((Q_BLOCK, r_heads * hd), lambda b, gi, i: (b * nq + i, gi)),
        out_shape=jax.ShapeDtypeStruct((bsz * seq, g * r_heads * hd), BF16),
        compiler_params=_cparams(3),
        name="nsa_attn_prompt",
    )(q_pad, gates_g, kct, vc, kst, vs, kwt, vw, wtab, atab, poolt)


def _rel_bucket_np(dist):
    n = np.maximum(dist, 0)
    exact = REL_BUCKETS // 2
    nf = np.maximum(n, 1).astype(np.float32)
    large = exact + (np.log(nf / np.float32(exact)) / np.float32(math.log(REL_MAX_DIST / exact))
                     * np.float32(REL_BUCKETS - exact)).astype(np.int32)
    return np.where(n < exact, n, np.minimum(large, REL_BUCKETS - 1))


FAR_DIST = Q_BLOCK - CMP_STRIDE + 1
assert np.all(_rel_bucket_np(np.arange(FAR_DIST, 1 << 16)) == REL_BUCKETS - 1)


def _bias_table(rel_bias, dist, valid):
    relc = rel_bias.astype(F32) - rel_bias[REL_BUCKETS - 1].astype(F32)[None]
    tab = relc[_rel_bucket_np(dist)]
    tab = jnp.where(valid[..., None], tab, NEG)
    return jnp.moveaxis(tab, -1, 0)


def _prompt_tables(rel_bias, nb, nc):
    g, r_heads = N_KV_HEADS, GQA_R
    qi = np.arange(Q_BLOCK)[:, None]
    nw = WINDOW + Q_BLOCK
    d_w = qi + WINDOW - np.arange(nw)[None, :]
    wtab = _bias_table(rel_bias, d_w, (d_w >= 0) & (d_w <= WINDOW)).reshape(g, r_heads * Q_BLOCK, nw)
    d_c = qi - CMP_STRIDE * (np.arange(16)[None, :] - 8) - (CMP_LEN - 1)
    valid_c = d_c >= 0
    a = _bias_table(rel_bias, d_c, valid_c).reshape(g, r_heads * Q_BLOCK, 16)
    hi = a.astype(BF16)
    lo = jnp.where(jnp.asarray(np.tile(valid_c, (r_heads, 1)))[None], a - hi.astype(F32), 0.0).astype(BF16)
    fut = jnp.full((g, r_heads * Q_BLOCK, 1), NEG, BF16)
    atab = jnp.concatenate([hi, lo, fut, jnp.zeros((g, r_heads * Q_BLOCK, 15), BF16)], axis=-1)
    jb = np.arange(nb)[:, None]
    n = np.arange(nc)[None, :]
    poolt = jnp.asarray(((n >= CMP_PER_SEL * jb - 1) & (n <= CMP_PER_SEL * jb + CMP_PER_SEL - 1)).astype(np.float32),
                        BF16)
    return wtab, atab, poolt


CMP_PAGES_PER_STEP = 4
SEL_PAGES_PER_STEP = 8


def _gather_cmp_kernel(pt_ref, *refs):
    del pt_ref
    page_refs = refs[:CMP_PAGES_PER_STEP]
    o_ref, t_ref = refs[CMP_PAGES_PER_STEP:]
    n_fb = t_ref.shape[0]
    for k in range(CMP_PAGES_PER_STEP):
        for fb in range(n_fb):
            t_ref[fb, k * PAGE_SIZE:(k + 1) * PAGE_SIZE, :] = page_refs[k][0, 0, fb * 128:(fb + 1) * 128, :].T
    rows = CMP_PAGES_PER_STEP * PAGE_SIZE // CMP_STRIDE
    for s in range(CMP_STRIDE):
        for fb in range(n_fb):
            c0 = (s * n_fb + fb) * 128
            o_ref[0, :, c0:c0 + 128] = t_ref[fb, pl.ds(s, rows, stride=CMP_STRIDE), :].astype(BF16)


def _gather_cmp(cache_t, layer, page_table):
    bsz, n_pages = page_table.shape
    half = cache_t.shape[2] // 2
    pp = CMP_PAGES_PER_STEP
    chunks = pp * PAGE_SIZE // CMP_STRIDE
    page_spec = lambda k: pl.BlockSpec((1, 1, half, PAGE_SIZE), lambda b, i, pt: (layer, pt[b, i * pp + k], 0, 0))
    return pl.pallas_call(
        _gather_cmp_kernel,
        grid_spec=pltpu.PrefetchScalarGridSpec(
            num_scalar_prefetch=1, grid=(bsz, n_pages // pp),
            in_specs=[page_spec(k) for k in range(pp)],
            out_specs=pl.BlockSpec((1, chunks, CMP_STRIDE * half), lambda b, i, pt: (b, i, 0)),
            scratch_shapes=[pltpu.VMEM((half // 128, pp * PAGE_SIZE, 128), F32)]),
        out_shape=jax.ShapeDtypeStruct((bsz, n_pages * PAGE_SIZE // CMP_STRIDE, CMP_STRIDE * half), BF16),
        compiler_params=_cparams(2),
        name="gather_cmp",
    )(page_table, *([cache_t] * pp))


def _nsa_attn_sample_kernel(pt_ref, q_ref, gt_ref, kct_ref, vc_ref, new_ref, win_ref,
                            tc_ref, tnear_ref, tnew_ref, tw_ref, pool_ref, e_ref, rsum_ref, *refs,
                            t, n_past_blk):
    del pt_ref
    pp = SEL_PAGES_PER_STEP
    k_refs, v_refs = refs[:pp], refs[pp:2 * pp]
    o_ref, m_ref, l_ref, acc_ref, selb_ref, oc_ref, ow_ref = refs[2 * pp:]
    g, r_heads, hd = N_KV_HEADS, GQA_R, HEAD_DIM
    gw = g * hd
    i = pl.program_id(1)
    n_steps = pl.num_programs(1)
    nt = (((1,), (1,)), ((), ()))
    q = q_ref[0]
    tile = pp * PAGE_SIZE

    @pl.when(i == 0)
    def _():
        new = new_ref[0]
        s_c = jnp.dot(q, kct_ref[0], preferred_element_type=F32) + tc_ref[...]
        m_c = jnp.max(s_c, axis=1, keepdims=True)
        e_c = jnp.where(s_c > NEG_VALID, jnp.exp(s_c - m_c), 0.0)
        p_c = e_c / jnp.maximum(jnp.sum(e_c, axis=1, keepdims=True), 1e-30)
        oc_ref[...] = jnp.dot(p_c.astype(BF16), vc_ref[0], preferred_element_type=F32)
        imp0 = jnp.zeros((q.shape[0], n_past_blk), F32)
        for part in _split3(p_c):
            imp0 = imp0 + jnp.dot(part, pool_ref[...], preferred_element_type=F32)
        imp = jnp.zeros((q.shape[0], n_past_blk), F32)
        for part in _split3(imp0):
            imp = imp + jnp.dot(rsum_ref[...], part, preferred_element_type=F32)
        blk = lax.broadcasted_iota(jnp.int32, (n_past_blk, q.shape[0]), 0)
        forced = (blk == 0) | (blk == n_past_blk - 1)
        sel = _top_blocks(jnp.where(forced, FORCE_SCORE, imp.T), blk, min(TOP_N - 1, n_past_blk), n_past_blk)
        selb_ref[...] = jnp.where(sel, 0.0, NEG).T.astype(BF16)
        win = win_ref[0, 0].astype(BF16)
        s_w = jnp.dot(q, win[:gw], preferred_element_type=F32) + tw_ref[...]
        s_wn = lax.dot_general(q, new[:, 4 * gw:5 * gw], nt, preferred_element_type=F32) + tnew_ref[...]
        m_w = jnp.maximum(jnp.max(s_w, axis=1, keepdims=True), jnp.max(s_wn, axis=1, keepdims=True))
        p_w = jnp.exp(s_w - m_w)
        p_wn = jnp.exp(s_wn - m_w)
        l_w = jnp.sum(p_w, axis=1, keepdims=True) + jnp.sum(p_wn, axis=1, keepdims=True)
        o_w = (lax.dot_general(p_w.astype(BF16), win[gw:], nt, preferred_element_type=F32)
               + jnp.dot(p_wn.astype(BF16), new[:, 5 * gw:6 * gw], preferred_element_type=F32))
        ow_ref[...] = o_w / l_w
        s_n = lax.dot_general(q, new[:, 2 * gw:3 * gw], nt, preferred_element_type=F32) + tnew_ref[...]
        m_n = jnp.max(s_n, axis=1, keepdims=True)
        p_n = jnp.exp(s_n - m_n)
        m_ref[...] = m_n
        l_ref[...] = jnp.sum(p_n, axis=1, keepdims=True)
        acc_ref[...] = jnp.dot(p_n.astype(BF16), new[:, 3 * gw:4 * gw], preferred_element_type=F32)

    k_t = jnp.concatenate([r[0, 0] for r in k_refs], axis=1).astype(BF16)
    v_t = jnp.concatenate([r[0, 0] for r in v_refs], axis=1).astype(BF16)
    s = (jnp.dot(q, k_t, preferred_element_type=F32)
         + jnp.dot(selb_ref[...], e_ref[:, pl.ds(pl.multiple_of(i * tile, tile), tile)], preferred_element_type=F32))
    s = s + jnp.where(i == n_steps - 1, tnear_ref[...], 0.0)
    m_old = m_ref[...]
    m_new = jnp.maximum(m_old, jnp.max(s, axis=1, keepdims=True))
    alpha = jnp.exp(m_old - m_new)
    p = jnp.exp(s - m_new)
    m_ref[...] = m_new
    l_ref[...] = alpha * l_ref[...] + jnp.sum(p, axis=1, keepdims=True)
    acc_ref[...] = alpha * acc_ref[...] + lax.dot_general(p.astype(BF16), v_t, nt, preferred_element_type=F32)

    @pl.when(i == n_steps - 1)
    def _():
        gt = gt_ref[0]
        o = gt[:, 0:1] * oc_ref[...] + gt[:, 1:2] * (acc_ref[...] / l_ref[...]) + gt[:, 2:3] * ow_ref[...]
        for gi in range(g):
            for r in range(r_heads):
                h = gi * r_heads + r
                o_ref[0, :, h * hd:(h + 1) * hd] = o[h * t:(h + 1) * t, gi * hd:(gi + 1) * hd].astype(BF16)


def _nsa_attn_sample(q_bd, gates, kct, vc, new_pad, win_t, layer, cache_t, page_table, tabs, t):
    bsz, n_pages = page_table.shape
    g, r_heads, hd = N_KV_HEADS, GQA_R, HEAD_DIM
    gw = g * hd
    lanes = g * r_heads * t
    pp = SEL_PAGES_PER_STEP
    n_past_blk = n_pages * PAGE_SIZE // SEL_LEN
    kern = functools.partial(_nsa_attn_sample_kernel, t=t, n_past_blk=n_past_blk)
    per_b = lambda a: pl.BlockSpec((1,) + a.shape[1:], lambda b, i, pt: (b, 0, 0))
    full = lambda a: pl.BlockSpec(a.shape, lambda b, i, pt: (0, 0))
    page_spec = lambda fblk, k: pl.BlockSpec((1, 1, gw, PAGE_SIZE),
                                             lambda b, i, pt: (layer, pt[b, i * pp + k], fblk, 0))
    return pl.pallas_call(
        kern,
        grid_spec=pltpu.PrefetchScalarGridSpec(
            num_scalar_prefetch=1, grid=(bsz, n_pages // pp),
            in_specs=[per_b(q_bd), per_b(gates), per_b(kct), per_b(vc), per_b(new_pad),
                      pl.BlockSpec((1, 1) + win_t.shape[2:], lambda b, i, pt: (layer, b, 0, 0))]
                     + [full(a) for a in tabs]
                     + [page_spec(2, k) for k in range(pp)] + [page_spec(3, k) for k in range(pp)],
            out_specs=pl.BlockSpec((1, t, g * r_heads * hd), lambda b, i, pt: (b, 0, 0)),
            scratch_shapes=[pltpu.VMEM((lanes, 1), F32), pltpu.VMEM((lanes, 1), F32), pltpu.VMEM((lanes, gw), F32),
                            pltpu.VMEM((lanes, n_past_blk), BF16), pltpu.VMEM((lanes, gw), F32),
                            pltpu.VMEM((lanes, gw), F32)]),
        out_shape=jax.ShapeDtypeStruct((bsz, t, g * r_heads * hd), BF16),
        compiler_params=_cparams(2),
        name="nsa_attn_sample",
    )(page_table, q_bd, gates, kct, vc, new_pad, win_t, *tabs, *([cache_t] * (2 * pp)))


def _sample_tables(rel_bias, pos0, t, nc):
    g, r_heads = N_KV_HEADS, GQA_R
    lanes = g * r_heads * t
    tq = pos0 + np.arange(t)[:, None]

    def table(kpos, extra_valid=True):
        dist = tq - kpos[None, :]
        tab = _bias_table(rel_bias, dist, (dist >= 0) & extra_valid)
        return tab.reshape(lanes, kpos.shape[0])

    tc = table(CMP_STRIDE * np.arange(nc) + CMP_LEN - 1)
    tnear = table(pos0 - Q_BLOCK + np.arange(Q_BLOCK))
    tnear = jnp.pad(tnear, ((0, 0), (SEL_PAGES_PER_STEP * PAGE_SIZE - Q_BLOCK, 0)))
    new_pos = pos0 + np.arange(128)
    tnew = table(new_pos, (new_pos < pos0 + t)[None, :])
    kw = pos0 - WINDOW + np.arange(WINDOW)
    tw = table(kw, (tq - kw[None, :]) <= WINDOW)
    n_past_blk = pos0 // SEL_LEN
    jb = np.arange(n_past_blk)[None, :]
    n = np.arange(nc)[:, None]
    pool = jnp.asarray(((n >= CMP_PER_SEL * jb - 1) & (n <= CMP_PER_SEL * jb + CMP_PER_SEL - 1)).astype(np.float32),
                       BF16)
    e = jnp.asarray((np.arange(pos0)[None, :] // SEL_LEN == np.arange(n_past_blk)[:, None]).astype(np.float32), BF16)
    li = np.arange(lanes)
    same = (li[:, None] // (r_heads * t) == li[None, :] // (r_heads * t)) & (li[:, None] % t == li[None, :] % t)
    rsum = jnp.asarray(same.astype(np.float32), BF16)
    return tc, tnear, tnew, tw, pool, e, rsum


def _nsa_sample(xs2d, bsz, t, layer, cache_t, page_table, win_t, wts, tabs):
    g, r_heads, hd = N_KV_HEADS, GQA_R, HEAD_DIM
    gw = g * hd
    q, kv, kvb, gt = _nsa_proj(xs2d, wts["wq"], wts["wkv"], wts["wg"], tm=xs2d.shape[0])
    chunks = _gather_cmp(cache_t, layer, page_table)
    kct, vc = _nsa_cmp(chunks, 1, wts["w1"], wts["pe_term"], wts["w2k"].T, wts["w2v"], transposed=False)
    q5 = q.reshape(bsz, t, g, r_heads, hd)
    q_bd = jnp.einsum('btgrd,gh->bgrthd', q5, jnp.eye(g, dtype=q.dtype)).reshape(bsz, g * r_heads * t, gw)
    gates = jnp.pad(gt[:, :3 * g * r_heads].reshape(bsz, t, g * r_heads, 3).transpose(0, 2, 1, 3)
                    .reshape(bsz, g * r_heads * t, 3), ((0, 0), (0, 0), (0, 125)))
    new_pad = jnp.pad(kvb.reshape(bsz, t, 6 * gw), ((0, 0), (0, 128 - t), (0, 0)))
    o = _nsa_attn_sample(q_bd, gates, kct, vc, new_pad, win_t, layer, cache_t, page_table, tabs, t)
    return o.reshape(bsz * t, g * r_heads * hd), kv


def _nsa_weights(w_in, cmp_pe, cmp_w1, cmp_b1, cmp_w2):
    d = w_in.shape[0]
    g, r_heads, hd = N_KV_HEADS, GQA_R, HEAD_DIM
    qd, kvd = g * r_heads * hd, 6 * g * hd
    wq = (w_in[:, :qd] * hd ** -0.5).astype(BF16)
    wq_pad = jnp.pad(wq.reshape(d, g * r_heads, hd), ((0, 0), (0, 0), (0, 128 - hd))).reshape(d, g * r_heads * 128)
    wkv = w_in[:, qd:qd + kvd].astype(BF16)
    wg = jnp.pad(w_in[:, qd + kvd:], ((0, 0), (0, 128 - 3 * g * r_heads))).astype(BF16)
    w1r = cmp_w1.reshape(2, CMP_LEN, hd, CMP_HID)
    w1 = jnp.concatenate([w1r[:, :CMP_STRIDE], w1r[:, CMP_STRIDE:]], axis=-1).astype(BF16)
    w1 = w1.reshape(2, CMP_STRIDE // CMP_PACK, CMP_PACK * hd, 2 * CMP_HID)
    pe_term = jnp.einsum('csd,csdh->ch', cmp_pe, w1r) + cmp_b1
    return dict(wq=wq, wq_pad=wq_pad, wkv=wkv, wg=wg, w1=w1, pe_term=pe_term,
                w2k=cmp_w2[0].astype(BF16), w2v=cmp_w2[1].astype(BF16))


def _nsa_prompt(x2d, bsz, seq, wts, tables):
    g, r_heads, hd = N_KV_HEADS, GQA_R, HEAD_DIM
    m = bsz * seq
    nc, nb = seq // CMP_STRIDE, seq // SEL_LEN
    q_pad, kv, kvb, gt = _nsa_proj(x2d, wts["wq_pad"], wts["wkv"], wts["wg"], tm=512)
    w2k_t = jnp.pad(wts["w2k"].T, ((0, 128 - hd), (0, 0)))
    w2v_p = jnp.pad(wts["w2v"], ((0, 0), (0, 128 - hd)))
    kct, vc = _nsa_cmp(kvb.reshape(bsz, nc, CMP_STRIDE * kvb.shape[1]), kvb.shape[1] // (2 * g * hd),
                       wts["w1"], wts["pe_term"], w2k_t, w2v_p, transposed=True)
    kst, kwt, vs, vw = _nsa_prep(kv.reshape(bsz, seq, kv.shape[1]), nb)
    gates_g = jnp.pad(gt[:, :3 * g * r_heads].reshape(m, g, 3 * r_heads).transpose(1, 0, 2),
                      ((0, 0), (0, 0), (0, 128 - 3 * r_heads)))
    wtab, atab, poolt = tables
    o = _nsa_attn_prompt(q_pad, gates_g, kct, vc, kst, vs, kwt, vw, wtab, atab, poolt, bsz, seq)
    return o, kv


def kernel(x_prompt, x_sample, cache_nsa, state_nsa_win, state_pool, state_ffn, page_table, p_prompt, p_sample,
           rel_bias, nsa_w_in, nsa_cmp_pe, nsa_cmp_w1, nsa_cmp_b1, nsa_cmp_w2, nsa_w_out, pool_w, pool_b,
           pool_scale, ffn_w_up, ffn_conv_w, ffn_conv_b, ffn_w_down, ln_g, ln_b, ple_w_proj, ple_w_gate,
           ple_b_gate):
    bsz, seq, d = x_prompt.shape
    db, dt, _ = x_sample.shape
    depth = ffn_w_up.shape[0]
    alpha = (2.0 * depth) ** 0.25
    g, hd = N_KV_HEADS, HEAD_DIM
    gw = g * hd
    past_len = page_table.shape[1] * PAGE_SIZE
    mp, ms = bsz * seq, db * dt
    assert g * GQA_R * dt == 128 and past_len % SEL_LEN == 0 and seq % PADL == 0
    assert page_table.shape[1] % SEL_PAGES_PER_STEP == 0 and page_table.shape[1] % CMP_PAGES_PER_STEP == 0
    assert state_nsa_win.shape[2] == WINDOW and seq >= WINDOW and state_pool.shape[2] == POOL_HIST

    xp = x_prompt.reshape(mp, d)
    xs = x_sample.reshape(ms, d)
    tables_p = _prompt_tables(rel_bias, seq // SEL_LEN, seq // CMP_STRIDE)
    tables_s = _sample_tables(rel_bias, past_len, dt, past_len // CMP_STRIDE)
    cache_t = jnp.transpose(cache_nsa, (0, 1, 3, 4, 5, 2)).reshape(cache_nsa.shape[0], cache_nsa.shape[1], 4 * gw,
                                                                   PAGE_SIZE)
    win_t = jnp.transpose(state_nsa_win, (0, 1, 3, 4, 5, 2)).reshape(state_nsa_win.shape[0], db, 2 * gw, WINDOW)
    nsa_rp, nsa_rs, nsa_wp, nsa_ws, pool_hp, pool_hs, ffn_hp, ffn_hs = ([] for _ in range(8))
    for i in range(depth):
        j = i // 2
        g0, b0, g1, b1 = ln_g[i, 0][None], ln_b[i, 0][None], ln_g[i, 1][None], ln_b[i, 1][None]
        if i % 2 == 0:
            wts = _nsa_weights(nsa_w_in[j], nsa_cmp_pe[j], nsa_cmp_w1[j], nsa_cmp_b1[j], nsa_cmp_w2[j])
            o_p, kv_p = _nsa_prompt(xp, bsz, seq, wts, tables_p)
            o_s, kv_s = _nsa_sample(xs, db, dt, j, cache_t, page_table, win_t, wts, tables_s)
            w_out = nsa_w_out[j].astype(BF16)
            xp = _linear_ln(o_p, w_out, xp, g0, b0, alpha, tm=512)
            xs = _linear_ln(o_s, w_out, xs, g0, b0, alpha, tm=ms)
            kv_p3 = kv_p.reshape(bsz, seq, 6, g, hd)
            kv_s3 = kv_s.reshape(db, dt, 6, g, hd)
            nsa_rp.append(kv_p3[:, :, :4])
            nsa_rs.append(kv_s3[:, :, :4])
            nsa_wp.append(kv_p3[:, seq - WINDOW:, 4:])
            nsa_ws.append(jnp.concatenate([state_nsa_win[j][:, dt:], kv_s3[:, :, 4:]], axis=1))
        else:
            xp3 = xp.reshape(bsz, seq, d)
            xe = jnp.concatenate([state_pool[j], xs.reshape(db, dt, d)], axis=1)
            pool_hp.append(xp3[:, seq - POOL_HIST:])
            pool_hs.append(xe[:, xe.shape[1] - POOL_HIST:])
            pw = pool_w[j].astype(BF16)
            xp = _pool_ln_prompt(xp3, pw, pool_b[j][None], pool_scale[j][None], g0, b0, alpha).reshape(mp, d)
            xs = _pool_ln_sample(xe, dt, past_len, pw, pool_b[j][None], pool_scale[j][None], g0, b0,
                                 alpha).reshape(ms, d)
        w_up = ffn_w_up[i].astype(BF16)
        w_down = ffn_w_down[i].astype(BF16)
        cw, cb = ffn_conv_w[i], ffn_conv_b[i][None]
        h_p = _ffn_up_prompt(xp, w_up, cw, cb, seq)
        tail = xp.reshape(bsz, seq, d)[:, seq - 8:].reshape(bsz * 8, d)
        u_tail = _matmul(tail, w_up, tm=bsz * 8).reshape(bsz, 8, -1)
        ffn_hp.append(u_tail[:, 8 - (CONV_W - 1):])
        u_s = _matmul(xs, w_up, tm=ms).reshape(db, dt, -1)
        ue = jnp.concatenate([state_ffn[i], u_s], axis=1)
        ffn_hs.append(ue[:, ue.shape[1] - (CONV_W - 1):])
        h_s = _ffn_gate_sample(ue, cw, cb, dt).reshape(ms, -1)
        ple = (ple_w_gate[i].astype(BF16), ple_b_gate[i][None], ple_w_proj[i].astype(BF16))
        xp = _ffn_down_ln_ple(h_p, w_down, xp, g1, b1, p_prompt[i].reshape(mp, -1), *ple, alpha, 512)
        xs = _ffn_down_ln_ple(h_s, w_down, xs, g1, b1, p_sample[i].reshape(ms, -1), *ple, alpha, ms)
    return (xp.reshape(bsz, seq, d), xs.reshape(db, dt, d),
            jnp.stack(nsa_rp), jnp.stack(nsa_rs), jnp.stack(nsa_wp), jnp.stack(nsa_ws),
            jnp.stack(pool_hp), jnp.stack(pool_hs), jnp.stack(ffn_hp), jnp.stack(ffn_hs))
```

```python
import functools
import math

import numpy as np
import jax
import jax.numpy as jnp
from jax import lax
from jax.experimental import pallas as pl
from jax.experimental.pallas import tpu as pltpu

N_KV_HEADS = 4
GQA_R = 4
HEAD_DIM = 64
CMP_LEN = 32
CMP_STRIDE = 16
CMP_HID = 256
CMP_PACK = 4
SEL_LEN = 64
CMP_PER_SEL = SEL_LEN // CMP_STRIDE
TOP_N = 16
WINDOW = 512
Q_BLOCK = 128
FORCE_SCORE = 1.0e4
REL_BUCKETS = 32
REL_MAX_DIST = 128
POOL_WINDOWS = (2, 4, 8, 16)
POOL_MAX = max(POOL_WINDOWS)
POOL_HIST = POOL_MAX - 1
CONV_W = 3
LN_EPS = 1e-5
PAGE_SIZE = 128

NEG = -1.0e30
NEG_VALID = -1.0e29
BF16 = jnp.bfloat16
F32 = jnp.float32

VMEM_LIMIT_BYTES = 56 * 1024 * 1024


def _cparams(n_grid):
    return pltpu.CompilerParams(dimension_semantics=("arbitrary",) * n_grid,
                                vmem_limit_bytes=VMEM_LIMIT_BYTES)


def _layer_norm(y, g, b):
    mu = jnp.mean(y, axis=-1, keepdims=True)
    yc = y - mu
    var = jnp.mean(yc * yc, axis=-1, keepdims=True)
    return yc * lax.rsqrt(var + LN_EPS) * g + b


FFN_TN = 256
HALO = 16


def _ffn_up_prompt_kernel(x_ref, xh_ref, w_ref, cw_ref, cb_ref, h_ref, xe_ref, ue_ref, *,
                          tm, d_ff, tiles_per_batch):
    i = pl.program_id(0)
    first = (i % tiles_per_batch) == 0
    halo = xh_ref[...]
    xe_ref[0:HALO, :] = jnp.where(first, jnp.zeros_like(halo), halo).astype(BF16)
    xe_ref[HALO:HALO + tm, :] = x_ref[...].astype(BF16)
    xe = xe_ref[...]
    for j in range(d_ff // FFN_TN):
        c_parts = []
        for half in range(2):
            c0 = half * d_ff + j * FFN_TN
            ue_ref[...] = jnp.dot(xe, w_ref[:, c0:c0 + FFN_TN], preferred_element_type=F32)
            cw = cw_ref[:, c0:c0 + FFN_TN]
            c = cb_ref[:, c0:c0 + FFN_TN]
            for k in range(CONV_W):
                off = HALO - (CONV_W - 1) + k
                c = c + cw[k:k + 1, :] * ue_ref[off:off + tm, :]
            c_parts.append(c)
        h_ref[:, j * FFN_TN:(j + 1) * FFN_TN] = (jax.nn.gelu(c_parts[1]) * c_parts[0]).astype(BF16)


def _ffn_up_prompt(x, w_up, conv_w, conv_b, seq, tm=512):
    m, d = x.shape
    d_ff = w_up.shape[1] // 2
    tpb = seq // tm
    kern = functools.partial(_ffn_up_prompt_kernel, tm=tm, d_ff=d_ff, tiles_per_batch=tpb)
    return pl.pallas_call(
        kern,
        grid=(m // tm,),
        in_specs=[
            pl.BlockSpec((tm, d), lambda i: (i, 0)),
            pl.BlockSpec((HALO, d), lambda i: (jnp.maximum(i * (tm // HALO) - 1, 0), 0)),
            pl.BlockSpec(w_up.shape, lambda i: (0, 0)),
            pl.BlockSpec(conv_w.shape, lambda i: (0, 0)),
            pl.BlockSpec(conv_b.shape, lambda i: (0, 0)),
        ],
        out_specs=pl.BlockSpec((tm, d_ff), lambda i: (i, 0)),
        out_shape=jax.ShapeDtypeStruct((m, d_ff), BF16),
        scratch_shapes=[pltpu.VMEM((HALO + tm, d), BF16), pltpu.VMEM((HALO + tm, FFN_TN), F32)],
        compiler_params=_cparams(1),
        name="ffn_up_prompt",
    )(x, x, w_up, conv_w, conv_b)


def _ffn_gate_sample_kernel(ue_ref, cw_ref, cb_ref, h_ref, *, t, d_ff):
    for j in range(d_ff // FFN_TN):
        c_parts = []
        for half in range(2):
            c0 = half * d_ff + j * FFN_TN
            cw = cw_ref[:, c0:c0 + FFN_TN]
            c = cb_ref[:, c0:c0 + FFN_TN][None]
            for k in range(CONV_W):
                c = c + cw[k:k + 1, :][None] * ue_ref[:, k:k + t, c0:c0 + FFN_TN]
            c_parts.append(c)
        h_ref[:, :, j * FFN_TN:(j + 1) * FFN_TN] = (jax.nn.gelu(c_parts[1]) * c_parts[0]).astype(BF16)


def _ffn_gate_sample(ue, conv_w, conv_b, t):
    b, te, n2 = ue.shape
    d_ff = n2 // 2
    bb = 8
    kern = functools.partial(_ffn_gate_sample_kernel, t=t, d_ff=d_ff)
    return pl.pallas_call(
        kern,
        grid=(b // bb,),
        in_specs=[
            pl.BlockSpec((bb, te, n2), lambda i: (i, 0, 0)),
            pl.BlockSpec(conv_w.shape, lambda i: (0, 0)),
            pl.BlockSpec(conv_b.shape, lambda i: (0, 0)),
        ],
        out_specs=pl.BlockSpec((bb, t, d_ff), lambda i: (i, 0, 0)),
        out_shape=jax.ShapeDtypeStruct((b, t, d_ff), BF16),
        compiler_params=_cparams(1),
        name="ffn_gate_sample",
    )(ue, conv_w, conv_b)


def _matmul_kernel(x_ref, w_ref, o_ref):
    o_ref[...] = jnp.dot(x_ref[...].astype(BF16), w_ref[...], preferred_element_type=F32).astype(o_ref.dtype)


def _matmul(x, w, tm, tn=None, out_dtype=F32):
    m, k = x.shape
    n = w.shape[1]
    tn = n if tn is None else tn
    return pl.pallas_call(
        _matmul_kernel,
        grid=(m // tm, n // tn),
        in_specs=[pl.BlockSpec((tm, k), lambda i, j: (i, 0)),
                  pl.BlockSpec((k, tn), lambda i, j: (0, j))],
        out_specs=pl.BlockSpec((tm, tn), lambda i, j: (i, j)),
        out_shape=jax.ShapeDtypeStruct((m, n), out_dtype),
        compiler_params=_cparams(2),
        name="matmul",
    )(x, w)


def _linear_ln_kernel(a_ref, w_ref, r_ref, g_ref, b_ref, o_ref, *, alpha):
    f = jnp.dot(a_ref[...].astype(BF16), w_ref[...], preferred_element_type=F32)
    o_ref[...] = _layer_norm(alpha * r_ref[...] + f, g_ref[...], b_ref[...])


def _linear_ln(a, w, resid, g, b, alpha, tm):
    m, k = a.shape
    d = w.shape[1]
    kern = functools.partial(_linear_ln_kernel, alpha=alpha)
    return pl.pallas_call(
        kern,
        grid=(m // tm,),
        in_specs=[pl.BlockSpec((tm, k), lambda i: (i, 0)),
                  pl.BlockSpec((k, d), lambda i: (0, 0)),
                  pl.BlockSpec((tm, d), lambda i: (i, 0)),
                  pl.BlockSpec((1, d), lambda i: (0, 0)),
                  pl.BlockSpec((1, d), lambda i: (0, 0))],
        out_specs=pl.BlockSpec((tm, d), lambda i: (i, 0)),
        out_shape=jax.ShapeDtypeStruct((m, d), F32),
        compiler_params=_cparams(1),
        name="linear_ln",
    )(a, w, resid, g, b)


def _ffn_down_kernel(h_ref, wd_ref, r_ref, g_ref, b_ref, p_ref, wg_ref, bg_ref, wp_ref, o_ref, *, alpha):
    f = jnp.dot(h_ref[...], wd_ref[...], preferred_element_type=F32)
    x2 = _layer_norm(alpha * r_ref[...] + f, g_ref[...], b_ref[...])
    gate = jax.nn.sigmoid(jnp.dot(x2.astype(BF16), wg_ref[...], preferred_element_type=F32) + bg_ref[...])
    inj = jnp.dot(p_ref[...].astype(BF16), wp_ref[...], preferred_element_type=F32)
    o_ref[...] = x2 + gate * inj


def _ffn_down_ln_ple(h, w_down, resid, g, b, p, w_gate, b_gate, w_proj, alpha, tm):
    m, k = h.shape
    d = w_down.shape[1]
    pd = p.shape[1]
    kern = functools.partial(_ffn_down_kernel, alpha=alpha)
    full = lambda shape: pl.BlockSpec(shape, lambda i: (0, 0))
    return pl.pallas_call(
        kern,
        grid=(m // tm,),
        in_specs=[pl.BlockSpec((tm, k), lambda i: (i, 0)), full((k, d)),
                  pl.BlockSpec((tm, d), lambda i: (i, 0)), full((1, d)), full((1, d)),
                  pl.BlockSpec((tm, pd), lambda i: (i, 0)), full((d, d)), full((1, d)), full((pd, d))],
        out_specs=pl.BlockSpec((tm, d), lambda i: (i, 0)),
        out_shape=jax.ShapeDtypeStruct((m, d), F32),
        compiler_params=_cparams(1),
        name="ffn_down_ln_ple",
    )(h, w_down, resid, g, b, p, w_gate, b_gate, w_proj)


def _pool_finish(x, diffs, w_ref, pb_ref, ps_ref, g_ref, b_ref, alpha):
    ys = [jnp.dot(dg.astype(BF16), w_ref[gi], preferred_element_type=F32) for gi, dg in enumerate(diffs)]
    y = jnp.concatenate(ys, axis=-1) + pb_ref[...]
    return _layer_norm(alpha * x + ps_ref[...] * y, g_ref[...], b_ref[...])


def _pool_prompt_kernel(x_ref, xh_ref, w_ref, pb_ref, ps_ref, g_ref, b_ref, o_ref, xe_ref, *,
                        tt, gw, alpha):
    i = pl.program_id(1)
    halo = xh_ref[0]
    xe_ref[0:POOL_MAX, :] = jnp.where(i == 0, jnp.zeros_like(halo), halo)
    xe_ref[POOL_MAX:POOL_MAX + tt, :] = x_ref[0]
    pos = i * tt + lax.broadcasted_iota(jnp.int32, (tt, 1), 0)
    x = x_ref[0]
    diffs = []
    for gi, win in enumerate(POOL_WINDOWS):
        lo = gi * gw
        tot = x[:, lo:lo + gw]
        for k in range(1, win):
            tot = tot + xe_ref[POOL_MAX - k:POOL_MAX - k + tt, lo:lo + gw]
        cnt = jnp.minimum(win, pos + 1).astype(F32)
        diffs.append(tot / cnt - x[:, lo:lo + gw])
    o_ref[0] = _pool_finish(x, diffs, w_ref, pb_ref, ps_ref, g_ref, b_ref, alpha)


def _pool_ln_prompt(x, w, pb, ps, g, b, alpha, tt=512):
    bsz, t, d = x.shape
    gw = d // len(POOL_WINDOWS)
    kern = functools.partial(_pool_prompt_kernel, tt=tt, gw=gw, alpha=alpha)
    vec = pl.BlockSpec((1, d), lambda bi, i: (0, 0))
    return pl.pallas_call(
        kern,
        grid=(bsz, t // tt),
        in_specs=[pl.BlockSpec((1, tt, d), lambda bi, i: (bi, i, 0)),
                  pl.BlockSpec((1, POOL_MAX, d), lambda bi, i: (bi, jnp.maximum(i * (tt // POOL_MAX) - 1, 0), 0)),
                  pl.BlockSpec(w.shape, lambda bi, i: (0, 0, 0)), vec, vec, vec, vec],
        out_specs=pl.BlockSpec((1, tt, d), lambda bi, i: (bi, i, 0)),
        out_shape=jax.ShapeDtypeStruct((bsz, t, d), F32),
        scratch_shapes=[pltpu.VMEM((POOL_MAX + tt, d), F32)],
        compiler_params=_cparams(2),
        name="pool_ln_prompt",
    )(x, x, w, pb, ps, g, b)


def _pool_sample_kernel(xe_ref, w_ref, pb_ref, ps_ref, g_ref, b_ref, o_ref, *, t, nh, gw, pos0, alpha):
    bb = xe_ref.shape[0]
    d = xe_ref.shape[2]
    x = xe_ref[:, nh:nh + t, :].reshape(bb * t, d)
    cnt_np = np.minimum(np.array(POOL_WINDOWS)[:, None], pos0 + np.arange(t)[None, :] + 1).astype(np.float32)
    diffs = []
    for gi, win in enumerate(POOL_WINDOWS):
        lo = gi * gw
        tot = xe_ref[:, nh:nh + t, lo:lo + gw]
        for k in range(1, win):
            tot = tot + xe_ref[:, nh - k:nh - k + t, lo:lo + gw]
        if np.all(cnt_np[gi] == win):
            mean = tot / float(win)
        else:
            t_idx = lax.broadcasted_iota(jnp.int32, (1, t, 1), 1)
            mean = tot / jnp.minimum(win, pos0 + t_idx + 1).astype(F32)
        diffs.append(mean.reshape(bb * t, gw) - x[:, lo:lo + gw])
    o_ref[...] = _pool_finish(x, diffs, w_ref, pb_ref, ps_ref, g_ref, b_ref, alpha).reshape(bb, t, d)


def _pool_ln_sample(xe, t, pos0, w, pb, ps, g, b, alpha, bb=8):
    bsz, te, d = xe.shape
    gw = d // len(POOL_WINDOWS)
    kern = functools.partial(_pool_sample_kernel, t=t, nh=te - t, gw=gw, pos0=pos0, alpha=alpha)
    vec = pl.BlockSpec((1, d), lambda i: (0, 0))
    return pl.pallas_call(
        kern,
        grid=(bsz // bb,),
        in_specs=[pl.BlockSpec((bb, te, d), lambda i: (i, 0, 0)),
                  pl.BlockSpec(w.shape, lambda i: (0, 0, 0)), vec, vec, vec, vec],
        out_specs=pl.BlockSpec((bb, t, d), lambda i: (i, 0, 0)),
        out_shape=jax.ShapeDtypeStruct((bsz, t, d), F32),
        compiler_params=_cparams(1),
        name="pool_ln_sample",
    )(xe, w, pb, ps, g, b)


def _nsa_proj_kernel(x_ref, wq_ref, wkv_ref, wg_ref, q_ref, kv_ref, kvb_ref, gt_ref):
    xb = x_ref[...].astype(BF16)
    q_ref[...] = jnp.dot(xb, wq_ref[...], preferred_element_type=F32).astype(BF16)
    kv = jnp.dot(xb, wkv_ref[...], preferred_element_type=F32)
    kv_ref[...] = kv
    kvb_ref[...] = kv.astype(BF16)
    gt_ref[...] = jax.nn.sigmoid(jnp.dot(xb, wg_ref[...], preferred_element_type=F32))


def _nsa_proj(x, wq, wkv, wg, tm):
    m, d = x.shape
    nq, nkv, ng = wq.shape[1], wkv.shape[1], wg.shape[1]
    full = lambda w: pl.BlockSpec(w.shape, lambda i: (0, 0))
    row = lambda n: pl.BlockSpec((tm, n), lambda i: (i, 0))
    return pl.pallas_call(
        _nsa_proj_kernel,
        grid=(m // tm,),
        in_specs=[row(d), full(wq), full(wkv), full(wg)],
        out_specs=[row(nq), row(nkv), row(nkv), row(ng)],
        out_shape=[jax.ShapeDtypeStruct((m, nq), BF16), jax.ShapeDtypeStruct((m, nkv), F32),
                   jax.ShapeDtypeStruct((m, nkv), BF16), jax.ShapeDtypeStruct((m, ng), F32)],
        compiler_params=_cparams(1),
        name="nsa_proj",
    )(x, wq, wkv, wg)


def _nsa_cmp_kernel(*refs, nc, transposed):
    ns = CMP_STRIDE
    x_refs = refs[:ns]
    w1_ref, pe_ref, w2k_ref, w2v_ref, ko_ref, vo_ref = refs[ns:]
    hd = HEAD_DIM
    gw = N_KV_HEADS * hd
    last = lax.broadcasted_iota(jnp.int32, (nc, CMP_HID), 0) == nc - 1
    nt = (((1,), (1,)), ((), ()))
    for c in range(2):
        for g in range(N_KV_HEADS):
            acc = jnp.zeros((nc, 2 * CMP_HID), F32)
            lo = c * gw + g * hd
            for k in range(ns // CMP_PACK):
                xs = jnp.concatenate([x_refs[k * CMP_PACK + i][0, :, lo:lo + hd] for i in range(CMP_PACK)], axis=1)
                acc = acc + jnp.dot(xs, w1_ref[c, k], preferred_element_type=F32)
            h1 = jnp.where(last, 0.0, pltpu.roll(acc[:, CMP_HID:], nc - 1, 0))
            hid = jax.nn.gelu(acc[:, :CMP_HID] + h1 + pe_ref[c:c + 1, :]).astype(BF16)
            if c == 0 and transposed:
                ko_ref[0, g] = lax.dot_general(w2k_ref[...], hid, nt, preferred_element_type=F32).astype(BF16)
            elif c == 0:
                ko_ref[0, g * hd:(g + 1) * hd, :] = lax.dot_general(w2k_ref[...], hid, nt,
                                                                    preferred_element_type=F32).astype(BF16)
            elif transposed:
                vo_ref[0, g] = jnp.dot(hid, w2v_ref[...], preferred_element_type=F32).astype(BF16)
            else:
                vo_ref[0, :, g * hd:(g + 1) * hd] = jnp.dot(hid, w2v_ref[...], preferred_element_type=F32).astype(BF16)


def _nsa_cmp(xv, col_block_stride, w1, pe_term, w2k, w2v, transposed):
    bsz, nc, _ = xv.shape
    g, hd = N_KV_HEADS, HEAD_DIM
    kern = functools.partial(_nsa_cmp_kernel, nc=nc, transposed=transposed)
    x_specs = [pl.BlockSpec((1, nc, 2 * g * hd), functools.partial(lambda b, s: (b, 0, s * col_block_stride), s=s))
               for s in range(CMP_STRIDE)]
    full = lambda w: pl.BlockSpec(w.shape, lambda b: (0,) * w.ndim)
    if transposed:
        out_specs = [pl.BlockSpec((1, g, 128, nc), lambda b: (b, 0, 0, 0)),
                     pl.BlockSpec((1, g, nc, 128), lambda b: (b, 0, 0, 0))]
        out_shape = [jax.ShapeDtypeStruct((bsz, g, 128, nc), BF16), jax.ShapeDtypeStruct((bsz, g, nc, 128), BF16)]
    else:
        out_specs = [pl.BlockSpec((1, g * hd, nc), lambda b: (b, 0, 0)), pl.BlockSpec((1, nc, g * hd), lambda b: (b, 0, 0))]
        out_shape = [jax.ShapeDtypeStruct((bsz, g * hd, nc), BF16), jax.ShapeDtypeStruct((bsz, nc, g * hd), BF16)]
    return pl.pallas_call(
        kern,
        grid=(bsz,),
        in_specs=x_specs + [full(w1), full(pe_term), full(w2k), full(w2v)],
        out_specs=out_specs,
        out_shape=out_shape,
        compiler_params=_cparams(1),
        name="nsa_cmp",
    )(*([xv] * CMP_STRIDE), w1, pe_term, w2k, w2v)


PADL = 512
FAR_TILE = 512


def _nsa_prep_kernel(kv_ref, kst_ref, kwt_ref, vs_ref, vw_ref, *, nb):
    j = pl.program_id(1)
    tl = PADL
    hd = HEAD_DIM
    gw = N_KV_HEADS * hd

    @pl.when(j == 0)
    def _():
        for g in range(N_KV_HEADS):
            kst_ref[0, g] = (lax.broadcasted_iota(jnp.int32, (128 + nb, tl), 0) == hd).astype(BF16)
            kwt_ref[0, g] = (lax.broadcasted_iota(jnp.int32, (128, tl), 0) == hd).astype(BF16)
            vs_ref[0, g] = jnp.zeros((tl, 128), BF16)
            vw_ref[0, g] = jnp.zeros((tl, 128), BF16)

    @pl.when(j > 0)
    def _():
        kst = kv_ref[0, :, 2 * gw:3 * gw].T
        kwt = kv_ref[0, :, 4 * gw:5 * gw].T
        pos = (j - 1) * tl + lax.broadcasted_iota(jnp.int32, (nb, tl), 1)
        onehot = (pos // SEL_LEN == lax.broadcasted_iota(jnp.int32, (nb, tl), 0)).astype(BF16)
        zeros = jnp.zeros((128 - hd, tl), BF16)
        lane = lax.broadcasted_iota(jnp.int32, (tl, 128), 1)
        tail = (lane == hd).astype(F32)
        for g in range(N_KV_HEADS):
            kst_ref[0, g, 0:hd, :] = kst[g * hd:(g + 1) * hd].astype(BF16)
            kst_ref[0, g, hd:128, :] = zeros
            kst_ref[0, g, 128:128 + nb, :] = onehot
            kwt_ref[0, g, 0:hd, :] = kwt[g * hd:(g + 1) * hd].astype(BF16)
            kwt_ref[0, g, hd:128, :] = zeros
            for stream, ref in ((3, vs_ref), (5, vw_ref)):
                a = kv_ref[0, :, stream * gw + (g // 2) * 128:stream * gw + (g // 2 + 1) * 128]
                if g % 2:
                    a = pltpu.roll(a, hd, 1)
                ref[0, g] = jnp.where(lane < hd, a, tail).astype(BF16)


def _nsa_prep(kv, nb):
    bsz, t, n = kv.shape
    g = N_KV_HEADS
    tl = PADL
    tp = PADL + t
    kern = functools.partial(_nsa_prep_kernel, nb=nb)
    kspec = lambda rows: pl.BlockSpec((1, g, rows, tl), lambda b, j: (b, 0, 0, j))
    vspec = pl.BlockSpec((1, g, tl, 128), lambda b, j: (b, 0, j, 0))
    return pl.pallas_call(
        kern,
        grid=(bsz, tp // tl),
        in_specs=[pl.BlockSpec((1, tl, n), lambda b, j: (b, jnp.maximum(j - 1, 0), 0))],
        out_specs=[kspec(128 + nb), kspec(128), vspec, vspec],
        out_shape=[jax.ShapeDtypeStruct((bsz, g, 128 + nb, tp), BF16), jax.ShapeDtypeStruct((bsz, g, 128, tp), BF16),
                   jax.ShapeDtypeStruct((bsz, g, tp, 128), BF16), jax.ShapeDtypeStruct((bsz, g, tp, 128), BF16)],
        compiler_params=_cparams(2),
        name="nsa_prep",
    )(kv)


def _split3(x):
    a = x.astype(BF16)
    r = x - a.astype(F32)
    b = r.astype(BF16)
    return a, b, (r - b.astype(F32)).astype(BF16)


def _top_blocks(score, blk, n_sel, n_rows):
    sel = jnp.zeros(score.shape, jnp.bool_)
    for _ in range(n_sel):
        m = jnp.max(score, axis=0, keepdims=True)
        first = jnp.min(jnp.where(score == m, blk, n_rows), axis=0, keepdims=True)
        pick = blk == first
        sel = sel | pick
        score = jnp.where(pick, -3.0e38, score)
    return sel


def _nsa_attn_prompt_kernel(q_ref, gt_ref, kct_ref, vc_ref, kst_ref, vs_ref, kwt_ref, vw_ref,
                            wtab_ref, stab_ref, atab_ref, poolt_ref, o_ref,
                            sw_ref, sa_ref, sb_ref, pa_ref, pb_ref, ala_ref, alb_ref, m_ref, acc_ref, *, nb, nc):
    qi = pl.program_id(2)
    t0 = qi * Q_BLOCK
    r_heads = GQA_R
    hd = HEAD_DIM
    rows = r_heads * Q_BLOCK
    nw = WINDOW + Q_BLOCK
    lane = lax.broadcasted_iota(jnp.int32, (Q_BLOCK, 128), 1)
    q = jnp.concatenate([jnp.where(lane == hd, NEG, q_ref[:, r * 128:(r + 1) * 128]) for r in range(r_heads)],
                        axis=0)

    n_idx = lax.broadcasted_iota(jnp.int32, (48, nc), 1)
    j_idx = lax.broadcasted_iota(jnp.int32, (48, nc), 0)
    rel = n_idx - (qi * (Q_BLOCK // CMP_STRIDE) - 8)
    shift = (((j_idx < 32) & (rel == jnp.where(j_idx < 16, j_idx, j_idx - 16)))
             | ((j_idx == 32) & (rel >= 16))).astype(F32).astype(BF16)
    s_c = (jnp.dot(q, kct_ref[0, 0], preferred_element_type=F32)
           + jnp.dot(atab_ref[0], shift, preferred_element_type=F32))
    m_c = jnp.max(s_c, axis=1, keepdims=True)
    e_c = jnp.where(s_c > NEG_VALID, jnp.exp(s_c - m_c), 0.0)
    p_c = e_c / jnp.maximum(jnp.sum(e_c, axis=1, keepdims=True), 1e-30)
    o_c = jnp.dot(p_c.astype(BF16), vc_ref[0, 0], preferred_element_type=F32)

    grp = p_c[0:Q_BLOCK]
    for r in range(1, r_heads):
        grp = grp + p_c[r * Q_BLOCK:(r + 1) * Q_BLOCK]
    imp_t = jnp.zeros((nb, Q_BLOCK), F32)
    for part in _split3(grp):
        imp_t = imp_t + lax.dot_general(poolt_ref[...], part, (((1,), (1,)), ((), ())),
                                        preferred_element_type=F32)
    blk = lax.broadcasted_iota(jnp.int32, (nb, Q_BLOCK), 0)
    tq = t0 + lax.broadcasted_iota(jnp.int32, (nb, Q_BLOCK), 1)
    cur = tq // SEL_LEN
    forced = (blk == 0) | (blk == cur) | (blk == cur - 1)
    score = jnp.where(blk * SEL_LEN > tq, -1.0, jnp.where(forced, -3.0e38, imp_t))
    sel = forced | _top_blocks(score, blk, min(TOP_N, nb) - 3, nb)
    selb = jnp.where(sel, 0.0, NEG).T.astype(BF16)
    q_aug = jnp.concatenate([q, jnp.concatenate([selb] * r_heads, axis=0)], axis=1)

    c_w = pl.multiple_of(t0 + PADL - WINDOW, Q_BLOCK)
    sw_ref[...] = jnp.dot(q, kwt_ref[0, 0, :, pl.ds(c_w, nw)], preferred_element_type=F32) + wtab_ref[0]

    s = jnp.dot(q_aug, kst_ref[0, 0, :, pl.ds(c_w, nw)], preferred_element_type=F32) + stab_ref[0]
    m_s = jnp.max(s, axis=1, keepdims=True)
    m_ref[...] = m_s
    acc_ref[...] = jnp.dot(jnp.exp(s - m_s).astype(BF16), vs_ref[0, 0, pl.ds(c_w, nw), :],
                           preferred_element_type=F32)

    n_far = jnp.maximum((t0 + PADL - WINDOW + FAR_TILE - 1) // FAR_TILE - PADL // FAR_TILE, 0)

    def tile_start(k):
        return pl.multiple_of(c_w - FAR_TILE * (jnp.clip(k, 0, n_far - 1) + 1), Q_BLOCK)

    def scores(k, s_out):
        s_out[...] = jnp.dot(q_aug, kst_ref[0, 0, :, pl.ds(tile_start(k), FAR_TILE)], preferred_element_type=F32)

    def pv_update(k, p_in, al_in):
        acc_ref[...] = al_in[...] * acc_ref[...] + jnp.dot(p_in[...], vs_ref[0, 0, pl.ds(tile_start(k), FAR_TILE), :],
                                                          preferred_element_type=F32)

    def stage(k, s_cur, s_nxt, p_cur, al_cur, p_prev, al_prev):
        scores(k + 1, s_nxt)
        pv_update(k - 1, p_prev, al_prev)
        m_old = m_ref[...]
        m_new = jnp.maximum(m_old, jnp.max(s_cur[...], axis=1, keepdims=True))
        al_cur[...] = jnp.exp(m_old - m_new)
        p_cur[...] = jnp.exp(s_cur[...] - m_new).astype(BF16)
        m_ref[...] = m_new

    @pl.when(n_far > 0)
    def _():
        scores(0, sa_ref)
        pb_ref[...] = jnp.zeros_like(pb_ref)
        alb_ref[...] = jnp.ones_like(alb_ref)

        def pair(j, _):
            stage(2 * j, sa_ref, sb_ref, pa_ref, ala_ref, pb_ref, alb_ref)

            @pl.when(2 * j + 1 < n_far)
            def _():
                stage(2 * j + 1, sb_ref, sa_ref, pb_ref, alb_ref, pa_ref, ala_ref)
            return 0

        lax.fori_loop(0, (n_far + 1) // 2, pair, 0)

        @pl.when(n_far % 2 == 1)
        def _():
            pv_update(n_far - 1, pa_ref, ala_ref)

        @pl.when(n_far % 2 == 0)
        def _():
            pv_update(n_far - 1, pb_ref, alb_ref)

    acc = acc_ref[...]
    o_s = acc[:, :hd] / acc[:, hd:hd + 1]

    s_w = sw_ref[...]
    m_w = jnp.max(s_w, axis=1, keepdims=True)
    acc_w = jnp.dot(jnp.exp(s_w - m_w).astype(BF16), vw_ref[0, 0, pl.ds(c_w, nw), :], preferred_element_type=F32)
    o_w = acc_w[:, :hd] / acc_w[:, hd:hd + 1]

    gt = gt_ref[0]
    for r in range(r_heads):
        sl = slice(r * Q_BLOCK, (r + 1) * Q_BLOCK)
        o = (gt[:, 3 * r:3 * r + 1] * o_c[sl, :hd] + gt[:, 3 * r + 1:3 * r + 2] * o_s[sl]
             + gt[:, 3 * r + 2:3 * r + 3] * o_w[sl])
        o_ref[:, r * hd:(r + 1) * hd] = o.astype(BF16)


def _nsa_attn_prompt(q_pad, gates_g, kct, vc, kst, vs, kwt, vw, wtab, stab, atab, poolt, bsz, seq):
    g, r_heads, hd = N_KV_HEADS, GQA_R, HEAD_DIM
    nq = seq // Q_BLOCK
    nb, nc = poolt.shape
    tp = PADL + seq
    kern = functools.partial(_nsa_attn_prompt_kernel, nb=nb, nc=nc)
    rows = r_heads * Q_BLOCK
    per_bg = lambda shape: pl.BlockSpec((1, 1) + shape, lambda b, gi, i: (b, gi, 0, 0))
    per_g = lambda shape: pl.BlockSpec((1,) + shape, lambda b, gi, i: (gi, 0, 0))
    return pl.pallas_call(
        kern,
        grid=(bsz, g, nq),
        in_specs=[pl.BlockSpec((Q_BLOCK, r_heads * 128), lambda b, gi, i: (b * nq + i, gi)),
                  pl.BlockSpec((1, Q_BLOCK, 128), lambda b, gi, i: (gi, b * nq + i, 0)),
                  per_bg((128, nc)), per_bg((nc, 128)),
                  per_bg((128 + nb, tp)), per_bg((tp, 128)),
                  per_bg((128, tp)), per_bg((tp, 128)),
                  per_g((r_heads * Q_BLOCK, WINDOW + Q_BLOCK)), per_g((r_heads * Q_BLOCK, WINDOW + Q_BLOCK)),
                  per_g((r_heads * Q_BLOCK, 48)),
                  pl.BlockSpec((nb, nc), lambda b, gi, i: (0, 0))],
        out_specs=pl.BlockSpec((Q_BLOCK, r_heads * hd), lambda b, gi, i: (b * nq + i, gi)),
        out_shape=jax.ShapeDtypeStruct((bsz * seq, g * r_heads * hd), BF16),
        scratch_shapes=[pltpu.VMEM((rows, WINDOW + Q_BLOCK), F32),
                        pltpu.VMEM((rows, FAR_TILE), F32), pltpu.VMEM((rows, FAR_TILE), F32),
                        pltpu.VMEM((rows, FAR_TILE), BF16), pltpu.VMEM((rows, FAR_TILE), BF16),
                        pltpu.VMEM((rows, 1), F32), pltpu.VMEM((rows, 1), F32),
                        pltpu.VMEM((rows, 1), F32), pltpu.VMEM((rows, 128), F32)],
        compiler_params=_cparams(3),
        name="nsa_attn_prompt",
    )(q_pad, gates_g, kct, vc, kst, vs, kwt, vw, wtab, stab, atab, poolt)


def _rel_bucket_np(dist):
    n = np.maximum(dist, 0)
    exact = REL_BUCKETS // 2
    nf = np.maximum(n, 1).astype(np.float32)
    large = exact + (np.log(nf / np.float32(exact)) / np.float32(math.log(REL_MAX_DIST / exact))
                     * np.float32(REL_BUCKETS - exact)).astype(np.int32)
    return np.where(n < exact, n, np.minimum(large, REL_BUCKETS - 1))


FAR_DIST = Q_BLOCK - CMP_STRIDE + 1
assert np.all(_rel_bucket_np(np.arange(FAR_DIST, 1 << 16)) == REL_BUCKETS - 1)


def _bias_table(rel_bias, dist, valid):
    relc = rel_bias.astype(F32) - rel_bias[REL_BUCKETS - 1].astype(F32)[None]
    tab = relc[_rel_bucket_np(dist)]
    tab = jnp.where(valid[..., None], tab, NEG)
    return jnp.moveaxis(tab, -1, 0)


def _prompt_tables(rel_bias, nb, nc):
    g, r_heads = N_KV_HEADS, GQA_R
    qi = np.arange(Q_BLOCK)[:, None]
    nw = WINDOW + Q_BLOCK
    d_w = qi + WINDOW - np.arange(nw)[None, :]
    wtab = _bias_table(rel_bias, d_w, (d_w >= 0) & (d_w <= WINDOW)).reshape(g, r_heads * Q_BLOCK, nw)
    stab = _bias_table(rel_bias, d_w, d_w >= 0).reshape(g, r_heads * Q_BLOCK, nw)
    d_c = qi - CMP_STRIDE * (np.arange(16)[None, :] - 8) - (CMP_LEN - 1)
    valid_c = d_c >= 0
    a = _bias_table(rel_bias, d_c, valid_c).reshape(g, r_heads * Q_BLOCK, 16)
    hi = a.astype(BF16)
    lo = jnp.where(jnp.asarray(np.tile(valid_c, (r_heads, 1)))[None], a - hi.astype(F32), 0.0).astype(BF16)
    fut = jnp.full((g, r_heads * Q_BLOCK, 1), NEG, BF16)
    atab = jnp.concatenate([hi, lo, fut, jnp.zeros((g, r_heads * Q_BLOCK, 15), BF16)], axis=-1)
    jb = np.arange(nb)[:, None]
    n = np.arange(nc)[None, :]
    poolt = jnp.asarray(((n >= CMP_PER_SEL * jb - 1) & (n <= CMP_PER_SEL * jb + CMP_PER_SEL - 1)).astype(np.float32),
                        BF16)
    return wtab, stab, atab, poolt


CMP_PAGES_PER_STEP = 8
SEL_PAGES_PER_STEP = 8


def _gather_cmp_kernel(pt_ref, *refs):
    del pt_ref
    page_refs = refs[:CMP_PAGES_PER_STEP]
    o_ref, t_ref = refs[CMP_PAGES_PER_STEP:]
    n_fb = t_ref.shape[0]
    for k in range(CMP_PAGES_PER_STEP):
        for fb in range(n_fb):
            t_ref[fb, k * PAGE_SIZE:(k + 1) * PAGE_SIZE, :] = page_refs[k][0, 0, fb * 128:(fb + 1) * 128, :].T
    rows = CMP_PAGES_PER_STEP * PAGE_SIZE // CMP_STRIDE
    for s in range(CMP_STRIDE):
        for fb in range(n_fb):
            c0 = (s * n_fb + fb) * 128
            o_ref[0, :, c0:c0 + 128] = t_ref[fb, pl.ds(s, rows, stride=CMP_STRIDE), :].astype(BF16)


def _gather_cmp(cache_t, layer, page_table):
    bsz, n_pages = page_table.shape
    half = cache_t.shape[2] // 2
    pp = CMP_PAGES_PER_STEP
    chunks = pp * PAGE_SIZE // CMP_STRIDE
    page_spec = lambda k: pl.BlockSpec((1, 1, half, PAGE_SIZE), lambda b, i, pt: (layer, pt[b, i * pp + k], 0, 0))
    return pl.pallas_call(
        _gather_cmp_kernel,
        grid_spec=pltpu.PrefetchScalarGridSpec(
            num_scalar_prefetch=1, grid=(bsz, n_pages // pp),
            in_specs=[page_spec(k) for k in range(pp)],
            out_specs=pl.BlockSpec((1, chunks, CMP_STRIDE * half), lambda b, i, pt: (b, i, 0)),
            scratch_shapes=[pltpu.VMEM((half // 128, pp * PAGE_SIZE, 128), F32)]),
        out_shape=jax.ShapeDtypeStruct((bsz, n_pages * PAGE_SIZE // CMP_STRIDE, CMP_STRIDE * half), BF16),
        compiler_params=_cparams(2),
        name="gather_cmp",
    )(page_table, *([cache_t] * pp))


def _nsa_attn_sample_kernel(pt_ref, q_ref, gt_ref, kct_ref, vc_ref, new_ref, win_ref,
                            tc_ref, tnear_ref, tnew_ref, tw_ref, pool_ref, e_ref, rsum_ref, *refs,
                            t, n_past_blk):
    del pt_ref
    pp = SEL_PAGES_PER_STEP
    k_refs, v_refs = refs[:pp], refs[pp:2 * pp]
    o_ref, m_ref, l_ref, acc_ref, selb_ref, oc_ref, ow_ref = refs[2 * pp:]
    g, r_heads, hd = N_KV_HEADS, GQA_R, HEAD_DIM
    gw = g * hd
    i = pl.program_id(1)
    n_steps = pl.num_programs(1)
    nt = (((1,), (1,)), ((), ()))
    q = q_ref[0]
    tile = pp * PAGE_SIZE

    @pl.when(i == 0)
    def _():
        new = new_ref[0]
        s_c = jnp.dot(q, kct_ref[0], preferred_element_type=F32) + tc_ref[...]
        m_c = jnp.max(s_c, axis=1, keepdims=True)
        e_c = jnp.where(s_c > NEG_VALID, jnp.exp(s_c - m_c), 0.0)
        p_c = e_c / jnp.maximum(jnp.sum(e_c, axis=1, keepdims=True), 1e-30)
        oc_ref[...] = jnp.dot(p_c.astype(BF16), vc_ref[0], preferred_element_type=F32)
        imp0 = jnp.zeros((q.shape[0], n_past_blk), F32)
        for part in _split3(p_c):
            imp0 = imp0 + jnp.dot(part, pool_ref[...], preferred_element_type=F32)
        imp = jnp.zeros((q.shape[0], n_past_blk), F32)
        for part in _split3(imp0):
            imp = imp + jnp.dot(rsum_ref[...], part, preferred_element_type=F32)
        blk = lax.broadcasted_iota(jnp.int32, (n_past_blk, q.shape[0]), 0)
        forced = (blk == 0) | (blk == n_past_blk - 1)
        sel = _top_blocks(jnp.where(forced, FORCE_SCORE, imp.T), blk, min(TOP_N - 1, n_past_blk), n_past_blk)
        selb_ref[...] = jnp.where(sel, 0.0, NEG).T.astype(BF16)
        win = win_ref[0, 0].astype(BF16)
        s_w = jnp.dot(q, win[:gw], preferred_element_type=F32) + tw_ref[...]
        s_wn = lax.dot_general(q, new[:, 4 * gw:5 * gw], nt, preferred_element_type=F32) + tnew_ref[...]
        m_w = jnp.maximum(jnp.max(s_w, axis=1, keepdims=True), jnp.max(s_wn, axis=1, keepdims=True))
        p_w = jnp.exp(s_w - m_w)
        p_wn = jnp.exp(s_wn - m_w)
        l_w = jnp.sum(p_w, axis=1, keepdims=True) + jnp.sum(p_wn, axis=1, keepdims=True)
        o_w = (lax.dot_general(p_w.astype(BF16), win[gw:], nt, preferred_element_type=F32)
               + jnp.dot(p_wn.astype(BF16), new[:, 5 * gw:6 * gw], preferred_element_type=F32))
        ow_ref[...] = o_w / l_w
        s_n = lax.dot_general(q, new[:, 2 * gw:3 * gw], nt, preferred_element_type=F32) + tnew_ref[...]
        m_n = jnp.max(s_n, axis=1, keepdims=True)
        p_n = jnp.exp(s_n - m_n)
        m_ref[...] = m_n
        l_ref[...] = jnp.sum(p_n, axis=1, keepdims=True)
        acc_ref[...] = jnp.dot(p_n.astype(BF16), new[:, 3 * gw:4 * gw], preferred_element_type=F32)

    k_t = jnp.concatenate([r[0, 0] for r in k_refs], axis=1).astype(BF16)
    v_t = jnp.concatenate([r[0, 0] for r in v_refs], axis=1).astype(BF16)
    s = (jnp.dot(q, k_t, preferred_element_type=F32)
         + jnp.dot(selb_ref[...], e_ref[:, pl.ds(pl.multiple_of(i * tile, tile), tile)], preferred_element_type=F32))
    s = s + jnp.where(i == n_steps - 1, tnear_ref[...], 0.0)
    m_old = m_ref[...]
    m_new = jnp.maximum(m_old, jnp.max(s, axis=1, keepdims=True))
    alpha = jnp.exp(m_old - m_new)
    p = jnp.exp(s - m_new)
    m_ref[...] = m_new
    l_ref[...] = alpha * l_ref[...] + jnp.sum(p, axis=1, keepdims=True)
    acc_ref[...] = alpha * acc_ref[...] + lax.dot_general(p.astype(BF16), v_t, nt, preferred_element_type=F32)

    @pl.when(i == n_steps - 1)
    def _():
        gt = gt_ref[0]
        o = gt[:, 0:1] * oc_ref[...] + gt[:, 1:2] * (acc_ref[...] / l_ref[...]) + gt[:, 2:3] * ow_ref[...]
        for gi in range(g):
            for r in range(r_heads):
                h = gi * r_heads + r
                o_ref[0, :, h * hd:(h + 1) * hd] = o[h * t:(h + 1) * t, gi * hd:(gi + 1) * hd].astype(BF16)


def _nsa_attn_sample(q_bd, gates, kct, vc, new_pad, win_t, layer, cache_t, page_table, tabs, t):
    bsz, n_pages = page_table.shape
    g, r_heads, hd = N_KV_HEADS, GQA_R, HEAD_DIM
    gw = g * hd
    lanes = g * r_heads * t
    pp = SEL_PAGES_PER_STEP
    n_past_blk = n_pages * PAGE_SIZE // SEL_LEN
    kern = functools.partial(_nsa_attn_sample_kernel, t=t, n_past_blk=n_past_blk)
    per_b = lambda a: pl.BlockSpec((1,) + a.shape[1:], lambda b, i, pt: (b, 0, 0))
    full = lambda a: pl.BlockSpec(a.shape, lambda b, i, pt: (0, 0))
    page_spec = lambda fblk, k: pl.BlockSpec((1, 1, gw, PAGE_SIZE),
                                             lambda b, i, pt: (layer, pt[b, i * pp + k], fblk, 0))
    return pl.pallas_call(
        kern,
        grid_spec=pltpu.PrefetchScalarGridSpec(
            num_scalar_prefetch=1, grid=(bsz, n_pages // pp),
            in_specs=[per_b(q_bd), per_b(gates), per_b(kct), per_b(vc), per_b(new_pad),
                      pl.BlockSpec((1, 1) + win_t.shape[2:], lambda b, i, pt: (layer, b, 0, 0))]
                     + [full(a) for a in tabs]
                     + [page_spec(2, k) for k in range(pp)] + [page_spec(3, k) for k in range(pp)],
            out_specs=pl.BlockSpec((1, t, g * r_heads * hd), lambda b, i, pt: (b, 0, 0)),
            scratch_shapes=[pltpu.VMEM((lanes, 1), F32), pltpu.VMEM((lanes, 1), F32), pltpu.VMEM((lanes, gw), F32),
                            pltpu.VMEM((lanes, n_past_blk), BF16), pltpu.VMEM((lanes, gw), F32),
                            pltpu.VMEM((lanes, gw), F32)]),
        out_shape=jax.ShapeDtypeStruct((bsz, t, g * r_heads * hd), BF16),
        compiler_params=_cparams(2),
        name="nsa_attn_sample",
    )(page_table, q_bd, gates, kct, vc, new_pad, win_t, *tabs, *([cache_t] * (2 * pp)))


def _sample_tables(rel_bias, pos0, t, nc):
    g, r_heads = N_KV_HEADS, GQA_R
    lanes = g * r_heads * t
    tq = pos0 + np.arange(t)[:, None]

    def table(kpos, extra_valid=True):
        dist = tq - kpos[None, :]
        tab = _bias_table(rel_bias, dist, (dist >= 0) & extra_valid)
        return tab.reshape(lanes, kpos.shape[0])

    tc = table(CMP_STRIDE * np.arange(nc) + CMP_LEN - 1)
    tnear = table(pos0 - Q_BLOCK + np.arange(Q_BLOCK))
    tnear = jnp.pad(tnear, ((0, 0), (SEL_PAGES_PER_STEP * PAGE_SIZE - Q_BLOCK, 0)))
    new_pos = pos0 + np.arange(128)
    tnew = table(new_pos, (new_pos < pos0 + t)[None, :])
    kw = pos0 - WINDOW + np.arange(WINDOW)
    tw = table(kw, (tq - kw[None, :]) <= WINDOW)
    n_past_blk = pos0 // SEL_LEN
    jb = np.arange(n_past_blk)[None, :]
    n = np.arange(nc)[:, None]
    pool = jnp.asarray(((n >= CMP_PER_SEL * jb - 1) & (n <= CMP_PER_SEL * jb + CMP_PER_SEL - 1)).astype(np.float32),
                       BF16)
    e = jnp.asarray((np.arange(pos0)[None, :] // SEL_LEN == np.arange(n_past_blk)[:, None]).astype(np.float32), BF16)
    li = np.arange(lanes)
    same = (li[:, None] // (r_heads * t) == li[None, :] // (r_heads * t)) & (li[:, None] % t == li[None, :] % t)
    rsum = jnp.asarray(same.astype(np.float32), BF16)
    return tc, tnear, tnew, tw, pool, e, rsum


def _nsa_sample(xs2d, bsz, t, layer, cache_t, page_table, win_t, wts, tabs):
    g, r_heads, hd = N_KV_HEADS, GQA_R, HEAD_DIM
    gw = g * hd
    q, kv, kvb, gt = _nsa_proj(xs2d, wts["wq"], wts["wkv"], wts["wg"], tm=xs2d.shape[0])
    chunks = _gather_cmp(cache_t, layer, page_table)
    kct, vc = _nsa_cmp(chunks, 1, wts["w1"], wts["pe_term"], wts["w2k"].T, wts["w2v"], transposed=False)
    q5 = q.reshape(bsz, t, g, r_heads, hd)
    q_bd = jnp.einsum('btgrd,gh->bgrthd', q5, jnp.eye(g, dtype=q.dtype)).reshape(bsz, g * r_heads * t, gw)
    gates = jnp.pad(gt[:, :3 * g * r_heads].reshape(bsz, t, g * r_heads, 3).transpose(0, 2, 1, 3)
                    .reshape(bsz, g * r_heads * t, 3), ((0, 0), (0, 0), (0, 125)))
    new_pad = jnp.pad(kvb.reshape(bsz, t, 6 * gw), ((0, 0), (0, 128 - t), (0, 0)))
    o = _nsa_attn_sample(q_bd, gates, kct, vc, new_pad, win_t, layer, cache_t, page_table, tabs, t)
    return o.reshape(bsz * t, g * r_heads * hd), kv


def _nsa_weights(w_in, cmp_pe, cmp_w1, cmp_b1, cmp_w2):
    d = w_in.shape[0]
    g, r_heads, hd = N_KV_HEADS, GQA_R, HEAD_DIM
    qd, kvd = g * r_heads * hd, 6 * g * hd
    wq = (w_in[:, :qd] * hd ** -0.5).astype(BF16)
    wq_pad = jnp.pad(wq.reshape(d, g * r_heads, hd), ((0, 0), (0, 0), (0, 128 - hd))).reshape(d, g * r_heads * 128)
    wkv = w_in[:, qd:qd + kvd].astype(BF16)
    wg = jnp.pad(w_in[:, qd + kvd:], ((0, 0), (0, 128 - 3 * g * r_heads))).astype(BF16)
    w1r = cmp_w1.reshape(2, CMP_LEN, hd, CMP_HID)
    w1 = jnp.concatenate([w1r[:, :CMP_STRIDE], w1r[:, CMP_STRIDE:]], axis=-1).astype(BF16)
    w1 = w1.reshape(2, CMP_STRIDE // CMP_PACK, CMP_PACK * hd, 2 * CMP_HID)
    pe_term = jnp.einsum('csd,csdh->ch', cmp_pe, w1r) + cmp_b1
    return dict(wq=wq, wq_pad=wq_pad, wkv=wkv, wg=wg, w1=w1, pe_term=pe_term,
                w2k=cmp_w2[0].astype(BF16), w2v=cmp_w2[1].astype(BF16))


def _nsa_prompt(x2d, bsz, seq, wts, tables):
    g, r_heads, hd = N_KV_HEADS, GQA_R, HEAD_DIM
    m = bsz * seq
    nc, nb = seq // CMP_STRIDE, seq // SEL_LEN
    q_pad, kv, kvb, gt = _nsa_proj(x2d, wts["wq_pad"], wts["wkv"], wts["wg"], tm=512)
    w2k_t = jnp.pad(wts["w2k"].T, ((0, 128 - hd), (0, 0)))
    w2v_p = jnp.pad(wts["w2v"], ((0, 0), (0, 128 - hd)))
    kct, vc = _nsa_cmp(kvb.reshape(bsz, nc, CMP_STRIDE * kvb.shape[1]), kvb.shape[1] // (2 * g * hd),
                       wts["w1"], wts["pe_term"], w2k_t, w2v_p, transposed=True)
    kst, kwt, vs, vw = _nsa_prep(kv.reshape(bsz, seq, kv.shape[1]), nb)
    gates_g = jnp.pad(gt[:, :3 * g * r_heads].reshape(m, g, 3 * r_heads).transpose(1, 0, 2),
                      ((0, 0), (0, 0), (0, 128 - 3 * r_heads)))
    o = _nsa_attn_prompt(q_pad, gates_g, kct, vc, kst, vs, kwt, vw, *tables, bsz, seq)
    return o, kv


def kernel(x_prompt, x_sample, cache_nsa, state_nsa_win, state_pool, state_ffn, page_table, p_prompt, p_sample,
           rel_bias, nsa_w_in, nsa_cmp_pe, nsa_cmp_w1, nsa_cmp_b1, nsa_cmp_w2, nsa_w_out, pool_w, pool_b,
           pool_scale, ffn_w_up, ffn_conv_w, ffn_conv_b, ffn_w_down, ln_g, ln_b, ple_w_proj, ple_w_gate,
           ple_b_gate):
    bsz, seq, d = x_prompt.shape
    db, dt, _ = x_sample.shape
    depth = ffn_w_up.shape[0]
    alpha = (2.0 * depth) ** 0.25
    g, hd = N_KV_HEADS, HEAD_DIM
    gw = g * hd
    past_len = page_table.shape[1] * PAGE_SIZE
    mp, ms = bsz * seq, db * dt
    assert g * GQA_R * dt == 128 and past_len % SEL_LEN == 0 and seq % PADL == 0
    assert page_table.shape[1] % SEL_PAGES_PER_STEP == 0 and page_table.shape[1] % CMP_PAGES_PER_STEP == 0
    assert state_nsa_win.shape[2] == WINDOW and seq >= WINDOW and state_pool.shape[2] == POOL_HIST

    xp = x_prompt.reshape(mp, d)
    xs = x_sample.reshape(ms, d)
    tables_p = _prompt_tables(rel_bias, seq // SEL_LEN, seq // CMP_STRIDE)
    tables_s = _sample_tables(rel_bias, past_len, dt, past_len // CMP_STRIDE)
    cache_t = jnp.transpose(cache_nsa, (0, 1, 3, 4, 5, 2)).reshape(cache_nsa.shape[0], cache_nsa.shape[1], 4 * gw,
                                                                   PAGE_SIZE)
    win_t = jnp.transpose(state_nsa_win, (0, 1, 3, 4, 5, 2)).reshape(state_nsa_win.shape[0], db, 2 * gw, WINDOW)
    nsa_rp, nsa_rs, nsa_wp, nsa_ws, pool_hp, pool_hs, ffn_hp, ffn_hs = ([] for _ in range(8))
    for i in range(depth):
        j = i // 2
        g0, b0, g1, b1 = ln_g[i, 0][None], ln_b[i, 0][None], ln_g[i, 1][None], ln_b[i, 1][None]
        if i % 2 == 0:
            wts = _nsa_weights(nsa_w_in[j], nsa_cmp_pe[j], nsa_cmp_w1[j], nsa_cmp_b1[j], nsa_cmp_w2[j])
            o_p, kv_p = _nsa_prompt(xp, bsz, seq, wts, tables_p)
            o_s, kv_s = _nsa_sample(xs, db, dt, j, cache_t, page_table, win_t, wts, tables_s)
            w_out = nsa_w_out[j].astype(BF16)
            xp = _linear_ln(o_p, w_out, xp, g0, b0, alpha, tm=512)
            xs = _linear_ln(o_s, w_out, xs, g0, b0, alpha, tm=ms)
            kv_p3 = kv_p.reshape(bsz, seq, 6, g, hd)
            kv_s3 = kv_s.reshape(db, dt, 6, g, hd)
            nsa_rp.append(kv_p3[:, :, :4])
            nsa_rs.append(kv_s3[:, :, :4])
            nsa_wp.append(kv_p3[:, seq - WINDOW:, 4:])
            nsa_ws.append(jnp.concatenate([state_nsa_win[j][:, dt:], kv_s3[:, :, 4:]], axis=1))
        else:
            xp3 = xp.reshape(bsz, seq, d)
            xe = jnp.concatenate([state_pool[j], xs.reshape(db, dt, d)], axis=1)
            pool_hp.append(xp3[:, seq - POOL_HIST:])
            pool_hs.append(xe[:, xe.shape[1] - POOL_HIST:])
            pw = pool_w[j].astype(BF16)
            xp = _pool_ln_prompt(xp3, pw, pool_b[j][None], pool_scale[j][None], g0, b0, alpha).reshape(mp, d)
            xs = _pool_ln_sample(xe, dt, past_len, pw, pool_b[j][None], pool_scale[j][None], g0, b0,
                                 alpha).reshape(ms, d)
        w_up = ffn_w_up[i].astype(BF16)
        w_down = ffn_w_down[i].astype(BF16)
        cw, cb = ffn_conv_w[i], ffn_conv_b[i][None]
        h_p = _ffn_up_prompt(xp, w_up, cw, cb, seq)
        tail = xp.reshape(bsz, seq, d)[:, seq - 8:].reshape(bsz * 8, d)
        u_tail = _matmul(tail, w_up, tm=bsz * 8).reshape(bsz, 8, -1)
        ffn_hp.append(u_tail[:, 8 - (CONV_W - 1):])
        u_s = _matmul(xs, w_up, tm=ms).reshape(db, dt, -1)
        ue = jnp.concatenate([state_ffn[i], u_s], axis=1)
        ffn_hs.append(ue[:, ue.shape[1] - (CONV_W - 1):])
        h_s = _ffn_gate_sample(ue, cw, cb, dt).reshape(ms, -1)
        ple = (ple_w_gate[i].astype(BF16), ple_b_gate[i][None], ple_w_proj[i].astype(BF16))
        xp = _ffn_down_ln_ple(h_p, w_down, xp, g1, b1, p_prompt[i].reshape(mp, -1), *ple, alpha, 512)
        xs = _ffn_down_ln_ple(h_s, w_down, xs, g1, b1, p_sample[i].reshape(ms, -1), *ple, alpha, ms)
    return (xp.reshape(bsz, seq, d), xs.reshape(db, dt, d),
            jnp.stack(nsa_rp), jnp.stack(nsa_rs), jnp.stack(nsa_wp), jnp.stack(nsa_ws),
            jnp.stack(pool_hp), jnp.stack(pool_hs), jnp.stack(ffn_hp), jnp.stack(ffn_hs))
```

```python
import functools
import math

import numpy as np
import jax
import jax.numpy as jnp
from jax import lax
from jax.experimental import pallas as pl
from jax.experimental.pallas import tpu as pltpu

N_KV_HEADS = 4
GQA_R = 4
HEAD_DIM = 64
CMP_LEN = 32
CMP_STRIDE = 16
CMP_HID = 256
CMP_PACK = 4
SEL_LEN = 64
CMP_PER_SEL = SEL_LEN // CMP_STRIDE
TOP_N = 16
WINDOW = 512
Q_BLOCK = 128
FORCE_SCORE = 1.0e4
REL_BUCKETS = 32
REL_MAX_DIST = 128
POOL_WINDOWS = (2, 4, 8, 16)
POOL_MAX = max(POOL_WINDOWS)
POOL_HIST = POOL_MAX - 1
CONV_W = 3
LN_EPS = 1e-5
PAGE_SIZE = 128

NEG = -1.0e30
NEG_VALID = -1.0e29
BF16 = jnp.bfloat16
F32 = jnp.float32

VMEM_LIMIT_BYTES = 56 * 1024 * 1024


def _cparams(n_grid):
    return pltpu.CompilerParams(dimension_semantics=("arbitrary",) * n_grid,
                                vmem_limit_bytes=VMEM_LIMIT_BYTES)


def _layer_norm(y, g, b):
    mu = jnp.mean(y, axis=-1, keepdims=True)
    yc = y - mu
    var = jnp.mean(yc * yc, axis=-1, keepdims=True)
    return yc * lax.rsqrt(var + LN_EPS) * g + b


FFN_TN = 256
HALO = 16


def _ffn_up_prompt_kernel(x_ref, xh_ref, w_ref, cw_ref, cb_ref, h_ref, xe_ref, ue_ref, *,
                          tm, d_ff, tiles_per_batch):
    i = pl.program_id(0)
    first = (i % tiles_per_batch) == 0
    halo = xh_ref[...]
    xe_ref[0:HALO, :] = jnp.where(first, jnp.zeros_like(halo), halo).astype(BF16)
    xe_ref[HALO:HALO + tm, :] = x_ref[...].astype(BF16)
    xe = xe_ref[...]
    for j in range(d_ff // FFN_TN):
        c_parts = []
        for half in range(2):
            c0 = half * d_ff + j * FFN_TN
            ue_ref[...] = jnp.dot(xe, w_ref[:, c0:c0 + FFN_TN], preferred_element_type=F32)
            cw = cw_ref[:, c0:c0 + FFN_TN]
            c = cb_ref[:, c0:c0 + FFN_TN]
            for k in range(CONV_W):
                off = HALO - (CONV_W - 1) + k
                c = c + cw[k:k + 1, :] * ue_ref[off:off + tm, :]
            c_parts.append(c)
        h_ref[:, j * FFN_TN:(j + 1) * FFN_TN] = (jax.nn.gelu(c_parts[1]) * c_parts[0]).astype(BF16)


def _ffn_up_prompt(x, w_up, conv_w, conv_b, seq, tm=512):
    m, d = x.shape
    d_ff = w_up.shape[1] // 2
    tpb = seq // tm
    kern = functools.partial(_ffn_up_prompt_kernel, tm=tm, d_ff=d_ff, tiles_per_batch=tpb)
    return pl.pallas_call(
        kern,
        grid=(m // tm,),
        in_specs=[
            pl.BlockSpec((tm, d), lambda i: (i, 0)),
            pl.BlockSpec((HALO, d), lambda i: (jnp.maximum(i * (tm // HALO) - 1, 0), 0)),
            pl.BlockSpec(w_up.shape, lambda i: (0, 0)),
            pl.BlockSpec(conv_w.shape, lambda i: (0, 0)),
            pl.BlockSpec(conv_b.shape, lambda i: (0, 0)),
        ],
        out_specs=pl.BlockSpec((tm, d_ff), lambda i: (i, 0)),
        out_shape=jax.ShapeDtypeStruct((m, d_ff), BF16),
        scratch_shapes=[pltpu.VMEM((HALO + tm, d), BF16), pltpu.VMEM((HALO + tm, FFN_TN), F32)],
        compiler_params=_cparams(1),
        name="ffn_up_prompt",
    )(x, x, w_up, conv_w, conv_b)


def _ffn_gate_sample_kernel(ue_ref, cw_ref, cb_ref, h_ref, *, t, d_ff):
    for j in range(d_ff // FFN_TN):
        c_parts = []
        for half in range(2):
            c0 = half * d_ff + j * FFN_TN
            cw = cw_ref[:, c0:c0 + FFN_TN]
            c = cb_ref[:, c0:c0 + FFN_TN][None]
            for k in range(CONV_W):
                c = c + cw[k:k + 1, :][None] * ue_ref[:, k:k + t, c0:c0 + FFN_TN]
            c_parts.append(c)
        h_ref[:, :, j * FFN_TN:(j + 1) * FFN_TN] = (jax.nn.gelu(c_parts[1]) * c_parts[0]).astype(BF16)


def _ffn_gate_sample(ue, conv_w, conv_b, t):
    b, te, n2 = ue.shape
    d_ff = n2 // 2
    bb = 8
    kern = functools.partial(_ffn_gate_sample_kernel, t=t, d_ff=d_ff)
    return pl.pallas_call(
        kern,
        grid=(b // bb,),
        in_specs=[
            pl.BlockSpec((bb, te, n2), lambda i: (i, 0, 0)),
            pl.BlockSpec(conv_w.shape, lambda i: (0, 0)),
            pl.BlockSpec(conv_b.shape, lambda i: (0, 0)),
        ],
        out_specs=pl.BlockSpec((bb, t, d_ff), lambda i: (i, 0, 0)),
        out_shape=jax.ShapeDtypeStruct((b, t, d_ff), BF16),
        compiler_params=_cparams(1),
        name="ffn_gate_sample",
    )(ue, conv_w, conv_b)


def _matmul_kernel(x_ref, w_ref, o_ref):
    o_ref[...] = jnp.dot(x_ref[...].astype(BF16), w_ref[...], preferred_element_type=F32).astype(o_ref.dtype)


def _matmul(x, w, tm, tn=None, out_dtype=F32):
    m, k = x.shape
    n = w.shape[1]
    tn = n if tn is None else tn
    return pl.pallas_call(
        _matmul_kernel,
        grid=(m // tm, n // tn),
        in_specs=[pl.BlockSpec((tm, k), lambda i, j: (i, 0)),
                  pl.BlockSpec((k, tn), lambda i, j: (0, j))],
        out_specs=pl.BlockSpec((tm, tn), lambda i, j: (i, j)),
        out_shape=jax.ShapeDtypeStruct((m, n), out_dtype),
        compiler_params=_cparams(2),
        name="matmul",
    )(x, w)


def _linear_ln_kernel(a_ref, w_ref, r_ref, g_ref, b_ref, o_ref, *, alpha):
    f = jnp.dot(a_ref[...].astype(BF16), w_ref[...], preferred_element_type=F32)
    o_ref[...] = _layer_norm(alpha * r_ref[...] + f, g_ref[...], b_ref[...])


def _linear_ln(a, w, resid, g, b, alpha, tm):
    m, k = a.shape
    d = w.shape[1]
    kern = functools.partial(_linear_ln_kernel, alpha=alpha)
    return pl.pallas_call(
        kern,
        grid=(m // tm,),
        in_specs=[pl.BlockSpec((tm, k), lambda i: (i, 0)),
                  pl.BlockSpec((k, d), lambda i: (0, 0)),
                  pl.BlockSpec((tm, d), lambda i: (i, 0)),
                  pl.BlockSpec((1, d), lambda i: (0, 0)),
                  pl.BlockSpec((1, d), lambda i: (0, 0))],
        out_specs=pl.BlockSpec((tm, d), lambda i: (i, 0)),
        out_shape=jax.ShapeDtypeStruct((m, d), F32),
        compiler_params=_cparams(1),
        name="linear_ln",
    )(a, w, resid, g, b)


def _ffn_down_kernel(h_ref, wd_ref, r_ref, g_ref, b_ref, p_ref, wg_ref, bg_ref, wp_ref, o_ref, *, alpha):
    f = jnp.dot(h_ref[...], wd_ref[...], preferred_element_type=F32)
    x2 = _layer_norm(alpha * r_ref[...] + f, g_ref[...], b_ref[...])
    gate = jax.nn.sigmoid(jnp.dot(x2.astype(BF16), wg_ref[...], preferred_element_type=F32) + bg_ref[...])
    inj = jnp.dot(p_ref[...].astype(BF16), wp_ref[...], preferred_element_type=F32)
    o_ref[...] = x2 + gate * inj


def _ffn_down_ln_ple(h, w_down, resid, g, b, p, w_gate, b_gate, w_proj, alpha, tm):
    m, k = h.shape
    d = w_down.shape[1]
    pd = p.shape[1]
    kern = functools.partial(_ffn_down_kernel, alpha=alpha)
    full = lambda shape: pl.BlockSpec(shape, lambda i: (0, 0))
    return pl.pallas_call(
        kern,
        grid=(m // tm,),
        in_specs=[pl.BlockSpec((tm, k), lambda i: (i, 0)), full((k, d)),
                  pl.BlockSpec((tm, d), lambda i: (i, 0)), full((1, d)), full((1, d)),
                  pl.BlockSpec((tm, pd), lambda i: (i, 0)), full((d, d)), full((1, d)), full((pd, d))],
        out_specs=pl.BlockSpec((tm, d), lambda i: (i, 0)),
        out_shape=jax.ShapeDtypeStruct((m, d), F32),
        compiler_params=_cparams(1),
        name="ffn_down_ln_ple",
    )(h, w_down, resid, g, b, p, w_gate, b_gate, w_proj)


def _pool_finish(x, diffs, w_ref, pb_ref, ps_ref, g_ref, b_ref, alpha):
    ys = [jnp.dot(dg.astype(BF16), w_ref[gi], preferred_element_type=F32) for gi, dg in enumerate(diffs)]
    y = jnp.concatenate(ys, axis=-1) + pb_ref[...]
    return _layer_norm(alpha * x + ps_ref[...] * y, g_ref[...], b_ref[...])


def _pool_prompt_kernel(x_ref, xh_ref, w_ref, pb_ref, ps_ref, g_ref, b_ref, o_ref, xe_ref, *,
                        tt, gw, alpha):
    i = pl.program_id(1)
    halo = xh_ref[0]
    xe_ref[0:POOL_MAX, :] = jnp.where(i == 0, jnp.zeros_like(halo), halo)
    xe_ref[POOL_MAX:POOL_MAX + tt, :] = x_ref[0]
    pos = i * tt + lax.broadcasted_iota(jnp.int32, (tt, 1), 0)
    x = x_ref[0]
    diffs = []
    for gi, win in enumerate(POOL_WINDOWS):
        lo = gi * gw
        tot = x[:, lo:lo + gw]
        for k in range(1, win):
            tot = tot + xe_ref[POOL_MAX - k:POOL_MAX - k + tt, lo:lo + gw]
        cnt = jnp.minimum(win, pos + 1).astype(F32)
        diffs.append(tot / cnt - x[:, lo:lo + gw])
    o_ref[0] = _pool_finish(x, diffs, w_ref, pb_ref, ps_ref, g_ref, b_ref, alpha)


def _pool_ln_prompt(x, w, pb, ps, g, b, alpha, tt=512):
    bsz, t, d = x.shape
    gw = d // len(POOL_WINDOWS)
    kern = functools.partial(_pool_prompt_kernel, tt=tt, gw=gw, alpha=alpha)
    vec = pl.BlockSpec((1, d), lambda bi, i: (0, 0))
    return pl.pallas_call(
        kern,
        grid=(bsz, t // tt),
        in_specs=[pl.BlockSpec((1, tt, d), lambda bi, i: (bi, i, 0)),
                  pl.BlockSpec((1, POOL_MAX, d), lambda bi, i: (bi, jnp.maximum(i * (tt // POOL_MAX) - 1, 0), 0)),
                  pl.BlockSpec(w.shape, lambda bi, i: (0, 0, 0)), vec, vec, vec, vec],
        out_specs=pl.BlockSpec((1, tt, d), lambda bi, i: (bi, i, 0)),
        out_shape=jax.ShapeDtypeStruct((bsz, t, d), F32),
        scratch_shapes=[pltpu.VMEM((POOL_MAX + tt, d), F32)],
        compiler_params=_cparams(2),
        name="pool_ln_prompt",
    )(x, x, w, pb, ps, g, b)


def _pool_sample_kernel(xe_ref, w_ref, pb_ref, ps_ref, g_ref, b_ref, o_ref, *, t, nh, gw, pos0, alpha):
    bb = xe_ref.shape[0]
    d = xe_ref.shape[2]
    x = xe_ref[:, nh:nh + t, :].reshape(bb * t, d)
    cnt_np = np.minimum(np.array(POOL_WINDOWS)[:, None], pos0 + np.arange(t)[None, :] + 1).astype(np.float32)
    diffs = []
    for gi, win in enumerate(POOL_WINDOWS):
        lo = gi * gw
        tot = xe_ref[:, nh:nh + t, lo:lo + gw]
        for k in range(1, win):
            tot = tot + xe_ref[:, nh - k:nh - k + t, lo:lo + gw]
        if np.all(cnt_np[gi] == win):
            mean = tot / float(win)
        else:
            t_idx = lax.broadcasted_iota(jnp.int32, (1, t, 1), 1)
            mean = tot / jnp.minimum(win, pos0 + t_idx + 1).astype(F32)
        diffs.append(mean.reshape(bb * t, gw) - x[:, lo:lo + gw])
    o_ref[...] = _pool_finish(x, diffs, w_ref, pb_ref, ps_ref, g_ref, b_ref, alpha).reshape(bb, t, d)


def _pool_ln_sample(xe, t, pos0, w, pb, ps, g, b, alpha, bb=8):
    bsz, te, d = xe.shape
    gw = d // len(POOL_WINDOWS)
    kern = functools.partial(_pool_sample_kernel, t=t, nh=te - t, gw=gw, pos0=pos0, alpha=alpha)
    vec = pl.BlockSpec((1, d), lambda i: (0, 0))
    return pl.pallas_call(
        kern,
        grid=(bsz // bb,),
        in_specs=[pl.BlockSpec((bb, te, d), lambda i: (i, 0, 0)),
                  pl.BlockSpec(w.shape, lambda i: (0, 0, 0)), vec, vec, vec, vec],
        out_specs=pl.BlockSpec((bb, t, d), lambda i: (i, 0, 0)),
        out_shape=jax.ShapeDtypeStruct((bsz, t, d), F32),
        compiler_params=_cparams(1),
        name="pool_ln_sample",
    )(xe, w, pb, ps, g, b)


def _nsa_proj_kernel(x_ref, wq_ref, wkv_ref, wg_ref, q_ref, kv_ref, kvb_ref, gt_ref, *scratch):
    xb = x_ref[...].astype(BF16)
    q_ref[...] = jnp.dot(xb, wq_ref[...], preferred_element_type=F32).astype(BF16)
    kv = jnp.dot(xb, wkv_ref[...], preferred_element_type=F32)
    kv_ref[...] = kv
    gt_ref[...] = jax.nn.sigmoid(jnp.dot(xb, wg_ref[...], preferred_element_type=F32))
    if not scratch:
        kvb_ref[...] = kv.astype(BF16)
        return
    (t_ref,) = scratch
    n_fb, tm = t_ref.shape[0], t_ref.shape[1]
    for fb in range(n_fb):
        t_ref[fb] = kv[:, fb * 128:(fb + 1) * 128]
    for s in range(CMP_STRIDE):
        for fb in range(n_fb):
            kvb_ref[0, s, :, fb * 128:(fb + 1) * 128] = (
                t_ref[fb, pl.ds(s, tm // CMP_STRIDE, stride=CMP_STRIDE), :].astype(BF16))


def _nsa_proj(x, wq, wkv, wg, tm, chunk_seq=None):
    m, d = x.shape
    nq, nkv, ng = wq.shape[1], wkv.shape[1], wg.shape[1]
    full = lambda w: pl.BlockSpec(w.shape, lambda i: (0, 0))
    row = lambda n: pl.BlockSpec((tm, n), lambda i: (i, 0))
    if chunk_seq is None:
        kvb_spec, kvb_shape, scratch = row(nkv), jax.ShapeDtypeStruct((m, nkv), BF16), []
    else:
        feat = 2 * N_KV_HEADS * HEAD_DIM
        tpb = chunk_seq // tm
        kvb_spec = pl.BlockSpec((1, CMP_STRIDE, tm // CMP_STRIDE, feat), lambda i: (i // tpb, 0, i % tpb, 0))
        kvb_shape = jax.ShapeDtypeStruct((m // chunk_seq, CMP_STRIDE, chunk_seq // CMP_STRIDE, feat), BF16)
        scratch = [pltpu.VMEM((feat // 128, tm, 128), F32)]
    return pl.pallas_call(
        _nsa_proj_kernel,
        grid=(m // tm,),
        in_specs=[row(d), full(wq), full(wkv), full(wg)],
        out_specs=[row(nq), row(nkv), kvb_spec, row(ng)],
        out_shape=[jax.ShapeDtypeStruct((m, nq), BF16), jax.ShapeDtypeStruct((m, nkv), F32),
                   kvb_shape, jax.ShapeDtypeStruct((m, ng), F32)],
        scratch_shapes=scratch,
        compiler_params=_cparams(1),
        name="nsa_proj",
    )(x, wq, wkv, wg)


def _nsa_cmp_kernel(*refs, nc, transposed):
    ns = CMP_STRIDE
    x_refs = refs[:ns]
    w1_ref, pe_ref, w2k_ref, w2v_ref, ko_ref, vo_ref = refs[ns:]
    hd = HEAD_DIM
    gw = N_KV_HEADS * hd
    last = lax.broadcasted_iota(jnp.int32, (nc, CMP_HID), 0) == nc - 1
    nt = (((1,), (1,)), ((), ()))
    for c in range(2):
        for g in range(N_KV_HEADS):
            acc = jnp.zeros((nc, 2 * CMP_HID), F32)
            lo = c * gw + g * hd
            for k in range(ns // CMP_PACK):
                xs = jnp.concatenate([x_refs[k * CMP_PACK + i][:, lo:lo + hd] for i in range(CMP_PACK)], axis=1)
                acc = acc + jnp.dot(xs, w1_ref[c, k], preferred_element_type=F32)
            h1 = jnp.where(last, 0.0, pltpu.roll(acc[:, CMP_HID:], nc - 1, 0))
            hid = jax.nn.gelu(acc[:, :CMP_HID] + h1 + pe_ref[c:c + 1, :]).astype(BF16)
            if c == 0 and transposed:
                ko_ref[0, g] = lax.dot_general(w2k_ref[...], hid, nt, preferred_element_type=F32).astype(BF16)
            elif c == 0:
                ko_ref[0, g * hd:(g + 1) * hd, :] = lax.dot_general(w2k_ref[...], hid, nt,
                                                                    preferred_element_type=F32).astype(BF16)
            elif transposed:
                vo_ref[0, g] = jnp.dot(hid, w2v_ref[...], preferred_element_type=F32).astype(BF16)
            else:
                vo_ref[0, :, g * hd:(g + 1) * hd] = jnp.dot(hid, w2v_ref[...], preferred_element_type=F32).astype(BF16)


def _nsa_cmp(xv, w1, pe_term, w2k, w2v, transposed):
    g, hd = N_KV_HEADS, HEAD_DIM
    feat = 2 * g * hd
    bsz = xv.shape[0]
    if xv.ndim == 4:
        nc = xv.shape[2]
        x_specs = [pl.BlockSpec((None, None, nc, feat), functools.partial(lambda b, s: (b, s, 0, 0), s=s))
                   for s in range(CMP_STRIDE)]
    else:
        nc = xv.shape[1]
        x_specs = [pl.BlockSpec((None, nc, feat), functools.partial(lambda b, s: (b, 0, s), s=s))
                   for s in range(CMP_STRIDE)]
    kern = functools.partial(_nsa_cmp_kernel, nc=nc, transposed=transposed)
    full = lambda w: pl.BlockSpec(w.shape, lambda b: (0,) * w.ndim)
    if transposed:
        out_specs = [pl.BlockSpec((1, g, 128, nc), lambda b: (b, 0, 0, 0)),
                     pl.BlockSpec((1, g, nc, 128), lambda b: (b, 0, 0, 0))]
        out_shape = [jax.ShapeDtypeStruct((bsz, g, 128, nc), BF16), jax.ShapeDtypeStruct((bsz, g, nc, 128), BF16)]
    else:
        out_specs = [pl.BlockSpec((1, g * hd, nc), lambda b: (b, 0, 0)), pl.BlockSpec((1, nc, g * hd), lambda b: (b, 0, 0))]
        out_shape = [jax.ShapeDtypeStruct((bsz, g * hd, nc), BF16), jax.ShapeDtypeStruct((bsz, nc, g * hd), BF16)]
    return pl.pallas_call(
        kern,
        grid=(bsz,),
        in_specs=x_specs + [full(w1), full(pe_term), full(w2k), full(w2v)],
        out_specs=out_specs,
        out_shape=out_shape,
        compiler_params=_cparams(1),
        name="nsa_cmp",
    )(*([xv] * CMP_STRIDE), w1, pe_term, w2k, w2v)


PADL = 512
FAR_TILE = 512


def _nsa_prep_kernel(kv_ref, kst_ref, kwt_ref, vs_ref, vw_ref, *, nb):
    j = pl.program_id(1)
    tl = PADL
    hd = HEAD_DIM
    gw = N_KV_HEADS * hd

    @pl.when(j == 0)
    def _():
        for g in range(N_KV_HEADS):
            kst_ref[0, g] = (lax.broadcasted_iota(jnp.int32, (128 + nb, tl), 0) == hd).astype(BF16)
            kwt_ref[0, g] = (lax.broadcasted_iota(jnp.int32, (128, tl), 0) == hd).astype(BF16)
            vs_ref[0, g] = jnp.zeros((tl, 128), BF16)
            vw_ref[0, g] = jnp.zeros((tl, 128), BF16)

    @pl.when(j > 0)
    def _():
        kst = kv_ref[0, :, 2 * gw:3 * gw].T
        kwt = kv_ref[0, :, 4 * gw:5 * gw].T
        pos = (j - 1) * tl + lax.broadcasted_iota(jnp.int32, (nb, tl), 1)
        onehot = (pos // SEL_LEN == lax.broadcasted_iota(jnp.int32, (nb, tl), 0)).astype(BF16)
        zeros = jnp.zeros((128 - hd, tl), BF16)
        lane = lax.broadcasted_iota(jnp.int32, (tl, 128), 1)
        tail = (lane == hd).astype(F32)
        for g in range(N_KV_HEADS):
            kst_ref[0, g, 0:hd, :] = kst[g * hd:(g + 1) * hd].astype(BF16)
            kst_ref[0, g, hd:128, :] = zeros
            kst_ref[0, g, 128:128 + nb, :] = onehot
            kwt_ref[0, g, 0:hd, :] = kwt[g * hd:(g + 1) * hd].astype(BF16)
            kwt_ref[0, g, hd:128, :] = zeros
            for stream, ref in ((3, vs_ref), (5, vw_ref)):
                a = kv_ref[0, :, stream * gw + (g // 2) * 128:stream * gw + (g // 2 + 1) * 128]
                if g % 2:
                    a = pltpu.roll(a, hd, 1)
                ref[0, g] = jnp.where(lane < hd, a, tail).astype(BF16)


def _nsa_prep(kv, nb):
    bsz, t, n = kv.shape
    g = N_KV_HEADS
    tl = PADL
    tp = PADL + t
    kern = functools.partial(_nsa_prep_kernel, nb=nb)
    kspec = lambda rows: pl.BlockSpec((1, g, rows, tl), lambda b, j: (b, 0, 0, j))
    vspec = pl.BlockSpec((1, g, tl, 128), lambda b, j: (b, 0, j, 0))
    return pl.pallas_call(
        kern,
        grid=(bsz, tp // tl),
        in_specs=[pl.BlockSpec((1, tl, n), lambda b, j: (b, jnp.maximum(j - 1, 0), 0))],
        out_specs=[kspec(128 + nb), kspec(128), vspec, vspec],
        out_shape=[jax.ShapeDtypeStruct((bsz, g, 128 + nb, tp), BF16), jax.ShapeDtypeStruct((bsz, g, 128, tp), BF16),
                   jax.ShapeDtypeStruct((bsz, g, tp, 128), BF16), jax.ShapeDtypeStruct((bsz, g, tp, 128), BF16)],
        compiler_params=_cparams(2),
        name="nsa_prep",
    )(kv)


def _split3(x):
    a = x.astype(BF16)
    r = x - a.astype(F32)
    b = r.astype(BF16)
    return a, b, (r - b.astype(F32)).astype(BF16)


def _top_blocks(score, blk, n_sel, n_rows):
    sel = jnp.zeros(score.shape, jnp.bool_)
    for _ in range(n_sel):
        m = jnp.max(score, axis=0, keepdims=True)
        first = jnp.min(jnp.where(score == m, blk, n_rows), axis=0, keepdims=True)
        pick = blk == first
        sel = sel | pick
        score = jnp.where(pick, -3.0e38, score)
    return sel


def _nsa_attn_prompt_kernel(q_ref, gt_ref, kct_ref, vc_ref, kst_ref, vs_ref, kwt_ref, vw_ref,
                            wtab_ref, stab_ref, atab_ref, poolt_ref, o_ref,
                            sw_ref, sa_ref, sb_ref, pa_ref, pb_ref, ala_ref, alb_ref, m_ref, acc_ref, *, nb, nc):
    qi = pl.program_id(2)
    t0 = qi * Q_BLOCK
    r_heads = GQA_R
    hd = HEAD_DIM
    rows = r_heads * Q_BLOCK
    nw = WINDOW + Q_BLOCK
    lane = lax.broadcasted_iota(jnp.int32, (Q_BLOCK, 128), 1)
    q = jnp.concatenate([jnp.where(lane == hd, NEG, q_ref[:, r * 128:(r + 1) * 128]) for r in range(r_heads)],
                        axis=0)

    n_idx = lax.broadcasted_iota(jnp.int32, (48, nc), 1)
    j_idx = lax.broadcasted_iota(jnp.int32, (48, nc), 0)
    rel = n_idx - (qi * (Q_BLOCK // CMP_STRIDE) - 8)
    shift = (((j_idx < 32) & (rel == jnp.where(j_idx < 16, j_idx, j_idx - 16)))
             | ((j_idx == 32) & (rel >= 16))).astype(F32).astype(BF16)
    s_c = (jnp.dot(q, kct_ref[0, 0], preferred_element_type=F32)
           + jnp.dot(atab_ref[0], shift, preferred_element_type=F32))
    m_c = jnp.max(s_c, axis=1, keepdims=True)
    e_c = jnp.exp(s_c - m_c)
    inv_c = jnp.where(m_c > NEG_VALID, 1.0 / jnp.maximum(jnp.sum(e_c, axis=1, keepdims=True), 1e-30), 0.0)
    p_c = e_c * inv_c
    o_c = jnp.dot(p_c.astype(BF16), vc_ref[0, 0], preferred_element_type=F32)

    grp = p_c[0:Q_BLOCK]
    for r in range(1, r_heads):
        grp = grp + p_c[r * Q_BLOCK:(r + 1) * Q_BLOCK]
    imp_t = jnp.zeros((nb, Q_BLOCK), F32)
    for part in _split3(grp):
        imp_t = imp_t + lax.dot_general(poolt_ref[...], part, (((1,), (1,)), ((), ())),
                                        preferred_element_type=F32)
    blk = lax.broadcasted_iota(jnp.int32, (nb, Q_BLOCK), 0)
    tq = t0 + lax.broadcasted_iota(jnp.int32, (nb, Q_BLOCK), 1)
    cur = tq // SEL_LEN
    forced = (blk == 0) | (blk == cur) | (blk == cur - 1)
    score = jnp.where(blk * SEL_LEN > tq, -1.0, jnp.where(forced, -3.0e38, imp_t))
    sel = forced | _top_blocks(score, blk, min(TOP_N, nb) - 3, nb)
    selb = jnp.where(sel, 0.0, NEG).T.astype(BF16)
    q_aug = jnp.concatenate([q, jnp.concatenate([selb] * r_heads, axis=0)], axis=1)

    c_w = pl.multiple_of(t0 + PADL - WINDOW, Q_BLOCK)
    sw_ref[...] = jnp.dot(q, kwt_ref[0, 0, :, pl.ds(c_w, nw)], preferred_element_type=F32) + wtab_ref[0]

    s = jnp.dot(q_aug, kst_ref[0, 0, :, pl.ds(c_w, nw)], preferred_element_type=F32) + stab_ref[0]
    m_s = jnp.max(s, axis=1, keepdims=True)
    m_ref[...] = m_s
    acc_ref[...] = jnp.dot(jnp.exp(s - m_s).astype(BF16), vs_ref[0, 0, pl.ds(c_w, nw), :],
                           preferred_element_type=F32)

    n_far = jnp.maximum((t0 + PADL - WINDOW + FAR_TILE - 1) // FAR_TILE - PADL // FAR_TILE, 0)

    def tile_start(k):
        return pl.multiple_of(c_w - FAR_TILE * (jnp.clip(k, 0, n_far - 1) + 1), Q_BLOCK)

    def scores(k, s_out):
        s_out[...] = jnp.dot(q_aug, kst_ref[0, 0, :, pl.ds(tile_start(k), FAR_TILE)], preferred_element_type=F32)

    def pv_update(k, p_in, al_in):
        acc_ref[...] = al_in[...] * acc_ref[...] + jnp.dot(p_in[...], vs_ref[0, 0, pl.ds(tile_start(k), FAR_TILE), :],
                                                          preferred_element_type=F32)

    def stage(k, s_cur, s_nxt, p_cur, al_cur, p_prev, al_prev):
        scores(k + 1, s_nxt)
        pv_update(k - 1, p_prev, al_prev)
        m_old = m_ref[...]
        m_new = jnp.maximum(m_old, jnp.max(s_cur[...], axis=1, keepdims=True))
        al_cur[...] = jnp.exp(m_old - m_new)
        p_cur[...] = jnp.exp(s_cur[...] - m_new).astype(BF16)
        m_ref[...] = m_new

    @pl.when(n_far > 0)
    def _():
        scores(0, sa_ref)
        pb_ref[...] = jnp.zeros_like(pb_ref)
        alb_ref[...] = jnp.ones_like(alb_ref)

        def pair(j, _):
            stage(2 * j, sa_ref, sb_ref, pa_ref, ala_ref, pb_ref, alb_ref)

            @pl.when(2 * j + 1 < n_far)
            def _():
                stage(2 * j + 1, sb_ref, sa_ref, pb_ref, alb_ref, pa_ref, ala_ref)
            return 0

        lax.fori_loop(0, (n_far + 1) // 2, pair, 0)

        @pl.when(n_far % 2 == 1)
        def _():
            pv_update(n_far - 1, pa_ref, ala_ref)

        @pl.when(n_far % 2 == 0)
        def _():
            pv_update(n_far - 1, pb_ref, alb_ref)

    acc = acc_ref[...]
    o_s = acc[:, :hd] * (1.0 / acc[:, hd:hd + 1])

    s_w = sw_ref[...]
    m_w = jnp.max(s_w, axis=1, keepdims=True)
    acc_w = jnp.dot(jnp.exp(s_w - m_w).astype(BF16), vw_ref[0, 0, pl.ds(c_w, nw), :], preferred_element_type=F32)
    o_w = acc_w[:, :hd] * (1.0 / acc_w[:, hd:hd + 1])

    gt = gt_ref[0]
    for r in range(r_heads):
        sl = slice(r * Q_BLOCK, (r + 1) * Q_BLOCK)
        o = (gt[:, 3 * r:3 * r + 1] * o_c[sl, :hd] + gt[:, 3 * r + 1:3 * r + 2] * o_s[sl]
             + gt[:, 3 * r + 2:3 * r + 3] * o_w[sl])
        o_ref[:, r * hd:(r + 1) * hd] = o.astype(BF16)


def _nsa_attn_prompt(q_pad, gates_g, kct, vc, kst, vs, kwt, vw, wtab, stab, atab, poolt, bsz, seq):
    g, r_heads, hd = N_KV_HEADS, GQA_R, HEAD_DIM
    nq = seq // Q_BLOCK
    nb, nc = poolt.shape
    tp = PADL + seq
    kern = functools.partial(_nsa_attn_prompt_kernel, nb=nb, nc=nc)
    rows = r_heads * Q_BLOCK
    per_bg = lambda shape: pl.BlockSpec((1, 1) + shape, lambda b, gi, i: (b, gi, 0, 0))
    per_g = lambda shape: pl.BlockSpec((1,) + shape, lambda b, gi, i: (gi, 0, 0))
    return pl.pallas_call(
        kern,
        grid=(bsz, g, nq),
        in_specs=[pl.BlockSpec((Q_BLOCK, r_heads * 128), lambda b, gi, i: (b * nq + i, gi)),
                  pl.BlockSpec((1, Q_BLOCK, 128), lambda b, gi, i: (gi, b * nq + i, 0)),
                  per_bg((128, nc)), per_bg((nc, 128)),
                  per_bg((128 + nb, tp)), per_bg((tp, 128)),
                  per_bg((128, tp)), per_bg((tp, 128)),
                  per_g((r_heads * Q_BLOCK, WINDOW + Q_BLOCK)), per_g((r_heads * Q_BLOCK, WINDOW + Q_BLOCK)),
                  per_g((r_heads * Q_BLOCK, 48)),
                  pl.BlockSpec((nb, nc), lambda b, gi, i: (0, 0))],
        out_specs=pl.BlockSpec((Q_BLOCK, r_heads * hd), lambda b, gi, i: (b * nq + i, gi)),
        out_shape=jax.ShapeDtypeStruct((bsz * seq, g * r_heads * hd), BF16),
        scratch_shapes=[pltpu.VMEM((rows, WINDOW + Q_BLOCK), F32),
                        pltpu.VMEM((rows, FAR_TILE), F32), pltpu.VMEM((rows, FAR_TILE), F32),
                        pltpu.VMEM((rows, FAR_TILE), BF16), pltpu.VMEM((rows, FAR_TILE), BF16),
                        pltpu.VMEM((rows, 1), F32), pltpu.VMEM((rows, 1), F32),
                        pltpu.VMEM((rows, 1), F32), pltpu.VMEM((rows, 128), F32)],
        compiler_params=_cparams(3),
        name="nsa_attn_prompt",
    )(q_pad, gates_g, kct, vc, kst, vs, kwt, vw, wtab, stab, atab, poolt)


def _rel_bucket_np(dist):
    n = np.maximum(dist, 0)
    exact = REL_BUCKETS // 2
    nf = np.maximum(n, 1).astype(np.float32)
    large = exact + (np.log(nf / np.float32(exact)) / np.float32(math.log(REL_MAX_DIST / exact))
                     * np.float32(REL_BUCKETS - exact)).astype(np.int32)
    return np.where(n < exact, n, np.minimum(large, REL_BUCKETS - 1))


FAR_DIST = Q_BLOCK - CMP_STRIDE + 1
assert np.all(_rel_bucket_np(np.arange(FAR_DIST, 1 << 16)) == REL_BUCKETS - 1)


def _bias_table(rel_bias, dist, valid):
    relc = rel_bias.astype(F32) - rel_bias[REL_BUCKETS - 1].astype(F32)[None]
    onehot = np.eye(REL_BUCKETS, dtype=np.float32)[_rel_bucket_np(dist)]
    tab = jnp.einsum('...b,bh->...h', onehot, relc, precision=lax.Precision.HIGHEST)
    tab = jnp.where(valid[..., None], tab, NEG)
    return jnp.moveaxis(tab, -1, 0)


def _prompt_tables(rel_bias, nb, nc):
    g, r_heads = N_KV_HEADS, GQA_R
    qi = np.arange(Q_BLOCK)[:, None]
    nw = WINDOW + Q_BLOCK
    d_w = qi + WINDOW - np.arange(nw)[None, :]
    wtab = _bias_table(rel_bias, d_w, (d_w >= 0) & (d_w <= WINDOW)).reshape(g, r_heads * Q_BLOCK, nw)
    stab = _bias_table(rel_bias, d_w, d_w >= 0).reshape(g, r_heads * Q_BLOCK, nw)
    d_c = qi - CMP_STRIDE * (np.arange(16)[None, :] - 8) - (CMP_LEN - 1)
    valid_c = d_c >= 0
    a = _bias_table(rel_bias, d_c, valid_c).reshape(g, r_heads * Q_BLOCK, 16)
    hi = a.astype(BF16)
    lo = jnp.where(jnp.asarray(np.tile(valid_c, (r_heads, 1)))[None], a - hi.astype(F32), 0.0).astype(BF16)
    fut = jnp.full((g, r_heads * Q_BLOCK, 1), NEG, BF16)
    atab = jnp.concatenate([hi, lo, fut, jnp.zeros((g, r_heads * Q_BLOCK, 15), BF16)], axis=-1)
    jb = np.arange(nb)[:, None]
    n = np.arange(nc)[None, :]
    poolt = jnp.asarray(((n >= CMP_PER_SEL * jb - 1) & (n <= CMP_PER_SEL * jb + CMP_PER_SEL - 1)).astype(np.float32),
                        BF16)
    return wtab, stab, atab, poolt


CMP_PAGES_PER_STEP = 8
SEL_PAGES_PER_STEP = 8


def _gather_cmp_kernel(pt_ref, *refs):
    del pt_ref
    page_refs = refs[:CMP_PAGES_PER_STEP]
    o_ref, t_ref = refs[CMP_PAGES_PER_STEP:]
    n_fb = t_ref.shape[0]
    for k in range(CMP_PAGES_PER_STEP):
        for fb in range(n_fb):
            t_ref[fb, k * PAGE_SIZE:(k + 1) * PAGE_SIZE, :] = page_refs[k][0, 0, fb * 128:(fb + 1) * 128, :].T
    rows = CMP_PAGES_PER_STEP * PAGE_SIZE // CMP_STRIDE
    for s in range(CMP_STRIDE):
        for fb in range(n_fb):
            c0 = (s * n_fb + fb) * 128
            o_ref[0, :, c0:c0 + 128] = t_ref[fb, pl.ds(s, rows, stride=CMP_STRIDE), :].astype(BF16)


def _gather_cmp(cache_t, layer, page_table):
    bsz, n_pages = page_table.shape
    half = cache_t.shape[2] // 2
    pp = CMP_PAGES_PER_STEP
    chunks = pp * PAGE_SIZE // CMP_STRIDE
    page_spec = lambda k: pl.BlockSpec((1, 1, half, PAGE_SIZE), lambda b, i, pt: (layer, pt[b, i * pp + k], 0, 0))
    return pl.pallas_call(
        _gather_cmp_kernel,
        grid_spec=pltpu.PrefetchScalarGridSpec(
            num_scalar_prefetch=1, grid=(bsz, n_pages // pp),
            in_specs=[page_spec(k) for k in range(pp)],
            out_specs=pl.BlockSpec((1, chunks, CMP_STRIDE * half), lambda b, i, pt: (b, i, 0)),
            scratch_shapes=[pltpu.VMEM((half // 128, pp * PAGE_SIZE, 128), F32)]),
        out_shape=jax.ShapeDtypeStruct((bsz, n_pages * PAGE_SIZE // CMP_STRIDE, CMP_STRIDE * half), BF16),
        compiler_params=_cparams(2),
        name="gather_cmp",
    )(page_table, *([cache_t] * pp))


def _nsa_attn_sample_kernel(pt_ref, q_ref, gt_ref, kct_ref, vc_ref, new_ref, win_ref,
                            tc_ref, tnear_ref, tnew_ref, tw_ref, pool_ref, e_ref, rsum_ref, *refs,
                            t, n_past_blk):
    del pt_ref
    pp = SEL_PAGES_PER_STEP
    k_refs, v_refs = refs[:pp], refs[pp:2 * pp]
    o_ref, m_ref, l_ref, acc_ref, selb_ref, oc_ref, ow_ref = refs[2 * pp:]
    g, r_heads, hd = N_KV_HEADS, GQA_R, HEAD_DIM
    gw = g * hd
    i = pl.program_id(1)
    n_steps = pl.num_programs(1)
    nt = (((1,), (1,)), ((), ()))
    q = q_ref[0]
    tile = pp * PAGE_SIZE

    @pl.when(i == 0)
    def _():
        new = new_ref[0]
        s_c = jnp.dot(q, kct_ref[0], preferred_element_type=F32) + tc_ref[...]
        m_c = jnp.max(s_c, axis=1, keepdims=True)
        e_c = jnp.where(s_c > NEG_VALID, jnp.exp(s_c - m_c), 0.0)
        p_c = e_c / jnp.maximum(jnp.sum(e_c, axis=1, keepdims=True), 1e-30)
        oc_ref[...] = jnp.dot(p_c.astype(BF16), vc_ref[0], preferred_element_type=F32)
        imp0 = jnp.zeros((q.shape[0], n_past_blk), F32)
        for part in _split3(p_c):
            imp0 = imp0 + jnp.dot(part, pool_ref[...], preferred_element_type=F32)
        imp = jnp.zeros((q.shape[0], n_past_blk), F32)
        for part in _split3(imp0):
            imp = imp + jnp.dot(rsum_ref[...], part, preferred_element_type=F32)
        blk = lax.broadcasted_iota(jnp.int32, (n_past_blk, q.shape[0]), 0)
        forced = (blk == 0) | (blk == n_past_blk - 1)
        sel = _top_blocks(jnp.where(forced, FORCE_SCORE, imp.T), blk, min(TOP_N - 1, n_past_blk), n_past_blk)
        selb_ref[...] = jnp.where(sel, 0.0, NEG).T.astype(BF16)
        win = win_ref[0, 0].astype(BF16)
        s_w = jnp.dot(q, win[:gw], preferred_element_type=F32) + tw_ref[...]
        s_wn = lax.dot_general(q, new[:, 4 * gw:5 * gw], nt, preferred_element_type=F32) + tnew_ref[...]
        m_w = jnp.maximum(jnp.max(s_w, axis=1, keepdims=True), jnp.max(s_wn, axis=1, keepdims=True))
        p_w = jnp.exp(s_w - m_w)
        p_wn = jnp.exp(s_wn - m_w)
        l_w = jnp.sum(p_w, axis=1, keepdims=True) + jnp.sum(p_wn, axis=1, keepdims=True)
        o_w = (lax.dot_general(p_w.astype(BF16), win[gw:], nt, preferred_element_type=F32)
               + jnp.dot(p_wn.astype(BF16), new[:, 5 * gw:6 * gw], preferred_element_type=F32))
        ow_ref[...] = o_w / l_w
        s_n = lax.dot_general(q, new[:, 2 * gw:3 * gw], nt, preferred_element_type=F32) + tnew_ref[...]
        m_n = jnp.max(s_n, axis=1, keepdims=True)
        p_n = jnp.exp(s_n - m_n)
        m_ref[...] = m_n
        l_ref[...] = jnp.sum(p_n, axis=1, keepdims=True)
        acc_ref[...] = jnp.dot(p_n.astype(BF16), new[:, 3 * gw:4 * gw], preferred_element_type=F32)

    k_t = jnp.concatenate([r[0, 0] for r in k_refs], axis=1).astype(BF16)
    v_t = jnp.concatenate([r[0, 0] for r in v_refs], axis=1).astype(BF16)
    s = (jnp.dot(q, k_t, preferred_element_type=F32)
         + jnp.dot(selb_ref[...], e_ref[:, pl.ds(pl.multiple_of(i * tile, tile), tile)], preferred_element_type=F32))
    s = s + jnp.where(i == n_steps - 1, tnear_ref[...], 0.0)
    m_old = m_ref[...]
    m_new = jnp.maximum(m_old, jnp.max(s, axis=1, keepdims=True))
    alpha = jnp.exp(m_old - m_new)
    p = jnp.exp(s - m_new)
    m_ref[...] = m_new
    l_ref[...] = alpha * l_ref[...] + jnp.sum(p, axis=1, keepdims=True)
    acc_ref[...] = alpha * acc_ref[...] + lax.dot_general(p.astype(BF16), v_t, nt, preferred_element_type=F32)

    @pl.when(i == n_steps - 1)
    def _():
        gt = gt_ref[0]
        o = gt[:, 0:1] * oc_ref[...] + gt[:, 1:2] * (acc_ref[...] / l_ref[...]) + gt[:, 2:3] * ow_ref[...]
        for gi in range(g):
            for r in range(r_heads):
                h = gi * r_heads + r
                o_ref[0, :, h * hd:(h + 1) * hd] = o[h * t:(h + 1) * t, gi * hd:(gi + 1) * hd].astype(BF16)


def _nsa_attn_sample(q_bd, gates, kct, vc, new_pad, win_t, layer, cache_t, page_table, tabs, t):
    bsz, n_pages = page_table.shape
    g, r_heads, hd = N_KV_HEADS, GQA_R, HEAD_DIM
    gw = g * hd
    lanes = g * r_heads * t
    pp = SEL_PAGES_PER_STEP
    n_past_blk = n_pages * PAGE_SIZE // SEL_LEN
    kern = functools.partial(_nsa_attn_sample_kernel, t=t, n_past_blk=n_past_blk)
    per_b = lambda a: pl.BlockSpec((1,) + a.shape[1:], lambda b, i, pt: (b, 0, 0))
    full = lambda a: pl.BlockSpec(a.shape, lambda b, i, pt: (0, 0))
    page_spec = lambda fblk, k: pl.BlockSpec((1, 1, gw, PAGE_SIZE),
                                             lambda b, i, pt: (layer, pt[b, i * pp + k], fblk, 0))
    return pl.pallas_call(
        kern,
        grid_spec=pltpu.PrefetchScalarGridSpec(
            num_scalar_prefetch=1, grid=(bsz, n_pages // pp),
            in_specs=[per_b(q_bd), per_b(gates), per_b(kct), per_b(vc), per_b(new_pad),
                      pl.BlockSpec((1, 1) + win_t.shape[2:], lambda b, i, pt: (layer, b, 0, 0))]
                     + [full(a) for a in tabs]
                     + [page_spec(2, k) for k in range(pp)] + [page_spec(3, k) for k in range(pp)],
            out_specs=pl.BlockSpec((1, t, g * r_heads * hd), lambda b, i, pt: (b, 0, 0)),
            scratch_shapes=[pltpu.VMEM((lanes, 1), F32), pltpu.VMEM((lanes, 1), F32), pltpu.VMEM((lanes, gw), F32),
                            pltpu.VMEM((lanes, n_past_blk), BF16), pltpu.VMEM((lanes, gw), F32),
                            pltpu.VMEM((lanes, gw), F32)]),
        out_shape=jax.ShapeDtypeStruct((bsz, t, g * r_heads * hd), BF16),
        compiler_params=_cparams(2),
        name="nsa_attn_sample",
    )(page_table, q_bd, gates, kct, vc, new_pad, win_t, *tabs, *([cache_t] * (2 * pp)))


def _sample_tables(rel_bias, pos0, t, nc):
    g, r_heads = N_KV_HEADS, GQA_R
    lanes = g * r_heads * t
    tq = pos0 + np.arange(t)[:, None]

    def table(kpos, extra_valid=True):
        dist = tq - kpos[None, :]
        tab = _bias_table(rel_bias, dist, (dist >= 0) & extra_valid)
        return tab.reshape(lanes, kpos.shape[0])

    tc = table(CMP_STRIDE * np.arange(nc) + CMP_LEN - 1)
    tnear = table(pos0 - Q_BLOCK + np.arange(Q_BLOCK))
    tnear = jnp.pad(tnear, ((0, 0), (SEL_PAGES_PER_STEP * PAGE_SIZE - Q_BLOCK, 0)))
    new_pos = pos0 + np.arange(128)
    tnew = table(new_pos, (new_pos < pos0 + t)[None, :])
    kw = pos0 - WINDOW + np.arange(WINDOW)
    tw = table(kw, (tq - kw[None, :]) <= WINDOW)
    n_past_blk = pos0 // SEL_LEN
    jb = np.arange(n_past_blk)[None, :]
    n = np.arange(nc)[:, None]
    pool = jnp.asarray(((n >= CMP_PER_SEL * jb - 1) & (n <= CMP_PER_SEL * jb + CMP_PER_SEL - 1)).astype(np.float32),
                       BF16)
    e = jnp.asarray((np.arange(pos0)[None, :] // SEL_LEN == np.arange(n_past_blk)[:, None]).astype(np.float32), BF16)
    li = np.arange(lanes)
    same = (li[:, None] // (r_heads * t) == li[None, :] // (r_heads * t)) & (li[:, None] % t == li[None, :] % t)
    rsum = jnp.asarray(same.astype(np.float32), BF16)
    return tc, tnear, tnew, tw, pool, e, rsum


def _nsa_sample(xs2d, bsz, t, layer, cache_t, page_table, win_t, wts, tabs):
    g, r_heads, hd = N_KV_HEADS, GQA_R, HEAD_DIM
    gw = g * hd
    q, kv, kvb, gt = _nsa_proj(xs2d, wts["wq"], wts["wkv"], wts["wg"], tm=xs2d.shape[0])
    chunks = _gather_cmp(cache_t, layer, page_table)
    kct, vc = _nsa_cmp(chunks, wts["w1"], wts["pe_term"], wts["w2k"].T, wts["w2v"], transposed=False)
    q5 = q.reshape(bsz, t, g, r_heads, hd)
    q_bd = jnp.einsum('btgrd,gh->bgrthd', q5, jnp.eye(g, dtype=q.dtype)).reshape(bsz, g * r_heads * t, gw)
    gates = jnp.pad(gt[:, :3 * g * r_heads].reshape(bsz, t, g * r_heads, 3).transpose(0, 2, 1, 3)
                    .reshape(bsz, g * r_heads * t, 3), ((0, 0), (0, 0), (0, 125)))
    new_pad = jnp.pad(kvb.reshape(bsz, t, 6 * gw), ((0, 0), (0, 128 - t), (0, 0)))
    o = _nsa_attn_sample(q_bd, gates, kct, vc, new_pad, win_t, layer, cache_t, page_table, tabs, t)
    return o.reshape(bsz * t, g * r_heads * hd), kv


def _nsa_weights(w_in, cmp_pe, cmp_w1, cmp_b1, cmp_w2):
    d = w_in.shape[0]
    g, r_heads, hd = N_KV_HEADS, GQA_R, HEAD_DIM
    qd, kvd = g * r_heads * hd, 6 * g * hd
    wq = (w_in[:, :qd] * hd ** -0.5).astype(BF16)
    wq_pad = jnp.pad(wq.reshape(d, g * r_heads, hd), ((0, 0), (0, 0), (0, 128 - hd))).reshape(d, g * r_heads * 128)
    wkv = w_in[:, qd:qd + kvd].astype(BF16)
    wg = jnp.pad(w_in[:, qd + kvd:], ((0, 0), (0, 128 - 3 * g * r_heads))).astype(BF16)
    w1r = cmp_w1.reshape(2, CMP_LEN, hd, CMP_HID)
    w1 = jnp.concatenate([w1r[:, :CMP_STRIDE], w1r[:, CMP_STRIDE:]], axis=-1).astype(BF16)
    w1 = w1.reshape(2, CMP_STRIDE // CMP_PACK, CMP_PACK * hd, 2 * CMP_HID)
    pe_term = jnp.einsum('csd,csdh->ch', cmp_pe, w1r) + cmp_b1
    return dict(wq=wq, wq_pad=wq_pad, wkv=wkv, wg=wg, w1=w1, pe_term=pe_term,
                w2k=cmp_w2[0].astype(BF16), w2v=cmp_w2[1].astype(BF16))


def _nsa_prompt(x2d, bsz, seq, wts, tables):
    g, r_heads, hd = N_KV_HEADS, GQA_R, HEAD_DIM
    m = bsz * seq
    nc, nb = seq // CMP_STRIDE, seq // SEL_LEN
    q_pad, kv, kv_chunks, gt = _nsa_proj(x2d, wts["wq_pad"], wts["wkv"], wts["wg"], tm=512, chunk_seq=seq)
    w2k_t = jnp.pad(wts["w2k"].T, ((0, 128 - hd), (0, 0)))
    w2v_p = jnp.pad(wts["w2v"], ((0, 0), (0, 128 - hd)))
    kct, vc = _nsa_cmp(kv_chunks, wts["w1"], wts["pe_term"], w2k_t, w2v_p, transposed=True)
    kst, kwt, vs, vw = _nsa_prep(kv.reshape(bsz, seq, kv.shape[1]), nb)
    gates_g = jnp.pad(gt[:, :3 * g * r_heads].reshape(m, g, 3 * r_heads).transpose(1, 0, 2),
                      ((0, 0), (0, 0), (0, 128 - 3 * r_heads)))
    o = _nsa_attn_prompt(q_pad, gates_g, kct, vc, kst, vs, kwt, vw, *tables, bsz, seq)
    return o, kv


def kernel(x_prompt, x_sample, cache_nsa, state_nsa_win, state_pool, state_ffn, page_table, p_prompt, p_sample,
           rel_bias, nsa_w_in, nsa_cmp_pe, nsa_cmp_w1, nsa_cmp_b1, nsa_cmp_w2, nsa_w_out, pool_w, pool_b,
           pool_scale, ffn_w_up, ffn_conv_w, ffn_conv_b, ffn_w_down, ln_g, ln_b, ple_w_proj, ple_w_gate,
           ple_b_gate):
    bsz, seq, d = x_prompt.shape
    db, dt, _ = x_sample.shape
    depth = ffn_w_up.shape[0]
    alpha = (2.0 * depth) ** 0.25
    g, hd = N_KV_HEADS, HEAD_DIM
    gw = g * hd
    past_len = page_table.shape[1] * PAGE_SIZE
    mp, ms = bsz * seq, db * dt
    assert g * GQA_R * dt == 128 and past_len % SEL_LEN == 0 and seq % PADL == 0
    assert page_table.shape[1] % SEL_PAGES_PER_STEP == 0 and page_table.shape[1] % CMP_PAGES_PER_STEP == 0
    assert state_nsa_win.shape[2] == WINDOW and seq >= WINDOW and state_pool.shape[2] == POOL_HIST

    xp = x_prompt.reshape(mp, d)
    xs = x_sample.reshape(ms, d)
    tables_p = _prompt_tables(rel_bias, seq // SEL_LEN, seq // CMP_STRIDE)
    tables_s = _sample_tables(rel_bias, past_len, dt, past_len // CMP_STRIDE)
    cache_t = jnp.transpose(cache_nsa, (0, 1, 3, 4, 5, 2)).reshape(cache_nsa.shape[0], cache_nsa.shape[1], 4 * gw,
                                                                   PAGE_SIZE)
    win_t = jnp.transpose(state_nsa_win, (0, 1, 3, 4, 5, 2)).reshape(state_nsa_win.shape[0], db, 2 * gw, WINDOW)
    nsa_rp, nsa_rs, nsa_wp, nsa_ws, pool_hp, pool_hs, ffn_hp, ffn_hs = ([] for _ in range(8))
    for i in range(depth):
        j = i // 2
        g0, b0, g1, b1 = ln_g[i, 0][None], ln_b[i, 0][None], ln_g[i, 1][None], ln_b[i, 1][None]
        if i % 2 == 0:
            wts = _nsa_weights(nsa_w_in[j], nsa_cmp_pe[j], nsa_cmp_w1[j], nsa_cmp_b1[j], nsa_cmp_w2[j])
            o_p, kv_p = _nsa_prompt(xp, bsz, seq, wts, tables_p)
            o_s, kv_s = _nsa_sample(xs, db, dt, j, cache_t, page_table, win_t, wts, tables_s)
            w_out = nsa_w_out[j].astype(BF16)
            xp = _linear_ln(o_p, w_out, xp, g0, b0, alpha, tm=512)
            xs = _linear_ln(o_s, w_out, xs, g0, b0, alpha, tm=ms)
            kv_p3 = kv_p.reshape(bsz, seq, 6, g, hd)
            kv_s3 = kv_s.reshape(db, dt, 6, g, hd)
            nsa_rp.append(kv_p3[:, :, :4])
            nsa_rs.append(kv_s3[:, :, :4])
            nsa_wp.append(kv_p3[:, seq - WINDOW:, 4:])
            nsa_ws.append(jnp.concatenate([state_nsa_win[j][:, dt:], kv_s3[:, :, 4:]], axis=1))
        else:
            xp3 = xp.reshape(bsz, seq, d)
            xe = jnp.concatenate([state_pool[j], xs.reshape(db, dt, d)], axis=1)
            pool_hp.append(xp3[:, seq - POOL_HIST:])
            pool_hs.append(xe[:, xe.shape[1] - POOL_HIST:])
            pw = pool_w[j].astype(BF16)
            xp = _pool_ln_prompt(xp3, pw, pool_b[j][None], pool_scale[j][None], g0, b0, alpha).reshape(mp, d)
            xs = _pool_ln_sample(xe, dt, past_len, pw, pool_b[j][None], pool_scale[j][None], g0, b0,
                                 alpha).reshape(ms, d)
        w_up = ffn_w_up[i].astype(BF16)
        w_down = ffn_w_down[i].astype(BF16)
        cw, cb = ffn_conv_w[i], ffn_conv_b[i][None]
        h_p = _ffn_up_prompt(xp, w_up, cw, cb, seq)
        tail = xp.reshape(bsz, seq, d)[:, seq - 8:].reshape(bsz * 8, d)
        u_tail = _matmul(tail, w_up, tm=bsz * 8).reshape(bsz, 8, -1)
        ffn_hp.append(u_tail[:, 8 - (CONV_W - 1):])
        u_s = _matmul(xs, w_up, tm=ms).reshape(db, dt, -1)
        ue = jnp.concatenate([state_ffn[i], u_s], axis=1)
        ffn_hs.append(ue[:, ue.shape[1] - (CONV_W - 1):])
        h_s = _ffn_gate_sample(ue, cw, cb, dt).reshape(ms, -1)
        ple = (ple_w_gate[i].astype(BF16), ple_b_gate[i][None], ple_w_proj[i].astype(BF16))
        xp = _ffn_down_ln_ple(h_p, w_down, xp, g1, b1, p_prompt[i].reshape(mp, -1), *ple, alpha, 512)
        xs = _ffn_down_ln_ple(h_s, w_down, xs, g1, b1, p_sample[i].reshape(ms, -1), *ple, alpha, ms)
    return (xp.reshape(bsz, seq, d), xs.reshape(db, dt, d),
            jnp.stack(nsa_rp), jnp.stack(nsa_rs), jnp.stack(nsa_wp), jnp.stack(nsa_ws),
            jnp.stack(pool_hp), jnp.stack(pool_hs), jnp.stack(ffn_hp), jnp.stack(ffn_hs))
```

```python
import functools
import math

import numpy as np
import jax
import jax.numpy as jnp
from jax import lax
from jax.experimental import pallas as pl
from jax.experimental.pallas import tpu as pltpu

N_KV_HEADS = 4
GQA_R = 4
HEAD_DIM = 64
CMP_LEN = 32
CMP_STRIDE = 16
CMP_HID = 256
CMP_PACK = 4
SEL_LEN = 64
CMP_PER_SEL = SEL_LEN // CMP_STRIDE
TOP_N = 16
WINDOW = 512
Q_BLOCK = 128
FORCE_SCORE = 1.0e4
REL_BUCKETS = 32
REL_MAX_DIST = 128
POOL_WINDOWS = (2, 4, 8, 16)
POOL_MAX = max(POOL_WINDOWS)
POOL_HIST = POOL_MAX - 1
CONV_W = 3
LN_EPS = 1e-5
PAGE_SIZE = 128

NEG = -1.0e30
NEG_VALID = -1.0e29
BF16 = jnp.bfloat16
F32 = jnp.float32

VMEM_LIMIT_BYTES = 56 * 1024 * 1024


def _cparams(n_grid):
    return pltpu.CompilerParams(dimension_semantics=("arbitrary",) * n_grid,
                                vmem_limit_bytes=VMEM_LIMIT_BYTES)


def _layer_norm(y, g, b):
    mu = jnp.mean(y, axis=-1, keepdims=True)
    yc = y - mu
    var = jnp.mean(yc * yc, axis=-1, keepdims=True)
    return yc * lax.rsqrt(var + LN_EPS) * g + b


FFN_TN = 256
HALO = 16


def _ffn_up_prompt_kernel(x_ref, xh_ref, w_ref, cw_ref, cb_ref, h_ref, xe_ref, ue_ref, *,
                          tm, d_ff, tiles_per_batch):
    i = pl.program_id(0)
    first = (i % tiles_per_batch) == 0
    halo = xh_ref[...]
    xe_ref[0:HALO, :] = jnp.where(first, jnp.zeros_like(halo), halo).astype(BF16)
    xe_ref[HALO:HALO + tm, :] = x_ref[...].astype(BF16)
    xe = xe_ref[...]
    for j in range(d_ff // FFN_TN):
        c_parts = []
        for half in range(2):
            c0 = half * d_ff + j * FFN_TN
            ue_ref[...] = jnp.dot(xe, w_ref[:, c0:c0 + FFN_TN], preferred_element_type=F32)
            cw = cw_ref[:, c0:c0 + FFN_TN]
            c = cb_ref[:, c0:c0 + FFN_TN]
            for k in range(CONV_W):
                off = HALO - (CONV_W - 1) + k
                c = c + cw[k:k + 1, :] * ue_ref[off:off + tm, :]
            c_parts.append(c)
        h_ref[:, j * FFN_TN:(j + 1) * FFN_TN] = (jax.nn.gelu(c_parts[1]) * c_parts[0]).astype(BF16)


def _ffn_up_prompt(x, w_up, conv_w, conv_b, seq, tm=512):
    m, d = x.shape
    d_ff = w_up.shape[1] // 2
    tpb = seq // tm
    kern = functools.partial(_ffn_up_prompt_kernel, tm=tm, d_ff=d_ff, tiles_per_batch=tpb)
    return pl.pallas_call(
        kern,
        grid=(m // tm,),
        in_specs=[
            pl.BlockSpec((tm, d), lambda i: (i, 0)),
            pl.BlockSpec((HALO, d), lambda i: (jnp.maximum(i * (tm // HALO) - 1, 0), 0)),
            pl.BlockSpec(w_up.shape, lambda i: (0, 0)),
            pl.BlockSpec(conv_w.shape, lambda i: (0, 0)),
            pl.BlockSpec(conv_b.shape, lambda i: (0, 0)),
        ],
        out_specs=pl.BlockSpec((tm, d_ff), lambda i: (i, 0)),
        out_shape=jax.ShapeDtypeStruct((m, d_ff), BF16),
        scratch_shapes=[pltpu.VMEM((HALO + tm, d), BF16), pltpu.VMEM((HALO + tm, FFN_TN), F32)],
        compiler_params=_cparams(1),
        name="ffn_up_prompt",
    )(x, x, w_up, conv_w, conv_b)


def _ffn_gate_sample_kernel(ue_ref, cw_ref, cb_ref, h_ref, *, t, d_ff):
    for j in range(d_ff // FFN_TN):
        c_parts = []
        for half in range(2):
            c0 = half * d_ff + j * FFN_TN
            cw = cw_ref[:, c0:c0 + FFN_TN]
            c = cb_ref[:, c0:c0 + FFN_TN][None]
            for k in range(CONV_W):
                c = c + cw[k:k + 1, :][None] * ue_ref[:, k:k + t, c0:c0 + FFN_TN]
            c_parts.append(c)
        h_ref[:, :, j * FFN_TN:(j + 1) * FFN_TN] = (jax.nn.gelu(c_parts[1]) * c_parts[0]).astype(BF16)


def _ffn_gate_sample(ue, conv_w, conv_b, t):
    b, te, n2 = ue.shape
    d_ff = n2 // 2
    bb = 8
    kern = functools.partial(_ffn_gate_sample_kernel, t=t, d_ff=d_ff)
    return pl.pallas_call(
        kern,
        grid=(b // bb,),
        in_specs=[
            pl.BlockSpec((bb, te, n2), lambda i: (i, 0, 0)),
            pl.BlockSpec(conv_w.shape, lambda i: (0, 0)),
            pl.BlockSpec(conv_b.shape, lambda i: (0, 0)),
        ],
        out_specs=pl.BlockSpec((bb, t, d_ff), lambda i: (i, 0, 0)),
        out_shape=jax.ShapeDtypeStruct((b, t, d_ff), BF16),
        compiler_params=_cparams(1),
        name="ffn_gate_sample",
    )(ue, conv_w, conv_b)


def _matmul_kernel(x_ref, w_ref, o_ref):
    o_ref[...] = jnp.dot(x_ref[...].astype(BF16), w_ref[...], preferred_element_type=F32).astype(o_ref.dtype)


def _matmul(x, w, tm, tn=None, out_dtype=F32):
    m, k = x.shape
    n = w.shape[1]
    tn = n if tn is None else tn
    return pl.pallas_call(
        _matmul_kernel,
        grid=(m // tm, n // tn),
        in_specs=[pl.BlockSpec((tm, k), lambda i, j: (i, 0)),
                  pl.BlockSpec((k, tn), lambda i, j: (0, j))],
        out_specs=pl.BlockSpec((tm, tn), lambda i, j: (i, j)),
        out_shape=jax.ShapeDtypeStruct((m, n), out_dtype),
        compiler_params=_cparams(2),
        name="matmul",
    )(x, w)


def _linear_ln_kernel(a_ref, w_ref, r_ref, g_ref, b_ref, o_ref, *, alpha):
    f = jnp.dot(a_ref[...].astype(BF16), w_ref[...], preferred_element_type=F32)
    o_ref[...] = _layer_norm(alpha * r_ref[...] + f, g_ref[...], b_ref[...])


def _linear_ln(a, w, resid, g, b, alpha, tm):
    m, k = a.shape
    d = w.shape[1]
    kern = functools.partial(_linear_ln_kernel, alpha=alpha)
    return pl.pallas_call(
        kern,
        grid=(m // tm,),
        in_specs=[pl.BlockSpec((tm, k), lambda i: (i, 0)),
                  pl.BlockSpec((k, d), lambda i: (0, 0)),
                  pl.BlockSpec((tm, d), lambda i: (i, 0)),
                  pl.BlockSpec((1, d), lambda i: (0, 0)),
                  pl.BlockSpec((1, d), lambda i: (0, 0))],
        out_specs=pl.BlockSpec((tm, d), lambda i: (i, 0)),
        out_shape=jax.ShapeDtypeStruct((m, d), F32),
        compiler_params=_cparams(1),
        name="linear_ln",
    )(a, w, resid, g, b)


def _ffn_down_kernel(h_ref, wd_ref, r_ref, g_ref, b_ref, p_ref, wg_ref, bg_ref, wp_ref, o_ref, *, alpha):
    f = jnp.dot(h_ref[...], wd_ref[...], preferred_element_type=F32)
    x2 = _layer_norm(alpha * r_ref[...] + f, g_ref[...], b_ref[...])
    gate = jax.nn.sigmoid(jnp.dot(x2.astype(BF16), wg_ref[...], preferred_element_type=F32) + bg_ref[...])
    inj = jnp.dot(p_ref[...].astype(BF16), wp_ref[...], preferred_element_type=F32)
    o_ref[...] = x2 + gate * inj


def _ffn_down_ln_ple(h, w_down, resid, g, b, p, w_gate, b_gate, w_proj, alpha, tm):
    m, k = h.shape
    d = w_down.shape[1]
    pd = p.shape[1]
    kern = functools.partial(_ffn_down_kernel, alpha=alpha)
    full = lambda shape: pl.BlockSpec(shape, lambda i: (0, 0))
    return pl.pallas_call(
        kern,
        grid=(m // tm,),
        in_specs=[pl.BlockSpec((tm, k), lambda i: (i, 0)), full((k, d)),
                  pl.BlockSpec((tm, d), lambda i: (i, 0)), full((1, d)), full((1, d)),
                  pl.BlockSpec((tm, pd), lambda i: (i, 0)), full((d, d)), full((1, d)), full((pd, d))],
        out_specs=pl.BlockSpec((tm, d), lambda i: (i, 0)),
        out_shape=jax.ShapeDtypeStruct((m, d), F32),
        compiler_params=_cparams(1),
        name="ffn_down_ln_ple",
    )(h, w_down, resid, g, b, p, w_gate, b_gate, w_proj)


def _pool_finish(x, diffs, w_ref, pb_ref, ps_ref, g_ref, b_ref, alpha):
    ys = [jnp.dot(dg.astype(BF16), w_ref[gi], preferred_element_type=F32) for gi, dg in enumerate(diffs)]
    y = jnp.concatenate(ys, axis=-1) + pb_ref[...]
    return _layer_norm(alpha * x + ps_ref[...] * y, g_ref[...], b_ref[...])


def _pool_prompt_kernel(x_ref, xh_ref, w_ref, pb_ref, ps_ref, g_ref, b_ref, o_ref, xe_ref, *,
                        tt, gw, alpha):
    i = pl.program_id(1)
    halo = xh_ref[0]
    xe_ref[0:POOL_MAX, :] = jnp.where(i == 0, jnp.zeros_like(halo), halo)
    xe_ref[POOL_MAX:POOL_MAX + tt, :] = x_ref[0]
    pos = i * tt + lax.broadcasted_iota(jnp.int32, (tt, 1), 0)
    x = x_ref[0]
    diffs = []
    for gi, win in enumerate(POOL_WINDOWS):
        lo = gi * gw
        tot = x[:, lo:lo + gw]
        for k in range(1, win):
            tot = tot + xe_ref[POOL_MAX - k:POOL_MAX - k + tt, lo:lo + gw]
        cnt = jnp.minimum(win, pos + 1).astype(F32)
        diffs.append(tot / cnt - x[:, lo:lo + gw])
    o_ref[0] = _pool_finish(x, diffs, w_ref, pb_ref, ps_ref, g_ref, b_ref, alpha)


def _pool_ln_prompt(x, w, pb, ps, g, b, alpha, tt=512):
    bsz, t, d = x.shape
    gw = d // len(POOL_WINDOWS)
    kern = functools.partial(_pool_prompt_kernel, tt=tt, gw=gw, alpha=alpha)
    vec = pl.BlockSpec((1, d), lambda bi, i: (0, 0))
    return pl.pallas_call(
        kern,
        grid=(bsz, t // tt),
        in_specs=[pl.BlockSpec((1, tt, d), lambda bi, i: (bi, i, 0)),
                  pl.BlockSpec((1, POOL_MAX, d), lambda bi, i: (bi, jnp.maximum(i * (tt // POOL_MAX) - 1, 0), 0)),
                  pl.BlockSpec(w.shape, lambda bi, i: (0, 0, 0)), vec, vec, vec, vec],
        out_specs=pl.BlockSpec((1, tt, d), lambda bi, i: (bi, i, 0)),
        out_shape=jax.ShapeDtypeStruct((bsz, t, d), F32),
        scratch_shapes=[pltpu.VMEM((POOL_MAX + tt, d), F32)],
        compiler_params=_cparams(2),
        name="pool_ln_prompt",
    )(x, x, w, pb, ps, g, b)


def _pool_sample_kernel(xe_ref, w_ref, pb_ref, ps_ref, g_ref, b_ref, o_ref, *, t, nh, gw, pos0, alpha):
    bb = xe_ref.shape[0]
    d = xe_ref.shape[2]
    x = xe_ref[:, nh:nh + t, :].reshape(bb * t, d)
    cnt_np = np.minimum(np.array(POOL_WINDOWS)[:, None], pos0 + np.arange(t)[None, :] + 1).astype(np.float32)
    diffs = []
    for gi, win in enumerate(POOL_WINDOWS):
        lo = gi * gw
        tot = xe_ref[:, nh:nh + t, lo:lo + gw]
        for k in range(1, win):
            tot = tot + xe_ref[:, nh - k:nh - k + t, lo:lo + gw]
        if np.all(cnt_np[gi] == win):
            mean = tot / float(win)
        else:
            t_idx = lax.broadcasted_iota(jnp.int32, (1, t, 1), 1)
            mean = tot / jnp.minimum(win, pos0 + t_idx + 1).astype(F32)
        diffs.append(mean.reshape(bb * t, gw) - x[:, lo:lo + gw])
    o_ref[...] = _pool_finish(x, diffs, w_ref, pb_ref, ps_ref, g_ref, b_ref, alpha).reshape(bb, t, d)


def _pool_ln_sample(xe, t, pos0, w, pb, ps, g, b, alpha, bb=8):
    bsz, te, d = xe.shape
    gw = d // len(POOL_WINDOWS)
    kern = functools.partial(_pool_sample_kernel, t=t, nh=te - t, gw=gw, pos0=pos0, alpha=alpha)
    vec = pl.BlockSpec((1, d), lambda i: (0, 0))
    return pl.pallas_call(
        kern,
        grid=(bsz // bb,),
        in_specs=[pl.BlockSpec((bb, te, d), lambda i: (i, 0, 0)),
                  pl.BlockSpec(w.shape, lambda i: (0, 0, 0)), vec, vec, vec, vec],
        out_specs=pl.BlockSpec((bb, t, d), lambda i: (i, 0, 0)),
        out_shape=jax.ShapeDtypeStruct((bsz, t, d), F32),
        compiler_params=_cparams(1),
        name="pool_ln_sample",
    )(xe, w, pb, ps, g, b)


def _nsa_proj_kernel(x_ref, wq_ref, wkv_ref, wg_ref, q_ref, kv_ref, kvb_ref, gt_ref, *scratch):
    xb = x_ref[...].astype(BF16)
    q_ref[...] = jnp.dot(xb, wq_ref[...], preferred_element_type=F32).astype(BF16)
    kv = jnp.dot(xb, wkv_ref[...], preferred_element_type=F32)
    kv_ref[...] = kv
    gt_ref[...] = jax.nn.sigmoid(jnp.dot(xb, wg_ref[...], preferred_element_type=F32))
    if not scratch:
        kvb_ref[...] = kv.astype(BF16)
        return
    (t_ref,) = scratch
    n_fb, tm = t_ref.shape[0], t_ref.shape[1]
    for fb in range(n_fb):
        t_ref[fb] = kv[:, fb * 128:(fb + 1) * 128]
    for s in range(CMP_STRIDE):
        for fb in range(n_fb):
            kvb_ref[0, s, :, fb * 128:(fb + 1) * 128] = (
                t_ref[fb, pl.ds(s, tm // CMP_STRIDE, stride=CMP_STRIDE), :].astype(BF16))


def _nsa_proj(x, wq, wkv, wg, tm, chunk_seq=None):
    m, d = x.shape
    nq, nkv, ng = wq.shape[1], wkv.shape[1], wg.shape[1]
    full = lambda w: pl.BlockSpec(w.shape, lambda i: (0, 0))
    row = lambda n: pl.BlockSpec((tm, n), lambda i: (i, 0))
    if chunk_seq is None:
        kvb_spec, kvb_shape, scratch = row(nkv), jax.ShapeDtypeStruct((m, nkv), BF16), []
    else:
        feat = 2 * N_KV_HEADS * HEAD_DIM
        tpb = chunk_seq // tm
        kvb_spec = pl.BlockSpec((1, CMP_STRIDE, tm // CMP_STRIDE, feat), lambda i: (i // tpb, 0, i % tpb, 0))
        kvb_shape = jax.ShapeDtypeStruct((m // chunk_seq, CMP_STRIDE, chunk_seq // CMP_STRIDE, feat), BF16)
        scratch = [pltpu.VMEM((feat // 128, tm, 128), F32)]
    return pl.pallas_call(
        _nsa_proj_kernel,
        grid=(m // tm,),
        in_specs=[row(d), full(wq), full(wkv), full(wg)],
        out_specs=[row(nq), row(nkv), kvb_spec, row(ng)],
        out_shape=[jax.ShapeDtypeStruct((m, nq), BF16), jax.ShapeDtypeStruct((m, nkv), F32),
                   kvb_shape, jax.ShapeDtypeStruct((m, ng), F32)],
        scratch_shapes=scratch,
        compiler_params=_cparams(1),
        name="nsa_proj",
    )(x, wq, wkv, wg)


def _nsa_cmp_kernel(*refs, nc, transposed):
    ns = CMP_STRIDE
    x_refs = refs[:ns]
    w1_ref, pe_ref, w2k_ref, w2v_ref, ko_ref, vo_ref = refs[ns:]
    hd = HEAD_DIM
    gw = N_KV_HEADS * hd
    last = lax.broadcasted_iota(jnp.int32, (nc, CMP_HID), 0) == nc - 1
    nt = (((1,), (1,)), ((), ()))
    for c in range(2):
        for g in range(N_KV_HEADS):
            acc = jnp.zeros((nc, 2 * CMP_HID), F32)
            lo = c * gw + g * hd
            for k in range(ns // CMP_PACK):
                xs = jnp.concatenate([x_refs[k * CMP_PACK + i][:, lo:lo + hd] for i in range(CMP_PACK)], axis=1)
                acc = acc + jnp.dot(xs, w1_ref[c, k], preferred_element_type=F32)
            h1 = jnp.where(last, 0.0, pltpu.roll(acc[:, CMP_HID:], nc - 1, 0))
            hid = jax.nn.gelu(acc[:, :CMP_HID] + h1 + pe_ref[c:c + 1, :]).astype(BF16)
            if c == 0 and transposed:
                ko_ref[0, g] = lax.dot_general(w2k_ref[...], hid, nt, preferred_element_type=F32).astype(BF16)
            elif c == 0:
                ko_ref[0, g * hd:(g + 1) * hd, :] = lax.dot_general(w2k_ref[...], hid, nt,
                                                                    preferred_element_type=F32).astype(BF16)
            elif transposed:
                vo_ref[0, g] = jnp.dot(hid, w2v_ref[...], preferred_element_type=F32).astype(BF16)
            else:
                vo_ref[0, :, g * hd:(g + 1) * hd] = jnp.dot(hid, w2v_ref[...], preferred_element_type=F32).astype(BF16)


def _nsa_cmp(xv, w1, pe_term, w2k, w2v, transposed):
    g, hd = N_KV_HEADS, HEAD_DIM
    feat = 2 * g * hd
    bsz = xv.shape[0]
    if xv.ndim == 4:
        nc = xv.shape[2]
        x_specs = [pl.BlockSpec((None, None, nc, feat), functools.partial(lambda b, s: (b, s, 0, 0), s=s))
                   for s in range(CMP_STRIDE)]
    else:
        nc = xv.shape[1]
        x_specs = [pl.BlockSpec((None, nc, feat), functools.partial(lambda b, s: (b, 0, s), s=s))
                   for s in range(CMP_STRIDE)]
    kern = functools.partial(_nsa_cmp_kernel, nc=nc, transposed=transposed)
    full = lambda w: pl.BlockSpec(w.shape, lambda b: (0,) * w.ndim)
    if transposed:
        out_specs = [pl.BlockSpec((1, g, 128, nc), lambda b: (b, 0, 0, 0)),
                     pl.BlockSpec((1, g, nc, 128), lambda b: (b, 0, 0, 0))]
        out_shape = [jax.ShapeDtypeStruct((bsz, g, 128, nc), BF16), jax.ShapeDtypeStruct((bsz, g, nc, 128), BF16)]
    else:
        out_specs = [pl.BlockSpec((1, g * hd, nc), lambda b: (b, 0, 0)), pl.BlockSpec((1, nc, g * hd), lambda b: (b, 0, 0))]
        out_shape = [jax.ShapeDtypeStruct((bsz, g * hd, nc), BF16), jax.ShapeDtypeStruct((bsz, nc, g * hd), BF16)]
    return pl.pallas_call(
        kern,
        grid=(bsz,),
        in_specs=x_specs + [full(w1), full(pe_term), full(w2k), full(w2v)],
        out_specs=out_specs,
        out_shape=out_shape,
        compiler_params=_cparams(1),
        name="nsa_cmp",
    )(*([xv] * CMP_STRIDE), w1, pe_term, w2k, w2v)


PADL = 512
FAR_TILE = 512
ATAB_LANE = 80


def _nsa_prep_kernel(kv_ref, kst_ref, kwt_ref, vs_ref, vw_ref, *, nb):
    j = pl.program_id(1)
    tl = PADL
    hd = HEAD_DIM
    gw = N_KV_HEADS * hd

    @pl.when(j == 0)
    def _():
        for g in range(N_KV_HEADS):
            kst_ref[0, g] = (lax.broadcasted_iota(jnp.int32, (128 + nb, tl), 0) == hd).astype(BF16)
            kwt_ref[0, g] = (lax.broadcasted_iota(jnp.int32, (128, tl), 0) == hd).astype(BF16)
            vs_ref[0, g] = jnp.zeros((tl, 128), BF16)
            vw_ref[0, g] = jnp.zeros((tl, 128), BF16)

    @pl.when(j > 0)
    def _():
        kst = kv_ref[0, :, 2 * gw:3 * gw].T
        kwt = kv_ref[0, :, 4 * gw:5 * gw].T
        pos = (j - 1) * tl + lax.broadcasted_iota(jnp.int32, (nb, tl), 1)
        onehot = (pos // SEL_LEN == lax.broadcasted_iota(jnp.int32, (nb, tl), 0)).astype(BF16)
        zeros = jnp.zeros((128 - hd, tl), BF16)
        lane = lax.broadcasted_iota(jnp.int32, (tl, 128), 1)
        tail = (lane == hd).astype(F32)
        for g in range(N_KV_HEADS):
            kst_ref[0, g, 0:hd, :] = kst[g * hd:(g + 1) * hd].astype(BF16)
            kst_ref[0, g, hd:128, :] = zeros
            kst_ref[0, g, 128:128 + nb, :] = onehot
            kwt_ref[0, g, 0:hd, :] = kwt[g * hd:(g + 1) * hd].astype(BF16)
            kwt_ref[0, g, hd:128, :] = zeros
            for stream, ref in ((3, vs_ref), (5, vw_ref)):
                a = kv_ref[0, :, stream * gw + (g // 2) * 128:stream * gw + (g // 2 + 1) * 128]
                if g % 2:
                    a = pltpu.roll(a, hd, 1)
                ref[0, g] = jnp.where(lane < hd, a, tail).astype(BF16)


def _nsa_prep(kv, nb):
    bsz, t, n = kv.shape
    g = N_KV_HEADS
    tl = PADL
    tp = PADL + t
    kern = functools.partial(_nsa_prep_kernel, nb=nb)
    kspec = lambda rows: pl.BlockSpec((1, g, rows, tl), lambda b, j: (b, 0, 0, j))
    vspec = pl.BlockSpec((1, g, tl, 128), lambda b, j: (b, 0, j, 0))
    return pl.pallas_call(
        kern,
        grid=(bsz, tp // tl),
        in_specs=[pl.BlockSpec((1, tl, n), lambda b, j: (b, jnp.maximum(j - 1, 0), 0))],
        out_specs=[kspec(128 + nb), kspec(128), vspec, vspec],
        out_shape=[jax.ShapeDtypeStruct((bsz, g, 128 + nb, tp), BF16), jax.ShapeDtypeStruct((bsz, g, 128, tp), BF16),
                   jax.ShapeDtypeStruct((bsz, g, tp, 128), BF16), jax.ShapeDtypeStruct((bsz, g, tp, 128), BF16)],
        compiler_params=_cparams(2),
        name="nsa_prep",
    )(kv)


def _split3(x):
    a = x.astype(BF16)
    r = x - a.astype(F32)
    b = r.astype(BF16)
    return a, b, (r - b.astype(F32)).astype(BF16)


def _top_blocks(score, blk, n_sel, n_rows):
    sel = jnp.zeros(score.shape, jnp.bool_)
    for _ in range(n_sel):
        m = jnp.max(score, axis=0, keepdims=True)
        first = jnp.min(jnp.where(score == m, blk, n_rows), axis=0, keepdims=True)
        pick = blk == first
        sel = sel | pick
        score = jnp.where(pick, -3.0e38, score)
    return sel


def _nsa_attn_prompt_kernel(q_ref, gt_ref, kct_ref, vc_ref, kst_ref, vs_ref, kwt_ref, vw_ref,
                            wtab_ref, stab_ref, atab_ref, poolt_ref, o_ref,
                            kc_ref, sw_ref, sa_ref, sb_ref, pa_ref, pb_ref, ala_ref, alb_ref, m_ref, acc_ref, *, nb, nc):
    qi = pl.program_id(2)
    t0 = qi * Q_BLOCK
    r_heads = GQA_R
    hd = HEAD_DIM
    rows = r_heads * Q_BLOCK
    nw = WINDOW + Q_BLOCK
    lane = lax.broadcasted_iota(jnp.int32, (Q_BLOCK, 128), 1)
    q = jnp.concatenate([jnp.where(lane == hd, NEG, q_ref[:, r * 128:(r + 1) * 128]) for r in range(r_heads)],
                        axis=0)

    n_idx = lax.broadcasted_iota(jnp.int32, (48, nc), 1)
    j_idx = lax.broadcasted_iota(jnp.int32, (48, nc), 0)
    rel = n_idx - (qi * (Q_BLOCK // CMP_STRIDE) - 8)
    shift = (((j_idx < 32) & (rel == jnp.where(j_idx < 16, j_idx, j_idx - 16)))
             | ((j_idx == 32) & (rel >= 16))).astype(F32).astype(BF16)
    kc_ref[0:ATAB_LANE, :] = kct_ref[0, 0, 0:ATAB_LANE, :]
    kc_ref[ATAB_LANE:128, :] = shift
    lane_r = lax.broadcasted_iota(jnp.int32, (rows, 128), 1)
    s_c = jnp.dot(jnp.where(lane_r >= ATAB_LANE, atab_ref[0], q), kc_ref[...], preferred_element_type=F32)
    m_c = jnp.max(s_c, axis=1, keepdims=True)
    e_c = jnp.exp(s_c - m_c)
    inv_c = jnp.where(m_c > NEG_VALID, 1.0 / jnp.maximum(jnp.sum(e_c, axis=1, keepdims=True), 1e-30), 0.0)
    p_c = e_c * inv_c
    o_c = jnp.dot(p_c.astype(BF16), vc_ref[0, 0], preferred_element_type=F32)

    grp = p_c[0:Q_BLOCK]
    for r in range(1, r_heads):
        grp = grp + p_c[r * Q_BLOCK:(r + 1) * Q_BLOCK]
    imp_t = jnp.zeros((nb, Q_BLOCK), F32)
    for part in _split3(grp):
        imp_t = imp_t + lax.dot_general(poolt_ref[...], part, (((1,), (1,)), ((), ())),
                                        preferred_element_type=F32)
    blk = lax.broadcasted_iota(jnp.int32, (nb, Q_BLOCK), 0)
    tq = t0 + lax.broadcasted_iota(jnp.int32, (nb, Q_BLOCK), 1)
    cur = tq // SEL_LEN
    forced = (blk == 0) | (blk == cur) | (blk == cur - 1)
    score = jnp.where(blk * SEL_LEN > tq, -1.0, jnp.where(forced, -3.0e38, imp_t))
    sel = forced | _top_blocks(score, blk, min(TOP_N, nb) - 3, nb)
    selb = jnp.where(sel, 0.0, NEG).T.astype(BF16)
    q_aug = jnp.concatenate([q, jnp.concatenate([selb] * r_heads, axis=0)], axis=1)

    c_w = pl.multiple_of(t0 + PADL - WINDOW, Q_BLOCK)
    sw_ref[...] = jnp.dot(q, kwt_ref[0, 0, :, pl.ds(c_w, nw)], preferred_element_type=F32) + wtab_ref[0]

    s = jnp.dot(q_aug, kst_ref[0, 0, :, pl.ds(c_w, nw)], preferred_element_type=F32) + stab_ref[0]
    m_s = jnp.max(s, axis=1, keepdims=True)
    m_ref[...] = m_s
    acc_ref[...] = jnp.dot(jnp.exp(s - m_s).astype(BF16), vs_ref[0, 0, pl.ds(c_w, nw), :],
                           preferred_element_type=F32)

    n_far = jnp.maximum((t0 + PADL - WINDOW + FAR_TILE - 1) // FAR_TILE - PADL // FAR_TILE, 0)

    def tile_start(k):
        return pl.multiple_of(c_w - FAR_TILE * (jnp.clip(k, 0, n_far - 1) + 1), Q_BLOCK)

    def scores(k, s_out):
        s_out[...] = jnp.dot(q_aug, kst_ref[0, 0, :, pl.ds(tile_start(k), FAR_TILE)], preferred_element_type=F32)

    def pv_update(k, p_in, al_in):
        acc_ref[...] = al_in[...] * acc_ref[...] + jnp.dot(p_in[...], vs_ref[0, 0, pl.ds(tile_start(k), FAR_TILE), :],
                                                          preferred_element_type=F32)

    def stage(k, s_cur, s_nxt, p_cur, al_cur, p_prev, al_prev):
        scores(k + 1, s_nxt)
        pv_update(k - 1, p_prev, al_prev)
        m_old = m_ref[...]
        m_new = jnp.maximum(m_old, jnp.max(s_cur[...], axis=1, keepdims=True))
        al_cur[...] = jnp.exp(m_old - m_new)
        p_cur[...] = jnp.exp(s_cur[...] - m_new).astype(BF16)
        m_ref[...] = m_new

    @pl.when(n_far > 0)
    def _():
        scores(0, sa_ref)
        pb_ref[...] = jnp.zeros_like(pb_ref)
        alb_ref[...] = jnp.ones_like(alb_ref)

        def pair(j, _):
            stage(2 * j, sa_ref, sb_ref, pa_ref, ala_ref, pb_ref, alb_ref)

            @pl.when(2 * j + 1 < n_far)
            def _():
                stage(2 * j + 1, sb_ref, sa_ref, pb_ref, alb_ref, pa_ref, ala_ref)
            return 0

        lax.fori_loop(0, (n_far + 1) // 2, pair, 0)

        @pl.when(n_far % 2 == 1)
        def _():
            pv_update(n_far - 1, pa_ref, ala_ref)

        @pl.when(n_far % 2 == 0)
        def _():
            pv_update(n_far - 1, pb_ref, alb_ref)

    acc = acc_ref[...]
    o_s = acc[:, :hd] * (1.0 / acc[:, hd:hd + 1])

    s_w = sw_ref[...]
    m_w = jnp.max(s_w, axis=1, keepdims=True)
    acc_w = jnp.dot(jnp.exp(s_w - m_w).astype(BF16), vw_ref[0, 0, pl.ds(c_w, nw), :], preferred_element_type=F32)
    o_w = acc_w[:, :hd] * (1.0 / acc_w[:, hd:hd + 1])

    gt = gt_ref[0]
    for r in range(r_heads):
        sl = slice(r * Q_BLOCK, (r + 1) * Q_BLOCK)
        o = (gt[:, 3 * r:3 * r + 1] * o_c[sl, :hd] + gt[:, 3 * r + 1:3 * r + 2] * o_s[sl]
             + gt[:, 3 * r + 2:3 * r + 3] * o_w[sl])
        o_ref[:, r * hd:(r + 1) * hd] = o.astype(BF16)


def _nsa_attn_prompt(q_pad, gates_g, kct, vc, kst, vs, kwt, vw, wtab, stab, atab, poolt, bsz, seq):
    g, r_heads, hd = N_KV_HEADS, GQA_R, HEAD_DIM
    nq = seq // Q_BLOCK
    nb, nc = poolt.shape
    tp = PADL + seq
    kern = functools.partial(_nsa_attn_prompt_kernel, nb=nb, nc=nc)
    rows = r_heads * Q_BLOCK
    per_bg = lambda shape: pl.BlockSpec((1, 1) + shape, lambda b, gi, i: (b, gi, 0, 0))
    per_g = lambda shape: pl.BlockSpec((1,) + shape, lambda b, gi, i: (gi, 0, 0))
    return pl.pallas_call(
        kern,
        grid=(bsz, g, nq),
        in_specs=[pl.BlockSpec((Q_BLOCK, r_heads * 128), lambda b, gi, i: (b * nq + i, gi)),
                  pl.BlockSpec((1, Q_BLOCK, 128), lambda b, gi, i: (gi, b * nq + i, 0)),
                  per_bg((128, nc)), per_bg((nc, 128)),
                  per_bg((128 + nb, tp)), per_bg((tp, 128)),
                  per_bg((128, tp)), per_bg((tp, 128)),
                  per_g((r_heads * Q_BLOCK, WINDOW + Q_BLOCK)), per_g((r_heads * Q_BLOCK, WINDOW + Q_BLOCK)),
                  per_g((r_heads * Q_BLOCK, 128)),
                  pl.BlockSpec((nb, nc), lambda b, gi, i: (0, 0))],
        out_specs=pl.BlockSpec((Q_BLOCK, r_heads * hd), lambda b, gi, i: (b * nq + i, gi)),
        out_shape=jax.ShapeDtypeStruct((bsz * seq, g * r_heads * hd), BF16),
        scratch_shapes=[pltpu.VMEM((128, nc), BF16), pltpu.VMEM((rows, WINDOW + Q_BLOCK), F32),
                        pltpu.VMEM((rows, FAR_TILE), F32), pltpu.VMEM((rows, FAR_TILE), F32),
                        pltpu.VMEM((rows, FAR_TILE), BF16), pltpu.VMEM((rows, FAR_TILE), BF16),
                        pltpu.VMEM((rows, 1), F32), pltpu.VMEM((rows, 1), F32),
                        pltpu.VMEM((rows, 1), F32), pltpu.VMEM((rows, 128), F32)],
        compiler_params=_cparams(3),
        name="nsa_attn_prompt",
    )(q_pad, gates_g, kct, vc, kst, vs, kwt, vw, wtab, stab, atab, poolt)


def _rel_bucket_np(dist):
    n = np.maximum(dist, 0)
    exact = REL_BUCKETS // 2
    nf = np.maximum(n, 1).astype(np.float32)
    large = exact + (np.log(nf / np.float32(exact)) / np.float32(math.log(REL_MAX_DIST / exact))
                     * np.float32(REL_BUCKETS - exact)).astype(np.int32)
    return np.where(n < exact, n, np.minimum(large, REL_BUCKETS - 1))


FAR_DIST = Q_BLOCK - CMP_STRIDE + 1
assert np.all(_rel_bucket_np(np.arange(FAR_DIST, 1 << 16)) == REL_BUCKETS - 1)


def _bias_table(rel_bias, dist, valid):
    relc = rel_bias.astype(F32) - rel_bias[REL_BUCKETS - 1].astype(F32)[None]
    onehot = np.eye(REL_BUCKETS, dtype=np.float32)[_rel_bucket_np(dist)]
    tab = jnp.einsum('...b,bh->...h', onehot, relc, precision=lax.Precision.HIGHEST)
    tab = jnp.where(valid[..., None], tab, NEG)
    return jnp.moveaxis(tab, -1, 0)


def _prompt_tables(rel_bias, nb, nc):
    g, r_heads = N_KV_HEADS, GQA_R
    qi = np.arange(Q_BLOCK)[:, None]
    nw = WINDOW + Q_BLOCK
    d_w = qi + WINDOW - np.arange(nw)[None, :]
    wtab = _bias_table(rel_bias, d_w, (d_w >= 0) & (d_w <= WINDOW)).reshape(g, r_heads * Q_BLOCK, nw)
    stab = _bias_table(rel_bias, d_w, d_w >= 0).reshape(g, r_heads * Q_BLOCK, nw)
    d_c = qi - CMP_STRIDE * (np.arange(16)[None, :] - 8) - (CMP_LEN - 1)
    valid_c = d_c >= 0
    a = _bias_table(rel_bias, d_c, valid_c).reshape(g, r_heads * Q_BLOCK, 16)
    hi = a.astype(BF16)
    lo = jnp.where(jnp.asarray(np.tile(valid_c, (r_heads, 1)))[None], a - hi.astype(F32), 0.0).astype(BF16)
    fut = jnp.full((g, r_heads * Q_BLOCK, 1), NEG, BF16)
    atab = jnp.concatenate([jnp.zeros((g, r_heads * Q_BLOCK, ATAB_LANE), BF16), hi, lo, fut,
                            jnp.zeros((g, r_heads * Q_BLOCK, 15), BF16)], axis=-1)
    jb = np.arange(nb)[:, None]
    n = np.arange(nc)[None, :]
    poolt = jnp.asarray(((n >= CMP_PER_SEL * jb - 1) & (n <= CMP_PER_SEL * jb + CMP_PER_SEL - 1)).astype(np.float32),
                        BF16)
    return wtab, stab, atab, poolt


CMP_PAGES_PER_STEP = 8
SEL_PAGES_PER_STEP = 16


def _gather_cmp_kernel(pt_ref, *refs):
    del pt_ref
    page_refs = refs[:CMP_PAGES_PER_STEP]
    o_ref, t_ref = refs[CMP_PAGES_PER_STEP:]
    n_fb = t_ref.shape[0]
    for k in range(CMP_PAGES_PER_STEP):
        for fb in range(n_fb):
            t_ref[fb, k * PAGE_SIZE:(k + 1) * PAGE_SIZE, :] = page_refs[k][0, 0, fb * 128:(fb + 1) * 128, :].T
    rows = CMP_PAGES_PER_STEP * PAGE_SIZE // CMP_STRIDE
    for s in range(CMP_STRIDE):
        for fb in range(n_fb):
            c0 = (s * n_fb + fb) * 128
            o_ref[0, :, c0:c0 + 128] = t_ref[fb, pl.ds(s, rows, stride=CMP_STRIDE), :].astype(BF16)


def _gather_cmp(cache_t, layer, page_table):
    bsz, n_pages = page_table.shape
    half = cache_t.shape[2] // 2
    pp = CMP_PAGES_PER_STEP
    chunks = pp * PAGE_SIZE // CMP_STRIDE
    page_spec = lambda k: pl.BlockSpec((1, 1, half, PAGE_SIZE), lambda b, i, pt: (layer, pt[b, i * pp + k], 0, 0))
    return pl.pallas_call(
        _gather_cmp_kernel,
        grid_spec=pltpu.PrefetchScalarGridSpec(
            num_scalar_prefetch=1, grid=(bsz, n_pages // pp),
            in_specs=[page_spec(k) for k in range(pp)],
            out_specs=pl.BlockSpec((1, chunks, CMP_STRIDE * half), lambda b, i, pt: (b, i, 0)),
            scratch_shapes=[pltpu.VMEM((half // 128, pp * PAGE_SIZE, 128), F32)]),
        out_shape=jax.ShapeDtypeStruct((bsz, n_pages * PAGE_SIZE // CMP_STRIDE, CMP_STRIDE * half), BF16),
        compiler_params=_cparams(2),
        name="gather_cmp",
    )(page_table, *([cache_t] * pp))


def _nsa_attn_sample_kernel(pt_ref, q_ref, gt_ref, kct_ref, vc_ref, new_ref, win_ref,
                            tc_ref, tnear_ref, tnew_ref, tw_ref, pool_ref, e_ref, rsum_ref, *refs,
                            t, n_past_blk):
    del pt_ref
    pp = SEL_PAGES_PER_STEP
    k_refs, v_refs = refs[:pp], refs[pp:2 * pp]
    o_ref, m_ref, l_ref, acc_ref, selb_ref, oc_ref, ow_ref = refs[2 * pp:]
    g, r_heads, hd = N_KV_HEADS, GQA_R, HEAD_DIM
    gw = g * hd
    i = pl.program_id(1)
    n_steps = pl.num_programs(1)
    nt = (((1,), (1,)), ((), ()))
    q = q_ref[0]
    tile = pp * PAGE_SIZE

    @pl.when(i == 0)
    def _():
        new = new_ref[0]
        s_c = jnp.dot(q, kct_ref[0], preferred_element_type=F32) + tc_ref[...]
        m_c = jnp.max(s_c, axis=1, keepdims=True)
        e_c = jnp.exp(s_c - m_c)
        inv_c = jnp.where(m_c > NEG_VALID, 1.0 / jnp.maximum(jnp.sum(e_c, axis=1, keepdims=True), 1e-30), 0.0)
        p_c = e_c * inv_c
        oc_ref[...] = jnp.dot(p_c.astype(BF16), vc_ref[0], preferred_element_type=F32)
        imp0 = jnp.zeros((q.shape[0], n_past_blk), F32)
        for part in _split3(p_c):
            imp0 = imp0 + jnp.dot(part, pool_ref[...], preferred_element_type=F32)
        imp = jnp.zeros((q.shape[0], n_past_blk), F32)
        for part in _split3(imp0):
            imp = imp + jnp.dot(rsum_ref[...], part, preferred_element_type=F32)
        blk = lax.broadcasted_iota(jnp.int32, (n_past_blk, q.shape[0]), 0)
        forced = (blk == 0) | (blk == n_past_blk - 1)
        sel = _top_blocks(jnp.where(forced, FORCE_SCORE, imp.T), blk, min(TOP_N - 1, n_past_blk), n_past_blk)
        selb_ref[...] = jnp.where(sel, 0.0, NEG).T.astype(BF16)
        win = win_ref[0, 0].astype(BF16)
        s_w = jnp.dot(q, win[:gw], preferred_element_type=F32) + tw_ref[...]
        s_wn = lax.dot_general(q, new[:, 4 * gw:5 * gw], nt, preferred_element_type=F32) + tnew_ref[...]
        m_w = jnp.maximum(jnp.max(s_w, axis=1, keepdims=True), jnp.max(s_wn, axis=1, keepdims=True))
        p_w = jnp.exp(s_w - m_w)
        p_wn = jnp.exp(s_wn - m_w)
        l_w = jnp.sum(p_w, axis=1, keepdims=True) + jnp.sum(p_wn, axis=1, keepdims=True)
        o_w = (lax.dot_general(p_w.astype(BF16), win[gw:], nt, preferred_element_type=F32)
               + jnp.dot(p_wn.astype(BF16), new[:, 5 * gw:6 * gw], preferred_element_type=F32))
        ow_ref[...] = o_w / l_w
        s_n = lax.dot_general(q, new[:, 2 * gw:3 * gw], nt, preferred_element_type=F32) + tnew_ref[...]
        m_n = jnp.max(s_n, axis=1, keepdims=True)
        p_n = jnp.exp(s_n - m_n)
        m_ref[...] = m_n
        l_ref[...] = jnp.sum(p_n, axis=1, keepdims=True)
        acc_ref[...] = jnp.dot(p_n.astype(BF16), new[:, 3 * gw:4 * gw], preferred_element_type=F32)

    k_t = jnp.concatenate([r[0, 0] for r in k_refs], axis=1).astype(BF16)
    v_t = jnp.concatenate([r[0, 0] for r in v_refs], axis=1).astype(BF16)
    s = (jnp.dot(q, k_t, preferred_element_type=F32)
         + jnp.dot(selb_ref[...], e_ref[:, pl.ds(pl.multiple_of(i * tile, tile), tile)], preferred_element_type=F32))
    s = s + jnp.where(i == n_steps - 1, tnear_ref[...], 0.0)
    m_old = m_ref[...]
    m_new = jnp.maximum(m_old, jnp.max(s, axis=1, keepdims=True))
    alpha = jnp.exp(m_old - m_new)
    p = jnp.exp(s - m_new)
    m_ref[...] = m_new
    l_ref[...] = alpha * l_ref[...] + jnp.sum(p, axis=1, keepdims=True)
    acc_ref[...] = alpha * acc_ref[...] + lax.dot_general(p.astype(BF16), v_t, nt, preferred_element_type=F32)

    @pl.when(i == n_steps - 1)
    def _():
        gt = gt_ref[0]
        o = gt[:, 0:1] * oc_ref[...] + gt[:, 1:2] * (acc_ref[...] / l_ref[...]) + gt[:, 2:3] * ow_ref[...]
        for gi in range(g):
            for r in range(r_heads):
                h = gi * r_heads + r
                o_ref[0, :, h * hd:(h + 1) * hd] = o[h * t:(h + 1) * t, gi * hd:(gi + 1) * hd].astype(BF16)


def _nsa_attn_sample(q_bd, gates, kct, vc, new_pad, win_t, layer, cache_t, page_table, tabs, t):
    bsz, n_pages = page_table.shape
    g, r_heads, hd = N_KV_HEADS, GQA_R, HEAD_DIM
    gw = g * hd
    lanes = g * r_heads * t
    pp = SEL_PAGES_PER_STEP
    n_past_blk = n_pages * PAGE_SIZE // SEL_LEN
    kern = functools.partial(_nsa_attn_sample_kernel, t=t, n_past_blk=n_past_blk)
    per_b = lambda a: pl.BlockSpec((1,) + a.shape[1:], lambda b, i, pt: (b, 0, 0))
    full = lambda a: pl.BlockSpec(a.shape, lambda b, i, pt: (0, 0))
    page_spec = lambda fblk, k: pl.BlockSpec((1, 1, gw, PAGE_SIZE),
                                             lambda b, i, pt: (layer, pt[b, i * pp + k], fblk, 0))
    return pl.pallas_call(
        kern,
        grid_spec=pltpu.PrefetchScalarGridSpec(
            num_scalar_prefetch=1, grid=(bsz, n_pages // pp),
            in_specs=[per_b(q_bd), per_b(gates), per_b(kct), per_b(vc), per_b(new_pad),
                      pl.BlockSpec((1, 1) + win_t.shape[2:], lambda b, i, pt: (layer, b, 0, 0))]
                     + [full(a) for a in tabs]
                     + [page_spec(2, k) for k in range(pp)] + [page_spec(3, k) for k in range(pp)],
            out_specs=pl.BlockSpec((1, t, g * r_heads * hd), lambda b, i, pt: (b, 0, 0)),
            scratch_shapes=[pltpu.VMEM((lanes, 1), F32), pltpu.VMEM((lanes, 1), F32), pltpu.VMEM((lanes, gw), F32),
                            pltpu.VMEM((lanes, n_past_blk), BF16), pltpu.VMEM((lanes, gw), F32),
                            pltpu.VMEM((lanes, gw), F32)]),
        out_shape=jax.ShapeDtypeStruct((bsz, t, g * r_heads * hd), BF16),
        compiler_params=_cparams(2),
        name="nsa_attn_sample",
    )(page_table, q_bd, gates, kct, vc, new_pad, win_t, *tabs, *([cache_t] * (2 * pp)))


def _sample_tables(rel_bias, pos0, t, nc):
    g, r_heads = N_KV_HEADS, GQA_R
    lanes = g * r_heads * t
    tq = pos0 + np.arange(t)[:, None]

    def table(kpos, extra_valid=True):
        dist = tq - kpos[None, :]
        tab = _bias_table(rel_bias, dist, (dist >= 0) & extra_valid)
        return tab.reshape(lanes, kpos.shape[0])

    tc = table(CMP_STRIDE * np.arange(nc) + CMP_LEN - 1)
    tnear = table(pos0 - Q_BLOCK + np.arange(Q_BLOCK))
    tnear = jnp.pad(tnear, ((0, 0), (SEL_PAGES_PER_STEP * PAGE_SIZE - Q_BLOCK, 0)))
    new_pos = pos0 + np.arange(128)
    tnew = table(new_pos, (new_pos < pos0 + t)[None, :])
    kw = pos0 - WINDOW + np.arange(WINDOW)
    tw = table(kw, (tq - kw[None, :]) <= WINDOW)
    n_past_blk = pos0 // SEL_LEN
    jb = np.arange(n_past_blk)[None, :]
    n = np.arange(nc)[:, None]
    pool = jnp.asarray(((n >= CMP_PER_SEL * jb - 1) & (n <= CMP_PER_SEL * jb + CMP_PER_SEL - 1)).astype(np.float32),
                       BF16)
    e = jnp.asarray((np.arange(pos0)[None, :] // SEL_LEN == np.arange(n_past_blk)[:, None]).astype(np.float32), BF16)
    li = np.arange(lanes)
    same = (li[:, None] // (r_heads * t) == li[None, :] // (r_heads * t)) & (li[:, None] % t == li[None, :] % t)
    rsum = jnp.asarray(same.astype(np.float32), BF16)
    return tc, tnear, tnew, tw, pool, e, rsum


def _nsa_sample(xs2d, bsz, t, layer, cache_t, page_table, win_t, wts, tabs):
    g, r_heads, hd = N_KV_HEADS, GQA_R, HEAD_DIM
    gw = g * hd
    q, kv, kvb, gt = _nsa_proj(xs2d, wts["wq"], wts["wkv"], wts["wg"], tm=xs2d.shape[0])
    chunks = _gather_cmp(cache_t, layer, page_table)
    kct, vc = _nsa_cmp(chunks, wts["w1"], wts["pe_term"], wts["w2k"].T, wts["w2v"], transposed=False)
    q5 = q.reshape(bsz, t, g, r_heads, hd)
    q_bd = jnp.einsum('btgrd,gh->bgrthd', q5, jnp.eye(g, dtype=q.dtype)).reshape(bsz, g * r_heads * t, gw)
    gates = jnp.pad(gt[:, :3 * g * r_heads].reshape(bsz, t, g * r_heads, 3).transpose(0, 2, 1, 3)
                    .reshape(bsz, g * r_heads * t, 3), ((0, 0), (0, 0), (0, 125)))
    new_pad = jnp.pad(kvb.reshape(bsz, t, 6 * gw), ((0, 0), (0, 128 - t), (0, 0)))
    o = _nsa_attn_sample(q_bd, gates, kct, vc, new_pad, win_t, layer, cache_t, page_table, tabs, t)
    return o.reshape(bsz * t, g * r_heads * hd), kv


def _nsa_weights(w_in, cmp_pe, cmp_w1, cmp_b1, cmp_w2):
    d = w_in.shape[0]
    g, r_heads, hd = N_KV_HEADS, GQA_R, HEAD_DIM
    qd, kvd = g * r_heads * hd, 6 * g * hd
    wq = (w_in[:, :qd] * hd ** -0.5).astype(BF16)
    wq_pad = jnp.pad(wq.reshape(d, g * r_heads, hd), ((0, 0), (0, 0), (0, 128 - hd))).reshape(d, g * r_heads * 128)
    wkv = w_in[:, qd:qd + kvd].astype(BF16)
    wg = jnp.pad(w_in[:, qd + kvd:], ((0, 0), (0, 128 - 3 * g * r_heads))).astype(BF16)
    w1r = cmp_w1.reshape(2, CMP_LEN, hd, CMP_HID)
    w1 = jnp.concatenate([w1r[:, :CMP_STRIDE], w1r[:, CMP_STRIDE:]], axis=-1).astype(BF16)
    w1 = w1.reshape(2, CMP_STRIDE // CMP_PACK, CMP_PACK * hd, 2 * CMP_HID)
    pe_term = jnp.einsum('csd,csdh->ch', cmp_pe, w1r) + cmp_b1
    return dict(wq=wq, wq_pad=wq_pad, wkv=wkv, wg=wg, w1=w1, pe_term=pe_term,
                w2k=cmp_w2[0].astype(BF16), w2v=cmp_w2[1].astype(BF16))


def _nsa_prompt(x2d, bsz, seq, wts, tables):
    g, r_heads, hd = N_KV_HEADS, GQA_R, HEAD_DIM
    m = bsz * seq
    nc, nb = seq // CMP_STRIDE, seq // SEL_LEN
    q_pad, kv, kv_chunks, gt = _nsa_proj(x2d, wts["wq_pad"], wts["wkv"], wts["wg"], tm=512, chunk_seq=seq)
    w2k_t = jnp.pad(wts["w2k"].T, ((0, 128 - hd), (0, 0)))
    w2v_p = jnp.pad(wts["w2v"], ((0, 0), (0, 128 - hd)))
    kct, vc = _nsa_cmp(kv_chunks, wts["w1"], wts["pe_term"], w2k_t, w2v_p, transposed=True)
    kst, kwt, vs, vw = _nsa_prep(kv.reshape(bsz, seq, kv.shape[1]), nb)
    gates_g = jnp.pad(gt[:, :3 * g * r_heads].reshape(m, g, 3 * r_heads).transpose(1, 0, 2),
                      ((0, 0), (0, 0), (0, 128 - 3 * r_heads)))
    o = _nsa_attn_prompt(q_pad, gates_g, kct, vc, kst, vs, kwt, vw, *tables, bsz, seq)
    return o, kv


def kernel(x_prompt, x_sample, cache_nsa, state_nsa_win, state_pool, state_ffn, page_table, p_prompt, p_sample,
           rel_bias, nsa_w_in, nsa_cmp_pe, nsa_cmp_w1, nsa_cmp_b1, nsa_cmp_w2, nsa_w_out, pool_w, pool_b,
           pool_scale, ffn_w_up, ffn_conv_w, ffn_conv_b, ffn_w_down, ln_g, ln_b, ple_w_proj, ple_w_gate,
           ple_b_gate):
    bsz, seq, d = x_prompt.shape
    db, dt, _ = x_sample.shape
    depth = ffn_w_up.shape[0]
    alpha = (2.0 * depth) ** 0.25
    g, hd = N_KV_HEADS, HEAD_DIM
    gw = g * hd
    past_len = page_table.shape[1] * PAGE_SIZE
    mp, ms = bsz * seq, db * dt
    assert g * GQA_R * dt == 128 and past_len % SEL_LEN == 0 and seq % PADL == 0
    assert page_table.shape[1] % SEL_PAGES_PER_STEP == 0 and page_table.shape[1] % CMP_PAGES_PER_STEP == 0
    assert state_nsa_win.shape[2] == WINDOW and seq >= WINDOW and state_pool.shape[2] == POOL_HIST

    xp = x_prompt.reshape(mp, d)
    xs = x_sample.reshape(ms, d)
    tables_p = _prompt_tables(rel_bias, seq // SEL_LEN, seq // CMP_STRIDE)
    tables_s = _sample_tables(rel_bias, past_len, dt, past_len // CMP_STRIDE)
    cache_t = jnp.transpose(cache_nsa, (0, 1, 3, 4, 5, 2)).reshape(cache_nsa.shape[0], cache_nsa.shape[1], 4 * gw,
                                                                   PAGE_SIZE)
    win_t = jnp.transpose(state_nsa_win, (0, 1, 3, 4, 5, 2)).reshape(state_nsa_win.shape[0], db, 2 * gw, WINDOW)
    nsa_rp, nsa_rs, nsa_wp, nsa_ws, pool_hp, pool_hs, ffn_hp, ffn_hs = ([] for _ in range(8))
    for i in range(depth):
        j = i // 2
        g0, b0, g1, b1 = ln_g[i, 0][None], ln_b[i, 0][None], ln_g[i, 1][None], ln_b[i, 1][None]
        if i % 2 == 0:
            wts = _nsa_weights(nsa_w_in[j], nsa_cmp_pe[j], nsa_cmp_w1[j], nsa_cmp_b1[j], nsa_cmp_w2[j])
            o_p, kv_p = _nsa_prompt(xp, bsz, seq, wts, tables_p)
            o_s, kv_s = _nsa_sample(xs, db, dt, j, cache_t, page_table, win_t, wts, tables_s)
            w_out = nsa_w_out[j].astype(BF16)
            xp = _linear_ln(o_p, w_out, xp, g0, b0, alpha, tm=512)
            xs = _linear_ln(o_s, w_out, xs, g0, b0, alpha, tm=ms)
            kv_p3 = kv_p.reshape(bsz, seq, 6, g, hd)
            kv_s3 = kv_s.reshape(db, dt, 6, g, hd)
            nsa_rp.append(kv_p3[:, :, :4])
            nsa_rs.append(kv_s3[:, :, :4])
            nsa_wp.append(kv_p3[:, seq - WINDOW:, 4:])
            nsa_ws.append(jnp.concatenate([state_nsa_win[j][:, dt:], kv_s3[:, :, 4:]], axis=1))
        else:
            xp3 = xp.reshape(bsz, seq, d)
            xe = jnp.concatenate([state_pool[j], xs.reshape(db, dt, d)], axis=1)
            pool_hp.append(xp3[:, seq - POOL_HIST:])
            pool_hs.append(xe[:, xe.shape[1] - POOL_HIST:])
            pw = pool_w[j].astype(BF16)
            xp = _pool_ln_prompt(xp3, pw, pool_b[j][None], pool_scale[j][None], g0, b0, alpha).reshape(mp, d)
            xs = _pool_ln_sample(xe, dt, past_len, pw, pool_b[j][None], pool_scale[j][None], g0, b0,
                                 alpha).reshape(ms, d)
        w_up = ffn_w_up[i].astype(BF16)
        w_down = ffn_w_down[i].astype(BF16)
        cw, cb = ffn_conv_w[i], ffn_conv_b[i][None]
        h_p = _ffn_up_prompt(xp, w_up, cw, cb, seq)
        tail = xp.reshape(bsz, seq, d)[:, seq - 8:].reshape(bsz * 8, d)
        u_tail = _matmul(tail, w_up, tm=bsz * 8).reshape(bsz, 8, -1)
        ffn_hp.append(u_tail[:, 8 - (CONV_W - 1):])
        u_s = _matmul(xs, w_up, tm=ms).reshape(db, dt, -1)
        ue = jnp.concatenate([state_ffn[i], u_s], axis=1)
        ffn_hs.append(ue[:, ue.shape[1] - (CONV_W - 1):])
        h_s = _ffn_gate_sample(ue, cw, cb, dt).reshape(ms, -1)
        ple = (ple_w_gate[i].astype(BF16), ple_b_gate[i][None], ple_w_proj[i].astype(BF16))
        xp = _ffn_down_ln_ple(h_p, w_down, xp, g1, b1, p_prompt[i].reshape(mp, -1), *ple, alpha, 512)
        xs = _ffn_down_ln_ple(h_s, w_down, xs, g1, b1, p_sample[i].reshape(ms, -1), *ple, alpha, ms)
    return (xp.reshape(bsz, seq, d), xs.reshape(db, dt, d),
            jnp.stack(nsa_rp), jnp.stack(nsa_rs), jnp.stack(nsa_wp), jnp.stack(nsa_ws),
            jnp.stack(pool_hp), jnp.stack(pool_hs), jnp.stack(ffn_hp), jnp.stack(ffn_hs))
```

```python
import functools
import math

import numpy as np
import jax
import jax.numpy as jnp
from jax import lax
from jax.experimental import pallas as pl
from jax.experimental.pallas import tpu as pltpu

N_KV_HEADS = 4
GQA_R = 4
HEAD_DIM = 64
CMP_LEN = 32
CMP_STRIDE = 16
CMP_HID = 256
CMP_PACK = 4
SEL_LEN = 64
CMP_PER_SEL = SEL_LEN // CMP_STRIDE
TOP_N = 16
WINDOW = 512
Q_BLOCK = 128
FORCE_SCORE = 1.0e4
REL_BUCKETS = 32
REL_MAX_DIST = 128
POOL_WINDOWS = (2, 4, 8, 16)
POOL_MAX = max(POOL_WINDOWS)
POOL_HIST = POOL_MAX - 1
CONV_W = 3
LN_EPS = 1e-5
PAGE_SIZE = 128

NEG = -1.0e30
NEG_VALID = -1.0e29
BF16 = jnp.bfloat16
F32 = jnp.float32

VMEM_LIMIT_BYTES = 56 * 1024 * 1024


def _cparams(n_grid):
    return pltpu.CompilerParams(dimension_semantics=("arbitrary",) * n_grid,
                                vmem_limit_bytes=VMEM_LIMIT_BYTES)


def _layer_norm(y, g, b):
    mu = jnp.mean(y, axis=-1, keepdims=True)
    yc = y - mu
    var = jnp.mean(yc * yc, axis=-1, keepdims=True)
    return yc * lax.rsqrt(var + LN_EPS) * g + b


FFN_TN = 256
HALO = 16


def _ffn_up_prompt_kernel(x_ref, xh_ref, w_ref, cw_ref, cb_ref, h_ref, xe_ref, ue_ref, *,
                          tm, d_ff, tiles_per_batch):
    i = pl.program_id(0)
    first = (i % tiles_per_batch) == 0
    halo = xh_ref[...]
    xe_ref[0:HALO, :] = jnp.where(first, jnp.zeros_like(halo), halo).astype(BF16)
    xe_ref[HALO:HALO + tm, :] = x_ref[...].astype(BF16)
    xe = xe_ref[...]
    for j in range(d_ff // FFN_TN):
        c_parts = []
        for half in range(2):
            c0 = half * d_ff + j * FFN_TN
            ue_ref[...] = jnp.dot(xe, w_ref[:, c0:c0 + FFN_TN], preferred_element_type=F32)
            cw = cw_ref[:, c0:c0 + FFN_TN]
            c = cb_ref[:, c0:c0 + FFN_TN]
            for k in range(CONV_W):
                off = HALO - (CONV_W - 1) + k
                c = c + cw[k:k + 1, :] * ue_ref[off:off + tm, :]
            c_parts.append(c)
        h_ref[:, j * FFN_TN:(j + 1) * FFN_TN] = (jax.nn.gelu(c_parts[1]) * c_parts[0]).astype(BF16)


def _ffn_up_prompt(x, w_up, conv_w, conv_b, seq, tm=512):
    m, d = x.shape
    d_ff = w_up.shape[1] // 2
    tpb = seq // tm
    kern = functools.partial(_ffn_up_prompt_kernel, tm=tm, d_ff=d_ff, tiles_per_batch=tpb)
    return pl.pallas_call(
        kern,
        grid=(m // tm,),
        in_specs=[
            pl.BlockSpec((tm, d), lambda i: (i, 0)),
            pl.BlockSpec((HALO, d), lambda i: (jnp.maximum(i * (tm // HALO) - 1, 0), 0)),
            pl.BlockSpec(w_up.shape, lambda i: (0, 0)),
            pl.BlockSpec(conv_w.shape, lambda i: (0, 0)),
            pl.BlockSpec(conv_b.shape, lambda i: (0, 0)),
        ],
        out_specs=pl.BlockSpec((tm, d_ff), lambda i: (i, 0)),
        out_shape=jax.ShapeDtypeStruct((m, d_ff), BF16),
        scratch_shapes=[pltpu.VMEM((HALO + tm, d), BF16), pltpu.VMEM((HALO + tm, FFN_TN), F32)],
        compiler_params=_cparams(1),
        name="ffn_up_prompt",
    )(x, x, w_up, conv_w, conv_b)


def _ffn_gate_sample_kernel(ue_ref, cw_ref, cb_ref, h_ref, *, t, d_ff):
    for j in range(d_ff // FFN_TN):
        c_parts = []
        for half in range(2):
            c0 = half * d_ff + j * FFN_TN
            cw = cw_ref[:, c0:c0 + FFN_TN]
            c = cb_ref[:, c0:c0 + FFN_TN][None]
            for k in range(CONV_W):
                c = c + cw[k:k + 1, :][None] * ue_ref[:, k:k + t, c0:c0 + FFN_TN]
            c_parts.append(c)
        h_ref[:, :, j * FFN_TN:(j + 1) * FFN_TN] = (jax.nn.gelu(c_parts[1]) * c_parts[0]).astype(BF16)


def _ffn_gate_sample(ue, conv_w, conv_b, t):
    b, te, n2 = ue.shape
    d_ff = n2 // 2
    bb = 8
    kern = functools.partial(_ffn_gate_sample_kernel, t=t, d_ff=d_ff)
    return pl.pallas_call(
        kern,
        grid=(b // bb,),
        in_specs=[
            pl.BlockSpec((bb, te, n2), lambda i: (i, 0, 0)),
            pl.BlockSpec(conv_w.shape, lambda i: (0, 0)),
            pl.BlockSpec(conv_b.shape, lambda i: (0, 0)),
        ],
        out_specs=pl.BlockSpec((bb, t, d_ff), lambda i: (i, 0, 0)),
        out_shape=jax.ShapeDtypeStruct((b, t, d_ff), BF16),
        compiler_params=_cparams(1),
        name="ffn_gate_sample",
    )(ue, conv_w, conv_b)


def _matmul_kernel(x_ref, w_ref, o_ref):
    o_ref[...] = jnp.dot(x_ref[...].astype(BF16), w_ref[...], preferred_element_type=F32).astype(o_ref.dtype)


def _matmul(x, w, tm, tn=None, out_dtype=F32):
    m, k = x.shape
    n = w.shape[1]
    tn = n if tn is None else tn
    return pl.pallas_call(
        _matmul_kernel,
        grid=(m // tm, n // tn),
        in_specs=[pl.BlockSpec((tm, k), lambda i, j: (i, 0)),
                  pl.BlockSpec((k, tn), lambda i, j: (0, j))],
        out_specs=pl.BlockSpec((tm, tn), lambda i, j: (i, j)),
        out_shape=jax.ShapeDtypeStruct((m, n), out_dtype),
        compiler_params=_cparams(2),
        name="matmul",
    )(x, w)


def _linear_ln_kernel(a_ref, w_ref, r_ref, g_ref, b_ref, o_ref, *, alpha):
    f = jnp.dot(a_ref[...].astype(BF16), w_ref[...], preferred_element_type=F32)
    o_ref[...] = _layer_norm(alpha * r_ref[...] + f, g_ref[...], b_ref[...])


def _linear_ln(a, w, resid, g, b, alpha, tm):
    m, k = a.shape
    d = w.shape[1]
    kern = functools.partial(_linear_ln_kernel, alpha=alpha)
    return pl.pallas_call(
        kern,
        grid=(m // tm,),
        in_specs=[pl.BlockSpec((tm, k), lambda i: (i, 0)),
                  pl.BlockSpec((k, d), lambda i: (0, 0)),
                  pl.BlockSpec((tm, d), lambda i: (i, 0)),
                  pl.BlockSpec((1, d), lambda i: (0, 0)),
                  pl.BlockSpec((1, d), lambda i: (0, 0))],
        out_specs=pl.BlockSpec((tm, d), lambda i: (i, 0)),
        out_shape=jax.ShapeDtypeStruct((m, d), F32),
        compiler_params=_cparams(1),
        name="linear_ln",
    )(a, w, resid, g, b)


def _ffn_down_kernel(h_ref, wd_ref, r_ref, g_ref, b_ref, p_ref, wg_ref, bg_ref, wp_ref, o_ref, *, alpha):
    f = jnp.dot(h_ref[...], wd_ref[...], preferred_element_type=F32)
    x2 = _layer_norm(alpha * r_ref[...] + f, g_ref[...], b_ref[...])
    gate = jax.nn.sigmoid(jnp.dot(x2.astype(BF16), wg_ref[...], preferred_element_type=F32) + bg_ref[...])
    inj = jnp.dot(p_ref[...].astype(BF16), wp_ref[...], preferred_element_type=F32)
    o_ref[...] = x2 + gate * inj


def _ffn_down_ln_ple(h, w_down, resid, g, b, p, w_gate, b_gate, w_proj, alpha, tm):
    m, k = h.shape
    d = w_down.shape[1]
    pd = p.shape[1]
    kern = functools.partial(_ffn_down_kernel, alpha=alpha)
    full = lambda shape: pl.BlockSpec(shape, lambda i: (0, 0))
    return pl.pallas_call(
        kern,
        grid=(m // tm,),
        in_specs=[pl.BlockSpec((tm, k), lambda i: (i, 0)), full((k, d)),
                  pl.BlockSpec((tm, d), lambda i: (i, 0)), full((1, d)), full((1, d)),
                  pl.BlockSpec((tm, pd), lambda i: (i, 0)), full((d, d)), full((1, d)), full((pd, d))],
        out_specs=pl.BlockSpec((tm, d), lambda i: (i, 0)),
        out_shape=jax.ShapeDtypeStruct((m, d), F32),
        compiler_params=_cparams(1),
        name="ffn_down_ln_ple",
    )(h, w_down, resid, g, b, p, w_gate, b_gate, w_proj)


def _pool_finish(x, diffs, w_ref, pb_ref, ps_ref, g_ref, b_ref, alpha):
    ys = [jnp.dot(dg.astype(BF16), w_ref[gi], preferred_element_type=F32) for gi, dg in enumerate(diffs)]
    y = jnp.concatenate(ys, axis=-1) + pb_ref[...]
    return _layer_norm(alpha * x + ps_ref[...] * y, g_ref[...], b_ref[...])


def _pool_prompt_kernel(x_ref, xh_ref, w_ref, pb_ref, ps_ref, g_ref, b_ref, o_ref, xe_ref, *,
                        tt, gw, alpha):
    i = pl.program_id(1)
    halo = xh_ref[0]
    xe_ref[0:POOL_MAX, :] = jnp.where(i == 0, jnp.zeros_like(halo), halo)
    xe_ref[POOL_MAX:POOL_MAX + tt, :] = x_ref[0]
    pos = i * tt + lax.broadcasted_iota(jnp.int32, (tt, 1), 0)
    x = x_ref[0]
    diffs = []
    for gi, win in enumerate(POOL_WINDOWS):
        lo = gi * gw
        tot = x[:, lo:lo + gw]
        for k in range(1, win):
            tot = tot + xe_ref[POOL_MAX - k:POOL_MAX - k + tt, lo:lo + gw]
        cnt = jnp.minimum(win, pos + 1).astype(F32)
        diffs.append(tot / cnt - x[:, lo:lo + gw])
    o_ref[0] = _pool_finish(x, diffs, w_ref, pb_ref, ps_ref, g_ref, b_ref, alpha)


def _pool_ln_prompt(x, w, pb, ps, g, b, alpha, tt=512):
    bsz, t, d = x.shape
    gw = d // len(POOL_WINDOWS)
    kern = functools.partial(_pool_prompt_kernel, tt=tt, gw=gw, alpha=alpha)
    vec = pl.BlockSpec((1, d), lambda bi, i: (0, 0))
    return pl.pallas_call(
        kern,
        grid=(bsz, t // tt),
        in_specs=[pl.BlockSpec((1, tt, d), lambda bi, i: (bi, i, 0)),
                  pl.BlockSpec((1, POOL_MAX, d), lambda bi, i: (bi, jnp.maximum(i * (tt // POOL_MAX) - 1, 0), 0)),
                  pl.BlockSpec(w.shape, lambda bi, i: (0, 0, 0)), vec, vec, vec, vec],
        out_specs=pl.BlockSpec((1, tt, d), lambda bi, i: (bi, i, 0)),
        out_shape=jax.ShapeDtypeStruct((bsz, t, d), F32),
        scratch_shapes=[pltpu.VMEM((POOL_MAX + tt, d), F32)],
        compiler_params=_cparams(2),
        name="pool_ln_prompt",
    )(x, x, w, pb, ps, g, b)


def _pool_sample_kernel(xe_ref, w_ref, pb_ref, ps_ref, g_ref, b_ref, o_ref, *, t, nh, gw, pos0, alpha):
    bb = xe_ref.shape[0]
    d = xe_ref.shape[2]
    x = xe_ref[:, nh:nh + t, :].reshape(bb * t, d)
    cnt_np = np.minimum(np.array(POOL_WINDOWS)[:, None], pos0 + np.arange(t)[None, :] + 1).astype(np.float32)
    diffs = []
    for gi, win in enumerate(POOL_WINDOWS):
        lo = gi * gw
        tot = xe_ref[:, nh:nh + t, lo:lo + gw]
        for k in range(1, win):
            tot = tot + xe_ref[:, nh - k:nh - k + t, lo:lo + gw]
        if np.all(cnt_np[gi] == win):
            mean = tot / float(win)
        else:
            t_idx = lax.broadcasted_iota(jnp.int32, (1, t, 1), 1)
            mean = tot / jnp.minimum(win, pos0 + t_idx + 1).astype(F32)
        diffs.append(mean.reshape(bb * t, gw) - x[:, lo:lo + gw])
    o_ref[...] = _pool_finish(x, diffs, w_ref, pb_ref, ps_ref, g_ref, b_ref, alpha).reshape(bb, t, d)


def _pool_ln_sample(xe, t, pos0, w, pb, ps, g, b, alpha, bb=8):
    bsz, te, d = xe.shape
    gw = d // len(POOL_WINDOWS)
    kern = functools.partial(_pool_sample_kernel, t=t, nh=te - t, gw=gw, pos0=pos0, alpha=alpha)
    vec = pl.BlockSpec((1, d), lambda i: (0, 0))
    return pl.pallas_call(
        kern,
        grid=(bsz // bb,),
        in_specs=[pl.BlockSpec((bb, te, d), lambda i: (i, 0, 0)),
                  pl.BlockSpec(w.shape, lambda i: (0, 0, 0)), vec, vec, vec, vec],
        out_specs=pl.BlockSpec((bb, t, d), lambda i: (i, 0, 0)),
        out_shape=jax.ShapeDtypeStruct((bsz, t, d), F32),
        compiler_params=_cparams(1),
        name="pool_ln_sample",
    )(xe, w, pb, ps, g, b)


def _nsa_proj_kernel(x_ref, wq_ref, wkv_ref, wg_ref, q_ref, kv_ref, kvb_ref, gt_ref, *scratch):
    xb = x_ref[...].astype(BF16)
    q_ref[...] = jnp.dot(xb, wq_ref[...], preferred_element_type=F32).astype(BF16)
    kv = jnp.dot(xb, wkv_ref[...], preferred_element_type=F32)
    kv_ref[...] = kv
    gt_ref[...] = jax.nn.sigmoid(jnp.dot(xb, wg_ref[...], preferred_element_type=F32))
    if not scratch:
        kvb_ref[...] = kv.astype(BF16)
        return
    (t_ref,) = scratch
    n_fb, tm = t_ref.shape[0], t_ref.shape[1]
    for fb in range(n_fb):
        t_ref[fb] = kv[:, fb * 128:(fb + 1) * 128]
    for s in range(CMP_STRIDE):
        for fb in range(n_fb):
            kvb_ref[0, s, :, fb * 128:(fb + 1) * 128] = (
                t_ref[fb, pl.ds(s, tm // CMP_STRIDE, stride=CMP_STRIDE), :].astype(BF16))


def _nsa_proj(x, wq, wkv, wg, tm, chunk_seq=None):
    m, d = x.shape
    nq, nkv, ng = wq.shape[1], wkv.shape[1], wg.shape[1]
    full = lambda w: pl.BlockSpec(w.shape, lambda i: (0, 0))
    row = lambda n: pl.BlockSpec((tm, n), lambda i: (i, 0))
    if chunk_seq is None:
        kvb_spec, kvb_shape, scratch = row(nkv), jax.ShapeDtypeStruct((m, nkv), BF16), []
    else:
        feat = 2 * N_KV_HEADS * HEAD_DIM
        tpb = chunk_seq // tm
        kvb_spec = pl.BlockSpec((1, CMP_STRIDE, tm // CMP_STRIDE, feat), lambda i: (i // tpb, 0, i % tpb, 0))
        kvb_shape = jax.ShapeDtypeStruct((m // chunk_seq, CMP_STRIDE, chunk_seq // CMP_STRIDE, feat), BF16)
        scratch = [pltpu.VMEM((feat // 128, tm, 128), F32)]
    return pl.pallas_call(
        _nsa_proj_kernel,
        grid=(m // tm,),
        in_specs=[row(d), full(wq), full(wkv), full(wg)],
        out_specs=[row(nq), row(nkv), kvb_spec, row(ng)],
        out_shape=[jax.ShapeDtypeStruct((m, nq), BF16), jax.ShapeDtypeStruct((m, nkv), F32),
                   kvb_shape, jax.ShapeDtypeStruct((m, ng), F32)],
        scratch_shapes=scratch,
        compiler_params=_cparams(1),
        name="nsa_proj",
    )(x, wq, wkv, wg)


def _nsa_cmp_kernel(*refs, nc, transposed):
    ns = CMP_STRIDE
    x_refs = refs[:ns]
    w1_ref, pe_ref, w2k_ref, w2v_ref, ko_ref, vo_ref = refs[ns:]
    hd = HEAD_DIM
    gw = N_KV_HEADS * hd
    last = lax.broadcasted_iota(jnp.int32, (nc, CMP_HID), 0) == nc - 1
    nt = (((1,), (1,)), ((), ()))
    for c in range(2):
        for g in range(N_KV_HEADS):
            acc = jnp.zeros((nc, 2 * CMP_HID), F32)
            lo = c * gw + g * hd
            for k in range(ns // CMP_PACK):
                xs = jnp.concatenate([x_refs[k * CMP_PACK + i][:, lo:lo + hd] for i in range(CMP_PACK)], axis=1)
                acc = acc + jnp.dot(xs, w1_ref[c, k], preferred_element_type=F32)
            h1 = jnp.where(last, 0.0, pltpu.roll(acc[:, CMP_HID:], nc - 1, 0))
            hid = jax.nn.gelu(acc[:, :CMP_HID] + h1 + pe_ref[c:c + 1, :]).astype(BF16)
            if c == 0 and transposed:
                ko_ref[0, g] = lax.dot_general(w2k_ref[...], hid, nt, preferred_element_type=F32).astype(BF16)
            elif c == 0:
                ko_ref[0, g * hd:(g + 1) * hd, :] = lax.dot_general(w2k_ref[...], hid, nt,
                                                                    preferred_element_type=F32).astype(BF16)
            elif transposed:
                vo_ref[0, g] = jnp.dot(hid, w2v_ref[...], preferred_element_type=F32).astype(BF16)
            else:
                vo_ref[0, :, g * hd:(g + 1) * hd] = jnp.dot(hid, w2v_ref[...], preferred_element_type=F32).astype(BF16)


def _nsa_cmp(xv, w1, pe_term, w2k, w2v, transposed):
    g, hd = N_KV_HEADS, HEAD_DIM
    feat = 2 * g * hd
    bsz = xv.shape[0]
    if xv.ndim == 4:
        nc = xv.shape[2]
        x_specs = [pl.BlockSpec((None, None, nc, feat), functools.partial(lambda b, s: (b, s, 0, 0), s=s))
                   for s in range(CMP_STRIDE)]
    else:
        nc = xv.shape[1]
        x_specs = [pl.BlockSpec((None, nc, feat), functools.partial(lambda b, s: (b, 0, s), s=s))
                   for s in range(CMP_STRIDE)]
    kern = functools.partial(_nsa_cmp_kernel, nc=nc, transposed=transposed)
    full = lambda w: pl.BlockSpec(w.shape, lambda b: (0,) * w.ndim)
    if transposed:
        out_specs = [pl.BlockSpec((1, g, 128, nc), lambda b: (b, 0, 0, 0)),
                     pl.BlockSpec((1, g, nc, 128), lambda b: (b, 0, 0, 0))]
        out_shape = [jax.ShapeDtypeStruct((bsz, g, 128, nc), BF16), jax.ShapeDtypeStruct((bsz, g, nc, 128), BF16)]
    else:
        out_specs = [pl.BlockSpec((1, g * hd, nc), lambda b: (b, 0, 0)), pl.BlockSpec((1, nc, g * hd), lambda b: (b, 0, 0))]
        out_shape = [jax.ShapeDtypeStruct((bsz, g * hd, nc), BF16), jax.ShapeDtypeStruct((bsz, nc, g * hd), BF16)]
    return pl.pallas_call(
        kern,
        grid=(bsz,),
        in_specs=x_specs + [full(w1), full(pe_term), full(w2k), full(w2v)],
        out_specs=out_specs,
        out_shape=out_shape,
        compiler_params=_cparams(1),
        name="nsa_cmp",
    )(*([xv] * CMP_STRIDE), w1, pe_term, w2k, w2v)


PADL = 512
FAR_TILE = 512
ATAB_LANE = 80


def _nsa_prep_kernel(kv_ref, kst_ref, kwt_ref, vs_ref, vw_ref, *, nb):
    j = pl.program_id(1)
    tl = PADL
    hd = HEAD_DIM
    gw = N_KV_HEADS * hd

    @pl.when(j == 0)
    def _():
        for g in range(N_KV_HEADS):
            kst_ref[0, g] = (lax.broadcasted_iota(jnp.int32, (128 + nb, tl), 0) == hd).astype(BF16)
            kwt_ref[0, g] = (lax.broadcasted_iota(jnp.int32, (128, tl), 0) == hd).astype(BF16)
            vs_ref[0, g] = jnp.zeros((tl, 128), BF16)
            vw_ref[0, g] = jnp.zeros((tl, 128), BF16)

    @pl.when(j > 0)
    def _():
        kst = kv_ref[0, :, 2 * gw:3 * gw].T
        kwt = kv_ref[0, :, 4 * gw:5 * gw].T
        pos = (j - 1) * tl + lax.broadcasted_iota(jnp.int32, (nb, tl), 1)
        onehot = (pos // SEL_LEN == lax.broadcasted_iota(jnp.int32, (nb, tl), 0)).astype(BF16)
        zeros = jnp.zeros((128 - hd, tl), BF16)
        lane = lax.broadcasted_iota(jnp.int32, (tl, 128), 1)
        tail = (lane == hd).astype(F32)
        for g in range(N_KV_HEADS):
            kst_ref[0, g, 0:hd, :] = kst[g * hd:(g + 1) * hd].astype(BF16)
            kst_ref[0, g, hd:128, :] = zeros
            kst_ref[0, g, 128:128 + nb, :] = onehot
            kwt_ref[0, g, 0:hd, :] = kwt[g * hd:(g + 1) * hd].astype(BF16)
            kwt_ref[0, g, hd:128, :] = zeros
            for stream, ref in ((3, vs_ref), (5, vw_ref)):
                a = kv_ref[0, :, stream * gw + (g // 2) * 128:stream * gw + (g // 2 + 1) * 128]
                if g % 2:
                    a = pltpu.roll(a, hd, 1)
                ref[0, g] = jnp.where(lane < hd, a, tail).astype(BF16)


def _nsa_prep(kv, nb):
    bsz, t, n = kv.shape
    g = N_KV_HEADS
    tl = PADL
    tp = PADL + t
    kern = functools.partial(_nsa_prep_kernel, nb=nb)
    kspec = lambda rows: pl.BlockSpec((1, g, rows, tl), lambda b, j: (b, 0, 0, j))
    vspec = pl.BlockSpec((1, g, tl, 128), lambda b, j: (b, 0, j, 0))
    return pl.pallas_call(
        kern,
        grid=(bsz, tp // tl),
        in_specs=[pl.BlockSpec((1, tl, n), lambda b, j: (b, jnp.maximum(j - 1, 0), 0))],
        out_specs=[kspec(128 + nb), kspec(128), vspec, vspec],
        out_shape=[jax.ShapeDtypeStruct((bsz, g, 128 + nb, tp), BF16), jax.ShapeDtypeStruct((bsz, g, 128, tp), BF16),
                   jax.ShapeDtypeStruct((bsz, g, tp, 128), BF16), jax.ShapeDtypeStruct((bsz, g, tp, 128), BF16)],
        compiler_params=_cparams(2),
        name="nsa_prep",
    )(kv)


def _split3(x):
    a = x.astype(BF16)
    r = x - a.astype(F32)
    b = r.astype(BF16)
    return a, b, (r - b.astype(F32)).astype(BF16)


def _top_blocks(score, blk, n_sel, n_rows):
    sel = jnp.zeros(score.shape, jnp.bool_)
    for _ in range(n_sel):
        m = jnp.max(score, axis=0, keepdims=True)
        first = jnp.min(jnp.where(score == m, blk, n_rows), axis=0, keepdims=True)
        pick = blk == first
        sel = sel | pick
        score = jnp.where(pick, -3.0e38, score)
    return sel


def _nsa_attn_prompt_kernel(q_ref, gt_ref, kct_ref, vc_ref, kst_ref, vs_ref, kwt_ref, vw_ref,
                            wtab_ref, stab_ref, atab_ref, poolt_ref, o_ref,
                            kc_ref, sw_ref, sa_ref, sb_ref, pa_ref, pb_ref, ala_ref, alb_ref, m_ref, acc_ref, *, nb, nc):
    qi = pl.program_id(2)
    t0 = qi * Q_BLOCK
    r_heads = GQA_R
    hd = HEAD_DIM
    rows = r_heads * Q_BLOCK
    nw = WINDOW + Q_BLOCK
    lane = lax.broadcasted_iota(jnp.int32, (Q_BLOCK, 128), 1)
    q = jnp.concatenate([jnp.where(lane == hd, NEG, q_ref[:, r * 128:(r + 1) * 128]) for r in range(r_heads)],
                        axis=0)

    n_idx = lax.broadcasted_iota(jnp.int32, (48, nc), 1)
    j_idx = lax.broadcasted_iota(jnp.int32, (48, nc), 0)
    rel = n_idx - (qi * (Q_BLOCK // CMP_STRIDE) - 8)
    shift = (((j_idx < 32) & (rel == jnp.where(j_idx < 16, j_idx, j_idx - 16)))
             | ((j_idx == 32) & (rel >= 16))).astype(F32).astype(BF16)
    kc_ref[0:ATAB_LANE, :] = kct_ref[0, 0, 0:ATAB_LANE, :]
    kc_ref[ATAB_LANE:128, :] = shift
    lane_r = lax.broadcasted_iota(jnp.int32, (rows, 128), 1)
    s_c = jnp.dot(jnp.where(lane_r >= ATAB_LANE, atab_ref[0], q), kc_ref[...], preferred_element_type=F32)
    m_c = jnp.max(s_c, axis=1, keepdims=True)
    e_c = jnp.exp(s_c - m_c)
    inv_c = jnp.where(m_c > NEG_VALID, 1.0 / jnp.maximum(jnp.sum(e_c, axis=1, keepdims=True), 1e-30), 0.0)
    p_c = e_c * inv_c
    o_c = jnp.dot(p_c.astype(BF16), vc_ref[0, 0], preferred_element_type=F32)

    grp = p_c[0:Q_BLOCK]
    for r in range(1, r_heads):
        grp = grp + p_c[r * Q_BLOCK:(r + 1) * Q_BLOCK]
    imp_t = jnp.zeros((nb, Q_BLOCK), F32)
    for part in _split3(grp):
        imp_t = imp_t + lax.dot_general(poolt_ref[...], part, (((1,), (1,)), ((), ())),
                                        preferred_element_type=F32)
    blk = lax.broadcasted_iota(jnp.int32, (nb, Q_BLOCK), 0)
    tq = t0 + lax.broadcasted_iota(jnp.int32, (nb, Q_BLOCK), 1)
    cur = tq // SEL_LEN
    forced = (blk == 0) | (blk == cur) | (blk == cur - 1)
    score = jnp.where(blk * SEL_LEN > tq, -1.0, jnp.where(forced, -3.0e38, imp_t))
    sel = forced | _top_blocks(score, blk, min(TOP_N, nb) - 3, nb)
    selb = jnp.where(sel, 0.0, NEG).T.astype(BF16)
    q_aug = jnp.concatenate([q, jnp.concatenate([selb] * r_heads, axis=0)], axis=1)

    c_w = pl.multiple_of(t0 + PADL - WINDOW, Q_BLOCK)
    sw_ref[...] = jnp.dot(q, kwt_ref[0, 0, :, pl.ds(c_w, nw)], preferred_element_type=F32) + wtab_ref[0]

    c_near = pl.multiple_of(t0 + PADL + Q_BLOCK - FAR_TILE, Q_BLOCK)
    n_far = jnp.maximum((c_near - PADL + FAR_TILE - 1) // FAR_TILE, 0)

    def tile_start(k):
        return pl.multiple_of(c_near - FAR_TILE * (jnp.clip(k, -1, n_far - 1) + 1), Q_BLOCK)

    def scores(k, s_out):
        s_out[...] = jnp.dot(q_aug, kst_ref[0, 0, :, pl.ds(tile_start(k), FAR_TILE)], preferred_element_type=F32)

    def pv_update(k, p_in, al_in):
        acc_ref[...] = al_in[...] * acc_ref[...] + jnp.dot(p_in[...], vs_ref[0, 0, pl.ds(tile_start(k), FAR_TILE), :],
                                                          preferred_element_type=F32)

    def stage(k, s_cur, s_nxt, p_cur, al_cur, p_prev, al_prev):
        scores(k + 1, s_nxt)
        pv_update(k - 1, p_prev, al_prev)
        m_old = m_ref[...]
        m_new = jnp.maximum(m_old, jnp.max(s_cur[...], axis=1, keepdims=True))
        al_cur[...] = jnp.exp(m_old - m_new)
        p_cur[...] = jnp.exp(s_cur[...] - m_new).astype(BF16)
        m_ref[...] = m_new

    s = jnp.dot(q_aug, kst_ref[0, 0, :, pl.ds(c_near, FAR_TILE)], preferred_element_type=F32) + stab_ref[0]
    scores(0, sa_ref)
    m_s = jnp.max(s, axis=1, keepdims=True)
    m_ref[...] = m_s
    pb_ref[...] = jnp.exp(s - m_s).astype(BF16)
    alb_ref[...] = jnp.ones_like(alb_ref)
    acc_ref[...] = jnp.zeros_like(acc_ref)

    def pair(j, _):
        stage(2 * j, sa_ref, sb_ref, pa_ref, ala_ref, pb_ref, alb_ref)

        @pl.when(2 * j + 1 < n_far)
        def _():
            stage(2 * j + 1, sb_ref, sa_ref, pb_ref, alb_ref, pa_ref, ala_ref)
        return 0

    lax.fori_loop(0, (n_far + 1) // 2, pair, 0)

    @pl.when(n_far % 2 == 1)
    def _():
        pv_update(n_far - 1, pa_ref, ala_ref)

    @pl.when(n_far % 2 == 0)
    def _():
        pv_update(n_far - 1, pb_ref, alb_ref)

    acc = acc_ref[...]
    o_s = acc[:, :hd] * (1.0 / acc[:, hd:hd + 1])

    s_w = sw_ref[...]
    m_w = jnp.max(s_w, axis=1, keepdims=True)
    acc_w = jnp.dot(jnp.exp(s_w - m_w).astype(BF16), vw_ref[0, 0, pl.ds(c_w, nw), :], preferred_element_type=F32)
    o_w = acc_w[:, :hd] * (1.0 / acc_w[:, hd:hd + 1])

    gt = gt_ref[0]
    for r in range(r_heads):
        sl = slice(r * Q_BLOCK, (r + 1) * Q_BLOCK)
        o = (gt[:, 3 * r:3 * r + 1] * o_c[sl, :hd] + gt[:, 3 * r + 1:3 * r + 2] * o_s[sl]
             + gt[:, 3 * r + 2:3 * r + 3] * o_w[sl])
        o_ref[:, r * hd:(r + 1) * hd] = o.astype(BF16)


def _nsa_attn_prompt(q_pad, gates_g, kct, vc, kst, vs, kwt, vw, wtab, stab, atab, poolt, bsz, seq):
    g, r_heads, hd = N_KV_HEADS, GQA_R, HEAD_DIM
    nq = seq // Q_BLOCK
    nb, nc = poolt.shape
    tp = PADL + seq
    kern = functools.partial(_nsa_attn_prompt_kernel, nb=nb, nc=nc)
    rows = r_heads * Q_BLOCK
    per_bg = lambda shape: pl.BlockSpec((1, 1) + shape, lambda b, gi, i: (b, gi, 0, 0))
    per_g = lambda shape: pl.BlockSpec((1,) + shape, lambda b, gi, i: (gi, 0, 0))
    return pl.pallas_call(
        kern,
        grid=(bsz, g, nq),
        in_specs=[pl.BlockSpec((Q_BLOCK, r_heads * 128), lambda b, gi, i: (b * nq + i, gi)),
                  pl.BlockSpec((1, Q_BLOCK, 128), lambda b, gi, i: (gi, b * nq + i, 0)),
                  per_bg((128, nc)), per_bg((nc, 128)),
                  per_bg((128 + nb, tp)), per_bg((tp, 128)),
                  per_bg((128, tp)), per_bg((tp, 128)),
                  per_g((r_heads * Q_BLOCK, WINDOW + Q_BLOCK)), per_g((r_heads * Q_BLOCK, FAR_TILE)),
                  per_g((r_heads * Q_BLOCK, 128)),
                  pl.BlockSpec((nb, nc), lambda b, gi, i: (0, 0))],
        out_specs=pl.BlockSpec((Q_BLOCK, r_heads * hd), lambda b, gi, i: (b * nq + i, gi)),
        out_shape=jax.ShapeDtypeStruct((bsz * seq, g * r_heads * hd), BF16),
        scratch_shapes=[pltpu.VMEM((128, nc), BF16), pltpu.VMEM((rows, WINDOW + Q_BLOCK), F32),
                        pltpu.VMEM((rows, FAR_TILE), F32), pltpu.VMEM((rows, FAR_TILE), F32),
                        pltpu.VMEM((rows, FAR_TILE), BF16), pltpu.VMEM((rows, FAR_TILE), BF16),
                        pltpu.VMEM((rows, 1), F32), pltpu.VMEM((rows, 1), F32),
                        pltpu.VMEM((rows, 1), F32), pltpu.VMEM((rows, 128), F32)],
        compiler_params=_cparams(3),
        name="nsa_attn_prompt",
    )(q_pad, gates_g, kct, vc, kst, vs, kwt, vw, wtab, stab, atab, poolt)


def _rel_bucket_np(dist):
    n = np.maximum(dist, 0)
    exact = REL_BUCKETS // 2
    nf = np.maximum(n, 1).astype(np.float32)
    large = exact + (np.log(nf / np.float32(exact)) / np.float32(math.log(REL_MAX_DIST / exact))
                     * np.float32(REL_BUCKETS - exact)).astype(np.int32)
    return np.where(n < exact, n, np.minimum(large, REL_BUCKETS - 1))


FAR_DIST = Q_BLOCK - CMP_STRIDE + 1
assert np.all(_rel_bucket_np(np.arange(FAR_DIST, 1 << 16)) == REL_BUCKETS - 1)


def _bias_table(rel_bias, dist, valid):
    relc = rel_bias.astype(F32) - rel_bias[REL_BUCKETS - 1].astype(F32)[None]
    onehot = np.eye(REL_BUCKETS, dtype=np.float32)[_rel_bucket_np(dist)]
    tab = jnp.einsum('...b,bh->...h', onehot, relc, precision=lax.Precision.HIGHEST)
    tab = jnp.where(valid[..., None], tab, NEG)
    return jnp.moveaxis(tab, -1, 0)


def _prompt_tables(rel_bias, nb, nc):
    g, r_heads = N_KV_HEADS, GQA_R
    qi = np.arange(Q_BLOCK)[:, None]
    nw = WINDOW + Q_BLOCK
    d_w = qi + WINDOW - np.arange(nw)[None, :]
    wtab = _bias_table(rel_bias, d_w, (d_w >= 0) & (d_w <= WINDOW)).reshape(g, r_heads * Q_BLOCK, nw)
    d_s = d_w[:, nw - FAR_TILE:]
    stab = _bias_table(rel_bias, d_s, d_s >= 0).reshape(g, r_heads * Q_BLOCK, FAR_TILE)
    d_c = qi - CMP_STRIDE * (np.arange(16)[None, :] - 8) - (CMP_LEN - 1)
    valid_c = d_c >= 0
    a = _bias_table(rel_bias, d_c, valid_c).reshape(g, r_heads * Q_BLOCK, 16)
    hi = a.astype(BF16)
    lo = jnp.where(jnp.asarray(np.tile(valid_c, (r_heads, 1)))[None], a - hi.astype(F32), 0.0).astype(BF16)
    fut = jnp.full((g, r_heads * Q_BLOCK, 1), NEG, BF16)
    atab = jnp.concatenate([jnp.zeros((g, r_heads * Q_BLOCK, ATAB_LANE), BF16), hi, lo, fut,
                            jnp.zeros((g, r_heads * Q_BLOCK, 15), BF16)], axis=-1)
    jb = np.arange(nb)[:, None]
    n = np.arange(nc)[None, :]
    poolt = jnp.asarray(((n >= CMP_PER_SEL * jb - 1) & (n <= CMP_PER_SEL * jb + CMP_PER_SEL - 1)).astype(np.float32),
                        BF16)
    return wtab, stab, atab, poolt


CMP_PAGES_PER_STEP = 8
SEL_PAGES_PER_STEP = 16


def _gather_cmp_kernel(pt_ref, *refs):
    del pt_ref
    page_refs = refs[:CMP_PAGES_PER_STEP]
    o_ref, t_ref = refs[CMP_PAGES_PER_STEP:]
    n_fb = t_ref.shape[0]
    for k in range(CMP_PAGES_PER_STEP):
        for fb in range(n_fb):
            t_ref[fb, k * PAGE_SIZE:(k + 1) * PAGE_SIZE, :] = page_refs[k][0, 0, fb * 128:(fb + 1) * 128, :].T
    rows = CMP_PAGES_PER_STEP * PAGE_SIZE // CMP_STRIDE
    for s in range(CMP_STRIDE):
        for fb in range(n_fb):
            c0 = (s * n_fb + fb) * 128
            o_ref[0, :, c0:c0 + 128] = t_ref[fb, pl.ds(s, rows, stride=CMP_STRIDE), :].astype(BF16)


def _gather_cmp(cache_t, layer, page_table):
    bsz, n_pages = page_table.shape
    half = cache_t.shape[2] // 2
    pp = CMP_PAGES_PER_STEP
    chunks = pp * PAGE_SIZE // CMP_STRIDE
    page_spec = lambda k: pl.BlockSpec((1, 1, half, PAGE_SIZE), lambda b, i, pt: (layer, pt[b, i * pp + k], 0, 0))
    return pl.pallas_call(
        _gather_cmp_kernel,
        grid_spec=pltpu.PrefetchScalarGridSpec(
            num_scalar_prefetch=1, grid=(bsz, n_pages // pp),
            in_specs=[page_spec(k) for k in range(pp)],
            out_specs=pl.BlockSpec((1, chunks, CMP_STRIDE * half), lambda b, i, pt: (b, i, 0)),
            scratch_shapes=[pltpu.VMEM((half // 128, pp * PAGE_SIZE, 128), F32)]),
        out_shape=jax.ShapeDtypeStruct((bsz, n_pages * PAGE_SIZE // CMP_STRIDE, CMP_STRIDE * half), BF16),
        compiler_params=_cparams(2),
        name="gather_cmp",
    )(page_table, *([cache_t] * pp))


def _nsa_attn_sample_kernel(pt_ref, q_ref, gt_ref, kct_ref, vc_ref, new_ref, win_ref,
                            tc_ref, tnear_ref, tnew_ref, tw_ref, pool_ref, e_ref, rsum_ref, *refs,
                            t, n_past_blk):
    del pt_ref
    pp = SEL_PAGES_PER_STEP
    k_refs, v_refs = refs[:pp], refs[pp:2 * pp]
    o_ref, m_ref, l_ref, acc_ref, selb_ref, oc_ref, ow_ref = refs[2 * pp:]
    g, r_heads, hd = N_KV_HEADS, GQA_R, HEAD_DIM
    gw = g * hd
    i = pl.program_id(1)
    n_steps = pl.num_programs(1)
    nt = (((1,), (1,)), ((), ()))
    q = q_ref[0]
    tile = pp * PAGE_SIZE

    @pl.when(i == 0)
    def _():
        new = new_ref[0]
        s_c = jnp.dot(q, kct_ref[0], preferred_element_type=F32) + tc_ref[...]
        m_c = jnp.max(s_c, axis=1, keepdims=True)
        e_c = jnp.exp(s_c - m_c)
        inv_c = jnp.where(m_c > NEG_VALID, 1.0 / jnp.maximum(jnp.sum(e_c, axis=1, keepdims=True), 1e-30), 0.0)
        p_c = e_c * inv_c
        oc_ref[...] = jnp.dot(p_c.astype(BF16), vc_ref[0], preferred_element_type=F32)
        imp0 = jnp.zeros((q.shape[0], n_past_blk), F32)
        for part in _split3(p_c):
            imp0 = imp0 + jnp.dot(part, pool_ref[...], preferred_element_type=F32)
        imp = jnp.zeros((q.shape[0], n_past_blk), F32)
        for part in _split3(imp0):
            imp = imp + jnp.dot(rsum_ref[...], part, preferred_element_type=F32)
        blk = lax.broadcasted_iota(jnp.int32, (n_past_blk, q.shape[0]), 0)
        forced = (blk == 0) | (blk == n_past_blk - 1)
        sel = _top_blocks(jnp.where(forced, FORCE_SCORE, imp.T), blk, min(TOP_N - 1, n_past_blk), n_past_blk)
        selb_ref[...] = jnp.where(sel, 0.0, NEG).T.astype(BF16)
        win = win_ref[0, 0].astype(BF16)
        s_w = jnp.dot(q, win[:gw], preferred_element_type=F32) + tw_ref[...]
        s_wn = lax.dot_general(q, new[:, 4 * gw:5 * gw], nt, preferred_element_type=F32) + tnew_ref[...]
        m_w = jnp.maximum(jnp.max(s_w, axis=1, keepdims=True), jnp.max(s_wn, axis=1, keepdims=True))
        p_w = jnp.exp(s_w - m_w)
        p_wn = jnp.exp(s_wn - m_w)
        l_w = jnp.sum(p_w, axis=1, keepdims=True) + jnp.sum(p_wn, axis=1, keepdims=True)
        o_w = (lax.dot_general(p_w.astype(BF16), win[gw:], nt, preferred_element_type=F32)
               + jnp.dot(p_wn.astype(BF16), new[:, 5 * gw:6 * gw], preferred_element_type=F32))
        ow_ref[...] = o_w / l_w
        s_n = lax.dot_general(q, new[:, 2 * gw:3 * gw], nt, preferred_element_type=F32) + tnew_ref[...]
        m_n = jnp.max(s_n, axis=1, keepdims=True)
        p_n = jnp.exp(s_n - m_n)
        m_ref[...] = m_n
        l_ref[...] = jnp.sum(p_n, axis=1, keepdims=True)
        acc_ref[...] = jnp.dot(p_n.astype(BF16), new[:, 3 * gw:4 * gw], preferred_element_type=F32)

    k_t = jnp.concatenate([r[0, 0] for r in k_refs], axis=1).astype(BF16)
    v_t = jnp.concatenate([r[0, 0] for r in v_refs], axis=1).astype(BF16)
    s = (jnp.dot(q, k_t, preferred_element_type=F32)
         + jnp.dot(selb_ref[...], e_ref[:, pl.ds(pl.multiple_of(i * tile, tile), tile)], preferred_element_type=F32))
    s = s + jnp.where(i == n_steps - 1, tnear_ref[...], 0.0)
    m_old = m_ref[...]
    m_new = jnp.maximum(m_old, jnp.max(s, axis=1, keepdims=True))
    alpha = jnp.exp(m_old - m_new)
    p = jnp.exp(s - m_new)
    m_ref[...] = m_new
    l_ref[...] = alpha * l_ref[...] + jnp.sum(p, axis=1, keepdims=True)
    acc_ref[...] = alpha * acc_ref[...] + lax.dot_general(p.astype(BF16), v_t, nt, preferred_element_type=F32)

    @pl.when(i == n_steps - 1)
    def _():
        gt = gt_ref[0]
        o = gt[:, 0:1] * oc_ref[...] + gt[:, 1:2] * (acc_ref[...] / l_ref[...]) + gt[:, 2:3] * ow_ref[...]
        for gi in range(g):
            for r in range(r_heads):
                h = gi * r_heads + r
                o_ref[0, :, h * hd:(h + 1) * hd] = o[h * t:(h + 1) * t, gi * hd:(gi + 1) * hd].astype(BF16)


def _nsa_attn_sample(q_bd, gates, kct, vc, new_pad, win_t, layer, cache_t, page_table, tabs, t):
    bsz, n_pages = page_table.shape
    g, r_heads, hd = N_KV_HEADS, GQA_R, HEAD_DIM
    gw = g * hd
    lanes = g * r_heads * t
    pp = SEL_PAGES_PER_STEP
    n_past_blk = n_pages * PAGE_SIZE // SEL_LEN
    kern = functools.partial(_nsa_attn_sample_kernel, t=t, n_past_blk=n_past_blk)
    per_b = lambda a: pl.BlockSpec((1,) + a.shape[1:], lambda b, i, pt: (b, 0, 0))
    full = lambda a: pl.BlockSpec(a.shape, lambda b, i, pt: (0, 0))
    page_spec = lambda fblk, k: pl.BlockSpec((1, 1, gw, PAGE_SIZE),
                                             lambda b, i, pt: (layer, pt[b, i * pp + k], fblk, 0))
    return pl.pallas_call(
        kern,
        grid_spec=pltpu.PrefetchScalarGridSpec(
            num_scalar_prefetch=1, grid=(bsz, n_pages // pp),
            in_specs=[per_b(q_bd), per_b(gates), per_b(kct), per_b(vc), per_b(new_pad),
                      pl.BlockSpec((1, 1) + win_t.shape[2:], lambda b, i, pt: (layer, b, 0, 0))]
                     + [full(a) for a in tabs]
                     + [page_spec(2, k) for k in range(pp)] + [page_spec(3, k) for k in range(pp)],
            out_specs=pl.BlockSpec((1, t, g * r_heads * hd), lambda b, i, pt: (b, 0, 0)),
            scratch_shapes=[pltpu.VMEM((lanes, 1), F32), pltpu.VMEM((lanes, 1), F32), pltpu.VMEM((lanes, gw), F32),
                            pltpu.VMEM((lanes, n_past_blk), BF16), pltpu.VMEM((lanes, gw), F32),
                            pltpu.VMEM((lanes, gw), F32)]),
        out_shape=jax.ShapeDtypeStruct((bsz, t, g * r_heads * hd), BF16),
        compiler_params=_cparams(2),
        name="nsa_attn_sample",
    )(page_table, q_bd, gates, kct, vc, new_pad, win_t, *tabs, *([cache_t] * (2 * pp)))


def _sample_tables(rel_bias, pos0, t, nc):
    g, r_heads = N_KV_HEADS, GQA_R
    lanes = g * r_heads * t
    tq = pos0 + np.arange(t)[:, None]

    def table(kpos, extra_valid=True):
        dist = tq - kpos[None, :]
        tab = _bias_table(rel_bias, dist, (dist >= 0) & extra_valid)
        return tab.reshape(lanes, kpos.shape[0])

    tc = table(CMP_STRIDE * np.arange(nc) + CMP_LEN - 1)
    tnear = table(pos0 - Q_BLOCK + np.arange(Q_BLOCK))
    tnear = jnp.pad(tnear, ((0, 0), (SEL_PAGES_PER_STEP * PAGE_SIZE - Q_BLOCK, 0)))
    new_pos = pos0 + np.arange(128)
    tnew = table(new_pos, (new_pos < pos0 + t)[None, :])
    kw = pos0 - WINDOW + np.arange(WINDOW)
    tw = table(kw, (tq - kw[None, :]) <= WINDOW)
    n_past_blk = pos0 // SEL_LEN
    jb = np.arange(n_past_blk)[None, :]
    n = np.arange(nc)[:, None]
    pool = jnp.asarray(((n >= CMP_PER_SEL * jb - 1) & (n <= CMP_PER_SEL * jb + CMP_PER_SEL - 1)).astype(np.float32),
                       BF16)
    e = jnp.asarray((np.arange(pos0)[None, :] // SEL_LEN == np.arange(n_past_blk)[:, None]).astype(np.float32), BF16)
    li = np.arange(lanes)
    same = (li[:, None] // (r_heads * t) == li[None, :] // (r_heads * t)) & (li[:, None] % t == li[None, :] % t)
    rsum = jnp.asarray(same.astype(np.float32), BF16)
    return tc, tnear, tnew, tw, pool, e, rsum


def _nsa_sample(xs2d, bsz, t, layer, cache_t, page_table, win_t, wts, tabs):
    g, r_heads, hd = N_KV_HEADS, GQA_R, HEAD_DIM
    gw = g * hd
    q, kv, kvb, gt = _nsa_proj(xs2d, wts["wq"], wts["wkv"], wts["wg"], tm=xs2d.shape[0])
    chunks = _gather_cmp(cache_t, layer, page_table)
    kct, vc = _nsa_cmp(chunks, wts["w1"], wts["pe_term"], wts["w2k"].T, wts["w2v"], transposed=False)
    q5 = q.reshape(bsz, t, g, r_heads, hd)
    q_bd = jnp.einsum('btgrd,gh->bgrthd', q5, jnp.eye(g, dtype=q.dtype)).reshape(bsz, g * r_heads * t, gw)
    gates = jnp.pad(gt[:, :3 * g * r_heads].reshape(bsz, t, g * r_heads, 3).transpose(0, 2, 1, 3)
                    .reshape(bsz, g * r_heads * t, 3), ((0, 0), (0, 0), (0, 125)))
    new_pad = jnp.pad(kvb.reshape(bsz, t, 6 * gw), ((0, 0), (0, 128 - t), (0, 0)))
    o = _nsa_attn_sample(q_bd, gates, kct, vc, new_pad, win_t, layer, cache_t, page_table, tabs, t)
    return o.reshape(bsz * t, g * r_heads * hd), kv


def _nsa_weights(w_in, cmp_pe, cmp_w1, cmp_b1, cmp_w2):
    d = w_in.shape[0]
    g, r_heads, hd = N_KV_HEADS, GQA_R, HEAD_DIM
    qd, kvd = g * r_heads * hd, 6 * g * hd
    wq = (w_in[:, :qd] * hd ** -0.5).astype(BF16)
    wq_pad = jnp.pad(wq.reshape(d, g * r_heads, hd), ((0, 0), (0, 0), (0, 128 - hd))).reshape(d, g * r_heads * 128)
    wkv = w_in[:, qd:qd + kvd].astype(BF16)
    wg = jnp.pad(w_in[:, qd + kvd:], ((0, 0), (0, 128 - 3 * g * r_heads))).astype(BF16)
    w1r = cmp_w1.reshape(2, CMP_LEN, hd, CMP_HID)
    w1 = jnp.concatenate([w1r[:, :CMP_STRIDE], w1r[:, CMP_STRIDE:]], axis=-1).astype(BF16)
    w1 = w1.reshape(2, CMP_STRIDE // CMP_PACK, CMP_PACK * hd, 2 * CMP_HID)
    pe_term = jnp.einsum('csd,csdh->ch', cmp_pe, w1r) + cmp_b1
    return dict(wq=wq, wq_pad=wq_pad, wkv=wkv, wg=wg, w1=w1, pe_term=pe_term,
                w2k=cmp_w2[0].astype(BF16), w2v=cmp_w2[1].astype(BF16))


def _nsa_prompt(x2d, bsz, seq, wts, tables):
    g, r_heads, hd = N_KV_HEADS, GQA_R, HEAD_DIM
    m = bsz * seq
    nc, nb = seq // CMP_STRIDE, seq // SEL_LEN
    q_pad, kv, kv_chunks, gt = _nsa_proj(x2d, wts["wq_pad"], wts["wkv"], wts["wg"], tm=512, chunk_seq=seq)
    w2k_t = jnp.pad(wts["w2k"].T, ((0, 128 - hd), (0, 0)))
    w2v_p = jnp.pad(wts["w2v"], ((0, 0), (0, 128 - hd)))
    kct, vc = _nsa_cmp(kv_chunks, wts["w1"], wts["pe_term"], w2k_t, w2v_p, transposed=True)
    kst, kwt, vs, vw = _nsa_prep(kv.reshape(bsz, seq, kv.shape[1]), nb)
    gates_g = jnp.pad(gt[:, :3 * g * r_heads].reshape(m, g, 3 * r_heads).transpose(1, 0, 2),
                      ((0, 0), (0, 0), (0, 128 - 3 * r_heads)))
    o = _nsa_attn_prompt(q_pad, gates_g, kct, vc, kst, vs, kwt, vw, *tables, bsz, seq)
    return o, kv


def kernel(x_prompt, x_sample, cache_nsa, state_nsa_win, state_pool, state_ffn, page_table, p_prompt, p_sample,
           rel_bias, nsa_w_in, nsa_cmp_pe, nsa_cmp_w1, nsa_cmp_b1, nsa_cmp_w2, nsa_w_out, pool_w, pool_b,
           pool_scale, ffn_w_up, ffn_conv_w, ffn_conv_b, ffn_w_down, ln_g, ln_b, ple_w_proj, ple_w_gate,
           ple_b_gate):
    bsz, seq, d = x_prompt.shape
    db, dt, _ = x_sample.shape
    depth = ffn_w_up.shape[0]
    alpha = (2.0 * depth) ** 0.25
    g, hd = N_KV_HEADS, HEAD_DIM
    gw = g * hd
    past_len = page_table.shape[1] * PAGE_SIZE
    mp, ms = bsz * seq, db * dt
    assert g * GQA_R * dt == 128 and past_len % SEL_LEN == 0 and seq % PADL == 0
    assert page_table.shape[1] % SEL_PAGES_PER_STEP == 0 and page_table.shape[1] % CMP_PAGES_PER_STEP == 0
    assert state_nsa_win.shape[2] == WINDOW and seq >= WINDOW and state_pool.shape[2] == POOL_HIST

    xp = x_prompt.reshape(mp, d)
    xs = x_sample.reshape(ms, d)
    tables_p = _prompt_tables(rel_bias, seq // SEL_LEN, seq // CMP_STRIDE)
    tables_s = _sample_tables(rel_bias, past_len, dt, past_len // CMP_STRIDE)
    cache_t = jnp.transpose(cache_nsa, (0, 1, 3, 4, 5, 2)).reshape(cache_nsa.shape[0], cache_nsa.shape[1], 4 * gw,
                                                                   PAGE_SIZE)
    win_t = jnp.transpose(state_nsa_win, (0, 1, 3, 4, 5, 2)).reshape(state_nsa_win.shape[0], db, 2 * gw, WINDOW)
    nsa_rp, nsa_rs, nsa_wp, nsa_ws, pool_hp, pool_hs, ffn_hp, ffn_hs = ([] for _ in range(8))
    for i in range(depth):
        j = i // 2
        g0, b0, g1, b1 = ln_g[i, 0][None], ln_b[i, 0][None], ln_g[i, 1][None], ln_b[i, 1][None]
        if i % 2 == 0:
            wts = _nsa_weights(nsa_w_in[j], nsa_cmp_pe[j], nsa_cmp_w1[j], nsa_cmp_b1[j], nsa_cmp_w2[j])
            o_p, kv_p = _nsa_prompt(xp, bsz, seq, wts, tables_p)
            o_s, kv_s = _nsa_sample(xs, db, dt, j, cache_t, page_table, win_t, wts, tables_s)
            w_out = nsa_w_out[j].astype(BF16)
            xp = _linear_ln(o_p, w_out, xp, g0, b0, alpha, tm=512)
            xs = _linear_ln(o_s, w_out, xs, g0, b0, alpha, tm=ms)
            kv_p3 = kv_p.reshape(bsz, seq, 6, g, hd)
            kv_s3 = kv_s.reshape(db, dt, 6, g, hd)
            nsa_rp.append(kv_p3[:, :, :4])
            nsa_rs.append(kv_s3[:, :, :4])
            nsa_wp.append(kv_p3[:, seq - WINDOW:, 4:])
            nsa_ws.append(jnp.concatenate([state_nsa_win[j][:, dt:], kv_s3[:, :, 4:]], axis=1))
        else:
            xp3 = xp.reshape(bsz, seq, d)
            xe = jnp.concatenate([state_pool[j], xs.reshape(db, dt, d)], axis=1)
            pool_hp.append(xp3[:, seq - POOL_HIST:])
            pool_hs.append(xe[:, xe.shape[1] - POOL_HIST:])
            pw = pool_w[j].astype(BF16)
            xp = _pool_ln_prompt(xp3, pw, pool_b[j][None], pool_scale[j][None], g0, b0, alpha).reshape(mp, d)
            xs = _pool_ln_sample(xe, dt, past_len, pw, pool_b[j][None], pool_scale[j][None], g0, b0,
                                 alpha).reshape(ms, d)
        w_up = ffn_w_up[i].astype(BF16)
        w_down = ffn_w_down[i].astype(BF16)
        cw, cb = ffn_conv_w[i], ffn_conv_b[i][None]
        h_p = _ffn_up_prompt(xp, w_up, cw, cb, seq)
        tail = xp.reshape(bsz, seq, d)[:, seq - 8:].reshape(bsz * 8, d)
        u_tail = _matmul(tail, w_up, tm=bsz * 8).reshape(bsz, 8, -1)
        ffn_hp.append(u_tail[:, 8 - (CONV_W - 1):])
        u_s = _matmul(xs, w_up, tm=ms).reshape(db, dt, -1)
        ue = jnp.concatenate([state_ffn[i], u_s], axis=1)
        ffn_hs.append(ue[:, ue.shape[1] - (CONV_W - 1):])
        h_s = _ffn_gate_sample(ue, cw, cb, dt).reshape(ms, -1)
        ple = (ple_w_gate[i].astype(BF16), ple_b_gate[i][None], ple_w_proj[i].astype(BF16))
        xp = _ffn_down_ln_ple(h_p, w_down, xp, g1, b1, p_prompt[i].reshape(mp, -1), *ple, alpha, 512)
        xs = _ffn_down_ln_ple(h_s, w_down, xs, g1, b1, p_sample[i].reshape(ms, -1), *ple, alpha, ms)
    return (xp.reshape(bsz, seq, d), xs.reshape(db, dt, d),
            jnp.stack(nsa_rp), jnp.stack(nsa_rs), jnp.stack(nsa_wp), jnp.stack(nsa_ws),
            jnp.stack(pool_hp), jnp.stack(pool_hs), jnp.stack(ffn_hp), jnp.stack(ffn_hs))
```

```python
import functools
import math

import numpy as np
import jax
import jax.numpy as jnp
from jax import lax
from jax.experimental import pallas as pl
from jax.experimental.pallas import tpu as pltpu

N_KV_HEADS = 4
GQA_R = 4
HEAD_DIM = 64
CMP_LEN = 32
CMP_STRIDE = 16
CMP_HID = 256
CMP_PACK = 4
SEL_LEN = 64
CMP_PER_SEL = SEL_LEN // CMP_STRIDE
TOP_N = 16
WINDOW = 512
Q_BLOCK = 128
FORCE_SCORE = 1.0e4
REL_BUCKETS = 32
REL_MAX_DIST = 128
POOL_WINDOWS = (2, 4, 8, 16)
POOL_MAX = max(POOL_WINDOWS)
POOL_HIST = POOL_MAX - 1
CONV_W = 3
LN_EPS = 1e-5
PAGE_SIZE = 128

NEG = -1.0e30
NEG_VALID = -1.0e29
BF16 = jnp.bfloat16
F32 = jnp.float32

VMEM_LIMIT_BYTES = 56 * 1024 * 1024
ROW_TILE = 512


def _cparams(n_grid):
    return pltpu.CompilerParams(dimension_semantics=("arbitrary",) * n_grid,
                                vmem_limit_bytes=VMEM_LIMIT_BYTES)


def _layer_norm(y, g, b):
    mu = jnp.mean(y, axis=-1, keepdims=True)
    yc = y - mu
    var = jnp.mean(yc * yc, axis=-1, keepdims=True)
    return yc * lax.rsqrt(var + LN_EPS) * g + b


FFN_TN = 256
HALO = 16


def _ffn_up_prompt_kernel(x_ref, xh_ref, w_ref, cw_ref, cb_ref, h_ref, xe_ref, ue_ref, *,
                          tm, d_ff, tiles_per_batch):
    i = pl.program_id(0)
    first = (i % tiles_per_batch) == 0
    halo = xh_ref[...]
    xe_ref[0:HALO, :] = jnp.where(first, jnp.zeros_like(halo), halo).astype(BF16)
    xe_ref[HALO:HALO + tm, :] = x_ref[...].astype(BF16)
    xe = xe_ref[...]
    for j in range(d_ff // FFN_TN):
        c_parts = []
        for half in range(2):
            c0 = half * d_ff + j * FFN_TN
            ue_ref[...] = jnp.dot(xe, w_ref[:, c0:c0 + FFN_TN], preferred_element_type=F32)
            cw = cw_ref[:, c0:c0 + FFN_TN]
            c = cb_ref[:, c0:c0 + FFN_TN]
            for k in range(CONV_W):
                off = HALO - (CONV_W - 1) + k
                c = c + cw[k:k + 1, :] * ue_ref[off:off + tm, :]
            c_parts.append(c)
        h_ref[:, j * FFN_TN:(j + 1) * FFN_TN] = (jax.nn.gelu(c_parts[1]) * c_parts[0]).astype(BF16)


def _ffn_up_prompt(x, w_up, conv_w, conv_b, seq, tm=ROW_TILE):
    m, d = x.shape
    d_ff = w_up.shape[1] // 2
    tpb = seq // tm
    kern = functools.partial(_ffn_up_prompt_kernel, tm=tm, d_ff=d_ff, tiles_per_batch=tpb)
    return pl.pallas_call(
        kern,
        grid=(m // tm,),
        in_specs=[
            pl.BlockSpec((tm, d), lambda i: (i, 0)),
            pl.BlockSpec((HALO, d), lambda i: (jnp.maximum(i * (tm // HALO) - 1, 0), 0)),
            pl.BlockSpec(w_up.shape, lambda i: (0, 0)),
            pl.BlockSpec(conv_w.shape, lambda i: (0, 0)),
            pl.BlockSpec(conv_b.shape, lambda i: (0, 0)),
        ],
        out_specs=pl.BlockSpec((tm, d_ff), lambda i: (i, 0)),
        out_shape=jax.ShapeDtypeStruct((m, d_ff), BF16),
        scratch_shapes=[pltpu.VMEM((HALO + tm, d), BF16), pltpu.VMEM((HALO + tm, FFN_TN), F32)],
        compiler_params=_cparams(1),
        name="ffn_up_prompt",
    )(x, x, w_up, conv_w, conv_b)


def _ffn_gate_sample_kernel(ue_ref, cw_ref, cb_ref, h_ref, *, t, d_ff):
    for j in range(d_ff // FFN_TN):
        c_parts = []
        for half in range(2):
            c0 = half * d_ff + j * FFN_TN
            cw = cw_ref[:, c0:c0 + FFN_TN]
            c = cb_ref[:, c0:c0 + FFN_TN][None]
            for k in range(CONV_W):
                c = c + cw[k:k + 1, :][None] * ue_ref[:, k:k + t, c0:c0 + FFN_TN]
            c_parts.append(c)
        h_ref[:, :, j * FFN_TN:(j + 1) * FFN_TN] = (jax.nn.gelu(c_parts[1]) * c_parts[0]).astype(BF16)


def _ffn_gate_sample(ue, conv_w, conv_b, t):
    b, te, n2 = ue.shape
    d_ff = n2 // 2
    bb = 8
    kern = functools.partial(_ffn_gate_sample_kernel, t=t, d_ff=d_ff)
    return pl.pallas_call(
        kern,
        grid=(b // bb,),
        in_specs=[
            pl.BlockSpec((bb, te, n2), lambda i: (i, 0, 0)),
            pl.BlockSpec(conv_w.shape, lambda i: (0, 0)),
            pl.BlockSpec(conv_b.shape, lambda i: (0, 0)),
        ],
        out_specs=pl.BlockSpec((bb, t, d_ff), lambda i: (i, 0, 0)),
        out_shape=jax.ShapeDtypeStruct((b, t, d_ff), BF16),
        compiler_params=_cparams(1),
        name="ffn_gate_sample",
    )(ue, conv_w, conv_b)


def _matmul_kernel(x_ref, w_ref, o_ref):
    o_ref[...] = jnp.dot(x_ref[...].astype(BF16), w_ref[...], preferred_element_type=F32).astype(o_ref.dtype)


def _matmul(x, w, tm, tn=None, out_dtype=F32):
    m, k = x.shape
    n = w.shape[1]
    tn = n if tn is None else tn
    return pl.pallas_call(
        _matmul_kernel,
        grid=(m // tm, n // tn),
        in_specs=[pl.BlockSpec((tm, k), lambda i, j: (i, 0)),
                  pl.BlockSpec((k, tn), lambda i, j: (0, j))],
        out_specs=pl.BlockSpec((tm, tn), lambda i, j: (i, j)),
        out_shape=jax.ShapeDtypeStruct((m, n), out_dtype),
        compiler_params=_cparams(2),
        name="matmul",
    )(x, w)


def _linear_ln_kernel(a_ref, w_ref, r_ref, g_ref, b_ref, o_ref, *, alpha):
    f = jnp.dot(a_ref[...].astype(BF16), w_ref[...], preferred_element_type=F32)
    o_ref[...] = _layer_norm(alpha * r_ref[...] + f, g_ref[...], b_ref[...])


def _linear_ln(a, w, resid, g, b, alpha, tm):
    m, k = a.shape
    d = w.shape[1]
    kern = functools.partial(_linear_ln_kernel, alpha=alpha)
    return pl.pallas_call(
        kern,
        grid=(m // tm,),
        in_specs=[pl.BlockSpec((tm, k), lambda i: (i, 0)),
                  pl.BlockSpec((k, d), lambda i: (0, 0)),
                  pl.BlockSpec((tm, d), lambda i: (i, 0)),
                  pl.BlockSpec((1, d), lambda i: (0, 0)),
                  pl.BlockSpec((1, d), lambda i: (0, 0))],
        out_specs=pl.BlockSpec((tm, d), lambda i: (i, 0)),
        out_shape=jax.ShapeDtypeStruct((m, d), F32),
        compiler_params=_cparams(1),
        name="linear_ln",
    )(a, w, resid, g, b)


def _ffn_down_kernel(h_ref, wd_ref, r_ref, g_ref, b_ref, p_ref, wg_ref, bg_ref, wp_ref, o_ref, *, alpha):
    f = jnp.dot(h_ref[...], wd_ref[...], preferred_element_type=F32)
    x2 = _layer_norm(alpha * r_ref[...] + f, g_ref[...], b_ref[...])
    gate = jax.nn.sigmoid(jnp.dot(x2.astype(BF16), wg_ref[...], preferred_element_type=F32) + bg_ref[...])
    inj = jnp.dot(p_ref[...].astype(BF16), wp_ref[...], preferred_element_type=F32)
    o_ref[...] = x2 + gate * inj


def _ffn_down_ln_ple(h, w_down, resid, g, b, p, w_gate, b_gate, w_proj, alpha, tm):
    m, k = h.shape
    d = w_down.shape[1]
    pd = p.shape[1]
    kern = functools.partial(_ffn_down_kernel, alpha=alpha)
    full = lambda shape: pl.BlockSpec(shape, lambda i: (0, 0))
    return pl.pallas_call(
        kern,
        grid=(m // tm,),
        in_specs=[pl.BlockSpec((tm, k), lambda i: (i, 0)), full((k, d)),
                  pl.BlockSpec((tm, d), lambda i: (i, 0)), full((1, d)), full((1, d)),
                  pl.BlockSpec((tm, pd), lambda i: (i, 0)), full((d, d)), full((1, d)), full((pd, d))],
        out_specs=pl.BlockSpec((tm, d), lambda i: (i, 0)),
        out_shape=jax.ShapeDtypeStruct((m, d), F32),
        compiler_params=_cparams(1),
        name="ffn_down_ln_ple",
    )(h, w_down, resid, g, b, p, w_gate, b_gate, w_proj)


def _pool_finish(x, diffs, w_ref, pb_ref, ps_ref, g_ref, b_ref, alpha):
    ys = [jnp.dot(dg.astype(BF16), w_ref[gi], preferred_element_type=F32) for gi, dg in enumerate(diffs)]
    y = jnp.concatenate(ys, axis=-1) + pb_ref[...]
    return _layer_norm(alpha * x + ps_ref[...] * y, g_ref[...], b_ref[...])


def _pool_prompt_kernel(x_ref, xh_ref, w_ref, pb_ref, ps_ref, g_ref, b_ref, o_ref, xe_ref, *,
                        tt, gw, alpha):
    i = pl.program_id(1)
    halo = xh_ref[0]
    xe_ref[0:POOL_MAX, :] = jnp.where(i == 0, jnp.zeros_like(halo), halo)
    xe_ref[POOL_MAX:POOL_MAX + tt, :] = x_ref[0]
    pos = i * tt + lax.broadcasted_iota(jnp.int32, (tt, 1), 0)
    x = x_ref[0]
    diffs = []
    for gi, win in enumerate(POOL_WINDOWS):
        lo = gi * gw
        tot = x[:, lo:lo + gw]
        for k in range(1, win):
            tot = tot + xe_ref[POOL_MAX - k:POOL_MAX - k + tt, lo:lo + gw]
        cnt = jnp.minimum(win, pos + 1).astype(F32)
        diffs.append(tot / cnt - x[:, lo:lo + gw])
    o_ref[0] = _pool_finish(x, diffs, w_ref, pb_ref, ps_ref, g_ref, b_ref, alpha)


def _pool_ln_prompt(x, w, pb, ps, g, b, alpha, tt=ROW_TILE):
    bsz, t, d = x.shape
    gw = d // len(POOL_WINDOWS)
    kern = functools.partial(_pool_prompt_kernel, tt=tt, gw=gw, alpha=alpha)
    vec = pl.BlockSpec((1, d), lambda bi, i: (0, 0))
    return pl.pallas_call(
        kern,
        grid=(bsz, t // tt),
        in_specs=[pl.BlockSpec((1, tt, d), lambda bi, i: (bi, i, 0)),
                  pl.BlockSpec((1, POOL_MAX, d), lambda bi, i: (bi, jnp.maximum(i * (tt // POOL_MAX) - 1, 0), 0)),
                  pl.BlockSpec(w.shape, lambda bi, i: (0, 0, 0)), vec, vec, vec, vec],
        out_specs=pl.BlockSpec((1, tt, d), lambda bi, i: (bi, i, 0)),
        out_shape=jax.ShapeDtypeStruct((bsz, t, d), F32),
        scratch_shapes=[pltpu.VMEM((POOL_MAX + tt, d), F32)],
        compiler_params=_cparams(2),
        name="pool_ln_prompt",
    )(x, x, w, pb, ps, g, b)


def _pool_sample_kernel(xe_ref, w_ref, pb_ref, ps_ref, g_ref, b_ref, o_ref, *, t, nh, gw, pos0, alpha):
    bb = xe_ref.shape[0]
    d = xe_ref.shape[2]
    x = xe_ref[:, nh:nh + t, :].reshape(bb * t, d)
    cnt_np = np.minimum(np.array(POOL_WINDOWS)[:, None], pos0 + np.arange(t)[None, :] + 1).astype(np.float32)
    diffs = []
    for gi, win in enumerate(POOL_WINDOWS):
        lo = gi * gw
        tot = xe_ref[:, nh:nh + t, lo:lo + gw]
        for k in range(1, win):
            tot = tot + xe_ref[:, nh - k:nh - k + t, lo:lo + gw]
        if np.all(cnt_np[gi] == win):
            mean = tot / float(win)
        else:
            t_idx = lax.broadcasted_iota(jnp.int32, (1, t, 1), 1)
            mean = tot / jnp.minimum(win, pos0 + t_idx + 1).astype(F32)
        diffs.append(mean.reshape(bb * t, gw) - x[:, lo:lo + gw])
    o_ref[...] = _pool_finish(x, diffs, w_ref, pb_ref, ps_ref, g_ref, b_ref, alpha).reshape(bb, t, d)


def _pool_ln_sample(xe, t, pos0, w, pb, ps, g, b, alpha, bb=8):
    bsz, te, d = xe.shape
    gw = d // len(POOL_WINDOWS)
    kern = functools.partial(_pool_sample_kernel, t=t, nh=te - t, gw=gw, pos0=pos0, alpha=alpha)
    vec = pl.BlockSpec((1, d), lambda i: (0, 0))
    return pl.pallas_call(
        kern,
        grid=(bsz // bb,),
        in_specs=[pl.BlockSpec((bb, te, d), lambda i: (i, 0, 0)),
                  pl.BlockSpec(w.shape, lambda i: (0, 0, 0)), vec, vec, vec, vec],
        out_specs=pl.BlockSpec((bb, t, d), lambda i: (i, 0, 0)),
        out_shape=jax.ShapeDtypeStruct((bsz, t, d), F32),
        compiler_params=_cparams(1),
        name="pool_ln_sample",
    )(xe, w, pb, ps, g, b)


def _nsa_proj_kernel(x_ref, wq_ref, wkv_ref, wg_ref, q_ref, kv_ref, kvb_ref, gt_ref, *scratch):
    xb = x_ref[...].astype(BF16)
    q_ref[...] = jnp.dot(xb, wq_ref[...], preferred_element_type=F32).astype(BF16)
    kv = jnp.dot(xb, wkv_ref[...], preferred_element_type=F32)
    kv_ref[...] = kv
    gt_ref[...] = jax.nn.sigmoid(jnp.dot(xb, wg_ref[...], preferred_element_type=F32))
    if not scratch:
        kvb_ref[...] = kv.astype(BF16)
        return
    (t_ref,) = scratch
    n_fb, tm = t_ref.shape[0], t_ref.shape[1]
    for fb in range(n_fb):
        t_ref[fb] = kv[:, fb * 128:(fb + 1) * 128]
    for s in range(CMP_STRIDE):
        for fb in range(n_fb):
            kvb_ref[0, s, :, fb * 128:(fb + 1) * 128] = (
                t_ref[fb, pl.ds(s, tm // CMP_STRIDE, stride=CMP_STRIDE), :].astype(BF16))


def _nsa_proj(x, wq, wkv, wg, tm, chunk_seq=None):
    m, d = x.shape
    nq, nkv, ng = wq.shape[1], wkv.shape[1], wg.shape[1]
    full = lambda w: pl.BlockSpec(w.shape, lambda i: (0, 0))
    row = lambda n: pl.BlockSpec((tm, n), lambda i: (i, 0))
    if chunk_seq is None:
        kvb_spec, kvb_shape, scratch = row(nkv), jax.ShapeDtypeStruct((m, nkv), BF16), []
    else:
        feat = 2 * N_KV_HEADS * HEAD_DIM
        tpb = chunk_seq // tm
        kvb_spec = pl.BlockSpec((1, CMP_STRIDE, tm // CMP_STRIDE, feat), lambda i: (i // tpb, 0, i % tpb, 0))
        kvb_shape = jax.ShapeDtypeStruct((m // chunk_seq, CMP_STRIDE, chunk_seq // CMP_STRIDE, feat), BF16)
        scratch = [pltpu.VMEM((feat // 128, tm, 128), F32)]
    return pl.pallas_call(
        _nsa_proj_kernel,
        grid=(m // tm,),
        in_specs=[row(d), full(wq), full(wkv), full(wg)],
        out_specs=[row(nq), row(nkv), kvb_spec, row(ng)],
        out_shape=[jax.ShapeDtypeStruct((m, nq), BF16), jax.ShapeDtypeStruct((m, nkv), F32),
                   kvb_shape, jax.ShapeDtypeStruct((m, ng), F32)],
        scratch_shapes=scratch,
        compiler_params=_cparams(1),
        name="nsa_proj",
    )(x, wq, wkv, wg)


def _nsa_cmp_kernel(*refs, nc, transposed):
    ns = CMP_STRIDE
    x_refs = refs[:ns]
    w1_ref, pe_ref, w2k_ref, w2v_ref, ko_ref, vo_ref = refs[ns:]
    hd = HEAD_DIM
    gw = N_KV_HEADS * hd
    last = lax.broadcasted_iota(jnp.int32, (nc, CMP_HID), 0) == nc - 1
    nt = (((1,), (1,)), ((), ()))
    for c in range(2):
        for g in range(N_KV_HEADS):
            acc = jnp.zeros((nc, 2 * CMP_HID), F32)
            lo = c * gw + g * hd
            for k in range(ns // CMP_PACK):
                xs = jnp.concatenate([x_refs[k * CMP_PACK + i][:, lo:lo + hd] for i in range(CMP_PACK)], axis=1)
                acc = acc + jnp.dot(xs, w1_ref[c, k], preferred_element_type=F32)
            h1 = jnp.where(last, 0.0, pltpu.roll(acc[:, CMP_HID:], nc - 1, 0))
            hid = jax.nn.gelu(acc[:, :CMP_HID] + h1 + pe_ref[c:c + 1, :]).astype(BF16)
            if c == 0 and transposed:
                ko_ref[0, g] = lax.dot_general(w2k_ref[...], hid, nt, preferred_element_type=F32).astype(BF16)
            elif c == 0:
                ko_ref[0, g * hd:(g + 1) * hd, :] = lax.dot_general(w2k_ref[...], hid, nt,
                                                                    preferred_element_type=F32).astype(BF16)
            elif transposed:
                vo_ref[0, g] = jnp.dot(hid, w2v_ref[...], preferred_element_type=F32).astype(BF16)
            else:
                vo_ref[0, :, g * hd:(g + 1) * hd] = jnp.dot(hid, w2v_ref[...], preferred_element_type=F32).astype(BF16)


def _nsa_cmp(xv, w1, pe_term, w2k, w2v, transposed):
    g, hd = N_KV_HEADS, HEAD_DIM
    feat = 2 * g * hd
    bsz = xv.shape[0]
    if xv.ndim == 4:
        nc = xv.shape[2]
        x_specs = [pl.BlockSpec((None, None, nc, feat), functools.partial(lambda b, s: (b, s, 0, 0), s=s))
                   for s in range(CMP_STRIDE)]
    else:
        nc = xv.shape[1]
        x_specs = [pl.BlockSpec((None, nc, feat), functools.partial(lambda b, s: (b, 0, s), s=s))
                   for s in range(CMP_STRIDE)]
    kern = functools.partial(_nsa_cmp_kernel, nc=nc, transposed=transposed)
    full = lambda w: pl.BlockSpec(w.shape, lambda b: (0,) * w.ndim)
    if transposed:
        out_specs = [pl.BlockSpec((1, g, 128, nc), lambda b: (b, 0, 0, 0)),
                     pl.BlockSpec((1, g, nc, 128), lambda b: (b, 0, 0, 0))]
        out_shape = [jax.ShapeDtypeStruct((bsz, g, 128, nc), BF16), jax.ShapeDtypeStruct((bsz, g, nc, 128), BF16)]
    else:
        out_specs = [pl.BlockSpec((1, g * hd, nc), lambda b: (b, 0, 0)), pl.BlockSpec((1, nc, g * hd), lambda b: (b, 0, 0))]
        out_shape = [jax.ShapeDtypeStruct((bsz, g * hd, nc), BF16), jax.ShapeDtypeStruct((bsz, nc, g * hd), BF16)]
    return pl.pallas_call(
        kern,
        grid=(bsz,),
        in_specs=x_specs + [full(w1), full(pe_term), full(w2k), full(w2v)],
        out_specs=out_specs,
        out_shape=out_shape,
        compiler_params=_cparams(1),
        name="nsa_cmp",
    )(*([xv] * CMP_STRIDE), w1, pe_term, w2k, w2v)


PADL = 512
FAR_TILE = 512
ATAB_LANE = 80


def _nsa_prep_kernel(kv_ref, kst_ref, kwt_ref, vs_ref, vw_ref, *, nb):
    j = pl.program_id(1)
    tl = PADL
    hd = HEAD_DIM
    gw = N_KV_HEADS * hd

    @pl.when(j == 0)
    def _():
        for g in range(N_KV_HEADS):
            kst_ref[0, g] = (lax.broadcasted_iota(jnp.int32, (128 + nb, tl), 0) == hd).astype(BF16)
            kwt_ref[0, g] = (lax.broadcasted_iota(jnp.int32, (128, tl), 0) == hd).astype(BF16)
            vs_ref[0, g] = jnp.zeros((tl, 128), BF16)
            vw_ref[0, g] = jnp.zeros((tl, 128), BF16)

    @pl.when(j > 0)
    def _():
        kst = kv_ref[0, :, 2 * gw:3 * gw].T
        kwt = kv_ref[0, :, 4 * gw:5 * gw].T
        pos = (j - 1) * tl + lax.broadcasted_iota(jnp.int32, (nb, tl), 1)
        onehot = (pos // SEL_LEN == lax.broadcasted_iota(jnp.int32, (nb, tl), 0)).astype(BF16)
        zeros = jnp.zeros((128 - hd, tl), BF16)
        lane = lax.broadcasted_iota(jnp.int32, (tl, 128), 1)
        tail = (lane == hd).astype(F32)
        for g in range(N_KV_HEADS):
            kst_ref[0, g, 0:hd, :] = kst[g * hd:(g + 1) * hd].astype(BF16)
            kst_ref[0, g, hd:128, :] = zeros
            kst_ref[0, g, 128:128 + nb, :] = onehot
            kwt_ref[0, g, 0:hd, :] = kwt[g * hd:(g + 1) * hd].astype(BF16)
            kwt_ref[0, g, hd:128, :] = zeros
            for stream, ref in ((3, vs_ref), (5, vw_ref)):
                a = kv_ref[0, :, stream * gw + (g // 2) * 128:stream * gw + (g // 2 + 1) * 128]
                if g % 2:
                    a = pltpu.roll(a, hd, 1)
                ref[0, g] = jnp.where(lane < hd, a, tail).astype(BF16)


def _nsa_prep(kv, nb):
    bsz, t, n = kv.shape
    g = N_KV_HEADS
    tl = PADL
    tp = PADL + t
    kern = functools.partial(_nsa_prep_kernel, nb=nb)
    kspec = lambda rows: pl.BlockSpec((1, g, rows, tl), lambda b, j: (b, 0, 0, j))
    vspec = pl.BlockSpec((1, g, tl, 128), lambda b, j: (b, 0, j, 0))
    return pl.pallas_call(
        kern,
        grid=(bsz, tp // tl),
        in_specs=[pl.BlockSpec((1, tl, n), lambda b, j: (b, jnp.maximum(j - 1, 0), 0))],
        out_specs=[kspec(128 + nb), kspec(128), vspec, vspec],
        out_shape=[jax.ShapeDtypeStruct((bsz, g, 128 + nb, tp), BF16), jax.ShapeDtypeStruct((bsz, g, 128, tp), BF16),
                   jax.ShapeDtypeStruct((bsz, g, tp, 128), BF16), jax.ShapeDtypeStruct((bsz, g, tp, 128), BF16)],
        compiler_params=_cparams(2),
        name="nsa_prep",
    )(kv)


def _split3(x):
    a = x.astype(BF16)
    r = x - a.astype(F32)
    b = r.astype(BF16)
    return a, b, (r - b.astype(F32)).astype(BF16)


def _top_blocks(score, blk, n_sel, n_rows):
    sel = jnp.zeros(score.shape, jnp.bool_)
    for _ in range(n_sel):
        m = jnp.max(score, axis=0, keepdims=True)
        first = jnp.min(jnp.where(score == m, blk, n_rows), axis=0, keepdims=True)
        pick = blk == first
        sel = sel | pick
        score = jnp.where(pick, -3.0e38, score)
    return sel


def _nsa_attn_prompt_kernel(q_ref, gt_ref, kct_ref, vc_ref, kst_ref, vs_ref, kwt_ref, vw_ref,
                            wtab_ref, stab_ref, atab_ref, poolt_ref, o_ref,
                            kc_ref, sw_ref, pw_ref, sa_ref, sb_ref, pa_ref, pb_ref, ala_ref, alb_ref, m_ref, acc_ref, *, nb, nc):
    qi = pl.program_id(2)
    t0 = qi * Q_BLOCK
    r_heads = GQA_R
    hd = HEAD_DIM
    rows = r_heads * Q_BLOCK
    nw = WINDOW + Q_BLOCK
    lane = lax.broadcasted_iota(jnp.int32, (Q_BLOCK, 128), 1)
    q = jnp.concatenate([jnp.where(lane == hd, NEG, q_ref[:, r * 128:(r + 1) * 128]) for r in range(r_heads)],
                        axis=0)

    n_idx = lax.broadcasted_iota(jnp.int32, (48, nc), 1)
    j_idx = lax.broadcasted_iota(jnp.int32, (48, nc), 0)
    rel = n_idx - (qi * (Q_BLOCK // CMP_STRIDE) - 8)
    shift = (((j_idx < 32) & (rel == jnp.where(j_idx < 16, j_idx, j_idx - 16)))
             | ((j_idx == 32) & (rel >= 16))).astype(F32).astype(BF16)
    kc_ref[0:ATAB_LANE, :] = kct_ref[0, 0, 0:ATAB_LANE, :]
    kc_ref[ATAB_LANE:128, :] = shift
    lane_r = lax.broadcasted_iota(jnp.int32, (rows, 128), 1)
    s_c = jnp.dot(jnp.where(lane_r >= ATAB_LANE, atab_ref[0], q), kc_ref[...], preferred_element_type=F32)
    m_c = jnp.max(s_c, axis=1, keepdims=True)
    e_c = jnp.exp(s_c - m_c)
    inv_c = jnp.where(m_c > NEG_VALID, 1.0 / jnp.maximum(jnp.sum(e_c, axis=1, keepdims=True), 1e-30), 0.0)
    p_c = e_c * inv_c
    o_c = jnp.dot(p_c.astype(BF16), vc_ref[0, 0], preferred_element_type=F32)

    grp = p_c[0:Q_BLOCK]
    for r in range(1, r_heads):
        grp = grp + p_c[r * Q_BLOCK:(r + 1) * Q_BLOCK]
    imp_t = jnp.zeros((nb, Q_BLOCK), F32)
    for part in _split3(grp):
        imp_t = imp_t + lax.dot_general(poolt_ref[...], part, (((1,), (1,)), ((), ())),
                                        preferred_element_type=F32)
    blk = lax.broadcasted_iota(jnp.int32, (nb, Q_BLOCK), 0)
    tq = t0 + lax.broadcasted_iota(jnp.int32, (nb, Q_BLOCK), 1)
    cur = tq // SEL_LEN
    forced = (blk == 0) | (blk == cur) | (blk == cur - 1)
    score = jnp.where(blk * SEL_LEN > tq, -1.0, jnp.where(forced, -3.0e38, imp_t))
    sel = forced | _top_blocks(score, blk, min(TOP_N, nb) - 3, nb)
    selb = jnp.where(sel, 0.0, NEG).T.astype(BF16)
    q_aug = jnp.concatenate([q, jnp.concatenate([selb] * r_heads, axis=0)], axis=1)

    c_w = pl.multiple_of(t0 + PADL - WINDOW, Q_BLOCK)
    sw_ref[...] = jnp.dot(q, kwt_ref[0, 0, :, pl.ds(c_w, nw)], preferred_element_type=F32) + wtab_ref[0]

    c_near = pl.multiple_of(t0 + PADL + Q_BLOCK - FAR_TILE, Q_BLOCK)
    n_far = jnp.maximum((c_near - PADL + FAR_TILE - 1) // FAR_TILE, 0)

    def tile_start(k):
        return pl.multiple_of(c_near - FAR_TILE * (jnp.clip(k, -1, n_far - 1) + 1), Q_BLOCK)

    def scores(k, s_out):
        s_out[...] = jnp.dot(q_aug, kst_ref[0, 0, :, pl.ds(tile_start(k), FAR_TILE)], preferred_element_type=F32)

    def pv_update(k, p_in, al_in):
        acc_ref[...] = al_in[...] * acc_ref[...] + jnp.dot(p_in[...], vs_ref[0, 0, pl.ds(tile_start(k), FAR_TILE), :],
                                                          preferred_element_type=F32)

    def stage(k, s_cur, s_nxt, p_cur, al_cur, p_prev, al_prev):
        scores(k + 1, s_nxt)
        pv_update(k - 1, p_prev, al_prev)
        m_old = m_ref[...]
        m_new = jnp.maximum(m_old, jnp.max(s_cur[...], axis=1, keepdims=True))
        al_cur[...] = jnp.exp(m_old - m_new)
        p_cur[...] = jnp.exp(s_cur[...] - m_new).astype(BF16)
        m_ref[...] = m_new

    s = jnp.dot(q_aug, kst_ref[0, 0, :, pl.ds(c_near, FAR_TILE)], preferred_element_type=F32) + stab_ref[0]
    scores(0, sa_ref)
    m_s = jnp.max(s, axis=1, keepdims=True)
    m_ref[...] = m_s
    pb_ref[...] = jnp.exp(s - m_s).astype(BF16)
    alb_ref[...] = jnp.ones_like(alb_ref)
    acc_ref[...] = jnp.zeros_like(acc_ref)

    def pair(j, _):
        stage(2 * j, sa_ref, sb_ref, pa_ref, ala_ref, pb_ref, alb_ref)

        @pl.when(2 * j + 1 < n_far)
        def _():
            stage(2 * j + 1, sb_ref, sa_ref, pb_ref, alb_ref, pa_ref, ala_ref)
        return 0

    lax.fori_loop(0, (n_far + 1) // 2, pair, 0)

    def last_stage(p_in, al_in):
        pv_update(n_far - 1, p_in, al_in)
        s_w = sw_ref[...]
        pw_ref[...] = jnp.exp(s_w - jnp.max(s_w, axis=1, keepdims=True)).astype(BF16)

    @pl.when(n_far % 2 == 1)
    def _():
        last_stage(pa_ref, ala_ref)

    @pl.when(n_far % 2 == 0)
    def _():
        last_stage(pb_ref, alb_ref)

    acc = acc_ref[...]
    o_s = acc[:, :hd] * (1.0 / acc[:, hd:hd + 1])

    acc_w = jnp.dot(pw_ref[...], vw_ref[0, 0, pl.ds(c_w, nw), :], preferred_element_type=F32)
    o_w = acc_w[:, :hd] * (1.0 / acc_w[:, hd:hd + 1])

    gt = gt_ref[0]
    for r in range(r_heads):
        sl = slice(r * Q_BLOCK, (r + 1) * Q_BLOCK)
        o = (gt[:, 3 * r:3 * r + 1] * o_c[sl, :hd] + gt[:, 3 * r + 1:3 * r + 2] * o_s[sl]
             + gt[:, 3 * r + 2:3 * r + 3] * o_w[sl])
        o_ref[:, r * hd:(r + 1) * hd] = o.astype(BF16)


def _nsa_attn_prompt(q_pad, gates_g, kct, vc, kst, vs, kwt, vw, wtab, stab, atab, poolt, bsz, seq):
    g, r_heads, hd = N_KV_HEADS, GQA_R, HEAD_DIM
    nq = seq // Q_BLOCK
    nb, nc = poolt.shape
    tp = PADL + seq
    kern = functools.partial(_nsa_attn_prompt_kernel, nb=nb, nc=nc)
    rows = r_heads * Q_BLOCK
    per_bg = lambda shape: pl.BlockSpec((1, 1) + shape, lambda b, gi, i: (b, gi, 0, 0))
    per_g = lambda shape: pl.BlockSpec((1,) + shape, lambda b, gi, i: (gi, 0, 0))
    return pl.pallas_call(
        kern,
        grid=(bsz, g, nq),
        in_specs=[pl.BlockSpec((Q_BLOCK, r_heads * 128), lambda b, gi, i: (b * nq + i, gi)),
                  pl.BlockSpec((1, Q_BLOCK, 128), lambda b, gi, i: (gi, b * nq + i, 0)),
                  per_bg((128, nc)), per_bg((nc, 128)),
                  per_bg((128 + nb, tp)), per_bg((tp, 128)),
                  per_bg((128, tp)), per_bg((tp, 128)),
                  per_g((r_heads * Q_BLOCK, WINDOW + Q_BLOCK)), per_g((r_heads * Q_BLOCK, FAR_TILE)),
                  per_g((r_heads * Q_BLOCK, 128)),
                  pl.BlockSpec((nb, nc), lambda b, gi, i: (0, 0))],
        out_specs=pl.BlockSpec((Q_BLOCK, r_heads * hd), lambda b, gi, i: (b * nq + i, gi)),
        out_shape=jax.ShapeDtypeStruct((bsz * seq, g * r_heads * hd), BF16),
        scratch_shapes=[pltpu.VMEM((128, nc), BF16), pltpu.VMEM((rows, WINDOW + Q_BLOCK), F32),
                        pltpu.VMEM((rows, WINDOW + Q_BLOCK), BF16),
                        pltpu.VMEM((rows, FAR_TILE), F32), pltpu.VMEM((rows, FAR_TILE), F32),
                        pltpu.VMEM((rows, FAR_TILE), BF16), pltpu.VMEM((rows, FAR_TILE), BF16),
                        pltpu.VMEM((rows, 1), F32), pltpu.VMEM((rows, 1), F32),
                        pltpu.VMEM((rows, 1), F32), pltpu.VMEM((rows, 128), F32)],
        compiler_params=_cparams(3),
        name="nsa_attn_prompt",
    )(q_pad, gates_g, kct, vc, kst, vs, kwt, vw, wtab, stab, atab, poolt)


def _rel_bucket_np(dist):
    n = np.maximum(dist, 0)
    exact = REL_BUCKETS // 2
    nf = np.maximum(n, 1).astype(np.float32)
    large = exact + (np.log(nf / np.float32(exact)) / np.float32(math.log(REL_MAX_DIST / exact))
                     * np.float32(REL_BUCKETS - exact)).astype(np.int32)
    return np.where(n < exact, n, np.minimum(large, REL_BUCKETS - 1))


FAR_DIST = Q_BLOCK - CMP_STRIDE + 1
assert np.all(_rel_bucket_np(np.arange(FAR_DIST, 1 << 16)) == REL_BUCKETS - 1)


def _bias_table(rel_bias, dist, valid):
    relc = rel_bias.astype(F32) - rel_bias[REL_BUCKETS - 1].astype(F32)[None]
    onehot = np.eye(REL_BUCKETS, dtype=np.float32)[_rel_bucket_np(dist)]
    tab = jnp.einsum('...b,bh->...h', onehot, relc, precision=lax.Precision.HIGHEST)
    tab = jnp.where(valid[..., None], tab, NEG)
    return jnp.moveaxis(tab, -1, 0)


def _prompt_tables(rel_bias, nb, nc):
    g, r_heads = N_KV_HEADS, GQA_R
    qi = np.arange(Q_BLOCK)[:, None]
    nw = WINDOW + Q_BLOCK
    d_w = qi + WINDOW - np.arange(nw)[None, :]
    wtab = _bias_table(rel_bias, d_w, (d_w >= 0) & (d_w <= WINDOW)).reshape(g, r_heads * Q_BLOCK, nw)
    d_s = d_w[:, nw - FAR_TILE:]
    stab = _bias_table(rel_bias, d_s, d_s >= 0).reshape(g, r_heads * Q_BLOCK, FAR_TILE)
    d_c = qi - CMP_STRIDE * (np.arange(16)[None, :] - 8) - (CMP_LEN - 1)
    valid_c = d_c >= 0
    a = _bias_table(rel_bias, d_c, valid_c).reshape(g, r_heads * Q_BLOCK, 16)
    hi = a.astype(BF16)
    lo = jnp.where(jnp.asarray(np.tile(valid_c, (r_heads, 1)))[None], a - hi.astype(F32), 0.0).astype(BF16)
    fut = jnp.full((g, r_heads * Q_BLOCK, 1), NEG, BF16)
    atab = jnp.concatenate([jnp.zeros((g, r_heads * Q_BLOCK, ATAB_LANE), BF16), hi, lo, fut,
                            jnp.zeros((g, r_heads * Q_BLOCK, 15), BF16)], axis=-1)
    jb = np.arange(nb)[:, None]
    n = np.arange(nc)[None, :]
    poolt = jnp.asarray(((n >= CMP_PER_SEL * jb - 1) & (n <= CMP_PER_SEL * jb + CMP_PER_SEL - 1)).astype(np.float32),
                        BF16)
    return wtab, stab, atab, poolt


CMP_PAGES_PER_STEP = 8
SEL_PAGES_PER_STEP = 16


def _gather_cmp_kernel(pt_ref, *refs):
    del pt_ref
    page_refs = refs[:CMP_PAGES_PER_STEP]
    o_ref, t_ref = refs[CMP_PAGES_PER_STEP:]
    n_fb = t_ref.shape[0]
    for k in range(CMP_PAGES_PER_STEP):
        for fb in range(n_fb):
            t_ref[fb, k * PAGE_SIZE:(k + 1) * PAGE_SIZE, :] = page_refs[k][0, 0, fb * 128:(fb + 1) * 128, :].T
    rows = CMP_PAGES_PER_STEP * PAGE_SIZE // CMP_STRIDE
    for s in range(CMP_STRIDE):
        for fb in range(n_fb):
            c0 = (s * n_fb + fb) * 128
            o_ref[0, :, c0:c0 + 128] = t_ref[fb, pl.ds(s, rows, stride=CMP_STRIDE), :].astype(BF16)


def _gather_cmp(cache_t, layer, page_table):
    bsz, n_pages = page_table.shape
    half = cache_t.shape[2] // 2
    pp = CMP_PAGES_PER_STEP
    chunks = pp * PAGE_SIZE // CMP_STRIDE
    page_spec = lambda k: pl.BlockSpec((1, 1, half, PAGE_SIZE), lambda b, i, pt: (layer, pt[b, i * pp + k], 0, 0))
    return pl.pallas_call(
        _gather_cmp_kernel,
        grid_spec=pltpu.PrefetchScalarGridSpec(
            num_scalar_prefetch=1, grid=(bsz, n_pages // pp),
            in_specs=[page_spec(k) for k in range(pp)],
            out_specs=pl.BlockSpec((1, chunks, CMP_STRIDE * half), lambda b, i, pt: (b, i, 0)),
            scratch_shapes=[pltpu.VMEM((half // 128, pp * PAGE_SIZE, 128), F32)]),
        out_shape=jax.ShapeDtypeStruct((bsz, n_pages * PAGE_SIZE // CMP_STRIDE, CMP_STRIDE * half), BF16),
        compiler_params=_cparams(2),
        name="gather_cmp",
    )(page_table, *([cache_t] * pp))


def _nsa_attn_sample_kernel(pt_ref, q_ref, gt_ref, kct_ref, vc_ref, new_ref, win_ref,
                            tc_ref, tnear_ref, tnew_ref, tw_ref, pool_ref, e_ref, rsum_ref, *refs,
                            t, n_past_blk):
    del pt_ref
    pp = SEL_PAGES_PER_STEP
    k_refs, v_refs = refs[:pp], refs[pp:2 * pp]
    o_ref, m_ref, l_ref, acc_ref, selb_ref, oc_ref, ow_ref = refs[2 * pp:]
    g, r_heads, hd = N_KV_HEADS, GQA_R, HEAD_DIM
    gw = g * hd
    i = pl.program_id(1)
    n_steps = pl.num_programs(1)
    nt = (((1,), (1,)), ((), ()))
    q = q_ref[0]
    tile = pp * PAGE_SIZE

    @pl.when(i == 0)
    def _():
        new = new_ref[0]
        s_c = jnp.dot(q, kct_ref[0], preferred_element_type=F32) + tc_ref[...]
        m_c = jnp.max(s_c, axis=1, keepdims=True)
        e_c = jnp.exp(s_c - m_c)
        inv_c = jnp.where(m_c > NEG_VALID, 1.0 / jnp.maximum(jnp.sum(e_c, axis=1, keepdims=True), 1e-30), 0.0)
        p_c = e_c * inv_c
        oc_ref[...] = jnp.dot(p_c.astype(BF16), vc_ref[0], preferred_element_type=F32)
        imp0 = jnp.zeros((q.shape[0], n_past_blk), F32)
        for part in _split3(p_c):
            imp0 = imp0 + jnp.dot(part, pool_ref[...], preferred_element_type=F32)
        imp = jnp.zeros((q.shape[0], n_past_blk), F32)
        for part in _split3(imp0):
            imp = imp + jnp.dot(rsum_ref[...], part, preferred_element_type=F32)
        blk = lax.broadcasted_iota(jnp.int32, (n_past_blk, q.shape[0]), 0)
        forced = (blk == 0) | (blk == n_past_blk - 1)
        sel = _top_blocks(jnp.where(forced, FORCE_SCORE, imp.T), blk, min(TOP_N - 1, n_past_blk), n_past_blk)
        selb_ref[...] = jnp.where(sel, 0.0, NEG).T.astype(BF16)
        win = win_ref[0, 0].astype(BF16)
        s_w = jnp.dot(q, win[:gw], preferred_element_type=F32) + tw_ref[...]
        s_wn = lax.dot_general(q, new[:, 4 * gw:5 * gw], nt, preferred_element_type=F32) + tnew_ref[...]
        m_w = jnp.maximum(jnp.max(s_w, axis=1, keepdims=True), jnp.max(s_wn, axis=1, keepdims=True))
        p_w = jnp.exp(s_w - m_w)
        p_wn = jnp.exp(s_wn - m_w)
        l_w = jnp.sum(p_w, axis=1, keepdims=True) + jnp.sum(p_wn, axis=1, keepdims=True)
        o_w = (lax.dot_general(p_w.astype(BF16), win[gw:], nt, preferred_element_type=F32)
               + jnp.dot(p_wn.astype(BF16), new[:, 5 * gw:6 * gw], preferred_element_type=F32))
        ow_ref[...] = o_w / l_w
        s_n = lax.dot_general(q, new[:, 2 * gw:3 * gw], nt, preferred_element_type=F32) + tnew_ref[...]
        m_n = jnp.max(s_n, axis=1, keepdims=True)
        p_n = jnp.exp(s_n - m_n)
        m_ref[...] = m_n
        l_ref[...] = jnp.sum(p_n, axis=1, keepdims=True)
        acc_ref[...] = jnp.dot(p_n.astype(BF16), new[:, 3 * gw:4 * gw], preferred_element_type=F32)

    k_t = jnp.concatenate([r[0, 0] for r in k_refs], axis=1).astype(BF16)
    v_t = jnp.concatenate([r[0, 0] for r in v_refs], axis=1).astype(BF16)
    s = (jnp.dot(q, k_t, preferred_element_type=F32)
         + jnp.dot(selb_ref[...], e_ref[:, pl.ds(pl.multiple_of(i * tile, tile), tile)], preferred_element_type=F32))
    s = s + jnp.where(i == n_steps - 1, tnear_ref[...], 0.0)
    m_old = m_ref[...]
    m_new = jnp.maximum(m_old, jnp.max(s, axis=1, keepdims=True))
    alpha = jnp.exp(m_old - m_new)
    p = jnp.exp(s - m_new)
    m_ref[...] = m_new
    l_ref[...] = alpha * l_ref[...] + jnp.sum(p, axis=1, keepdims=True)
    acc_ref[...] = alpha * acc_ref[...] + lax.dot_general(p.astype(BF16), v_t, nt, preferred_element_type=F32)

    @pl.when(i == n_steps - 1)
    def _():
        gt = gt_ref[0]
        o = gt[:, 0:1] * oc_ref[...] + gt[:, 1:2] * (acc_ref[...] / l_ref[...]) + gt[:, 2:3] * ow_ref[...]
        for gi in range(g):
            for r in range(r_heads):
                h = gi * r_heads + r
                o_ref[0, :, h * hd:(h + 1) * hd] = o[h * t:(h + 1) * t, gi * hd:(gi + 1) * hd].astype(BF16)


def _nsa_attn_sample(q_bd, gates, kct, vc, new_pad, win_t, layer, cache_t, page_table, tabs, t):
    bsz, n_pages = page_table.shape
    g, r_heads, hd = N_KV_HEADS, GQA_R, HEAD_DIM
    gw = g * hd
    lanes = g * r_heads * t
    pp = SEL_PAGES_PER_STEP
    n_past_blk = n_pages * PAGE_SIZE // SEL_LEN
    kern = functools.partial(_nsa_attn_sample_kernel, t=t, n_past_blk=n_past_blk)
    per_b = lambda a: pl.BlockSpec((1,) + a.shape[1:], lambda b, i, pt: (b, 0, 0))
    full = lambda a: pl.BlockSpec(a.shape, lambda b, i, pt: (0, 0))
    page_spec = lambda fblk, k: pl.BlockSpec((1, 1, gw, PAGE_SIZE),
                                             lambda b, i, pt: (layer, pt[b, i * pp + k], fblk, 0))
    return pl.pallas_call(
        kern,
        grid_spec=pltpu.PrefetchScalarGridSpec(
            num_scalar_prefetch=1, grid=(bsz, n_pages // pp),
            in_specs=[per_b(q_bd), per_b(gates), per_b(kct), per_b(vc), per_b(new_pad),
                      pl.BlockSpec((1, 1) + win_t.shape[2:], lambda b, i, pt: (layer, b, 0, 0))]
                     + [full(a) for a in tabs]
                     + [page_spec(2, k) for k in range(pp)] + [page_spec(3, k) for k in range(pp)],
            out_specs=pl.BlockSpec((1, t, g * r_heads * hd), lambda b, i, pt: (b, 0, 0)),
            scratch_shapes=[pltpu.VMEM((lanes, 1), F32), pltpu.VMEM((lanes, 1), F32), pltpu.VMEM((lanes, gw), F32),
                            pltpu.VMEM((lanes, n_past_blk), BF16), pltpu.VMEM((lanes, gw), F32),
                            pltpu.VMEM((lanes, gw), F32)]),
        out_shape=jax.ShapeDtypeStruct((bsz, t, g * r_heads * hd), BF16),
        compiler_params=_cparams(2),
        name="nsa_attn_sample",
    )(page_table, q_bd, gates, kct, vc, new_pad, win_t, *tabs, *([cache_t] * (2 * pp)))


def _sample_tables(rel_bias, pos0, t, nc):
    g, r_heads = N_KV_HEADS, GQA_R
    lanes = g * r_heads * t
    tq = pos0 + np.arange(t)[:, None]

    def table(kpos, extra_valid=True):
        dist = tq - kpos[None, :]
        tab = _bias_table(rel_bias, dist, (dist >= 0) & extra_valid)
        return tab.reshape(lanes, kpos.shape[0])

    tc = table(CMP_STRIDE * np.arange(nc) + CMP_LEN - 1)
    tnear = table(pos0 - Q_BLOCK + np.arange(Q_BLOCK))
    tnear = jnp.pad(tnear, ((0, 0), (SEL_PAGES_PER_STEP * PAGE_SIZE - Q_BLOCK, 0)))
    new_pos = pos0 + np.arange(128)
    tnew = table(new_pos, (new_pos < pos0 + t)[None, :])
    kw = pos0 - WINDOW + np.arange(WINDOW)
    tw = table(kw, (tq - kw[None, :]) <= WINDOW)
    n_past_blk = pos0 // SEL_LEN
    jb = np.arange(n_past_blk)[None, :]
    n = np.arange(nc)[:, None]
    pool = jnp.asarray(((n >= CMP_PER_SEL * jb - 1) & (n <= CMP_PER_SEL * jb + CMP_PER_SEL - 1)).astype(np.float32),
                       BF16)
    e = jnp.asarray((np.arange(pos0)[None, :] // SEL_LEN == np.arange(n_past_blk)[:, None]).astype(np.float32), BF16)
    li = np.arange(lanes)
    same = (li[:, None] // (r_heads * t) == li[None, :] // (r_heads * t)) & (li[:, None] % t == li[None, :] % t)
    rsum = jnp.asarray(same.astype(np.float32), BF16)
    return tc, tnear, tnew, tw, pool, e, rsum


def _nsa_sample(xs2d, bsz, t, layer, cache_t, page_table, win_t, wts, tabs):
    g, r_heads, hd = N_KV_HEADS, GQA_R, HEAD_DIM
    gw = g * hd
    q, kv, kvb, gt = _nsa_proj(xs2d, wts["wq"], wts["wkv"], wts["wg"], tm=xs2d.shape[0])
    chunks = _gather_cmp(cache_t, layer, page_table)
    kct, vc = _nsa_cmp(chunks, wts["w1"], wts["pe_term"], wts["w2k"].T, wts["w2v"], transposed=False)
    q5 = q.reshape(bsz, t, g, r_heads, hd)
    q_bd = jnp.einsum('btgrd,gh->bgrthd', q5, jnp.eye(g, dtype=q.dtype)).reshape(bsz, g * r_heads * t, gw)
    gates = jnp.pad(gt[:, :3 * g * r_heads].reshape(bsz, t, g * r_heads, 3).transpose(0, 2, 1, 3)
                    .reshape(bsz, g * r_heads * t, 3), ((0, 0), (0, 0), (0, 128 - 3)))
    new_pad = jnp.pad(kvb.reshape(bsz, t, 6 * gw), ((0, 0), (0, 128 - t), (0, 0)))
    o = _nsa_attn_sample(q_bd, gates, kct, vc, new_pad, win_t, layer, cache_t, page_table, tabs, t)
    return o.reshape(bsz * t, g * r_heads * hd), kv


def _nsa_weights(w_in, cmp_pe, cmp_w1, cmp_b1, cmp_w2):
    d = w_in.shape[0]
    g, r_heads, hd = N_KV_HEADS, GQA_R, HEAD_DIM
    qd, kvd = g * r_heads * hd, 6 * g * hd
    wq = (w_in[:, :qd] * hd ** -0.5).astype(BF16)
    wq_pad = jnp.pad(wq.reshape(d, g * r_heads, hd), ((0, 0), (0, 0), (0, 128 - hd))).reshape(d, g * r_heads * 128)
    wkv = w_in[:, qd:qd + kvd].astype(BF16)
    wg = jnp.pad(w_in[:, qd + kvd:], ((0, 0), (0, 128 - 3 * g * r_heads))).astype(BF16)
    w1r = cmp_w1.reshape(2, CMP_LEN, hd, CMP_HID)
    w1 = jnp.concatenate([w1r[:, :CMP_STRIDE], w1r[:, CMP_STRIDE:]], axis=-1).astype(BF16)
    w1 = w1.reshape(2, CMP_STRIDE // CMP_PACK, CMP_PACK * hd, 2 * CMP_HID)
    pe_term = jnp.einsum('csd,csdh->ch', cmp_pe, w1r) + cmp_b1
    return dict(wq=wq, wq_pad=wq_pad, wkv=wkv, wg=wg, w1=w1, pe_term=pe_term,
                w2k=cmp_w2[0].astype(BF16), w2v=cmp_w2[1].astype(BF16))


def _nsa_prompt(x2d, bsz, seq, wts, tables):
    g, r_heads, hd = N_KV_HEADS, GQA_R, HEAD_DIM
    m = bsz * seq
    nc, nb = seq // CMP_STRIDE, seq // SEL_LEN
    q_pad, kv, kv_chunks, gt = _nsa_proj(x2d, wts["wq_pad"], wts["wkv"], wts["wg"], tm=ROW_TILE, chunk_seq=seq)
    w2k_t = jnp.pad(wts["w2k"].T, ((0, 128 - hd), (0, 0)))
    w2v_p = jnp.pad(wts["w2v"], ((0, 0), (0, 128 - hd)))
    kct, vc = _nsa_cmp(kv_chunks, wts["w1"], wts["pe_term"], w2k_t, w2v_p, transposed=True)
    kst, kwt, vs, vw = _nsa_prep(kv.reshape(bsz, seq, kv.shape[1]), nb)
    gates_g = jnp.pad(gt[:, :3 * g * r_heads].reshape(m, g, 3 * r_heads).transpose(1, 0, 2),
                      ((0, 0), (0, 0), (0, 128 - 3 * r_heads)))
    o = _nsa_attn_prompt(q_pad, gates_g, kct, vc, kst, vs, kwt, vw, *tables, bsz, seq)
    return o, kv


def kernel(x_prompt, x_sample, cache_nsa, state_nsa_win, state_pool, state_ffn, page_table, p_prompt, p_sample,
           rel_bias, nsa_w_in, nsa_cmp_pe, nsa_cmp_w1, nsa_cmp_b1, nsa_cmp_w2, nsa_w_out, pool_w, pool_b,
           pool_scale, ffn_w_up, ffn_conv_w, ffn_conv_b, ffn_w_down, ln_g, ln_b, ple_w_proj, ple_w_gate,
           ple_b_gate):
    bsz, seq, d = x_prompt.shape
    db, dt, _ = x_sample.shape
    depth = ffn_w_up.shape[0]
    alpha = (2.0 * depth) ** 0.25
    g, hd = N_KV_HEADS, HEAD_DIM
    gw = g * hd
    past_len = page_table.shape[1] * PAGE_SIZE
    mp, ms = bsz * seq, db * dt
    assert g * GQA_R * dt == 128 and past_len % SEL_LEN == 0 and seq % PADL == 0 and seq % ROW_TILE == 0
    assert page_table.shape[1] % SEL_PAGES_PER_STEP == 0 and page_table.shape[1] % CMP_PAGES_PER_STEP == 0
    assert state_nsa_win.shape[2] == WINDOW and seq >= WINDOW and state_pool.shape[2] == POOL_HIST

    xp = x_prompt.reshape(mp, d)
    xs = x_sample.reshape(ms, d)
    tables_p = _prompt_tables(rel_bias, seq // SEL_LEN, seq // CMP_STRIDE)
    tables_s = _sample_tables(rel_bias, past_len, dt, past_len // CMP_STRIDE)
    cache_t = jnp.transpose(cache_nsa, (0, 1, 3, 4, 5, 2)).reshape(cache_nsa.shape[0], cache_nsa.shape[1], 4 * gw,
                                                                   PAGE_SIZE)
    win_t = jnp.transpose(state_nsa_win, (0, 1, 3, 4, 5, 2)).reshape(state_nsa_win.shape[0], db, 2 * gw, WINDOW)
    nsa_rp, nsa_rs, nsa_wp, nsa_ws, pool_hp, pool_hs, ffn_hp, ffn_hs = ([] for _ in range(8))
    for i in range(depth):
        j = i // 2
        g0, b0, g1, b1 = ln_g[i, 0][None], ln_b[i, 0][None], ln_g[i, 1][None], ln_b[i, 1][None]
        if i % 2 == 0:
            wts = _nsa_weights(nsa_w_in[j], nsa_cmp_pe[j], nsa_cmp_w1[j], nsa_cmp_b1[j], nsa_cmp_w2[j])
            o_p, kv_p = _nsa_prompt(xp, bsz, seq, wts, tables_p)
            o_s, kv_s = _nsa_sample(xs, db, dt, j, cache_t, page_table, win_t, wts, tables_s)
            w_out = nsa_w_out[j].astype(BF16)
            xp = _linear_ln(o_p, w_out, xp, g0, b0, alpha, tm=ROW_TILE)
            xs = _linear_ln(o_s, w_out, xs, g0, b0, alpha, tm=ms)
            kv_p3 = kv_p.reshape(bsz, seq, 6, g, hd)
            kv_s3 = kv_s.reshape(db, dt, 6, g, hd)
            nsa_rp.append(kv_p3[:, :, :4])
            nsa_rs.append(kv_s3[:, :, :4])
            nsa_wp.append(kv_p3[:, seq - WINDOW:, 4:])
            nsa_ws.append(jnp.concatenate([state_nsa_win[j][:, dt:], kv_s3[:, :, 4:]], axis=1))
        else:
            xp3 = xp.reshape(bsz, seq, d)
            xe = jnp.concatenate([state_pool[j], xs.reshape(db, dt, d)], axis=1)
            pool_hp.append(xp3[:, seq - POOL_HIST:])
            pool_hs.append(xe[:, xe.shape[1] - POOL_HIST:])
            pw = pool_w[j].astype(BF16)
            xp = _pool_ln_prompt(xp3, pw, pool_b[j][None], pool_scale[j][None], g0, b0, alpha).reshape(mp, d)
            xs = _pool_ln_sample(xe, dt, past_len, pw, pool_b[j][None], pool_scale[j][None], g0, b0,
                                 alpha).reshape(ms, d)
        w_up = ffn_w_up[i].astype(BF16)
        w_down = ffn_w_down[i].astype(BF16)
        cw, cb = ffn_conv_w[i], ffn_conv_b[i][None]
        h_p = _ffn_up_prompt(xp, w_up, cw, cb, seq)
        tail = xp.reshape(bsz, seq, d)[:, seq - 8:].reshape(bsz * 8, d)
        u_tail = _matmul(tail, w_up, tm=bsz * 8).reshape(bsz, 8, -1)
        ffn_hp.append(u_tail[:, 8 - (CONV_W - 1):])
        u_s = _matmul(xs, w_up, tm=ms).reshape(db, dt, -1)
        ue = jnp.concatenate([state_ffn[i], u_s], axis=1)
        ffn_hs.append(ue[:, ue.shape[1] - (CONV_W - 1):])
        h_s = _ffn_gate_sample(ue, cw, cb, dt).reshape(ms, -1)
        ple = (ple_w_gate[i].astype(BF16), ple_b_gate[i][None], ple_w_proj[i].astype(BF16))
        xp = _ffn_down_ln_ple(h_p, w_down, xp, g1, b1, p_prompt[i].reshape(mp, -1), *ple, alpha, ROW_TILE)
        xs = _ffn_down_ln_ple(h_s, w_down, xs, g1, b1, p_sample[i].reshape(ms, -1), *ple, alpha, ms)
    return (xp.reshape(bsz, seq, d), xs.reshape(db, dt, d),
            jnp.stack(nsa_rp), jnp.stack(nsa_rs), jnp.stack(nsa_wp), jnp.stack(nsa_ws),
            jnp.stack(pool_hp), jnp.stack(pool_hs), jnp.stack(ffn_hp), jnp.stack(ffn_hs))
```

```python
import functools
import math

import numpy as np
import jax
import jax.numpy as jnp
from jax import lax
from jax.experimental import pallas as pl
from jax.experimental.pallas import tpu as pltpu

N_KV_HEADS = 4
GQA_R = 4
HEAD_DIM = 64
CMP_LEN = 32
CMP_STRIDE = 16
CMP_HID = 256
CMP_PACK = 4
SEL_LEN = 64
CMP_PER_SEL = SEL_LEN // CMP_STRIDE
TOP_N = 16
WINDOW = 512
Q_BLOCK = 128
FORCE_SCORE = 1.0e4
REL_BUCKETS = 32
REL_MAX_DIST = 128
POOL_WINDOWS = (2, 4, 8, 16)
POOL_MAX = max(POOL_WINDOWS)
POOL_HIST = POOL_MAX - 1
CONV_W = 3
LN_EPS = 1e-5
PAGE_SIZE = 128

NEG = -1.0e30
NEG_VALID = -1.0e29
BF16 = jnp.bfloat16
F32 = jnp.float32

VMEM_LIMIT_BYTES = 56 * 1024 * 1024
ROW_TILE = 512


def _cparams(n_grid):
    return pltpu.CompilerParams(dimension_semantics=("arbitrary",) * n_grid,
                                vmem_limit_bytes=VMEM_LIMIT_BYTES)


def _layer_norm(y, g, b):
    mu = jnp.mean(y, axis=-1, keepdims=True)
    yc = y - mu
    var = jnp.mean(yc * yc, axis=-1, keepdims=True)
    return yc * lax.rsqrt(var + LN_EPS) * g + b


FFN_TN = 256
HALO = 16


def _ffn_up_prompt_kernel(x_ref, xh_ref, w_ref, cw_ref, cb_ref, h_ref, xe_ref, ue_ref, *,
                          tm, d_ff, tiles_per_batch):
    i = pl.program_id(0)
    first = (i % tiles_per_batch) == 0
    halo = xh_ref[...]
    xe_ref[0:HALO, :] = jnp.where(first, jnp.zeros_like(halo), halo).astype(BF16)
    xe_ref[HALO:HALO + tm, :] = x_ref[...].astype(BF16)
    xe = xe_ref[...]
    for j in range(d_ff // FFN_TN):
        c_parts = []
        for half in range(2):
            c0 = half * d_ff + j * FFN_TN
            ue_ref[...] = jnp.dot(xe, w_ref[:, c0:c0 + FFN_TN], preferred_element_type=F32)
            cw = cw_ref[:, c0:c0 + FFN_TN]
            c = cb_ref[:, c0:c0 + FFN_TN]
            for k in range(CONV_W):
                off = HALO - (CONV_W - 1) + k
                c = c + cw[k:k + 1, :] * ue_ref[off:off + tm, :]
            c_parts.append(c)
        h_ref[:, j * FFN_TN:(j + 1) * FFN_TN] = (jax.nn.gelu(c_parts[1]) * c_parts[0]).astype(BF16)


def _ffn_up_prompt(x, w_up, conv_w, conv_b, seq, tm=ROW_TILE):
    m, d = x.shape
    d_ff = w_up.shape[1] // 2
    tpb = seq // tm
    kern = functools.partial(_ffn_up_prompt_kernel, tm=tm, d_ff=d_ff, tiles_per_batch=tpb)
    return pl.pallas_call(
        kern,
        grid=(m // tm,),
        in_specs=[
            pl.BlockSpec((tm, d), lambda i: (i, 0)),
            pl.BlockSpec((HALO, d), lambda i: (jnp.maximum(i * (tm // HALO) - 1, 0), 0)),
            pl.BlockSpec(w_up.shape, lambda i: (0, 0)),
            pl.BlockSpec(conv_w.shape, lambda i: (0, 0)),
            pl.BlockSpec(conv_b.shape, lambda i: (0, 0)),
        ],
        out_specs=pl.BlockSpec((tm, d_ff), lambda i: (i, 0)),
        out_shape=jax.ShapeDtypeStruct((m, d_ff), BF16),
        scratch_shapes=[pltpu.VMEM((HALO + tm, d), BF16), pltpu.VMEM((HALO + tm, FFN_TN), F32)],
        compiler_params=_cparams(1),
        name="ffn_up_prompt",
    )(x, x, w_up, conv_w, conv_b)


def _ffn_gate_sample_kernel(ue_ref, cw_ref, cb_ref, h_ref, *, t, d_ff):
    for j in range(d_ff // FFN_TN):
        c_parts = []
        for half in range(2):
            c0 = half * d_ff + j * FFN_TN
            cw = cw_ref[:, c0:c0 + FFN_TN]
            c = cb_ref[:, c0:c0 + FFN_TN][None]
            for k in range(CONV_W):
                c = c + cw[k:k + 1, :][None] * ue_ref[:, k:k + t, c0:c0 + FFN_TN]
            c_parts.append(c)
        h_ref[:, :, j * FFN_TN:(j + 1) * FFN_TN] = (jax.nn.gelu(c_parts[1]) * c_parts[0]).astype(BF16)


def _ffn_gate_sample(ue, conv_w, conv_b, t):
    b, te, n2 = ue.shape
    d_ff = n2 // 2
    bb = 8
    kern = functools.partial(_ffn_gate_sample_kernel, t=t, d_ff=d_ff)
    return pl.pallas_call(
        kern,
        grid=(b // bb,),
        in_specs=[
            pl.BlockSpec((bb, te, n2), lambda i: (i, 0, 0)),
            pl.BlockSpec(conv_w.shape, lambda i: (0, 0)),
            pl.BlockSpec(conv_b.shape, lambda i: (0, 0)),
        ],
        out_specs=pl.BlockSpec((bb, t, d_ff), lambda i: (i, 0, 0)),
        out_shape=jax.ShapeDtypeStruct((b, t, d_ff), BF16),
        compiler_params=_cparams(1),
        name="ffn_gate_sample",
    )(ue, conv_w, conv_b)


def _matmul_kernel(x_ref, w_ref, o_ref):
    o_ref[...] = jnp.dot(x_ref[...].astype(BF16), w_ref[...], preferred_element_type=F32).astype(o_ref.dtype)


def _matmul(x, w, tm, tn=None, out_dtype=F32):
    m, k = x.shape
    n = w.shape[1]
    tn = n if tn is None else tn
    return pl.pallas_call(
        _matmul_kernel,
        grid=(m // tm, n // tn),
        in_specs=[pl.BlockSpec((tm, k), lambda i, j: (i, 0)),
                  pl.BlockSpec((k, tn), lambda i, j: (0, j))],
        out_specs=pl.BlockSpec((tm, tn), lambda i, j: (i, j)),
        out_shape=jax.ShapeDtypeStruct((m, n), out_dtype),
        compiler_params=_cparams(2),
        name="matmul",
    )(x, w)


def _linear_ln_kernel(a_ref, w_ref, r_ref, g_ref, b_ref, o_ref, *, alpha):
    f = jnp.dot(a_ref[...].astype(BF16), w_ref[...], preferred_element_type=F32)
    o_ref[...] = _layer_norm(alpha * r_ref[...] + f, g_ref[...], b_ref[...])


def _linear_ln(a, w, resid, g, b, alpha, tm):
    m, k = a.shape
    d = w.shape[1]
    kern = functools.partial(_linear_ln_kernel, alpha=alpha)
    return pl.pallas_call(
        kern,
        grid=(m // tm,),
        in_specs=[pl.BlockSpec((tm, k), lambda i: (i, 0)),
                  pl.BlockSpec((k, d), lambda i: (0, 0)),
                  pl.BlockSpec((tm, d), lambda i: (i, 0)),
                  pl.BlockSpec((1, d), lambda i: (0, 0)),
                  pl.BlockSpec((1, d), lambda i: (0, 0))],
        out_specs=pl.BlockSpec((tm, d), lambda i: (i, 0)),
        out_shape=jax.ShapeDtypeStruct((m, d), F32),
        compiler_params=_cparams(1),
        name="linear_ln",
    )(a, w, resid, g, b)


def _ffn_down_kernel(h_ref, wd_ref, r_ref, g_ref, b_ref, p_ref, wg_ref, bg_ref, wp_ref, o_ref, *, alpha):
    f = jnp.dot(h_ref[...], wd_ref[...], preferred_element_type=F32)
    x2 = _layer_norm(alpha * r_ref[...] + f, g_ref[...], b_ref[...])
    gate = jax.nn.sigmoid(jnp.dot(x2.astype(BF16), wg_ref[...], preferred_element_type=F32) + bg_ref[...])
    inj = jnp.dot(p_ref[...].astype(BF16), wp_ref[...], preferred_element_type=F32)
    o_ref[...] = x2 + gate * inj


def _ffn_down_ln_ple(h, w_down, resid, g, b, p, w_gate, b_gate, w_proj, alpha, tm):
    m, k = h.shape
    d = w_down.shape[1]
    pd = p.shape[1]
    kern = functools.partial(_ffn_down_kernel, alpha=alpha)
    full = lambda shape: pl.BlockSpec(shape, lambda i: (0, 0))
    return pl.pallas_call(
        kern,
        grid=(m // tm,),
        in_specs=[pl.BlockSpec((tm, k), lambda i: (i, 0)), full((k, d)),
                  pl.BlockSpec((tm, d), lambda i: (i, 0)), full((1, d)), full((1, d)),
                  pl.BlockSpec((tm, pd), lambda i: (i, 0)), full((d, d)), full((1, d)), full((pd, d))],
        out_specs=pl.BlockSpec((tm, d), lambda i: (i, 0)),
        out_shape=jax.ShapeDtypeStruct((m, d), F32),
        compiler_params=_cparams(1),
        name="ffn_down_ln_ple",
    )(h, w_down, resid, g, b, p, w_gate, b_gate, w_proj)


def _pool_finish(x, diffs, w_ref, pb_ref, ps_ref, g_ref, b_ref, alpha):
    ys = [jnp.dot(dg.astype(BF16), w_ref[gi], preferred_element_type=F32) for gi, dg in enumerate(diffs)]
    y = jnp.concatenate(ys, axis=-1) + pb_ref[...]
    return _layer_norm(alpha * x + ps_ref[...] * y, g_ref[...], b_ref[...])


def _pool_prompt_kernel(x_ref, xh_ref, w_ref, pb_ref, ps_ref, g_ref, b_ref, o_ref, xe_ref, *,
                        tt, gw, alpha):
    i = pl.program_id(1)
    halo = xh_ref[0]
    xe_ref[0:POOL_MAX, :] = jnp.where(i == 0, jnp.zeros_like(halo), halo)
    xe_ref[POOL_MAX:POOL_MAX + tt, :] = x_ref[0]
    pos = i * tt + lax.broadcasted_iota(jnp.int32, (tt, 1), 0)
    x = x_ref[0]
    diffs = []
    for gi, win in enumerate(POOL_WINDOWS):
        lo = gi * gw
        tot = x[:, lo:lo + gw]
        for k in range(1, win):
            tot = tot + xe_ref[POOL_MAX - k:POOL_MAX - k + tt, lo:lo + gw]
        cnt = jnp.minimum(win, pos + 1).astype(F32)
        diffs.append(tot / cnt - x[:, lo:lo + gw])
    o_ref[0] = _pool_finish(x, diffs, w_ref, pb_ref, ps_ref, g_ref, b_ref, alpha)


def _pool_ln_prompt(x, w, pb, ps, g, b, alpha, tt=ROW_TILE):
    bsz, t, d = x.shape
    gw = d // len(POOL_WINDOWS)
    kern = functools.partial(_pool_prompt_kernel, tt=tt, gw=gw, alpha=alpha)
    vec = pl.BlockSpec((1, d), lambda bi, i: (0, 0))
    return pl.pallas_call(
        kern,
        grid=(bsz, t // tt),
        in_specs=[pl.BlockSpec((1, tt, d), lambda bi, i: (bi, i, 0)),
                  pl.BlockSpec((1, POOL_MAX, d), lambda bi, i: (bi, jnp.maximum(i * (tt // POOL_MAX) - 1, 0), 0)),
                  pl.BlockSpec(w.shape, lambda bi, i: (0, 0, 0)), vec, vec, vec, vec],
        out_specs=pl.BlockSpec((1, tt, d), lambda bi, i: (bi, i, 0)),
        out_shape=jax.ShapeDtypeStruct((bsz, t, d), F32),
        scratch_shapes=[pltpu.VMEM((POOL_MAX + tt, d), F32)],
        compiler_params=_cparams(2),
        name="pool_ln_prompt",
    )(x, x, w, pb, ps, g, b)


def _pool_sample_kernel(xe_ref, w_ref, pb_ref, ps_ref, g_ref, b_ref, o_ref, *, t, nh, gw, pos0, alpha):
    bb = xe_ref.shape[0]
    d = xe_ref.shape[2]
    x = xe_ref[:, nh:nh + t, :].reshape(bb * t, d)
    cnt_np = np.minimum(np.array(POOL_WINDOWS)[:, None], pos0 + np.arange(t)[None, :] + 1).astype(np.float32)
    diffs = []
    for gi, win in enumerate(POOL_WINDOWS):
        lo = gi * gw
        tot = xe_ref[:, nh:nh + t, lo:lo + gw]
        for k in range(1, win):
            tot = tot + xe_ref[:, nh - k:nh - k + t, lo:lo + gw]
        if np.all(cnt_np[gi] == win):
            mean = tot / float(win)
        else:
            t_idx = lax.broadcasted_iota(jnp.int32, (1, t, 1), 1)
            mean = tot / jnp.minimum(win, pos0 + t_idx + 1).astype(F32)
        diffs.append(mean.reshape(bb * t, gw) - x[:, lo:lo + gw])
    o_ref[...] = _pool_finish(x, diffs, w_ref, pb_ref, ps_ref, g_ref, b_ref, alpha).reshape(bb, t, d)


def _pool_ln_sample(xe, t, pos0, w, pb, ps, g, b, alpha, bb=8):
    bsz, te, d = xe.shape
    gw = d // len(POOL_WINDOWS)
    kern = functools.partial(_pool_sample_kernel, t=t, nh=te - t, gw=gw, pos0=pos0, alpha=alpha)
    vec = pl.BlockSpec((1, d), lambda i: (0, 0))
    return pl.pallas_call(
        kern,
        grid=(bsz // bb,),
        in_specs=[pl.BlockSpec((bb, te, d), lambda i: (i, 0, 0)),
                  pl.BlockSpec(w.shape, lambda i: (0, 0, 0)), vec, vec, vec, vec],
        out_specs=pl.BlockSpec((bb, t, d), lambda i: (i, 0, 0)),
        out_shape=jax.ShapeDtypeStruct((bsz, t, d), F32),
        compiler_params=_cparams(1),
        name="pool_ln_sample",
    )(xe, w, pb, ps, g, b)


def _nsa_proj_kernel(x_ref, wq_ref, wkv_ref, wg_ref, q_ref, kv_ref, kvb_ref, gt_ref, *scratch):
    xb = x_ref[...].astype(BF16)
    q_ref[...] = jnp.dot(xb, wq_ref[...], preferred_element_type=F32).astype(BF16)
    kv = jnp.dot(xb, wkv_ref[...], preferred_element_type=F32)
    kv_ref[...] = kv
    gt_ref[...] = jax.nn.sigmoid(jnp.dot(xb, wg_ref[...], preferred_element_type=F32))
    if not scratch:
        kvb_ref[...] = kv.astype(BF16)
        return
    (t_ref,) = scratch
    n_fb, tm = t_ref.shape[0], t_ref.shape[1]
    for fb in range(n_fb):
        t_ref[fb] = kv[:, fb * 128:(fb + 1) * 128]
    for s in range(CMP_STRIDE):
        for fb in range(n_fb):
            kvb_ref[0, s, :, fb * 128:(fb + 1) * 128] = (
                t_ref[fb, pl.ds(s, tm // CMP_STRIDE, stride=CMP_STRIDE), :].astype(BF16))


def _nsa_proj(x, wq, wkv, wg, tm, chunk_seq=None):
    m, d = x.shape
    nq, nkv, ng = wq.shape[1], wkv.shape[1], wg.shape[1]
    full = lambda w: pl.BlockSpec(w.shape, lambda i: (0, 0))
    row = lambda n: pl.BlockSpec((tm, n), lambda i: (i, 0))
    if chunk_seq is None:
        kvb_spec, kvb_shape, scratch = row(nkv), jax.ShapeDtypeStruct((m, nkv), BF16), []
    else:
        feat = 2 * N_KV_HEADS * HEAD_DIM
        tpb = chunk_seq // tm
        kvb_spec = pl.BlockSpec((1, CMP_STRIDE, tm // CMP_STRIDE, feat), lambda i: (i // tpb, 0, i % tpb, 0))
        kvb_shape = jax.ShapeDtypeStruct((m // chunk_seq, CMP_STRIDE, chunk_seq // CMP_STRIDE, feat), BF16)
        scratch = [pltpu.VMEM((feat // 128, tm, 128), F32)]
    return pl.pallas_call(
        _nsa_proj_kernel,
        grid=(m // tm,),
        in_specs=[row(d), full(wq), full(wkv), full(wg)],
        out_specs=[row(nq), row(nkv), kvb_spec, row(ng)],
        out_shape=[jax.ShapeDtypeStruct((m, nq), BF16), jax.ShapeDtypeStruct((m, nkv), F32),
                   kvb_shape, jax.ShapeDtypeStruct((m, ng), F32)],
        scratch_shapes=scratch,
        compiler_params=_cparams(1),
        name="nsa_proj",
    )(x, wq, wkv, wg)


def _nsa_cmp_kernel(*refs, nc, transposed):
    ns = CMP_STRIDE
    x_refs = refs[:ns]
    w1_ref, pe_ref, w2k_ref, w2v_ref, ko_ref, vo_ref = refs[ns:]
    hd = HEAD_DIM
    gw = N_KV_HEADS * hd
    last = lax.broadcasted_iota(jnp.int32, (nc, CMP_HID), 0) == nc - 1
    nt = (((1,), (1,)), ((), ()))
    for c in range(2):
        for g in range(N_KV_HEADS):
            acc = jnp.zeros((nc, 2 * CMP_HID), F32)
            lo = c * gw + g * hd
            for k in range(ns // CMP_PACK):
                xs = jnp.concatenate([x_refs[k * CMP_PACK + i][:, lo:lo + hd] for i in range(CMP_PACK)], axis=1)
                acc = acc + jnp.dot(xs, w1_ref[c, k], preferred_element_type=F32)
            h1 = jnp.where(last, 0.0, pltpu.roll(acc[:, CMP_HID:], nc - 1, 0))
            hid = jax.nn.gelu(acc[:, :CMP_HID] + h1 + pe_ref[c:c + 1, :]).astype(BF16)
            if c == 0 and transposed:
                ko_ref[0, g] = lax.dot_general(w2k_ref[...], hid, nt, preferred_element_type=F32).astype(BF16)
            elif c == 0:
                ko_ref[0, g * hd:(g + 1) * hd, :] = lax.dot_general(w2k_ref[...], hid, nt,
                                                                    preferred_element_type=F32).astype(BF16)
            elif transposed:
                vo_ref[0, g] = jnp.dot(hid, w2v_ref[...], preferred_element_type=F32).astype(BF16)
            else:
                vo_ref[0, :, g * hd:(g + 1) * hd] = jnp.dot(hid, w2v_ref[...], preferred_element_type=F32).astype(BF16)


def _nsa_cmp(xv, w1, pe_term, w2k, w2v, transposed):
    g, hd = N_KV_HEADS, HEAD_DIM
    feat = 2 * g * hd
    bsz = xv.shape[0]
    if xv.ndim == 4:
        nc = xv.shape[2]
        x_specs = [pl.BlockSpec((None, None, nc, feat), functools.partial(lambda b, s: (b, s, 0, 0), s=s))
                   for s in range(CMP_STRIDE)]
    else:
        nc = xv.shape[1]
        x_specs = [pl.BlockSpec((None, nc, feat), functools.partial(lambda b, s: (b, 0, s), s=s))
                   for s in range(CMP_STRIDE)]
    kern = functools.partial(_nsa_cmp_kernel, nc=nc, transposed=transposed)
    full = lambda w: pl.BlockSpec(w.shape, lambda b: (0,) * w.ndim)
    if transposed:
        out_specs = [pl.BlockSpec((1, g, 128, nc), lambda b: (b, 0, 0, 0)),
                     pl.BlockSpec((1, g, nc, 128), lambda b: (b, 0, 0, 0))]
        out_shape = [jax.ShapeDtypeStruct((bsz, g, 128, nc), BF16), jax.ShapeDtypeStruct((bsz, g, nc, 128), BF16)]
    else:
        out_specs = [pl.BlockSpec((1, g * hd, nc), lambda b: (b, 0, 0)), pl.BlockSpec((1, nc, g * hd), lambda b: (b, 0, 0))]
        out_shape = [jax.ShapeDtypeStruct((bsz, g * hd, nc), BF16), jax.ShapeDtypeStruct((bsz, nc, g * hd), BF16)]
    return pl.pallas_call(
        kern,
        grid=(bsz,),
        in_specs=x_specs + [full(w1), full(pe_term), full(w2k), full(w2v)],
        out_specs=out_specs,
        out_shape=out_shape,
        compiler_params=_cparams(1),
        name="nsa_cmp",
    )(*([xv] * CMP_STRIDE), w1, pe_term, w2k, w2v)


PADL = 512
FAR_TILE = 512
ATAB_LANE = 80


def _nsa_prep_kernel(kv_ref, kst_ref, kwt_ref, vs_ref, vw_ref, *, nb):
    j = pl.program_id(1)
    tl = PADL
    hd = HEAD_DIM
    gw = N_KV_HEADS * hd

    @pl.when(j == 0)
    def _():
        for g in range(N_KV_HEADS):
            kst_ref[0, g] = (lax.broadcasted_iota(jnp.int32, (128 + nb, tl), 0) == hd).astype(BF16)
            kwt_ref[0, g] = (lax.broadcasted_iota(jnp.int32, (128, tl), 0) == hd).astype(BF16)
            vs_ref[0, g] = jnp.zeros((tl, 128), BF16)
            vw_ref[0, g] = jnp.zeros((tl, 128), BF16)

    @pl.when(j > 0)
    def _():
        kst = kv_ref[0, :, 2 * gw:3 * gw].T
        kwt = kv_ref[0, :, 4 * gw:5 * gw].T
        pos = (j - 1) * tl + lax.broadcasted_iota(jnp.int32, (nb, tl), 1)
        onehot = (pos // SEL_LEN == lax.broadcasted_iota(jnp.int32, (nb, tl), 0)).astype(BF16)
        zeros = jnp.zeros((128 - hd, tl), BF16)
        lane = lax.broadcasted_iota(jnp.int32, (tl, 128), 1)
        tail = (lane == hd).astype(F32)
        for g in range(N_KV_HEADS):
            kst_ref[0, g, 0:hd, :] = kst[g * hd:(g + 1) * hd].astype(BF16)
            kst_ref[0, g, hd:128, :] = zeros
            kst_ref[0, g, 128:128 + nb, :] = onehot
            kwt_ref[0, g, 0:hd, :] = kwt[g * hd:(g + 1) * hd].astype(BF16)
            kwt_ref[0, g, hd:128, :] = zeros
            for stream, ref in ((3, vs_ref), (5, vw_ref)):
                a = kv_ref[0, :, stream * gw + (g // 2) * 128:stream * gw + (g // 2 + 1) * 128]
                if g % 2:
                    a = pltpu.roll(a, hd, 1)
                ref[0, g] = jnp.where(lane < hd, a, tail).astype(BF16)


def _nsa_prep(kv, nb):
    bsz, t, n = kv.shape
    g = N_KV_HEADS
    tl = PADL
    tp = PADL + t
    kern = functools.partial(_nsa_prep_kernel, nb=nb)
    kspec = lambda rows: pl.BlockSpec((1, g, rows, tl), lambda b, j: (b, 0, 0, j))
    vspec = pl.BlockSpec((1, g, tl, 128), lambda b, j: (b, 0, j, 0))
    return pl.pallas_call(
        kern,
        grid=(bsz, tp // tl),
        in_specs=[pl.BlockSpec((1, tl, n), lambda b, j: (b, jnp.maximum(j - 1, 0), 0))],
        out_specs=[kspec(128 + nb), kspec(128), vspec, vspec],
        out_shape=[jax.ShapeDtypeStruct((bsz, g, 128 + nb, tp), BF16), jax.ShapeDtypeStruct((bsz, g, 128, tp), BF16),
                   jax.ShapeDtypeStruct((bsz, g, tp, 128), BF16), jax.ShapeDtypeStruct((bsz, g, tp, 128), BF16)],
        compiler_params=_cparams(2),
        name="nsa_prep",
    )(kv)


def _split3(x):
    a = x.astype(BF16)
    r = x - a.astype(F32)
    b = r.astype(BF16)
    return a, b, (r - b.astype(F32)).astype(BF16)


def _top_blocks(score, blk, n_sel, n_rows):
    sel = jnp.zeros(score.shape, jnp.bool_)
    for _ in range(n_sel):
        m = jnp.max(score, axis=0, keepdims=True)
        first = jnp.min(jnp.where(score == m, blk, n_rows), axis=0, keepdims=True)
        pick = blk == first
        sel = sel | pick
        score = jnp.where(pick, -3.0e38, score)
    return sel


def _nsa_attn_prompt_kernel(q_ref, gt_ref, kct_ref, vc_ref, kst_ref, vs_ref, kwt_ref, vw_ref,
                            wtab_ref, stab_ref, atab_ref, poolt_ref, o_ref,
                            kc_ref, sw_ref, sa_ref, sb_ref, pa_ref, pb_ref, ala_ref, alb_ref, m_ref, acc_ref, *, nb, nc):
    qi = pl.program_id(2)
    t0 = qi * Q_BLOCK
    r_heads = GQA_R
    hd = HEAD_DIM
    rows = r_heads * Q_BLOCK
    nw = WINDOW + Q_BLOCK
    lane = lax.broadcasted_iota(jnp.int32, (Q_BLOCK, 128), 1)
    q = jnp.concatenate([jnp.where(lane == hd, NEG, q_ref[:, r * 128:(r + 1) * 128]) for r in range(r_heads)],
                        axis=0)

    n_idx = lax.broadcasted_iota(jnp.int32, (48, nc), 1)
    j_idx = lax.broadcasted_iota(jnp.int32, (48, nc), 0)
    rel = n_idx - (qi * (Q_BLOCK // CMP_STRIDE) - 8)
    shift = (((j_idx < 32) & (rel == jnp.where(j_idx < 16, j_idx, j_idx - 16)))
             | ((j_idx == 32) & (rel >= 16))).astype(F32).astype(BF16)
    kc_ref[0:ATAB_LANE, :] = kct_ref[0, 0, 0:ATAB_LANE, :]
    kc_ref[ATAB_LANE:128, :] = shift
    lane_r = lax.broadcasted_iota(jnp.int32, (rows, 128), 1)
    s_c = jnp.dot(jnp.where(lane_r >= ATAB_LANE, atab_ref[0], q), kc_ref[...], preferred_element_type=F32)
    m_c = jnp.max(s_c, axis=1, keepdims=True)
    e_c = jnp.exp(s_c - m_c)
    inv_c = jnp.where(m_c > NEG_VALID, 1.0 / jnp.maximum(jnp.sum(e_c, axis=1, keepdims=True), 1e-30), 0.0)
    p_c = e_c * inv_c
    o_c = jnp.dot(p_c.astype(BF16), vc_ref[0, 0], preferred_element_type=F32)

    grp = p_c[0:Q_BLOCK]
    for r in range(1, r_heads):
        grp = grp + p_c[r * Q_BLOCK:(r + 1) * Q_BLOCK]
    imp_t = jnp.zeros((nb, Q_BLOCK), F32)
    for part in _split3(grp):
        imp_t = imp_t + lax.dot_general(poolt_ref[...], part, (((1,), (1,)), ((), ())),
                                        preferred_element_type=F32)
    blk = lax.broadcasted_iota(jnp.int32, (nb, Q_BLOCK), 0)
    tq = t0 + lax.broadcasted_iota(jnp.int32, (nb, Q_BLOCK), 1)
    cur = tq // SEL_LEN
    forced = (blk == 0) | (blk == cur) | (blk == cur - 1)
    score = jnp.where(blk * SEL_LEN > tq, -1.0, jnp.where(forced, -3.0e38, imp_t))
    sel = forced | _top_blocks(score, blk, min(TOP_N, nb) - 3, nb)
    selb = jnp.where(sel, 0.0, NEG).T.astype(BF16)
    q_aug = jnp.concatenate([q, jnp.concatenate([selb] * r_heads, axis=0)], axis=1)

    c_w = pl.multiple_of(t0 + PADL - WINDOW, Q_BLOCK)
    sw_ref[...] = jnp.dot(q, kwt_ref[0, 0, :, pl.ds(c_w, nw)], preferred_element_type=F32) + wtab_ref[0]

    c_near = pl.multiple_of(t0 + PADL + Q_BLOCK - FAR_TILE, Q_BLOCK)
    n_far = jnp.maximum((c_near - PADL + FAR_TILE - 1) // FAR_TILE, 0)

    def tile_start(k):
        return pl.multiple_of(c_near - FAR_TILE * (jnp.clip(k, -1, n_far - 1) + 1), Q_BLOCK)

    def scores(k, s_out):
        s_out[...] = jnp.dot(q_aug, kst_ref[0, 0, :, pl.ds(tile_start(k), FAR_TILE)], preferred_element_type=F32)

    def pv_update(k, p_in, al_in):
        acc_ref[...] = al_in[...] * acc_ref[...] + jnp.dot(p_in[...], vs_ref[0, 0, pl.ds(tile_start(k), FAR_TILE), :],
                                                          preferred_element_type=F32)

    def stage(k, s_cur, s_nxt, p_cur, al_cur, p_prev, al_prev):
        scores(k + 1, s_nxt)
        pv_update(k - 1, p_prev, al_prev)
        m_old = m_ref[...]
        m_new = jnp.maximum(m_old, jnp.max(s_cur[...], axis=1, keepdims=True))
        al_cur[...] = jnp.exp(m_old - m_new)
        p_cur[...] = jnp.exp(s_cur[...] - m_new).astype(BF16)
        m_ref[...] = m_new

    s = jnp.dot(q_aug, kst_ref[0, 0, :, pl.ds(c_near, FAR_TILE)], preferred_element_type=F32) + stab_ref[0]
    scores(0, sa_ref)
    m_s = jnp.max(s, axis=1, keepdims=True)
    m_ref[...] = m_s
    pb_ref[...] = jnp.exp(s - m_s).astype(BF16)
    alb_ref[...] = jnp.ones_like(alb_ref)
    acc_ref[...] = jnp.zeros_like(acc_ref)

    def pair(j, _):
        stage(2 * j, sa_ref, sb_ref, pa_ref, ala_ref, pb_ref, alb_ref)

        @pl.when(2 * j + 1 < n_far)
        def _():
            stage(2 * j + 1, sb_ref, sa_ref, pb_ref, alb_ref, pa_ref, ala_ref)
        return 0

    lax.fori_loop(0, (n_far + 1) // 2, pair, 0)

    @pl.when(n_far % 2 == 1)
    def _():
        pv_update(n_far - 1, pa_ref, ala_ref)

    @pl.when(n_far % 2 == 0)
    def _():
        pv_update(n_far - 1, pb_ref, alb_ref)

    acc = acc_ref[...]
    o_s = acc[:, :hd] * (1.0 / acc[:, hd:hd + 1])

    s_w = sw_ref[...]
    m_w = jnp.max(s_w, axis=1, keepdims=True)
    acc_w = jnp.dot(jnp.exp(s_w - m_w).astype(BF16), vw_ref[0, 0, pl.ds(c_w, nw), :], preferred_element_type=F32)
    o_w = acc_w[:, :hd] * (1.0 / acc_w[:, hd:hd + 1])

    gt = gt_ref[0]
    for r in range(r_heads):
        sl = slice(r * Q_BLOCK, (r + 1) * Q_BLOCK)
        o = (gt[:, 3 * r:3 * r + 1] * o_c[sl, :hd] + gt[:, 3 * r + 1:3 * r + 2] * o_s[sl]
             + gt[:, 3 * r + 2:3 * r + 3] * o_w[sl])
        o_ref[:, r * hd:(r + 1) * hd] = o.astype(BF16)


def _nsa_attn_prompt(q_pad, gates_g, kct, vc, kst, vs, kwt, vw, wtab, stab, atab, poolt, bsz, seq):
    g, r_heads, hd = N_KV_HEADS, GQA_R, HEAD_DIM
    nq = seq // Q_BLOCK
    nb, nc = poolt.shape
    tp = PADL + seq
    kern = functools.partial(_nsa_attn_prompt_kernel, nb=nb, nc=nc)
    rows = r_heads * Q_BLOCK
    per_bg = lambda shape: pl.BlockSpec((1, 1) + shape, lambda b, gi, i: (b, gi, 0, 0))
    per_g = lambda shape: pl.BlockSpec((1,) + shape, lambda b, gi, i: (gi, 0, 0))
    return pl.pallas_call(
        kern,
        grid=(bsz, g, nq),
        in_specs=[pl.BlockSpec((Q_BLOCK, r_heads * 128), lambda b, gi, i: (b * nq + i, gi)),
                  pl.BlockSpec((1, Q_BLOCK, 128), lambda b, gi, i: (gi, b * nq + i, 0)),
                  per_bg((128, nc)), per_bg((nc, 128)),
                  per_bg((128 + nb, tp)), per_bg((tp, 128)),
                  per_bg((128, tp)), per_bg((tp, 128)),
                  per_g((r_heads * Q_BLOCK, WINDOW + Q_BLOCK)), per_g((r_heads * Q_BLOCK, FAR_TILE)),
                  per_g((r_heads * Q_BLOCK, 128)),
                  pl.BlockSpec((nb, nc), lambda b, gi, i: (0, 0))],
        out_specs=pl.BlockSpec((Q_BLOCK, r_heads * hd), lambda b, gi, i: (b * nq + i, gi)),
        out_shape=jax.ShapeDtypeStruct((bsz * seq, g * r_heads * hd), BF16),
        scratch_shapes=[pltpu.VMEM((128, nc), BF16), pltpu.VMEM((rows, WINDOW + Q_BLOCK), F32),
                        pltpu.VMEM((rows, FAR_TILE), F32), pltpu.VMEM((rows, FAR_TILE), F32),
                        pltpu.VMEM((rows, FAR_TILE), BF16), pltpu.VMEM((rows, FAR_TILE), BF16),
                        pltpu.VMEM((rows, 1), F32), pltpu.VMEM((rows, 1), F32),
                        pltpu.VMEM((rows, 1), F32), pltpu.VMEM((rows, 128), F32)],
        compiler_params=_cparams(3),
        name="nsa_attn_prompt",
    )(q_pad, gates_g, kct, vc, kst, vs, kwt, vw, wtab, stab, atab, poolt)


def _rel_bucket_np(dist):
    n = np.maximum(dist, 0)
    exact = REL_BUCKETS // 2
    nf = np.maximum(n, 1).astype(np.float32)
    large = exact + (np.log(nf / np.float32(exact)) / np.float32(math.log(REL_MAX_DIST / exact))
                     * np.float32(REL_BUCKETS - exact)).astype(np.int32)
    return np.where(n < exact, n, np.minimum(large, REL_BUCKETS - 1))


FAR_DIST = Q_BLOCK - CMP_STRIDE + 1
assert np.all(_rel_bucket_np(np.arange(FAR_DIST, 1 << 16)) == REL_BUCKETS - 1)


def _bias_table(rel_bias, dist, valid):
    relc = rel_bias.astype(F32) - rel_bias[REL_BUCKETS - 1].astype(F32)[None]
    onehot = np.eye(REL_BUCKETS, dtype=np.float32)[_rel_bucket_np(dist)]
    tab = jnp.einsum('...b,bh->...h', onehot, relc, precision=lax.Precision.HIGHEST)
    tab = jnp.where(valid[..., None], tab, NEG)
    return jnp.moveaxis(tab, -1, 0)


def _prompt_tables(rel_bias, nb, nc):
    g, r_heads = N_KV_HEADS, GQA_R
    qi = np.arange(Q_BLOCK)[:, None]
    nw = WINDOW + Q_BLOCK
    d_w = qi + WINDOW - np.arange(nw)[None, :]
    wtab = _bias_table(rel_bias, d_w, (d_w >= 0) & (d_w <= WINDOW)).reshape(g, r_heads * Q_BLOCK, nw)
    d_s = d_w[:, nw - FAR_TILE:]
    stab = _bias_table(rel_bias, d_s, d_s >= 0).reshape(g, r_heads * Q_BLOCK, FAR_TILE)
    d_c = qi - CMP_STRIDE * (np.arange(16)[None, :] - 8) - (CMP_LEN - 1)
    valid_c = d_c >= 0
    a = _bias_table(rel_bias, d_c, valid_c).reshape(g, r_heads * Q_BLOCK, 16)
    hi = a.astype(BF16)
    lo = jnp.where(jnp.asarray(np.tile(valid_c, (r_heads, 1)))[None], a - hi.astype(F32), 0.0).astype(BF16)
    fut = jnp.full((g, r_heads * Q_BLOCK, 1), NEG, BF16)
    atab = jnp.concatenate([jnp.zeros((g, r_heads * Q_BLOCK, ATAB_LANE), BF16), hi, lo, fut,
                            jnp.zeros((g, r_heads * Q_BLOCK, 15), BF16)], axis=-1)
    jb = np.arange(nb)[:, None]
    n = np.arange(nc)[None, :]
    poolt = jnp.asarray(((n >= CMP_PER_SEL * jb - 1) & (n <= CMP_PER_SEL * jb + CMP_PER_SEL - 1)).astype(np.float32),
                        BF16)
    return wtab, stab, atab, poolt


CMP_PAGES_PER_STEP = 16
SEL_PAGES_PER_STEP = 16


def _gather_cmp_kernel(pt_ref, *refs):
    del pt_ref
    page_refs = refs[:CMP_PAGES_PER_STEP]
    o_ref, t_ref = refs[CMP_PAGES_PER_STEP:]
    n_fb = t_ref.shape[0]
    for k in range(CMP_PAGES_PER_STEP):
        for fb in range(n_fb):
            t_ref[fb, k * PAGE_SIZE:(k + 1) * PAGE_SIZE, :] = page_refs[k][0, 0, fb * 128:(fb + 1) * 128, :].T
    rows = CMP_PAGES_PER_STEP * PAGE_SIZE // CMP_STRIDE
    for s in range(CMP_STRIDE):
        for fb in range(n_fb):
            c0 = (s * n_fb + fb) * 128
            o_ref[0, :, c0:c0 + 128] = t_ref[fb, pl.ds(s, rows, stride=CMP_STRIDE), :].astype(BF16)


def _gather_cmp(cache_t, layer, page_table):
    bsz, n_pages = page_table.shape
    half = cache_t.shape[2] // 2
    pp = CMP_PAGES_PER_STEP
    chunks = pp * PAGE_SIZE // CMP_STRIDE
    page_spec = lambda k: pl.BlockSpec((1, 1, half, PAGE_SIZE), lambda b, i, pt: (layer, pt[b, i * pp + k], 0, 0))
    return pl.pallas_call(
        _gather_cmp_kernel,
        grid_spec=pltpu.PrefetchScalarGridSpec(
            num_scalar_prefetch=1, grid=(bsz, n_pages // pp),
            in_specs=[page_spec(k) for k in range(pp)],
            out_specs=pl.BlockSpec((1, chunks, CMP_STRIDE * half), lambda b, i, pt: (b, i, 0)),
            scratch_shapes=[pltpu.VMEM((half // 128, pp * PAGE_SIZE, 128), F32)]),
        out_shape=jax.ShapeDtypeStruct((bsz, n_pages * PAGE_SIZE // CMP_STRIDE, CMP_STRIDE * half), BF16),
        compiler_params=_cparams(2),
        name="gather_cmp",
    )(page_table, *([cache_t] * pp))


def _nsa_attn_sample_kernel(pt_ref, q_ref, gt_ref, kct_ref, vc_ref, new_ref, win_ref,
                            tc_ref, tnear_ref, tnew_ref, tw_ref, pool_ref, e_ref, rsum_ref, *refs,
                            t, n_past_blk):
    del pt_ref
    pp = SEL_PAGES_PER_STEP
    k_refs, v_refs = refs[:pp], refs[pp:2 * pp]
    o_ref, m_ref, l_ref, acc_ref, selb_ref, oc_ref, ow_ref = refs[2 * pp:]
    g, r_heads, hd = N_KV_HEADS, GQA_R, HEAD_DIM
    gw = g * hd
    i = pl.program_id(1)
    n_steps = pl.num_programs(1)
    nt = (((1,), (1,)), ((), ()))
    q = q_ref[0]
    tile = pp * PAGE_SIZE

    @pl.when(i == 0)
    def _():
        new = new_ref[0]
        s_c = jnp.dot(q, kct_ref[0], preferred_element_type=F32) + tc_ref[...]
        m_c = jnp.max(s_c, axis=1, keepdims=True)
        e_c = jnp.exp(s_c - m_c)
        inv_c = jnp.where(m_c > NEG_VALID, 1.0 / jnp.maximum(jnp.sum(e_c, axis=1, keepdims=True), 1e-30), 0.0)
        p_c = e_c * inv_c
        oc_ref[...] = jnp.dot(p_c.astype(BF16), vc_ref[0], preferred_element_type=F32)
        imp0 = jnp.zeros((q.shape[0], n_past_blk), F32)
        for part in _split3(p_c):
            imp0 = imp0 + jnp.dot(part, pool_ref[...], preferred_element_type=F32)
        imp = jnp.zeros((q.shape[0], n_past_blk), F32)
        for part in _split3(imp0):
            imp = imp + jnp.dot(rsum_ref[...], part, preferred_element_type=F32)
        blk = lax.broadcasted_iota(jnp.int32, (n_past_blk, q.shape[0]), 0)
        forced = (blk == 0) | (blk == n_past_blk - 1)
        sel = _top_blocks(jnp.where(forced, FORCE_SCORE, imp.T), blk, min(TOP_N - 1, n_past_blk), n_past_blk)
        selb_ref[...] = jnp.where(sel, 0.0, NEG).T.astype(BF16)
        win = win_ref[0, 0].astype(BF16)
        s_w = jnp.dot(q, win[:gw], preferred_element_type=F32) + tw_ref[...]
        s_wn = lax.dot_general(q, new[:, 4 * gw:5 * gw], nt, preferred_element_type=F32) + tnew_ref[...]
        m_w = jnp.maximum(jnp.max(s_w, axis=1, keepdims=True), jnp.max(s_wn, axis=1, keepdims=True))
        p_w = jnp.exp(s_w - m_w)
        p_wn = jnp.exp(s_wn - m_w)
        l_w = jnp.sum(p_w, axis=1, keepdims=True) + jnp.sum(p_wn, axis=1, keepdims=True)
        o_w = (lax.dot_general(p_w.astype(BF16), win[gw:], nt, preferred_element_type=F32)
               + jnp.dot(p_wn.astype(BF16), new[:, 5 * gw:6 * gw], preferred_element_type=F32))
        ow_ref[...] = o_w / l_w
        s_n = lax.dot_general(q, new[:, 2 * gw:3 * gw], nt, preferred_element_type=F32) + tnew_ref[...]
        m_n = jnp.max(s_n, axis=1, keepdims=True)
        p_n = jnp.exp(s_n - m_n)
        m_ref[...] = m_n
        l_ref[...] = jnp.sum(p_n, axis=1, keepdims=True)
        acc_ref[...] = jnp.dot(p_n.astype(BF16), new[:, 3 * gw:4 * gw], preferred_element_type=F32)

    k_t = jnp.concatenate([r[0, 0] for r in k_refs], axis=1).astype(BF16)
    v_t = jnp.concatenate([r[0, 0] for r in v_refs], axis=1).astype(BF16)
    s = (jnp.dot(q, k_t, preferred_element_type=F32)
         + jnp.dot(selb_ref[...], e_ref[:, pl.ds(pl.multiple_of(i * tile, tile), tile)], preferred_element_type=F32))
    s = s + jnp.where(i == n_steps - 1, tnear_ref[...], 0.0)
    m_old = m_ref[...]
    m_new = jnp.maximum(m_old, jnp.max(s, axis=1, keepdims=True))
    alpha = jnp.exp(m_old - m_new)
    p = jnp.exp(s - m_new)
    m_ref[...] = m_new
    l_ref[...] = alpha * l_ref[...] + jnp.sum(p, axis=1, keepdims=True)
    acc_ref[...] = alpha * acc_ref[...] + lax.dot_general(p.astype(BF16), v_t, nt, preferred_element_type=F32)

    @pl.when(i == n_steps - 1)
    def _():
        gt = gt_ref[0]
        o = gt[:, 0:1] * oc_ref[...] + gt[:, 1:2] * (acc_ref[...] / l_ref[...]) + gt[:, 2:3] * ow_ref[...]
        for gi in range(g):
            for r in range(r_heads):
                h = gi * r_heads + r
                o_ref[0, :, h * hd:(h + 1) * hd] = o[h * t:(h + 1) * t, gi * hd:(gi + 1) * hd].astype(BF16)


def _nsa_attn_sample(q_bd, gates, kct, vc, new_pad, win_t, layer, cache_t, page_table, tabs, t):
    bsz, n_pages = page_table.shape
    g, r_heads, hd = N_KV_HEADS, GQA_R, HEAD_DIM
    gw = g * hd
    lanes = g * r_heads * t
    pp = SEL_PAGES_PER_STEP
    n_past_blk = n_pages * PAGE_SIZE // SEL_LEN
    kern = functools.partial(_nsa_attn_sample_kernel, t=t, n_past_blk=n_past_blk)
    per_b = lambda a: pl.BlockSpec((1,) + a.shape[1:], lambda b, i, pt: (b, 0, 0))
    full = lambda a: pl.BlockSpec(a.shape, lambda b, i, pt: (0, 0))
    page_spec = lambda fblk, k: pl.BlockSpec((1, 1, gw, PAGE_SIZE),
                                             lambda b, i, pt: (layer, pt[b, i * pp + k], fblk, 0))
    return pl.pallas_call(
        kern,
        grid_spec=pltpu.PrefetchScalarGridSpec(
            num_scalar_prefetch=1, grid=(bsz, n_pages // pp),
            in_specs=[per_b(q_bd), per_b(gates), per_b(kct), per_b(vc), per_b(new_pad),
                      pl.BlockSpec((1, 1) + win_t.shape[2:], lambda b, i, pt: (layer, b, 0, 0))]
                     + [full(a) for a in tabs]
                     + [page_spec(2, k) for k in range(pp)] + [page_spec(3, k) for k in range(pp)],
            out_specs=pl.BlockSpec((1, t, g * r_heads * hd), lambda b, i, pt: (b, 0, 0)),
            scratch_shapes=[pltpu.VMEM((lanes, 1), F32), pltpu.VMEM((lanes, 1), F32), pltpu.VMEM((lanes, gw), F32),
                            pltpu.VMEM((lanes, n_past_blk), BF16), pltpu.VMEM((lanes, gw), F32),
                            pltpu.VMEM((lanes, gw), F32)]),
        out_shape=jax.ShapeDtypeStruct((bsz, t, g * r_heads * hd), BF16),
        compiler_params=_cparams(2),
        name="nsa_attn_sample",
    )(page_table, q_bd, gates, kct, vc, new_pad, win_t, *tabs, *([cache_t] * (2 * pp)))


def _sample_tables(rel_bias, pos0, t, nc):
    g, r_heads = N_KV_HEADS, GQA_R
    lanes = g * r_heads * t
    tq = pos0 + np.arange(t)[:, None]

    def table(kpos, extra_valid=True):
        dist = tq - kpos[None, :]
        tab = _bias_table(rel_bias, dist, (dist >= 0) & extra_valid)
        return tab.reshape(lanes, kpos.shape[0])

    tc = table(CMP_STRIDE * np.arange(nc) + CMP_LEN - 1)
    tnear = table(pos0 - Q_BLOCK + np.arange(Q_BLOCK))
    tnear = jnp.pad(tnear, ((0, 0), (SEL_PAGES_PER_STEP * PAGE_SIZE - Q_BLOCK, 0)))
    new_pos = pos0 + np.arange(128)
    tnew = table(new_pos, (new_pos < pos0 + t)[None, :])
    kw = pos0 - WINDOW + np.arange(WINDOW)
    tw = table(kw, (tq - kw[None, :]) <= WINDOW)
    n_past_blk = pos0 // SEL_LEN
    jb = np.arange(n_past_blk)[None, :]
    n = np.arange(nc)[:, None]
    pool = jnp.asarray(((n >= CMP_PER_SEL * jb - 1) & (n <= CMP_PER_SEL * jb + CMP_PER_SEL - 1)).astype(np.float32),
                       BF16)
    e = jnp.asarray((np.arange(pos0)[None, :] // SEL_LEN == np.arange(n_past_blk)[:, None]).astype(np.float32), BF16)
    li = np.arange(lanes)
    same = (li[:, None] // (r_heads * t) == li[None, :] // (r_heads * t)) & (li[:, None] % t == li[None, :] % t)
    rsum = jnp.asarray(same.astype(np.float32), BF16)
    return tc, tnear, tnew, tw, pool, e, rsum


def _nsa_sample(xs2d, bsz, t, layer, cache_t, page_table, win_t, wts, tabs):
    g, r_heads, hd = N_KV_HEADS, GQA_R, HEAD_DIM
    gw = g * hd
    q, kv, kvb, gt = _nsa_proj(xs2d, wts["wq"], wts["wkv"], wts["wg"], tm=xs2d.shape[0])
    chunks = _gather_cmp(cache_t, layer, page_table)
    kct, vc = _nsa_cmp(chunks, wts["w1"], wts["pe_term"], wts["w2k"].T, wts["w2v"], transposed=False)
    q5 = q.reshape(bsz, t, g, r_heads, hd)
    q_bd = jnp.einsum('btgrd,gh->bgrthd', q5, jnp.eye(g, dtype=q.dtype)).reshape(bsz, g * r_heads * t, gw)
    gates = jnp.pad(gt[:, :3 * g * r_heads].reshape(bsz, t, g * r_heads, 3).transpose(0, 2, 1, 3)
                    .reshape(bsz, g * r_heads * t, 3), ((0, 0), (0, 0), (0, 128 - 3)))
    new_pad = jnp.pad(kvb.reshape(bsz, t, 6 * gw), ((0, 0), (0, 128 - t), (0, 0)))
    o = _nsa_attn_sample(q_bd, gates, kct, vc, new_pad, win_t, layer, cache_t, page_table, tabs, t)
    return o.reshape(bsz * t, g * r_heads * hd), kv


def _nsa_weights(w_in, cmp_pe, cmp_w1, cmp_b1, cmp_w2):
    d = w_in.shape[0]
    g, r_heads, hd = N_KV_HEADS, GQA_R, HEAD_DIM
    qd, kvd = g * r_heads * hd, 6 * g * hd
    wq = (w_in[:, :qd] * hd ** -0.5).astype(BF16)
    wq_pad = jnp.pad(wq.reshape(d, g * r_heads, hd), ((0, 0), (0, 0), (0, 128 - hd))).reshape(d, g * r_heads * 128)
    wkv = w_in[:, qd:qd + kvd].astype(BF16)
    wg = jnp.pad(w_in[:, qd + kvd:], ((0, 0), (0, 128 - 3 * g * r_heads))).astype(BF16)
    w1r = cmp_w1.reshape(2, CMP_LEN, hd, CMP_HID)
    w1 = jnp.concatenate([w1r[:, :CMP_STRIDE], w1r[:, CMP_STRIDE:]], axis=-1).astype(BF16)
    w1 = w1.reshape(2, CMP_STRIDE // CMP_PACK, CMP_PACK * hd, 2 * CMP_HID)
    pe_term = jnp.einsum('csd,csdh->ch', cmp_pe, w1r) + cmp_b1
    return dict(wq=wq, wq_pad=wq_pad, wkv=wkv, wg=wg, w1=w1, pe_term=pe_term,
                w2k=cmp_w2[0].astype(BF16), w2v=cmp_w2[1].astype(BF16))


def _nsa_prompt(x2d, bsz, seq, wts, tables):
    g, r_heads, hd = N_KV_HEADS, GQA_R, HEAD_DIM
    m = bsz * seq
    nc, nb = seq // CMP_STRIDE, seq // SEL_LEN
    q_pad, kv, kv_chunks, gt = _nsa_proj(x2d, wts["wq_pad"], wts["wkv"], wts["wg"], tm=ROW_TILE, chunk_seq=seq)
    w2k_t = jnp.pad(wts["w2k"].T, ((0, 128 - hd), (0, 0)))
    w2v_p = jnp.pad(wts["w2v"], ((0, 0), (0, 128 - hd)))
    kct, vc = _nsa_cmp(kv_chunks, wts["w1"], wts["pe_term"], w2k_t, w2v_p, transposed=True)
    kst, kwt, vs, vw = _nsa_prep(kv.reshape(bsz, seq, kv.shape[1]), nb)
    gates_g = jnp.pad(gt[:, :3 * g * r_heads].reshape(m, g, 3 * r_heads).transpose(1, 0, 2),
                      ((0, 0), (0, 0), (0, 128 - 3 * r_heads)))
    o = _nsa_attn_prompt(q_pad, gates_g, kct, vc, kst, vs, kwt, vw, *tables, bsz, seq)
    return o, kv


def kernel(x_prompt, x_sample, cache_nsa, state_nsa_win, state_pool, state_ffn, page_table, p_prompt, p_sample,
           rel_bias, nsa_w_in, nsa_cmp_pe, nsa_cmp_w1, nsa_cmp_b1, nsa_cmp_w2, nsa_w_out, pool_w, pool_b,
           pool_scale, ffn_w_up, ffn_conv_w, ffn_conv_b, ffn_w_down, ln_g, ln_b, ple_w_proj, ple_w_gate,
           ple_b_gate):
    bsz, seq, d = x_prompt.shape
    db, dt, _ = x_sample.shape
    depth = ffn_w_up.shape[0]
    alpha = (2.0 * depth) ** 0.25
    g, hd = N_KV_HEADS, HEAD_DIM
    gw = g * hd
    past_len = page_table.shape[1] * PAGE_SIZE
    mp, ms = bsz * seq, db * dt
    assert g * GQA_R * dt == 128 and past_len % SEL_LEN == 0 and seq % PADL == 0 and seq % ROW_TILE == 0
    assert page_table.shape[1] % SEL_PAGES_PER_STEP == 0 and page_table.shape[1] % CMP_PAGES_PER_STEP == 0
    assert state_nsa_win.shape[2] == WINDOW and seq >= WINDOW and state_pool.shape[2] == POOL_HIST

    xp = x_prompt.reshape(mp, d)
    xs = x_sample.reshape(ms, d)
    tables_p = _prompt_tables(rel_bias, seq // SEL_LEN, seq // CMP_STRIDE)
    tables_s = _sample_tables(rel_bias, past_len, dt, past_len // CMP_STRIDE)
    cache_t = jnp.transpose(cache_nsa, (0, 1, 3, 4, 5, 2)).reshape(cache_nsa.shape[0], cache_nsa.shape[1], 4 * gw,
                                                                   PAGE_SIZE)
    win_t = jnp.transpose(state_nsa_win, (0, 1, 3, 4, 5, 2)).reshape(state_nsa_win.shape[0], db, 2 * gw, WINDOW)
    nsa_rp, nsa_rs, nsa_wp, nsa_ws, pool_hp, pool_hs, ffn_hp, ffn_hs = ([] for _ in range(8))
    for i in range(depth):
        j = i // 2
        g0, b0, g1, b1 = ln_g[i, 0][None], ln_b[i, 0][None], ln_g[i, 1][None], ln_b[i, 1][None]
        if i % 2 == 0:
            wts = _nsa_weights(nsa_w_in[j], nsa_cmp_pe[j], nsa_cmp_w1[j], nsa_cmp_b1[j], nsa_cmp_w2[j])
            o_p, kv_p = _nsa_prompt(xp, bsz, seq, wts, tables_p)
            o_s, kv_s = _nsa_sample(xs, db, dt, j, cache_t, page_table, win_t, wts, tables_s)
            w_out = nsa_w_out[j].astype(BF16)
            xp = _linear_ln(o_p, w_out, xp, g0, b0, alpha, tm=ROW_TILE)
            xs = _linear_ln(o_s, w_out, xs, g0, b0, alpha, tm=ms)
            kv_p3 = kv_p.reshape(bsz, seq, 6, g, hd)
            kv_s3 = kv_s.reshape(db, dt, 6, g, hd)
            nsa_rp.append(kv_p3[:, :, :4])
            nsa_rs.append(kv_s3[:, :, :4])
            nsa_wp.append(kv_p3[:, seq - WINDOW:, 4:])
            nsa_ws.append(jnp.concatenate([state_nsa_win[j][:, dt:], kv_s3[:, :, 4:]], axis=1))
        else:
            xp3 = xp.reshape(bsz, seq, d)
            xe = jnp.concatenate([state_pool[j], xs.reshape(db, dt, d)], axis=1)
            pool_hp.append(xp3[:, seq - POOL_HIST:])
            pool_hs.append(xe[:, xe.shape[1] - POOL_HIST:])
            pw = pool_w[j].astype(BF16)
            xp = _pool_ln_prompt(xp3, pw, pool_b[j][None], pool_scale[j][None], g0, b0, alpha).reshape(mp, d)
            xs = _pool_ln_sample(xe, dt, past_len, pw, pool_b[j][None], pool_scale[j][None], g0, b0,
                                 alpha).reshape(ms, d)
        w_up = ffn_w_up[i].astype(BF16)
        w_down = ffn_w_down[i].astype(BF16)
        cw, cb = ffn_conv_w[i], ffn_conv_b[i][None]
        h_p = _ffn_up_prompt(xp, w_up, cw, cb, seq)
        tail = xp.reshape(bsz, seq, d)[:, seq - 8:].reshape(bsz * 8, d)
        u_tail = _matmul(tail, w_up, tm=bsz * 8).reshape(bsz, 8, -1)
        ffn_hp.append(u_tail[:, 8 - (CONV_W - 1):])
        u_s = _matmul(xs, w_up, tm=ms).reshape(db, dt, -1)
        ue = jnp.concatenate([state_ffn[i], u_s], axis=1)
        ffn_hs.append(ue[:, ue.shape[1] - (CONV_W - 1):])
        h_s = _ffn_gate_sample(ue, cw, cb, dt).reshape(ms, -1)
        ple = (ple_w_gate[i].astype(BF16), ple_b_gate[i][None], ple_w_proj[i].astype(BF16))
        xp = _ffn_down_ln_ple(h_p, w_down, xp, g1, b1, p_prompt[i].reshape(mp, -1), *ple, alpha, ROW_TILE)
        xs = _ffn_down_ln_ple(h_s, w_down, xs, g1, b1, p_sample[i].reshape(ms, -1), *ple, alpha, ms)
    return (xp.reshape(bsz, seq, d), xs.reshape(db, dt, d),
            jnp.stack(nsa_rp), jnp.stack(nsa_rs), jnp.stack(nsa_wp), jnp.stack(nsa_ws),
            jnp.stack(pool_hp), jnp.stack(pool_hs), jnp.stack(ffn_hp), jnp.stack(ffn_hs))
```

```python
import functools
import math

import numpy as np
import jax
import jax.numpy as jnp
from jax import lax
from jax.experimental import pallas as pl
from jax.experimental.pallas import tpu as pltpu

N_KV_HEADS = 4
GQA_R = 4
HEAD_DIM = 64
CMP_LEN = 32
CMP_STRIDE = 16
CMP_HID = 256
CMP_PACK = 4
SEL_LEN = 64
CMP_PER_SEL = SEL_LEN // CMP_STRIDE
TOP_N = 16
WINDOW = 512
Q_BLOCK = 128
FORCE_SCORE = 1.0e4
REL_BUCKETS = 32
REL_MAX_DIST = 128
POOL_WINDOWS = (2, 4, 8, 16)
POOL_MAX = max(POOL_WINDOWS)
POOL_HIST = POOL_MAX - 1
CONV_W = 3
LN_EPS = 1e-5
PAGE_SIZE = 128

NEG = -1.0e30
NEG_VALID = -1.0e29
BF16 = jnp.bfloat16
F32 = jnp.float32

VMEM_LIMIT_BYTES = 56 * 1024 * 1024
ROW_TILE = 512


def _cparams(n_grid):
    return pltpu.CompilerParams(dimension_semantics=("arbitrary",) * n_grid,
                                vmem_limit_bytes=VMEM_LIMIT_BYTES)


def _layer_norm(y, g, b):
    mu = jnp.mean(y, axis=-1, keepdims=True)
    yc = y - mu
    var = jnp.mean(yc * yc, axis=-1, keepdims=True)
    return yc * lax.rsqrt(var + LN_EPS) * g + b


FFN_TN = 256
HALO = 16


def _ffn_up_prompt_kernel(x_ref, xh_ref, w_ref, cw_ref, cb_ref, h_ref, xe_ref, ue_ref, *,
                          tm, d_ff, tiles_per_batch):
    i = pl.program_id(0)
    first = (i % tiles_per_batch) == 0
    halo = xh_ref[...]
    xe_ref[0:HALO, :] = jnp.where(first, jnp.zeros_like(halo), halo).astype(BF16)
    xe_ref[HALO:HALO + tm, :] = x_ref[...].astype(BF16)
    xe = xe_ref[...]
    for j in range(d_ff // FFN_TN):
        c_parts = []
        for half in range(2):
            c0 = half * d_ff + j * FFN_TN
            ue_ref[...] = jnp.dot(xe, w_ref[:, c0:c0 + FFN_TN], preferred_element_type=F32)
            cw = cw_ref[:, c0:c0 + FFN_TN]
            c = cb_ref[:, c0:c0 + FFN_TN]
            for k in range(CONV_W):
                off = HALO - (CONV_W - 1) + k
                c = c + cw[k:k + 1, :] * ue_ref[off:off + tm, :]
            c_parts.append(c)
        h_ref[:, j * FFN_TN:(j + 1) * FFN_TN] = (jax.nn.gelu(c_parts[1]) * c_parts[0]).astype(BF16)


def _ffn_up_prompt(x, w_up, conv_w, conv_b, seq, tm=ROW_TILE):
    m, d = x.shape
    d_ff = w_up.shape[1] // 2
    tpb = seq // tm
    kern = functools.partial(_ffn_up_prompt_kernel, tm=tm, d_ff=d_ff, tiles_per_batch=tpb)
    return pl.pallas_call(
        kern,
        grid=(m // tm,),
        in_specs=[
            pl.BlockSpec((tm, d), lambda i: (i, 0)),
            pl.BlockSpec((HALO, d), lambda i: (jnp.maximum(i * (tm // HALO) - 1, 0), 0)),
            pl.BlockSpec(w_up.shape, lambda i: (0, 0)),
            pl.BlockSpec(conv_w.shape, lambda i: (0, 0)),
            pl.BlockSpec(conv_b.shape, lambda i: (0, 0)),
        ],
        out_specs=pl.BlockSpec((tm, d_ff), lambda i: (i, 0)),
        out_shape=jax.ShapeDtypeStruct((m, d_ff), BF16),
        scratch_shapes=[pltpu.VMEM((HALO + tm, d), BF16), pltpu.VMEM((HALO + tm, FFN_TN), F32)],
        compiler_params=_cparams(1),
        name="ffn_up_prompt",
    )(x, x, w_up, conv_w, conv_b)


def _ffn_gate_sample_kernel(ue_ref, cw_ref, cb_ref, h_ref, *, t, d_ff):
    for j in range(d_ff // FFN_TN):
        c_parts = []
        for half in range(2):
            c0 = half * d_ff + j * FFN_TN
            cw = cw_ref[:, c0:c0 + FFN_TN]
            c = cb_ref[:, c0:c0 + FFN_TN][None]
            for k in range(CONV_W):
                c = c + cw[k:k + 1, :][None] * ue_ref[:, k:k + t, c0:c0 + FFN_TN]
            c_parts.append(c)
        h_ref[:, :, j * FFN_TN:(j + 1) * FFN_TN] = (jax.nn.gelu(c_parts[1]) * c_parts[0]).astype(BF16)


def _ffn_gate_sample(ue, conv_w, conv_b, t):
    b, te, n2 = ue.shape
    d_ff = n2 // 2
    bb = 8
    kern = functools.partial(_ffn_gate_sample_kernel, t=t, d_ff=d_ff)
    return pl.pallas_call(
        kern,
        grid=(b // bb,),
        in_specs=[
            pl.BlockSpec((bb, te, n2), lambda i: (i, 0, 0)),
            pl.BlockSpec(conv_w.shape, lambda i: (0, 0)),
            pl.BlockSpec(conv_b.shape, lambda i: (0, 0)),
        ],
        out_specs=pl.BlockSpec((bb, t, d_ff), lambda i: (i, 0, 0)),
        out_shape=jax.ShapeDtypeStruct((b, t, d_ff), BF16),
        compiler_params=_cparams(1),
        name="ffn_gate_sample",
    )(ue, conv_w, conv_b)


def _matmul_kernel(x_ref, w_ref, o_ref):
    o_ref[...] = jnp.dot(x_ref[...].astype(BF16), w_ref[...], preferred_element_type=F32).astype(o_ref.dtype)


def _matmul(x, w, tm, tn=None, out_dtype=F32):
    m, k = x.shape
    n = w.shape[1]
    tn = n if tn is None else tn
    return pl.pallas_call(
        _matmul_kernel,
        grid=(m // tm, n // tn),
        in_specs=[pl.BlockSpec((tm, k), lambda i, j: (i, 0)),
                  pl.BlockSpec((k, tn), lambda i, j: (0, j))],
        out_specs=pl.BlockSpec((tm, tn), lambda i, j: (i, j)),
        out_shape=jax.ShapeDtypeStruct((m, n), out_dtype),
        compiler_params=_cparams(2),
        name="matmul",
    )(x, w)


def _linear_ln_kernel(a_ref, w_ref, r_ref, g_ref, b_ref, o_ref, *, alpha):
    f = jnp.dot(a_ref[...].astype(BF16), w_ref[...], preferred_element_type=F32)
    o_ref[...] = _layer_norm(alpha * r_ref[...] + f, g_ref[...], b_ref[...])


def _linear_ln(a, w, resid, g, b, alpha, tm):
    m, k = a.shape
    d = w.shape[1]
    kern = functools.partial(_linear_ln_kernel, alpha=alpha)
    return pl.pallas_call(
        kern,
        grid=(m // tm,),
        in_specs=[pl.BlockSpec((tm, k), lambda i: (i, 0)),
                  pl.BlockSpec((k, d), lambda i: (0, 0)),
                  pl.BlockSpec((tm, d), lambda i: (i, 0)),
                  pl.BlockSpec((1, d), lambda i: (0, 0)),
                  pl.BlockSpec((1, d), lambda i: (0, 0))],
        out_specs=pl.BlockSpec((tm, d), lambda i: (i, 0)),
        out_shape=jax.ShapeDtypeStruct((m, d), F32),
        compiler_params=_cparams(1),
        name="linear_ln",
    )(a, w, resid, g, b)


def _ffn_down_kernel(h_ref, wd_ref, r_ref, g_ref, b_ref, p_ref, wg_ref, bg_ref, wp_ref, o_ref, *, alpha):
    f = jnp.dot(h_ref[...], wd_ref[...], preferred_element_type=F32)
    x2 = _layer_norm(alpha * r_ref[...] + f, g_ref[...], b_ref[...])
    gate = jax.nn.sigmoid(jnp.dot(x2.astype(BF16), wg_ref[...], preferred_element_type=F32) + bg_ref[...])
    inj = jnp.dot(p_ref[...].astype(BF16), wp_ref[...], preferred_element_type=F32)
    o_ref[...] = x2 + gate * inj


def _ffn_down_ln_ple(h, w_down, resid, g, b, p, w_gate, b_gate, w_proj, alpha, tm):
    m, k = h.shape
    d = w_down.shape[1]
    pd = p.shape[1]
    kern = functools.partial(_ffn_down_kernel, alpha=alpha)
    full = lambda shape: pl.BlockSpec(shape, lambda i: (0, 0))
    return pl.pallas_call(
        kern,
        grid=(m // tm,),
        in_specs=[pl.BlockSpec((tm, k), lambda i: (i, 0)), full((k, d)),
                  pl.BlockSpec((tm, d), lambda i: (i, 0)), full((1, d)), full((1, d)),
                  pl.BlockSpec((tm, pd), lambda i: (i, 0)), full((d, d)), full((1, d)), full((pd, d))],
        out_specs=pl.BlockSpec((tm, d), lambda i: (i, 0)),
        out_shape=jax.ShapeDtypeStruct((m, d), F32),
        compiler_params=_cparams(1),
        name="ffn_down_ln_ple",
    )(h, w_down, resid, g, b, p, w_gate, b_gate, w_proj)


def _pool_finish(x, diffs, w_ref, pb_ref, ps_ref, g_ref, b_ref, alpha):
    ys = [jnp.dot(dg.astype(BF16), w_ref[gi], preferred_element_type=F32) for gi, dg in enumerate(diffs)]
    y = jnp.concatenate(ys, axis=-1) + pb_ref[...]
    return _layer_norm(alpha * x + ps_ref[...] * y, g_ref[...], b_ref[...])


def _pool_prompt_kernel(x_ref, xh_ref, w_ref, pb_ref, ps_ref, g_ref, b_ref, o_ref, xe_ref, *,
                        tt, gw, alpha):
    i = pl.program_id(1)
    halo = xh_ref[0]
    xe_ref[0:POOL_MAX, :] = jnp.where(i == 0, jnp.zeros_like(halo), halo)
    xe_ref[POOL_MAX:POOL_MAX + tt, :] = x_ref[0]
    pos = i * tt + lax.broadcasted_iota(jnp.int32, (tt, 1), 0)
    x = x_ref[0]
    diffs = []
    for gi, win in enumerate(POOL_WINDOWS):
        lo = gi * gw
        tot = x[:, lo:lo + gw]
        for k in range(1, win):
            tot = tot + xe_ref[POOL_MAX - k:POOL_MAX - k + tt, lo:lo + gw]
        cnt = jnp.minimum(win, pos + 1).astype(F32)
        diffs.append(tot / cnt - x[:, lo:lo + gw])
    o_ref[0] = _pool_finish(x, diffs, w_ref, pb_ref, ps_ref, g_ref, b_ref, alpha)


def _pool_ln_prompt(x, w, pb, ps, g, b, alpha, tt=ROW_TILE):
    bsz, t, d = x.shape
    gw = d // len(POOL_WINDOWS)
    kern = functools.partial(_pool_prompt_kernel, tt=tt, gw=gw, alpha=alpha)
    vec = pl.BlockSpec((1, d), lambda bi, i: (0, 0))
    return pl.pallas_call(
        kern,
        grid=(bsz, t // tt),
        in_specs=[pl.BlockSpec((1, tt, d), lambda bi, i: (bi, i, 0)),
                  pl.BlockSpec((1, POOL_MAX, d), lambda bi, i: (bi, jnp.maximum(i * (tt // POOL_MAX) - 1, 0), 0)),
                  pl.BlockSpec(w.shape, lambda bi, i: (0, 0, 0)), vec, vec, vec, vec],
        out_specs=pl.BlockSpec((1, tt, d), lambda bi, i: (bi, i, 0)),
        out_shape=jax.ShapeDtypeStruct((bsz, t, d), F32),
        scratch_shapes=[pltpu.VMEM((POOL_MAX + tt, d), F32)],
        compiler_params=_cparams(2),
        name="pool_ln_prompt",
    )(x, x, w, pb, ps, g, b)


def _pool_sample_kernel(xe_ref, w_ref, pb_ref, ps_ref, g_ref, b_ref, o_ref, *, t, nh, gw, pos0, alpha):
    bb = xe_ref.shape[0]
    d = xe_ref.shape[2]
    x = xe_ref[:, nh:nh + t, :].reshape(bb * t, d)
    cnt_np = np.minimum(np.array(POOL_WINDOWS)[:, None], pos0 + np.arange(t)[None, :] + 1).astype(np.float32)
    diffs = []
    for gi, win in enumerate(POOL_WINDOWS):
        lo = gi * gw
        tot = xe_ref[:, nh:nh + t, lo:lo + gw]
        for k in range(1, win):
            tot = tot + xe_ref[:, nh - k:nh - k + t, lo:lo + gw]
        if np.all(cnt_np[gi] == win):
            mean = tot / float(win)
        else:
            t_idx = lax.broadcasted_iota(jnp.int32, (1, t, 1), 1)
            mean = tot / jnp.minimum(win, pos0 + t_idx + 1).astype(F32)
        diffs.append(mean.reshape(bb * t, gw) - x[:, lo:lo + gw])
    o_ref[...] = _pool_finish(x, diffs, w_ref, pb_ref, ps_ref, g_ref, b_ref, alpha).reshape(bb, t, d)


def _pool_ln_sample(xe, t, pos0, w, pb, ps, g, b, alpha, bb=8):
    bsz, te, d = xe.shape
    gw = d // len(POOL_WINDOWS)
    kern = functools.partial(_pool_sample_kernel, t=t, nh=te - t, gw=gw, pos0=pos0, alpha=alpha)
    vec = pl.BlockSpec((1, d), lambda i: (0, 0))
    return pl.pallas_call(
        kern,
        grid=(bsz // bb,),
        in_specs=[pl.BlockSpec((bb, te, d), lambda i: (i, 0, 0)),
                  pl.BlockSpec(w.shape, lambda i: (0, 0, 0)), vec, vec, vec, vec],
        out_specs=pl.BlockSpec((bb, t, d), lambda i: (i, 0, 0)),
        out_shape=jax.ShapeDtypeStruct((bsz, t, d), F32),
        compiler_params=_cparams(1),
        name="pool_ln_sample",
    )(xe, w, pb, ps, g, b)


def _nsa_proj_kernel(x_ref, wq_ref, wkv_ref, wg_ref, q_ref, kv_ref, kvb_ref, gt_ref, *scratch):
    xb = x_ref[...].astype(BF16)
    q_ref[...] = jnp.dot(xb, wq_ref[...], preferred_element_type=F32).astype(BF16)
    kv = jnp.dot(xb, wkv_ref[...], preferred_element_type=F32)
    kv_ref[...] = kv
    gt_ref[...] = jax.nn.sigmoid(jnp.dot(xb, wg_ref[...], preferred_element_type=F32))
    if not scratch:
        kvb_ref[...] = kv.astype(BF16)
        return
    (t_ref,) = scratch
    n_fb, tm = t_ref.shape[0], t_ref.shape[1]
    for fb in range(n_fb):
        t_ref[fb] = kv[:, fb * 128:(fb + 1) * 128]
    for s in range(CMP_STRIDE):
        for fb in range(n_fb):
            kvb_ref[0, s, :, fb * 128:(fb + 1) * 128] = (
                t_ref[fb, pl.ds(s, tm // CMP_STRIDE, stride=CMP_STRIDE), :].astype(BF16))


def _nsa_proj(x, wq, wkv, wg, tm, chunk_seq=None):
    m, d = x.shape
    nq, nkv, ng = wq.shape[1], wkv.shape[1], wg.shape[1]
    full = lambda w: pl.BlockSpec(w.shape, lambda i: (0, 0))
    row = lambda n: pl.BlockSpec((tm, n), lambda i: (i, 0))
    if chunk_seq is None:
        kvb_spec, kvb_shape, scratch = row(nkv), jax.ShapeDtypeStruct((m, nkv), BF16), []
    else:
        feat = 2 * N_KV_HEADS * HEAD_DIM
        tpb = chunk_seq // tm
        kvb_spec = pl.BlockSpec((1, CMP_STRIDE, tm // CMP_STRIDE, feat), lambda i: (i // tpb, 0, i % tpb, 0))
        kvb_shape = jax.ShapeDtypeStruct((m // chunk_seq, CMP_STRIDE, chunk_seq // CMP_STRIDE, feat), BF16)
        scratch = [pltpu.VMEM((feat // 128, tm, 128), F32)]
    return pl.pallas_call(
        _nsa_proj_kernel,
        grid=(m // tm,),
        in_specs=[row(d), full(wq), full(wkv), full(wg)],
        out_specs=[row(nq), row(nkv), kvb_spec, row(ng)],
        out_shape=[jax.ShapeDtypeStruct((m, nq), BF16), jax.ShapeDtypeStruct((m, nkv), F32),
                   kvb_shape, jax.ShapeDtypeStruct((m, ng), F32)],
        scratch_shapes=scratch,
        compiler_params=_cparams(1),
        name="nsa_proj",
    )(x, wq, wkv, wg)


def _nsa_cmp_kernel(*refs, nc, transposed):
    ns = CMP_STRIDE
    x_refs = refs[:ns]
    w1_ref, pe_ref, w2k_ref, w2v_ref, ko_ref, vo_ref = refs[ns:]
    hd = HEAD_DIM
    gw = N_KV_HEADS * hd
    last = lax.broadcasted_iota(jnp.int32, (nc, CMP_HID), 0) == nc - 1
    nt = (((1,), (1,)), ((), ()))
    for c in range(2):
        for g in range(N_KV_HEADS):
            acc = jnp.zeros((nc, 2 * CMP_HID), F32)
            lo = c * gw + g * hd
            for k in range(ns // CMP_PACK):
                xs = jnp.concatenate([x_refs[k * CMP_PACK + i][:, lo:lo + hd] for i in range(CMP_PACK)], axis=1)
                acc = acc + jnp.dot(xs, w1_ref[c, k], preferred_element_type=F32)
            h1 = jnp.where(last, 0.0, pltpu.roll(acc[:, CMP_HID:], nc - 1, 0))
            hid = jax.nn.gelu(acc[:, :CMP_HID] + h1 + pe_ref[c:c + 1, :]).astype(BF16)
            if c == 0 and transposed:
                ko_ref[0, g] = lax.dot_general(w2k_ref[...], hid, nt, preferred_element_type=F32).astype(BF16)
            elif c == 0:
                ko_ref[0, g * hd:(g + 1) * hd, :] = lax.dot_general(w2k_ref[...], hid, nt,
                                                                    preferred_element_type=F32).astype(BF16)
            elif transposed:
                vo_ref[0, g] = jnp.dot(hid, w2v_ref[...], preferred_element_type=F32).astype(BF16)
            else:
                vo_ref[0, :, g * hd:(g + 1) * hd] = jnp.dot(hid, w2v_ref[...], preferred_element_type=F32).astype(BF16)


def _nsa_cmp(xv, w1, pe_term, w2k, w2v, transposed):
    g, hd = N_KV_HEADS, HEAD_DIM
    feat = 2 * g * hd
    bsz = xv.shape[0]
    if xv.ndim == 4:
        nc = xv.shape[2]
        x_specs = [pl.BlockSpec((None, None, nc, feat), functools.partial(lambda b, s: (b, s, 0, 0), s=s))
                   for s in range(CMP_STRIDE)]
    else:
        nc = xv.shape[1]
        x_specs = [pl.BlockSpec((None, nc, feat), functools.partial(lambda b, s: (b, 0, s), s=s))
                   for s in range(CMP_STRIDE)]
    kern = functools.partial(_nsa_cmp_kernel, nc=nc, transposed=transposed)
    full = lambda w: pl.BlockSpec(w.shape, lambda b: (0,) * w.ndim)
    if transposed:
        out_specs = [pl.BlockSpec((1, g, 128, nc), lambda b: (b, 0, 0, 0)),
                     pl.BlockSpec((1, g, nc, 128), lambda b: (b, 0, 0, 0))]
        out_shape = [jax.ShapeDtypeStruct((bsz, g, 128, nc), BF16), jax.ShapeDtypeStruct((bsz, g, nc, 128), BF16)]
    else:
        out_specs = [pl.BlockSpec((1, g * hd, nc), lambda b: (b, 0, 0)), pl.BlockSpec((1, nc, g * hd), lambda b: (b, 0, 0))]
        out_shape = [jax.ShapeDtypeStruct((bsz, g * hd, nc), BF16), jax.ShapeDtypeStruct((bsz, nc, g * hd), BF16)]
    return pl.pallas_call(
        kern,
        grid=(bsz,),
        in_specs=x_specs + [full(w1), full(pe_term), full(w2k), full(w2v)],
        out_specs=out_specs,
        out_shape=out_shape,
        compiler_params=_cparams(1),
        name="nsa_cmp",
    )(*([xv] * CMP_STRIDE), w1, pe_term, w2k, w2v)


PADL = 512
FAR_TILE = 512
ATAB_LANE = 80


def _nsa_prep_kernel(kv_ref, kst_ref, kwt_ref, vs_ref, vw_ref, *, nb):
    j = pl.program_id(1)
    tl = PADL
    hd = HEAD_DIM
    gw = N_KV_HEADS * hd

    @pl.when(j == 0)
    def _():
        for g in range(N_KV_HEADS):
            kst_ref[0, g] = (lax.broadcasted_iota(jnp.int32, (128 + nb, tl), 0) == hd).astype(BF16)
            kwt_ref[0, g] = (lax.broadcasted_iota(jnp.int32, (128, tl), 0) == hd).astype(BF16)
            vs_ref[0, g] = jnp.zeros((tl, 128), BF16)
            vw_ref[0, g] = jnp.zeros((tl, 128), BF16)

    @pl.when(j > 0)
    def _():
        kst = kv_ref[0, :, 2 * gw:3 * gw].T
        kwt = kv_ref[0, :, 4 * gw:5 * gw].T
        pos = (j - 1) * tl + lax.broadcasted_iota(jnp.int32, (nb, tl), 1)
        onehot = (pos // SEL_LEN == lax.broadcasted_iota(jnp.int32, (nb, tl), 0)).astype(BF16)
        zeros = jnp.zeros((128 - hd, tl), BF16)
        lane = lax.broadcasted_iota(jnp.int32, (tl, 128), 1)
        tail = (lane == hd).astype(F32)
        for g in range(N_KV_HEADS):
            kst_ref[0, g, 0:hd, :] = kst[g * hd:(g + 1) * hd].astype(BF16)
            kst_ref[0, g, hd:128, :] = zeros
            kst_ref[0, g, 128:128 + nb, :] = onehot
            kwt_ref[0, g, 0:hd, :] = kwt[g * hd:(g + 1) * hd].astype(BF16)
            kwt_ref[0, g, hd:128, :] = zeros
            for stream, ref in ((3, vs_ref), (5, vw_ref)):
                a = kv_ref[0, :, stream * gw + (g // 2) * 128:stream * gw + (g // 2 + 1) * 128]
                if g % 2:
                    a = pltpu.roll(a, hd, 1)
                ref[0, g] = jnp.where(lane < hd, a, tail).astype(BF16)


def _nsa_prep(kv, nb):
    bsz, t, n = kv.shape
    g = N_KV_HEADS
    tl = PADL
    tp = PADL + t
    kern = functools.partial(_nsa_prep_kernel, nb=nb)
    kspec = lambda rows: pl.BlockSpec((1, g, rows, tl), lambda b, j: (b, 0, 0, j))
    vspec = pl.BlockSpec((1, g, tl, 128), lambda b, j: (b, 0, j, 0))
    return pl.pallas_call(
        kern,
        grid=(bsz, tp // tl),
        in_specs=[pl.BlockSpec((1, tl, n), lambda b, j: (b, jnp.maximum(j - 1, 0), 0))],
        out_specs=[kspec(128 + nb), kspec(128), vspec, vspec],
        out_shape=[jax.ShapeDtypeStruct((bsz, g, 128 + nb, tp), BF16), jax.ShapeDtypeStruct((bsz, g, 128, tp), BF16),
                   jax.ShapeDtypeStruct((bsz, g, tp, 128), BF16), jax.ShapeDtypeStruct((bsz, g, tp, 128), BF16)],
        compiler_params=_cparams(2),
        name="nsa_prep",
    )(kv)


def _split3(x):
    a = x.astype(BF16)
    r = x - a.astype(F32)
    b = r.astype(BF16)
    return a, b, (r - b.astype(F32)).astype(BF16)


def _top_blocks(score, blk, n_sel, n_rows):
    sel = jnp.zeros(score.shape, jnp.bool_)
    for _ in range(n_sel):
        m = jnp.max(score, axis=0, keepdims=True)
        first = jnp.min(jnp.where(score == m, blk, n_rows), axis=0, keepdims=True)
        pick = blk == first
        sel = sel | pick
        score = jnp.where(pick, -3.0e38, score)
    return sel


def _nsa_attn_prompt_kernel(q_ref, gt_ref, kct_ref, vc_ref, kst_ref, vs_ref, kwt_ref, vw_ref,
                            wtab_ref, stab_ref, atab_ref, poolt_ref, o_ref,
                            kc_ref, sw_ref, sa_ref, sb_ref, pa_ref, pb_ref, ala_ref, alb_ref, m_ref, acc_ref, *, nb, nc):
    qi = pl.program_id(2)
    t0 = qi * Q_BLOCK
    r_heads = GQA_R
    hd = HEAD_DIM
    rows = r_heads * Q_BLOCK
    nw = WINDOW + Q_BLOCK
    lane = lax.broadcasted_iota(jnp.int32, (Q_BLOCK, 128), 1)
    q = jnp.concatenate([jnp.where(lane == hd, NEG, q_ref[:, r * 128:(r + 1) * 128]) for r in range(r_heads)],
                        axis=0)

    n_idx = lax.broadcasted_iota(jnp.int32, (48, nc), 1)
    j_idx = lax.broadcasted_iota(jnp.int32, (48, nc), 0)
    rel = n_idx - (qi * (Q_BLOCK // CMP_STRIDE) - 8)
    shift = (((j_idx < 32) & (rel == jnp.where(j_idx < 16, j_idx, j_idx - 16)))
             | ((j_idx == 32) & (rel >= 16))).astype(F32).astype(BF16)
    kc_ref[0:ATAB_LANE, :] = kct_ref[0, 0, 0:ATAB_LANE, :]
    kc_ref[ATAB_LANE:128, :] = shift
    lane_r = lax.broadcasted_iota(jnp.int32, (rows, 128), 1)
    s_c = jnp.dot(jnp.where(lane_r >= ATAB_LANE, atab_ref[0], q), kc_ref[...], preferred_element_type=F32)
    m_c = jnp.max(s_c, axis=1, keepdims=True)
    e_c = jnp.exp(s_c - m_c)
    inv_c = jnp.where(m_c > NEG_VALID, 1.0 / jnp.maximum(jnp.sum(e_c, axis=1, keepdims=True), 1e-30), 0.0)
    p_c = e_c * inv_c
    o_c = jnp.dot(p_c.astype(BF16), vc_ref[0, 0], preferred_element_type=F32)

    grp = p_c[0:Q_BLOCK]
    for r in range(1, r_heads):
        grp = grp + p_c[r * Q_BLOCK:(r + 1) * Q_BLOCK]
    imp_t = jnp.zeros((nb, Q_BLOCK), F32)
    for part in _split3(grp):
        imp_t = imp_t + lax.dot_general(poolt_ref[...], part, (((1,), (1,)), ((), ())),
                                        preferred_element_type=F32)
    blk = lax.broadcasted_iota(jnp.int32, (nb, Q_BLOCK), 0)
    tq = t0 + lax.broadcasted_iota(jnp.int32, (nb, Q_BLOCK), 1)
    cur = tq // SEL_LEN
    forced = (blk == 0) | (blk == cur) | (blk == cur - 1)
    score = jnp.where(blk * SEL_LEN > tq, -1.0, jnp.where(forced, -3.0e38, imp_t))
    sel = forced | _top_blocks(score, blk, min(TOP_N, nb) - 3, nb)
    selb = jnp.where(sel, 0.0, NEG).T.astype(BF16)
    q_aug = jnp.concatenate([q, jnp.concatenate([selb] * r_heads, axis=0)], axis=1)

    c_w = pl.multiple_of(t0 + PADL - WINDOW, Q_BLOCK)
    sw_ref[...] = jnp.dot(q, kwt_ref[0, 0, :, pl.ds(c_w, nw)], preferred_element_type=F32) + wtab_ref[0]

    c_near = pl.multiple_of(t0 + PADL + Q_BLOCK - FAR_TILE, Q_BLOCK)
    n_far = jnp.maximum((c_near - PADL + FAR_TILE - 1) // FAR_TILE, 0)

    def tile_start(k):
        return pl.multiple_of(c_near - FAR_TILE * (jnp.clip(k, -1, n_far - 1) + 1), Q_BLOCK)

    def scores(k, s_out):
        s_out[...] = jnp.dot(q_aug, kst_ref[0, 0, :, pl.ds(tile_start(k), FAR_TILE)], preferred_element_type=F32)

    def pv_update(k, p_in, al_in):
        acc_ref[...] = al_in[...] * acc_ref[...] + jnp.dot(p_in[...], vs_ref[0, 0, pl.ds(tile_start(k), FAR_TILE), :],
                                                          preferred_element_type=F32)

    def stage(k, s_cur, s_nxt, p_cur, al_cur, p_prev, al_prev):
        scores(k + 1, s_nxt)
        pv_update(k - 1, p_prev, al_prev)
        m_old = m_ref[...]
        m_new = jnp.maximum(m_old, jnp.max(s_cur[...], axis=1, keepdims=True))
        al_cur[...] = jnp.exp(m_old - m_new)
        p_cur[...] = jnp.exp(s_cur[...] - m_new).astype(BF16)
        m_ref[...] = m_new

    s = jnp.dot(q_aug, kst_ref[0, 0, :, pl.ds(c_near, FAR_TILE)], preferred_element_type=F32) + stab_ref[0]
    scores(0, sa_ref)
    m_s = jnp.max(s, axis=1, keepdims=True)
    m_ref[...] = m_s
    pb_ref[...] = jnp.exp(s - m_s).astype(BF16)
    alb_ref[...] = jnp.ones_like(alb_ref)
    acc_ref[...] = jnp.zeros_like(acc_ref)

    n_full = jnp.maximum(n_far - 1, 0)

    def pair(j, _):
        stage(2 * j, sa_ref, sb_ref, pa_ref, ala_ref, pb_ref, alb_ref)

        @pl.when(2 * j + 1 < n_full)
        def _():
            stage(2 * j + 1, sb_ref, sa_ref, pb_ref, alb_ref, pa_ref, ala_ref)
        return 0

    lax.fori_loop(0, (n_full + 1) // 2, pair, 0)

    def last_stage(s_cur, p_cur, al_cur, p_prev, al_prev):
        pv_update(n_far - 2, p_prev, al_prev)
        m_old = m_ref[...]
        m_new = jnp.maximum(m_old, jnp.max(s_cur[...], axis=1, keepdims=True))
        al_cur[...] = jnp.exp(m_old - m_new)
        p_cur[...] = jnp.exp(s_cur[...] - m_new).astype(BF16)
        pv_update(n_far - 1, p_cur, al_cur)

    @pl.when(n_far == 0)
    def _():
        pv_update(-1, pb_ref, alb_ref)

    @pl.when((n_far > 0) & (n_far % 2 == 1))
    def _():
        last_stage(sa_ref, pa_ref, ala_ref, pb_ref, alb_ref)

    @pl.when((n_far > 0) & (n_far % 2 == 0))
    def _():
        last_stage(sb_ref, pb_ref, alb_ref, pa_ref, ala_ref)

    acc = acc_ref[...]
    o_s = acc[:, :hd] * (1.0 / acc[:, hd:hd + 1])

    s_w = sw_ref[...]
    m_w = jnp.max(s_w, axis=1, keepdims=True)
    acc_w = jnp.dot(jnp.exp(s_w - m_w).astype(BF16), vw_ref[0, 0, pl.ds(c_w, nw), :], preferred_element_type=F32)
    o_w = acc_w[:, :hd] * (1.0 / acc_w[:, hd:hd + 1])

    gt = gt_ref[0]
    for r in range(r_heads):
        sl = slice(r * Q_BLOCK, (r + 1) * Q_BLOCK)
        o = (gt[:, 3 * r:3 * r + 1] * o_c[sl, :hd] + gt[:, 3 * r + 1:3 * r + 2] * o_s[sl]
             + gt[:, 3 * r + 2:3 * r + 3] * o_w[sl])
        o_ref[:, r * hd:(r + 1) * hd] = o.astype(BF16)


def _nsa_attn_prompt(q_pad, gates_g, kct, vc, kst, vs, kwt, vw, wtab, stab, atab, poolt, bsz, seq):
    g, r_heads, hd = N_KV_HEADS, GQA_R, HEAD_DIM
    nq = seq // Q_BLOCK
    nb, nc = poolt.shape
    tp = PADL + seq
    kern = functools.partial(_nsa_attn_prompt_kernel, nb=nb, nc=nc)
    rows = r_heads * Q_BLOCK
    per_bg = lambda shape: pl.BlockSpec((1, 1) + shape, lambda b, gi, i: (b, gi, 0, 0))
    per_g = lambda shape: pl.BlockSpec((1,) + shape, lambda b, gi, i: (gi, 0, 0))
    return pl.pallas_call(
        kern,
        grid=(bsz, g, nq),
        in_specs=[pl.BlockSpec((Q_BLOCK, r_heads * 128), lambda b, gi, i: (b * nq + i, gi)),
                  pl.BlockSpec((1, Q_BLOCK, 128), lambda b, gi, i: (gi, b * nq + i, 0)),
                  per_bg((128, nc)), per_bg((nc, 128)),
                  per_bg((128 + nb, tp)), per_bg((tp, 128)),
                  per_bg((128, tp)), per_bg((tp, 128)),
                  per_g((r_heads * Q_BLOCK, WINDOW + Q_BLOCK)), per_g((r_heads * Q_BLOCK, FAR_TILE)),
                  per_g((r_heads * Q_BLOCK, 128)),
                  pl.BlockSpec((nb, nc), lambda b, gi, i: (0, 0))],
        out_specs=pl.BlockSpec((Q_BLOCK, r_heads * hd), lambda b, gi, i: (b * nq + i, gi)),
        out_shape=jax.ShapeDtypeStruct((bsz * seq, g * r_heads * hd), BF16),
        scratch_shapes=[pltpu.VMEM((128, nc), BF16), pltpu.VMEM((rows, WINDOW + Q_BLOCK), F32),
                        pltpu.VMEM((rows, FAR_TILE), F32), pltpu.VMEM((rows, FAR_TILE), F32),
                        pltpu.VMEM((rows, FAR_TILE), BF16), pltpu.VMEM((rows, FAR_TILE), BF16),
                        pltpu.VMEM((rows, 1), F32), pltpu.VMEM((rows, 1), F32),
                        pltpu.VMEM((rows, 1), F32), pltpu.VMEM((rows, 128), F32)],
        compiler_params=_cparams(3),
        name="nsa_attn_prompt",
    )(q_pad, gates_g, kct, vc, kst, vs, kwt, vw, wtab, stab, atab, poolt)


def _rel_bucket_np(dist):
    n = np.maximum(dist, 0)
    exact = REL_BUCKETS // 2
    nf = np.maximum(n, 1).astype(np.float32)
    large = exact + (np.log(nf / np.float32(exact)) / np.float32(math.log(REL_MAX_DIST / exact))
                     * np.float32(REL_BUCKETS - exact)).astype(np.int32)
    return np.where(n < exact, n, np.minimum(large, REL_BUCKETS - 1))


FAR_DIST = Q_BLOCK - CMP_STRIDE + 1
assert np.all(_rel_bucket_np(np.arange(FAR_DIST, 1 << 16)) == REL_BUCKETS - 1)


def _bias_table(rel_bias, dist, valid):
    relc = rel_bias.astype(F32) - rel_bias[REL_BUCKETS - 1].astype(F32)[None]
    onehot = np.eye(REL_BUCKETS, dtype=np.float32)[_rel_bucket_np(dist)]
    tab = jnp.einsum('...b,bh->...h', onehot, relc, precision=lax.Precision.HIGHEST)
    tab = jnp.where(valid[..., None], tab, NEG)
    return jnp.moveaxis(tab, -1, 0)


def _prompt_tables(rel_bias, nb, nc):
    g, r_heads = N_KV_HEADS, GQA_R
    qi = np.arange(Q_BLOCK)[:, None]
    nw = WINDOW + Q_BLOCK
    d_w = qi + WINDOW - np.arange(nw)[None, :]
    wtab = _bias_table(rel_bias, d_w, (d_w >= 0) & (d_w <= WINDOW)).reshape(g, r_heads * Q_BLOCK, nw)
    d_s = d_w[:, nw - FAR_TILE:]
    stab = _bias_table(rel_bias, d_s, d_s >= 0).reshape(g, r_heads * Q_BLOCK, FAR_TILE)
    d_c = qi - CMP_STRIDE * (np.arange(16)[None, :] - 8) - (CMP_LEN - 1)
    valid_c = d_c >= 0
    a = _bias_table(rel_bias, d_c, valid_c).reshape(g, r_heads * Q_BLOCK, 16)
    hi = a.astype(BF16)
    lo = jnp.where(jnp.asarray(np.tile(valid_c, (r_heads, 1)))[None], a - hi.astype(F32), 0.0).astype(BF16)
    fut = jnp.full((g, r_heads * Q_BLOCK, 1), NEG, BF16)
    atab = jnp.concatenate([jnp.zeros((g, r_heads * Q_BLOCK, ATAB_LANE), BF16), hi, lo, fut,
                            jnp.zeros((g, r_heads * Q_BLOCK, 15), BF16)], axis=-1)
    jb = np.arange(nb)[:, None]
    n = np.arange(nc)[None, :]
    poolt = jnp.asarray(((n >= CMP_PER_SEL * jb - 1) & (n <= CMP_PER_SEL * jb + CMP_PER_SEL - 1)).astype(np.float32),
                        BF16)
    return wtab, stab, atab, poolt


CMP_PAGES_PER_STEP = 16
SEL_PAGES_PER_STEP = 16


def _gather_cmp_kernel(pt_ref, *refs):
    del pt_ref
    page_refs = refs[:CMP_PAGES_PER_STEP]
    o_ref, t_ref = refs[CMP_PAGES_PER_STEP:]
    n_fb = t_ref.shape[0]
    for k in range(CMP_PAGES_PER_STEP):
        for fb in range(n_fb):
            t_ref[fb, k * PAGE_SIZE:(k + 1) * PAGE_SIZE, :] = page_refs[k][0, 0, fb * 128:(fb + 1) * 128, :].T
    rows = CMP_PAGES_PER_STEP * PAGE_SIZE // CMP_STRIDE
    for s in range(CMP_STRIDE):
        for fb in range(n_fb):
            c0 = (s * n_fb + fb) * 128
            o_ref[0, :, c0:c0 + 128] = t_ref[fb, pl.ds(s, rows, stride=CMP_STRIDE), :].astype(BF16)


def _gather_cmp(cache_t, layer, page_table):
    bsz, n_pages = page_table.shape
    half = cache_t.shape[2] // 2
    pp = CMP_PAGES_PER_STEP
    chunks = pp * PAGE_SIZE // CMP_STRIDE
    page_spec = lambda k: pl.BlockSpec((1, 1, half, PAGE_SIZE), lambda b, i, pt: (layer, pt[b, i * pp + k], 0, 0))
    return pl.pallas_call(
        _gather_cmp_kernel,
        grid_spec=pltpu.PrefetchScalarGridSpec(
            num_scalar_prefetch=1, grid=(bsz, n_pages // pp),
            in_specs=[page_spec(k) for k in range(pp)],
            out_specs=pl.BlockSpec((1, chunks, CMP_STRIDE * half), lambda b, i, pt: (b, i, 0)),
            scratch_shapes=[pltpu.VMEM((half // 128, pp * PAGE_SIZE, 128), F32)]),
        out_shape=jax.ShapeDtypeStruct((bsz, n_pages * PAGE_SIZE // CMP_STRIDE, CMP_STRIDE * half), BF16),
        compiler_params=_cparams(2),
        name="gather_cmp",
    )(page_table, *([cache_t] * pp))


def _nsa_attn_sample_kernel(pt_ref, q_ref, gt_ref, kct_ref, vc_ref, new_ref, win_ref,
                            tc_ref, tnear_ref, tnew_ref, tw_ref, pool_ref, e_ref, rsum_ref, *refs,
                            t, n_past_blk):
    del pt_ref
    pp = SEL_PAGES_PER_STEP
    k_refs, v_refs = refs[:pp], refs[pp:2 * pp]
    o_ref, m_ref, l_ref, acc_ref, selb_ref, oc_ref, ow_ref = refs[2 * pp:]
    g, r_heads, hd = N_KV_HEADS, GQA_R, HEAD_DIM
    gw = g * hd
    i = pl.program_id(1)
    n_steps = pl.num_programs(1)
    nt = (((1,), (1,)), ((), ()))
    q = q_ref[0]
    tile = pp * PAGE_SIZE

    @pl.when(i == 0)
    def _():
        new = new_ref[0]
        s_c = jnp.dot(q, kct_ref[0], preferred_element_type=F32) + tc_ref[...]
        m_c = jnp.max(s_c, axis=1, keepdims=True)
        e_c = jnp.exp(s_c - m_c)
        inv_c = jnp.where(m_c > NEG_VALID, 1.0 / jnp.maximum(jnp.sum(e_c, axis=1, keepdims=True), 1e-30), 0.0)
        p_c = e_c * inv_c
        oc_ref[...] = jnp.dot(p_c.astype(BF16), vc_ref[0], preferred_element_type=F32)
        imp0 = jnp.zeros((q.shape[0], n_past_blk), F32)
        for part in _split3(p_c):
            imp0 = imp0 + jnp.dot(part, pool_ref[...], preferred_element_type=F32)
        imp = jnp.zeros((q.shape[0], n_past_blk), F32)
        for part in _split3(imp0):
            imp = imp + jnp.dot(rsum_ref[...], part, preferred_element_type=F32)
        blk = lax.broadcasted_iota(jnp.int32, (n_past_blk, q.shape[0]), 0)
        forced = (blk == 0) | (blk == n_past_blk - 1)
        sel = _top_blocks(jnp.where(forced, FORCE_SCORE, imp.T), blk, min(TOP_N - 1, n_past_blk), n_past_blk)
        selb_ref[...] = jnp.where(sel, 0.0, NEG).T.astype(BF16)
        win = win_ref[0, 0].astype(BF16)
        s_w = jnp.dot(q, win[:gw], preferred_element_type=F32) + tw_ref[...]
        s_wn = lax.dot_general(q, new[:, 4 * gw:5 * gw], nt, preferred_element_type=F32) + tnew_ref[...]
        m_w = jnp.maximum(jnp.max(s_w, axis=1, keepdims=True), jnp.max(s_wn, axis=1, keepdims=True))
        p_w = jnp.exp(s_w - m_w)
        p_wn = jnp.exp(s_wn - m_w)
        l_w = jnp.sum(p_w, axis=1, keepdims=True) + jnp.sum(p_wn, axis=1, keepdims=True)
        o_w = (lax.dot_general(p_w.astype(BF16), win[gw:], nt, preferred_element_type=F32)
               + jnp.dot(p_wn.astype(BF16), new[:, 5 * gw:6 * gw], preferred_element_type=F32))
        ow_ref[...] = o_w / l_w
        s_n = lax.dot_general(q, new[:, 2 * gw:3 * gw], nt, preferred_element_type=F32) + tnew_ref[...]
        m_n = jnp.max(s_n, axis=1, keepdims=True)
        p_n = jnp.exp(s_n - m_n)
        m_ref[...] = m_n
        l_ref[...] = jnp.sum(p_n, axis=1, keepdims=True)
        acc_ref[...] = jnp.dot(p_n.astype(BF16), new[:, 3 * gw:4 * gw], preferred_element_type=F32)

    k_t = jnp.concatenate([r[0, 0] for r in k_refs], axis=1).astype(BF16)
    v_t = jnp.concatenate([r[0, 0] for r in v_refs], axis=1).astype(BF16)
    s = (jnp.dot(q, k_t, preferred_element_type=F32)
         + jnp.dot(selb_ref[...], e_ref[:, pl.ds(pl.multiple_of(i * tile, tile), tile)], preferred_element_type=F32))
    s = s + jnp.where(i == n_steps - 1, tnear_ref[...], 0.0)
    m_old = m_ref[...]
    m_new = jnp.maximum(m_old, jnp.max(s, axis=1, keepdims=True))
    alpha = jnp.exp(m_old - m_new)
    p = jnp.exp(s - m_new)
    m_ref[...] = m_new
    l_ref[...] = alpha * l_ref[...] + jnp.sum(p, axis=1, keepdims=True)
    acc_ref[...] = alpha * acc_ref[...] + lax.dot_general(p.astype(BF16), v_t, nt, preferred_element_type=F32)

    @pl.when(i == n_steps - 1)
    def _():
        gt = gt_ref[0]
        o = gt[:, 0:1] * oc_ref[...] + gt[:, 1:2] * (acc_ref[...] / l_ref[...]) + gt[:, 2:3] * ow_ref[...]
        for gi in range(g):
            for r in range(r_heads):
                h = gi * r_heads + r
                o_ref[0, :, h * hd:(h + 1) * hd] = o[h * t:(h + 1) * t, gi * hd:(gi + 1) * hd].astype(BF16)


def _nsa_attn_sample(q_bd, gates, kct, vc, new_pad, win_t, layer, cache_t, page_table, tabs, t):
    bsz, n_pages = page_table.shape
    g, r_heads, hd = N_KV_HEADS, GQA_R, HEAD_DIM
    gw = g * hd
    lanes = g * r_heads * t
    pp = SEL_PAGES_PER_STEP
    n_past_blk = n_pages * PAGE_SIZE // SEL_LEN
    kern = functools.partial(_nsa_attn_sample_kernel, t=t, n_past_blk=n_past_blk)
    per_b = lambda a: pl.BlockSpec((1,) + a.shape[1:], lambda b, i, pt: (b, 0, 0))
    full = lambda a: pl.BlockSpec(a.shape, lambda b, i, pt: (0, 0))
    page_spec = lambda fblk, k: pl.BlockSpec((1, 1, gw, PAGE_SIZE),
                                             lambda b, i, pt: (layer, pt[b, i * pp + k], fblk, 0))
    return pl.pallas_call(
        kern,
        grid_spec=pltpu.PrefetchScalarGridSpec(
            num_scalar_prefetch=1, grid=(bsz, n_pages // pp),
            in_specs=[per_b(q_bd), per_b(gates), per_b(kct), per_b(vc), per_b(new_pad),
                      pl.BlockSpec((1, 1) + win_t.shape[2:], lambda b, i, pt: (layer, b, 0, 0))]
                     + [full(a) for a in tabs]
                     + [page_spec(2, k) for k in range(pp)] + [page_spec(3, k) for k in range(pp)],
            out_specs=pl.BlockSpec((1, t, g * r_heads * hd), lambda b, i, pt: (b, 0, 0)),
            scratch_shapes=[pltpu.VMEM((lanes, 1), F32), pltpu.VMEM((lanes, 1), F32), pltpu.VMEM((lanes, gw), F32),
                            pltpu.VMEM((lanes, n_past_blk), BF16), pltpu.VMEM((lanes, gw), F32),
                            pltpu.VMEM((lanes, gw), F32)]),
        out_shape=jax.ShapeDtypeStruct((bsz, t, g * r_heads * hd), BF16),
        compiler_params=_cparams(2),
        name="nsa_attn_sample",
    )(page_table, q_bd, gates, kct, vc, new_pad, win_t, *tabs, *([cache_t] * (2 * pp)))


def _sample_tables(rel_bias, pos0, t, nc):
    g, r_heads = N_KV_HEADS, GQA_R
    lanes = g * r_heads * t
    tq = pos0 + np.arange(t)[:, None]

    def table(kpos, extra_valid=True):
        dist = tq - kpos[None, :]
        tab = _bias_table(rel_bias, dist, (dist >= 0) & extra_valid)
        return tab.reshape(lanes, kpos.shape[0])

    tc = table(CMP_STRIDE * np.arange(nc) + CMP_LEN - 1)
    tnear = table(pos0 - Q_BLOCK + np.arange(Q_BLOCK))
    tnear = jnp.pad(tnear, ((0, 0), (SEL_PAGES_PER_STEP * PAGE_SIZE - Q_BLOCK, 0)))
    new_pos = pos0 + np.arange(128)
    tnew = table(new_pos, (new_pos < pos0 + t)[None, :])
    kw = pos0 - WINDOW + np.arange(WINDOW)
    tw = table(kw, (tq - kw[None, :]) <= WINDOW)
    n_past_blk = pos0 // SEL_LEN
    jb = np.arange(n_past_blk)[None, :]
    n = np.arange(nc)[:, None]
    pool = jnp.asarray(((n >= CMP_PER_SEL * jb - 1) & (n <= CMP_PER_SEL * jb + CMP_PER_SEL - 1)).astype(np.float32),
                       BF16)
    e = jnp.asarray((np.arange(pos0)[None, :] // SEL_LEN == np.arange(n_past_blk)[:, None]).astype(np.float32), BF16)
    li = np.arange(lanes)
    same = (li[:, None] // (r_heads * t) == li[None, :] // (r_heads * t)) & (li[:, None] % t == li[None, :] % t)
    rsum = jnp.asarray(same.astype(np.float32), BF16)
    return tc, tnear, tnew, tw, pool, e, rsum


def _nsa_sample(xs2d, bsz, t, layer, cache_t, page_table, win_t, wts, tabs):
    g, r_heads, hd = N_KV_HEADS, GQA_R, HEAD_DIM
    gw = g * hd
    q, kv, kvb, gt = _nsa_proj(xs2d, wts["wq"], wts["wkv"], wts["wg"], tm=xs2d.shape[0])
    chunks = _gather_cmp(cache_t, layer, page_table)
    kct, vc = _nsa_cmp(chunks, wts["w1"], wts["pe_term"], wts["w2k"].T, wts["w2v"], transposed=False)
    q5 = q.reshape(bsz, t, g, r_heads, hd)
    q_bd = jnp.einsum('btgrd,gh->bgrthd', q5, jnp.eye(g, dtype=q.dtype)).reshape(bsz, g * r_heads * t, gw)
    gates = jnp.pad(gt[:, :3 * g * r_heads].reshape(bsz, t, g * r_heads, 3).transpose(0, 2, 1, 3)
                    .reshape(bsz, g * r_heads * t, 3), ((0, 0), (0, 0), (0, 128 - 3)))
    new_pad = jnp.pad(kvb.reshape(bsz, t, 6 * gw), ((0, 0), (0, 128 - t), (0, 0)))
    o = _nsa_attn_sample(q_bd, gates, kct, vc, new_pad, win_t, layer, cache_t, page_table, tabs, t)
    return o.reshape(bsz * t, g * r_heads * hd), kv


def _nsa_weights(w_in, cmp_pe, cmp_w1, cmp_b1, cmp_w2):
    d = w_in.shape[0]
    g, r_heads, hd = N_KV_HEADS, GQA_R, HEAD_DIM
    qd, kvd = g * r_heads * hd, 6 * g * hd
    wq = (w_in[:, :qd] * hd ** -0.5).astype(BF16)
    wq_pad = jnp.pad(wq.reshape(d, g * r_heads, hd), ((0, 0), (0, 0), (0, 128 - hd))).reshape(d, g * r_heads * 128)
    wkv = w_in[:, qd:qd + kvd].astype(BF16)
    wg = jnp.pad(w_in[:, qd + kvd:], ((0, 0), (0, 128 - 3 * g * r_heads))).astype(BF16)
    w1r = cmp_w1.reshape(2, CMP_LEN, hd, CMP_HID)
    w1 = jnp.concatenate([w1r[:, :CMP_STRIDE], w1r[:, CMP_STRIDE:]], axis=-1).astype(BF16)
    w1 = w1.reshape(2, CMP_STRIDE // CMP_PACK, CMP_PACK * hd, 2 * CMP_HID)
    pe_term = jnp.einsum('csd,csdh->ch', cmp_pe, w1r) + cmp_b1
    return dict(wq=wq, wq_pad=wq_pad, wkv=wkv, wg=wg, w1=w1, pe_term=pe_term,
                w2k=cmp_w2[0].astype(BF16), w2v=cmp_w2[1].astype(BF16))


def _nsa_prompt(x2d, bsz, seq, wts, tables):
    g, r_heads, hd = N_KV_HEADS, GQA_R, HEAD_DIM
    m = bsz * seq
    nc, nb = seq // CMP_STRIDE, seq // SEL_LEN
    q_pad, kv, kv_chunks, gt = _nsa_proj(x2d, wts["wq_pad"], wts["wkv"], wts["wg"], tm=ROW_TILE, chunk_seq=seq)
    w2k_t = jnp.pad(wts["w2k"].T, ((0, 128 - hd), (0, 0)))
    w2v_p = jnp.pad(wts["w2v"], ((0, 0), (0, 128 - hd)))
    kct, vc = _nsa_cmp(kv_chunks, wts["w1"], wts["pe_term"], w2k_t, w2v_p, transposed=True)
    kst, kwt, vs, vw = _nsa_prep(kv.reshape(bsz, seq, kv.shape[1]), nb)
    gates_g = jnp.pad(gt[:, :3 * g * r_heads].reshape(m, g, 3 * r_heads).transpose(1, 0, 2),
                      ((0, 0), (0, 0), (0, 128 - 3 * r_heads)))
    o = _nsa_attn_prompt(q_pad, gates_g, kct, vc, kst, vs, kwt, vw, *tables, bsz, seq)
    return o, kv


def kernel(x_prompt, x_sample, cache_nsa, state_nsa_win, state_pool, state_ffn, page_table, p_prompt, p_sample,
           rel_bias, nsa_w_in, nsa_cmp_pe, nsa_cmp_w1, nsa_cmp_b1, nsa_cmp_w2, nsa_w_out, pool_w, pool_b,
           pool_scale, ffn_w_up, ffn_conv_w, ffn_conv_b, ffn_w_down, ln_g, ln_b, ple_w_proj, ple_w_gate,
           ple_b_gate):
    bsz, seq, d = x_prompt.shape
    db, dt, _ = x_sample.shape
    depth = ffn_w_up.shape[0]
    alpha = (2.0 * depth) ** 0.25
    g, hd = N_KV_HEADS, HEAD_DIM
    gw = g * hd
    past_len = page_table.shape[1] * PAGE_SIZE
    mp, ms = bsz * seq, db * dt
    assert g * GQA_R * dt == 128 and past_len % SEL_LEN == 0 and seq % PADL == 0 and seq % ROW_TILE == 0
    assert page_table.shape[1] % SEL_PAGES_PER_STEP == 0 and page_table.shape[1] % CMP_PAGES_PER_STEP == 0
    assert state_nsa_win.shape[2] == WINDOW and seq >= WINDOW and state_pool.shape[2] == POOL_HIST

    xp = x_prompt.reshape(mp, d)
    xs = x_sample.reshape(ms, d)
    tables_p = _prompt_tables(rel_bias, seq // SEL_LEN, seq // CMP_STRIDE)
    tables_s = _sample_tables(rel_bias, past_len, dt, past_len // CMP_STRIDE)
    cache_t = jnp.transpose(cache_nsa, (0, 1, 3, 4, 5, 2)).reshape(cache_nsa.shape[0], cache_nsa.shape[1], 4 * gw,
                                                                   PAGE_SIZE)
    win_t = jnp.transpose(state_nsa_win, (0, 1, 3, 4, 5, 2)).reshape(state_nsa_win.shape[0], db, 2 * gw, WINDOW)
    nsa_rp, nsa_rs, nsa_wp, nsa_ws, pool_hp, pool_hs, ffn_hp, ffn_hs = ([] for _ in range(8))
    for i in range(depth):
        j = i // 2
        g0, b0, g1, b1 = ln_g[i, 0][None], ln_b[i, 0][None], ln_g[i, 1][None], ln_b[i, 1][None]
        if i % 2 == 0:
            wts = _nsa_weights(nsa_w_in[j], nsa_cmp_pe[j], nsa_cmp_w1[j], nsa_cmp_b1[j], nsa_cmp_w2[j])
            o_p, kv_p = _nsa_prompt(xp, bsz, seq, wts, tables_p)
            o_s, kv_s = _nsa_sample(xs, db, dt, j, cache_t, page_table, win_t, wts, tables_s)
            w_out = nsa_w_out[j].astype(BF16)
            xp = _linear_ln(o_p, w_out, xp, g0, b0, alpha, tm=ROW_TILE)
            xs = _linear_ln(o_s, w_out, xs, g0, b0, alpha, tm=ms)
            kv_p3 = kv_p.reshape(bsz, seq, 6, g, hd)
            kv_s3 = kv_s.reshape(db, dt, 6, g, hd)
            nsa_rp.append(kv_p3[:, :, :4])
            nsa_rs.append(kv_s3[:, :, :4])
            nsa_wp.append(kv_p3[:, seq - WINDOW:, 4:])
            nsa_ws.append(jnp.concatenate([state_nsa_win[j][:, dt:], kv_s3[:, :, 4:]], axis=1))
        else:
            xp3 = xp.reshape(bsz, seq, d)
            xe = jnp.concatenate([state_pool[j], xs.reshape(db, dt, d)], axis=1)
            pool_hp.append(xp3[:, seq - POOL_HIST:])
            pool_hs.append(xe[:, xe.shape[1] - POOL_HIST:])
            pw = pool_w[j].astype(BF16)
            xp = _pool_ln_prompt(xp3, pw, pool_b[j][None], pool_scale[j][None], g0, b0, alpha).reshape(mp, d)
            xs = _pool_ln_sample(xe, dt, past_len, pw, pool_b[j][None], pool_scale[j][None], g0, b0,
                                 alpha).reshape(ms, d)
        w_up = ffn_w_up[i].astype(BF16)
        w_down = ffn_w_down[i].astype(BF16)
        cw, cb = ffn_conv_w[i], ffn_conv_b[i][None]
        h_p = _ffn_up_prompt(xp, w_up, cw, cb, seq)
        tail = xp.reshape(bsz, seq, d)[:, seq - 8:].reshape(bsz * 8, d)
        u_tail = _matmul(tail, w_up, tm=bsz * 8).reshape(bsz, 8, -1)
        ffn_hp.append(u_tail[:, 8 - (CONV_W - 1):])
        u_s = _matmul(xs, w_up, tm=ms).reshape(db, dt, -1)
        ue = jnp.concatenate([state_ffn[i], u_s], axis=1)
        ffn_hs.append(ue[:, ue.shape[1] - (CONV_W - 1):])
        h_s = _ffn_gate_sample(ue, cw, cb, dt).reshape(ms, -1)
        ple = (ple_w_gate[i].astype(BF16), ple_b_gate[i][None], ple_w_proj[i].astype(BF16))
        xp = _ffn_down_ln_ple(h_p, w_down, xp, g1, b1, p_prompt[i].reshape(mp, -1), *ple, alpha, ROW_TILE)
        xs = _ffn_down_ln_ple(h_s, w_down, xs, g1, b1, p_sample[i].reshape(ms, -1), *ple, alpha, ms)
    return (xp.reshape(bsz, seq, d), xs.reshape(db, dt, d),
            jnp.stack(nsa_rp), jnp.stack(nsa_rs), jnp.stack(nsa_wp), jnp.stack(nsa_ws),
            jnp.stack(pool_hp), jnp.stack(pool_hs), jnp.stack(ffn_hp), jnp.stack(ffn_hs))
```

```python
import functools
import math

import numpy as np
import jax
import jax.numpy as jnp
from jax import lax
from jax.experimental import pallas as pl
from jax.experimental.pallas import tpu as pltpu

N_KV_HEADS = 4
GQA_R = 4
HEAD_DIM = 64
CMP_LEN = 32
CMP_STRIDE = 16
CMP_HID = 256
CMP_PACK = 4
SEL_LEN = 64
CMP_PER_SEL = SEL_LEN // CMP_STRIDE
TOP_N = 16
WINDOW = 512
Q_BLOCK = 128
FORCE_SCORE = 1.0e4
REL_BUCKETS = 32
REL_MAX_DIST = 128
POOL_WINDOWS = (2, 4, 8, 16)
POOL_MAX = max(POOL_WINDOWS)
POOL_HIST = POOL_MAX - 1
CONV_W = 3
LN_EPS = 1e-5
PAGE_SIZE = 128

NEG = -1.0e30
NEG_VALID = -1.0e29
BF16 = jnp.bfloat16
F32 = jnp.float32

VMEM_LIMIT_BYTES = 56 * 1024 * 1024
ROW_TILE = 512


def _cparams(n_grid):
    return pltpu.CompilerParams(dimension_semantics=("arbitrary",) * n_grid,
                                vmem_limit_bytes=VMEM_LIMIT_BYTES)


def _layer_norm(y, g, b):
    mu = jnp.mean(y, axis=-1, keepdims=True)
    yc = y - mu
    var = jnp.mean(yc * yc, axis=-1, keepdims=True)
    return yc * lax.rsqrt(var + LN_EPS) * g + b


FFN_TN = 256
HALO = 16


def _ffn_up_prompt_kernel(x_ref, xh_ref, w_ref, cw_ref, cb_ref, h_ref, xe_ref, ue_ref, *,
                          tm, d_ff, tiles_per_batch):
    i = pl.program_id(0)
    first = (i % tiles_per_batch) == 0
    halo = xh_ref[...]
    xe_ref[0:HALO, :] = jnp.where(first, jnp.zeros_like(halo), halo).astype(BF16)
    xe_ref[HALO:HALO + tm, :] = x_ref[...].astype(BF16)
    xe = xe_ref[...]
    for j in range(d_ff // FFN_TN):
        c_parts = []
        for half in range(2):
            c0 = half * d_ff + j * FFN_TN
            ue_ref[...] = jnp.dot(xe, w_ref[:, c0:c0 + FFN_TN], preferred_element_type=F32)
            cw = cw_ref[:, c0:c0 + FFN_TN]
            c = cb_ref[:, c0:c0 + FFN_TN]
            for k in range(CONV_W):
                off = HALO - (CONV_W - 1) + k
                c = c + cw[k:k + 1, :] * ue_ref[off:off + tm, :]
            c_parts.append(c)
        h_ref[:, j * FFN_TN:(j + 1) * FFN_TN] = (jax.nn.gelu(c_parts[1]) * c_parts[0]).astype(BF16)


def _ffn_up_prompt(x, w_up, conv_w, conv_b, seq, tm=ROW_TILE):
    m, d = x.shape
    d_ff = w_up.shape[1] // 2
    tpb = seq // tm
    kern = functools.partial(_ffn_up_prompt_kernel, tm=tm, d_ff=d_ff, tiles_per_batch=tpb)
    return pl.pallas_call(
        kern,
        grid=(m // tm,),
        in_specs=[
            pl.BlockSpec((tm, d), lambda i: (i, 0)),
            pl.BlockSpec((HALO, d), lambda i: (jnp.maximum(i * (tm // HALO) - 1, 0), 0)),
            pl.BlockSpec(w_up.shape, lambda i: (0, 0)),
            pl.BlockSpec(conv_w.shape, lambda i: (0, 0)),
            pl.BlockSpec(conv_b.shape, lambda i: (0, 0)),
        ],
        out_specs=pl.BlockSpec((tm, d_ff), lambda i: (i, 0)),
        out_shape=jax.ShapeDtypeStruct((m, d_ff), BF16),
        scratch_shapes=[pltpu.VMEM((HALO + tm, d), BF16), pltpu.VMEM((HALO + tm, FFN_TN), F32)],
        compiler_params=_cparams(1),
        name="ffn_up_prompt",
    )(x, x, w_up, conv_w, conv_b)


def _ffn_gate_sample_kernel(ue_ref, cw_ref, cb_ref, h_ref, *, t, d_ff):
    for j in range(d_ff // FFN_TN):
        c_parts = []
        for half in range(2):
            c0 = half * d_ff + j * FFN_TN
            cw = cw_ref[:, c0:c0 + FFN_TN]
            c = cb_ref[:, c0:c0 + FFN_TN][None]
            for k in range(CONV_W):
                c = c + cw[k:k + 1, :][None] * ue_ref[:, k:k + t, c0:c0 + FFN_TN]
            c_parts.append(c)
        h_ref[:, :, j * FFN_TN:(j + 1) * FFN_TN] = (jax.nn.gelu(c_parts[1]) * c_parts[0]).astype(BF16)


def _ffn_gate_sample(ue, conv_w, conv_b, t):
    b, te, n2 = ue.shape
    d_ff = n2 // 2
    bb = 8
    kern = functools.partial(_ffn_gate_sample_kernel, t=t, d_ff=d_ff)
    return pl.pallas_call(
        kern,
        grid=(b // bb,),
        in_specs=[
            pl.BlockSpec((bb, te, n2), lambda i: (i, 0, 0)),
            pl.BlockSpec(conv_w.shape, lambda i: (0, 0)),
            pl.BlockSpec(conv_b.shape, lambda i: (0, 0)),
        ],
        out_specs=pl.BlockSpec((bb, t, d_ff), lambda i: (i, 0, 0)),
        out_shape=jax.ShapeDtypeStruct((b, t, d_ff), BF16),
        compiler_params=_cparams(1),
        name="ffn_gate_sample",
    )(ue, conv_w, conv_b)


def _matmul_kernel(x_ref, w_ref, o_ref):
    o_ref[...] = jnp.dot(x_ref[...].astype(BF16), w_ref[...], preferred_element_type=F32).astype(o_ref.dtype)


def _matmul(x, w, tm, tn=None, out_dtype=F32):
    m, k = x.shape
    n = w.shape[1]
    tn = n if tn is None else tn
    return pl.pallas_call(
        _matmul_kernel,
        grid=(m // tm, n // tn),
        in_specs=[pl.BlockSpec((tm, k), lambda i, j: (i, 0)),
                  pl.BlockSpec((k, tn), lambda i, j: (0, j))],
        out_specs=pl.BlockSpec((tm, tn), lambda i, j: (i, j)),
        out_shape=jax.ShapeDtypeStruct((m, n), out_dtype),
        compiler_params=_cparams(2),
        name="matmul",
    )(x, w)


def _linear_ln_kernel(a_ref, w_ref, r_ref, g_ref, b_ref, o_ref, *, alpha):
    f = jnp.dot(a_ref[...].astype(BF16), w_ref[...], preferred_element_type=F32)
    o_ref[...] = _layer_norm(alpha * r_ref[...] + f, g_ref[...], b_ref[...])


def _linear_ln(a, w, resid, g, b, alpha, tm):
    m, k = a.shape
    d = w.shape[1]
    kern = functools.partial(_linear_ln_kernel, alpha=alpha)
    return pl.pallas_call(
        kern,
        grid=(m // tm,),
        in_specs=[pl.BlockSpec((tm, k), lambda i: (i, 0)),
                  pl.BlockSpec((k, d), lambda i: (0, 0)),
                  pl.BlockSpec((tm, d), lambda i: (i, 0)),
                  pl.BlockSpec((1, d), lambda i: (0, 0)),
                  pl.BlockSpec((1, d), lambda i: (0, 0))],
        out_specs=pl.BlockSpec((tm, d), lambda i: (i, 0)),
        out_shape=jax.ShapeDtypeStruct((m, d), F32),
        compiler_params=_cparams(1),
        name="linear_ln",
    )(a, w, resid, g, b)


def _ffn_down_kernel(h_ref, wd_ref, r_ref, g_ref, b_ref, p_ref, wg_ref, bg_ref, wp_ref, o_ref, *, alpha):
    f = jnp.dot(h_ref[...], wd_ref[...], preferred_element_type=F32)
    x2 = _layer_norm(alpha * r_ref[...] + f, g_ref[...], b_ref[...])
    gate = jax.nn.sigmoid(jnp.dot(x2.astype(BF16), wg_ref[...], preferred_element_type=F32) + bg_ref[...])
    inj = jnp.dot(p_ref[...].astype(BF16), wp_ref[...], preferred_element_type=F32)
    o_ref[...] = x2 + gate * inj


def _ffn_down_ln_ple(h, w_down, resid, g, b, p, w_gate, b_gate, w_proj, alpha, tm):
    m, k = h.shape
    d = w_down.shape[1]
    pd = p.shape[1]
    kern = functools.partial(_ffn_down_kernel, alpha=alpha)
    full = lambda shape: pl.BlockSpec(shape, lambda i: (0, 0))
    return pl.pallas_call(
        kern,
        grid=(m // tm,),
        in_specs=[pl.BlockSpec((tm, k), lambda i: (i, 0)), full((k, d)),
                  pl.BlockSpec((tm, d), lambda i: (i, 0)), full((1, d)), full((1, d)),
                  pl.BlockSpec((tm, pd), lambda i: (i, 0)), full((d, d)), full((1, d)), full((pd, d))],
        out_specs=pl.BlockSpec((tm, d), lambda i: (i, 0)),
        out_shape=jax.ShapeDtypeStruct((m, d), F32),
        compiler_params=_cparams(1),
        name="ffn_down_ln_ple",
    )(h, w_down, resid, g, b, p, w_gate, b_gate, w_proj)


def _pool_finish(x, diffs, w_ref, pb_ref, ps_ref, g_ref, b_ref, alpha):
    ys = [jnp.dot(dg.astype(BF16), w_ref[gi], preferred_element_type=F32) for gi, dg in enumerate(diffs)]
    y = jnp.concatenate(ys, axis=-1) + pb_ref[...]
    return _layer_norm(alpha * x + ps_ref[...] * y, g_ref[...], b_ref[...])


def _pool_prompt_kernel(x_ref, xh_ref, w_ref, pb_ref, ps_ref, g_ref, b_ref, o_ref, xe_ref, *,
                        tt, gw, alpha):
    i = pl.program_id(1)
    halo = xh_ref[0]
    xe_ref[0:POOL_MAX, :] = jnp.where(i == 0, jnp.zeros_like(halo), halo)
    xe_ref[POOL_MAX:POOL_MAX + tt, :] = x_ref[0]
    pos = i * tt + lax.broadcasted_iota(jnp.int32, (tt, 1), 0)
    x = x_ref[0]
    diffs = []
    for gi, win in enumerate(POOL_WINDOWS):
        lo = gi * gw
        tot = x[:, lo:lo + gw]
        for k in range(1, win):
            tot = tot + xe_ref[POOL_MAX - k:POOL_MAX - k + tt, lo:lo + gw]
        cnt = jnp.minimum(win, pos + 1).astype(F32)
        diffs.append(tot / cnt - x[:, lo:lo + gw])
    o_ref[0] = _pool_finish(x, diffs, w_ref, pb_ref, ps_ref, g_ref, b_ref, alpha)


def _pool_ln_prompt(x, w, pb, ps, g, b, alpha, tt=ROW_TILE):
    bsz, t, d = x.shape
    gw = d // len(POOL_WINDOWS)
    kern = functools.partial(_pool_prompt_kernel, tt=tt, gw=gw, alpha=alpha)
    vec = pl.BlockSpec((1, d), lambda bi, i: (0, 0))
    return pl.pallas_call(
        kern,
        grid=(bsz, t // tt),
        in_specs=[pl.BlockSpec((1, tt, d), lambda bi, i: (bi, i, 0)),
                  pl.BlockSpec((1, POOL_MAX, d), lambda bi, i: (bi, jnp.maximum(i * (tt // POOL_MAX) - 1, 0), 0)),
                  pl.BlockSpec(w.shape, lambda bi, i: (0, 0, 0)), vec, vec, vec, vec],
        out_specs=pl.BlockSpec((1, tt, d), lambda bi, i: (bi, i, 0)),
        out_shape=jax.ShapeDtypeStruct((bsz, t, d), F32),
        scratch_shapes=[pltpu.VMEM((POOL_MAX + tt, d), F32)],
        compiler_params=_cparams(2),
        name="pool_ln_prompt",
    )(x, x, w, pb, ps, g, b)


def _pool_sample_kernel(xe_ref, w_ref, pb_ref, ps_ref, g_ref, b_ref, o_ref, *, t, nh, gw, pos0, alpha):
    bb = xe_ref.shape[0]
    d = xe_ref.shape[2]
    x = xe_ref[:, nh:nh + t, :].reshape(bb * t, d)
    cnt_np = np.minimum(np.array(POOL_WINDOWS)[:, None], pos0 + np.arange(t)[None, :] + 1).astype(np.float32)
    diffs = []
    for gi, win in enumerate(POOL_WINDOWS):
        lo = gi * gw
        tot = xe_ref[:, nh:nh + t, lo:lo + gw]
        for k in range(1, win):
            tot = tot + xe_ref[:, nh - k:nh - k + t, lo:lo + gw]
        if np.all(cnt_np[gi] == win):
            mean = tot / float(win)
        else:
            t_idx = lax.broadcasted_iota(jnp.int32, (1, t, 1), 1)
            mean = tot / jnp.minimum(win, pos0 + t_idx + 1).astype(F32)
        diffs.append(mean.reshape(bb * t, gw) - x[:, lo:lo + gw])
    o_ref[...] = _pool_finish(x, diffs, w_ref, pb_ref, ps_ref, g_ref, b_ref, alpha).reshape(bb, t, d)


def _pool_ln_sample(xe, t, pos0, w, pb, ps, g, b, alpha, bb=8):
    bsz, te, d = xe.shape
    gw = d // len(POOL_WINDOWS)
    kern = functools.partial(_pool_sample_kernel, t=t, nh=te - t, gw=gw, pos0=pos0, alpha=alpha)
    vec = pl.BlockSpec((1, d), lambda i: (0, 0))
    return pl.pallas_call(
        kern,
        grid=(bsz // bb,),
        in_specs=[pl.BlockSpec((bb, te, d), lambda i: (i, 0, 0)),
                  pl.BlockSpec(w.shape, lambda i: (0, 0, 0)), vec, vec, vec, vec],
        out_specs=pl.BlockSpec((bb, t, d), lambda i: (i, 0, 0)),
        out_shape=jax.ShapeDtypeStruct((bsz, t, d), F32),
        compiler_params=_cparams(1),
        name="pool_ln_sample",
    )(xe, w, pb, ps, g, b)


def _nsa_proj_kernel(x_ref, wq_ref, wkv_ref, wg_ref, q_ref, kv_ref, kvb_ref, gt_ref, *scratch):
    xb = x_ref[...].astype(BF16)
    q_ref[...] = jnp.dot(xb, wq_ref[...], preferred_element_type=F32).astype(BF16)
    kv = jnp.dot(xb, wkv_ref[...], preferred_element_type=F32)
    kv_ref[...] = kv
    gt_ref[...] = jax.nn.sigmoid(jnp.dot(xb, wg_ref[...], preferred_element_type=F32))
    if not scratch:
        kvb_ref[...] = kv.astype(BF16)
        return
    (t_ref,) = scratch
    n_fb, tm = t_ref.shape[0], t_ref.shape[1]
    for fb in range(n_fb):
        t_ref[fb] = kv[:, fb * 128:(fb + 1) * 128]
    for s in range(CMP_STRIDE):
        for fb in range(n_fb):
            kvb_ref[0, s, :, fb * 128:(fb + 1) * 128] = (
                t_ref[fb, pl.ds(s, tm // CMP_STRIDE, stride=CMP_STRIDE), :].astype(BF16))


def _nsa_proj(x, wq, wkv, wg, tm, chunk_seq=None):
    m, d = x.shape
    nq, nkv, ng = wq.shape[1], wkv.shape[1], wg.shape[1]
    full = lambda w: pl.BlockSpec(w.shape, lambda i: (0, 0))
    row = lambda n: pl.BlockSpec((tm, n), lambda i: (i, 0))
    if chunk_seq is None:
        kvb_spec, kvb_shape, scratch = row(nkv), jax.ShapeDtypeStruct((m, nkv), BF16), []
    else:
        feat = 2 * N_KV_HEADS * HEAD_DIM
        tpb = chunk_seq // tm
        kvb_spec = pl.BlockSpec((1, CMP_STRIDE, tm // CMP_STRIDE, feat), lambda i: (i // tpb, 0, i % tpb, 0))
        kvb_shape = jax.ShapeDtypeStruct((m // chunk_seq, CMP_STRIDE, chunk_seq // CMP_STRIDE, feat), BF16)
        scratch = [pltpu.VMEM((feat // 128, tm, 128), F32)]
    return pl.pallas_call(
        _nsa_proj_kernel,
        grid=(m // tm,),
        in_specs=[row(d), full(wq), full(wkv), full(wg)],
        out_specs=[row(nq), row(nkv), kvb_spec, row(ng)],
        out_shape=[jax.ShapeDtypeStruct((m, nq), BF16), jax.ShapeDtypeStruct((m, nkv), F32),
                   kvb_shape, jax.ShapeDtypeStruct((m, ng), F32)],
        scratch_shapes=scratch,
        compiler_params=_cparams(1),
        name="nsa_proj",
    )(x, wq, wkv, wg)


def _nsa_cmp_kernel(*refs, nc, transposed):
    ns = CMP_STRIDE
    x_refs = refs[:ns]
    w1_ref, pe_ref, w2k_ref, w2v_ref, ko_ref, vo_ref = refs[ns:]
    hd = HEAD_DIM
    gw = N_KV_HEADS * hd
    last = lax.broadcasted_iota(jnp.int32, (nc, CMP_HID), 0) == nc - 1
    nt = (((1,), (1,)), ((), ()))
    for c in range(2):
        for g in range(N_KV_HEADS):
            acc = jnp.zeros((nc, 2 * CMP_HID), F32)
            lo = c * gw + g * hd
            for k in range(ns // CMP_PACK):
                xs = jnp.concatenate([x_refs[k * CMP_PACK + i][:, lo:lo + hd] for i in range(CMP_PACK)], axis=1)
                acc = acc + jnp.dot(xs, w1_ref[c, k], preferred_element_type=F32)
            h1 = jnp.where(last, 0.0, pltpu.roll(acc[:, CMP_HID:], nc - 1, 0))
            hid = jax.nn.gelu(acc[:, :CMP_HID] + h1 + pe_ref[c:c + 1, :]).astype(BF16)
            if c == 0 and transposed:
                ko_ref[0, g] = lax.dot_general(w2k_ref[...], hid, nt, preferred_element_type=F32).astype(BF16)
            elif c == 0:
                ko_ref[0, g * hd:(g + 1) * hd, :] = lax.dot_general(w2k_ref[...], hid, nt,
                                                                    preferred_element_type=F32).astype(BF16)
            elif transposed:
                vo_ref[0, g] = jnp.dot(hid, w2v_ref[...], preferred_element_type=F32).astype(BF16)
            else:
                vo_ref[0, :, g * hd:(g + 1) * hd] = jnp.dot(hid, w2v_ref[...], preferred_element_type=F32).astype(BF16)


def _nsa_cmp(xv, w1, pe_term, w2k, w2v, transposed):
    g, hd = N_KV_HEADS, HEAD_DIM
    feat = 2 * g * hd
    bsz = xv.shape[0]
    if xv.ndim == 4:
        nc = xv.shape[2]
        x_specs = [pl.BlockSpec((None, None, nc, feat), functools.partial(lambda b, s: (b, s, 0, 0), s=s))
                   for s in range(CMP_STRIDE)]
    else:
        nc = xv.shape[1]
        x_specs = [pl.BlockSpec((None, nc, feat), functools.partial(lambda b, s: (b, 0, s), s=s))
                   for s in range(CMP_STRIDE)]
    kern = functools.partial(_nsa_cmp_kernel, nc=nc, transposed=transposed)
    full = lambda w: pl.BlockSpec(w.shape, lambda b: (0,) * w.ndim)
    if transposed:
        out_specs = [pl.BlockSpec((1, g, 128, nc), lambda b: (b, 0, 0, 0)),
                     pl.BlockSpec((1, g, nc, 128), lambda b: (b, 0, 0, 0))]
        out_shape = [jax.ShapeDtypeStruct((bsz, g, 128, nc), BF16), jax.ShapeDtypeStruct((bsz, g, nc, 128), BF16)]
    else:
        out_specs = [pl.BlockSpec((1, g * hd, nc), lambda b: (b, 0, 0)), pl.BlockSpec((1, nc, g * hd), lambda b: (b, 0, 0))]
        out_shape = [jax.ShapeDtypeStruct((bsz, g * hd, nc), BF16), jax.ShapeDtypeStruct((bsz, nc, g * hd), BF16)]
    return pl.pallas_call(
        kern,
        grid=(bsz,),
        in_specs=x_specs + [full(w1), full(pe_term), full(w2k), full(w2v)],
        out_specs=out_specs,
        out_shape=out_shape,
        compiler_params=_cparams(1),
        name="nsa_cmp",
    )(*([xv] * CMP_STRIDE), w1, pe_term, w2k, w2v)


PADL = 512
FAR_TILE = 512
ATAB_LANE = 80


def _nsa_prep_kernel(kv_ref, kst_ref, kwt_ref, vs_ref, vw_ref, *, nb):
    j = pl.program_id(1)
    tl = PADL
    hd = HEAD_DIM
    gw = N_KV_HEADS * hd

    @pl.when(j == 0)
    def _():
        for g in range(N_KV_HEADS):
            kst_ref[0, g] = (lax.broadcasted_iota(jnp.int32, (128 + nb, tl), 0) == hd).astype(BF16)
            kwt_ref[0, g] = (lax.broadcasted_iota(jnp.int32, (128, tl), 0) == hd).astype(BF16)
            vs_ref[0, g] = jnp.zeros((tl, 128), BF16)
            vw_ref[0, g] = jnp.zeros((tl, 128), BF16)

    @pl.when(j > 0)
    def _():
        kst = kv_ref[0, :, 2 * gw:3 * gw].T
        kwt = kv_ref[0, :, 4 * gw:5 * gw].T
        pos = (j - 1) * tl + lax.broadcasted_iota(jnp.int32, (nb, tl), 1)
        onehot = (pos // SEL_LEN == lax.broadcasted_iota(jnp.int32, (nb, tl), 0)).astype(BF16)
        zeros = jnp.zeros((128 - hd, tl), BF16)
        lane = lax.broadcasted_iota(jnp.int32, (tl, 128), 1)
        tail = (lane == hd).astype(F32)
        for g in range(N_KV_HEADS):
            kst_ref[0, g, 0:hd, :] = kst[g * hd:(g + 1) * hd].astype(BF16)
            kst_ref[0, g, hd:128, :] = zeros
            kst_ref[0, g, 128:128 + nb, :] = onehot
            kwt_ref[0, g, 0:hd, :] = kwt[g * hd:(g + 1) * hd].astype(BF16)
            kwt_ref[0, g, hd:128, :] = zeros
            for stream, ref in ((3, vs_ref), (5, vw_ref)):
                a = kv_ref[0, :, stream * gw + (g // 2) * 128:stream * gw + (g // 2 + 1) * 128]
                if g % 2:
                    a = pltpu.roll(a, hd, 1)
                ref[0, g] = jnp.where(lane < hd, a, tail).astype(BF16)


def _nsa_prep(kv, nb):
    bsz, t, n = kv.shape
    g = N_KV_HEADS
    tl = PADL
    tp = PADL + t
    kern = functools.partial(_nsa_prep_kernel, nb=nb)
    kspec = lambda rows: pl.BlockSpec((1, g, rows, tl), lambda b, j: (b, 0, 0, j))
    vspec = pl.BlockSpec((1, g, tl, 128), lambda b, j: (b, 0, j, 0))
    return pl.pallas_call(
        kern,
        grid=(bsz, tp // tl),
        in_specs=[pl.BlockSpec((1, tl, n), lambda b, j: (b, jnp.maximum(j - 1, 0), 0))],
        out_specs=[kspec(128 + nb), kspec(128), vspec, vspec],
        out_shape=[jax.ShapeDtypeStruct((bsz, g, 128 + nb, tp), BF16), jax.ShapeDtypeStruct((bsz, g, 128, tp), BF16),
                   jax.ShapeDtypeStruct((bsz, g, tp, 128), BF16), jax.ShapeDtypeStruct((bsz, g, tp, 128), BF16)],
        compiler_params=_cparams(2),
        name="nsa_prep",
    )(kv)


def _split3(x):
    a = x.astype(BF16)
    r = x - a.astype(F32)
    b = r.astype(BF16)
    return a, b, (r - b.astype(F32)).astype(BF16)


def _top_blocks(score, blk, n_sel, n_rows):
    sel = jnp.zeros(score.shape, jnp.bool_)
    for _ in range(n_sel):
        m = jnp.max(score, axis=0, keepdims=True)
        first = jnp.min(jnp.where(score == m, blk, n_rows), axis=0, keepdims=True)
        pick = blk == first
        sel = sel | pick
        score = jnp.where(pick, -3.0e38, score)
    return sel


def _nsa_attn_prompt_kernel(q_ref, gt_ref, kct_ref, vc_ref, kst_ref, vs_ref, kwt_ref, vw_ref,
                            wtab_ref, stab_ref, atab_ref, poolt_ref, o_ref,
                            kc_ref, oc_ref, selb_ref, sw_ref, sa_ref, sb_ref, pa_ref, pb_ref, ala_ref, alb_ref, m_ref, acc_ref, *, nb, nc):
    qi = pl.program_id(2)
    t0 = qi * Q_BLOCK
    r_heads = GQA_R
    hd = HEAD_DIM
    rows = r_heads * Q_BLOCK
    nw = WINDOW + Q_BLOCK
    lane = lax.broadcasted_iota(jnp.int32, (Q_BLOCK, 128), 1)
    q = jnp.concatenate([jnp.where(lane == hd, NEG, q_ref[:, r * 128:(r + 1) * 128]) for r in range(r_heads)],
                        axis=0)

    lane_r = lax.broadcasted_iota(jnp.int32, (rows, 128), 1)
    q_c = jnp.where(lane_r >= ATAB_LANE, atab_ref[0], q)

    def select_blocks(nc_v, nb_v):
        n_idx = lax.broadcasted_iota(jnp.int32, (48, nc_v), 1)
        j_idx = lax.broadcasted_iota(jnp.int32, (48, nc_v), 0)
        rel = n_idx - (qi * (Q_BLOCK // CMP_STRIDE) - 8)
        shift = (((j_idx < 32) & (rel == jnp.where(j_idx < 16, j_idx, j_idx - 16)))
                 | ((j_idx == 32) & (rel >= 16))).astype(F32).astype(BF16)
        kc_ref[0:ATAB_LANE, 0:nc_v] = kct_ref[0, 0, 0:ATAB_LANE, 0:nc_v]
        kc_ref[ATAB_LANE:128, 0:nc_v] = shift
        s_c = jnp.dot(q_c, kc_ref[:, 0:nc_v], preferred_element_type=F32)
        m_c = jnp.max(s_c, axis=1, keepdims=True)
        e_c = jnp.exp(s_c - m_c)
        inv_c = jnp.where(m_c > NEG_VALID, 1.0 / jnp.maximum(jnp.sum(e_c, axis=1, keepdims=True), 1e-30), 0.0)
        p_c = e_c * inv_c
        oc_ref[...] = jnp.dot(p_c.astype(BF16), vc_ref[0, 0, 0:nc_v, :], preferred_element_type=F32)

        grp = p_c[0:Q_BLOCK]
        for r in range(1, r_heads):
            grp = grp + p_c[r * Q_BLOCK:(r + 1) * Q_BLOCK]
        imp_t = jnp.zeros((nb_v, Q_BLOCK), F32)
        for part in _split3(grp):
            imp_t = imp_t + lax.dot_general(poolt_ref[0:nb_v, 0:nc_v], part, (((1,), (1,)), ((), ())),
                                            preferred_element_type=F32)
        blk = lax.broadcasted_iota(jnp.int32, (nb_v, Q_BLOCK), 0)
        tq = t0 + lax.broadcasted_iota(jnp.int32, (nb_v, Q_BLOCK), 1)
        cur = tq // SEL_LEN
        forced = (blk == 0) | (blk == cur) | (blk == cur - 1)
        score = jnp.where(blk * SEL_LEN > tq, -1.0, jnp.where(forced, -3.0e38, imp_t))
        sel = forced | _top_blocks(score, blk, min(TOP_N, nb) - 3, nb_v)
        bias_t = jnp.where(sel, 0.0, NEG)
        if nb_v < nb:
            bias_t = jnp.concatenate([bias_t, jnp.full((nb - nb_v, Q_BLOCK), NEG, F32)], axis=0)
        selb_ref[...] = bias_t.T.astype(BF16)

    first_half = (t0 + Q_BLOCK) * 2 <= nc * CMP_STRIDE

    @pl.when(first_half)
    def _():
        select_blocks(nc // 2, nb // 2)

    @pl.when(jnp.logical_not(first_half))
    def _():
        select_blocks(nc, nb)

    o_c = oc_ref[...]
    q_aug = jnp.concatenate([q, jnp.concatenate([selb_ref[...]] * r_heads, axis=0)], axis=1)

    c_w = pl.multiple_of(t0 + PADL - WINDOW, Q_BLOCK)
    sw_ref[...] = jnp.dot(q, kwt_ref[0, 0, :, pl.ds(c_w, nw)], preferred_element_type=F32) + wtab_ref[0]

    c_near = pl.multiple_of(t0 + PADL + Q_BLOCK - FAR_TILE, Q_BLOCK)
    n_far = jnp.maximum((c_near - PADL + FAR_TILE - 1) // FAR_TILE, 0)

    def tile_start(k):
        return pl.multiple_of(c_near - FAR_TILE * (jnp.clip(k, -1, n_far - 1) + 1), Q_BLOCK)

    def scores(k, s_out):
        s_out[...] = jnp.dot(q_aug, kst_ref[0, 0, :, pl.ds(tile_start(k), FAR_TILE)], preferred_element_type=F32)

    def pv_update(k, p_in, al_in):
        acc_ref[...] = al_in[...] * acc_ref[...] + jnp.dot(p_in[...], vs_ref[0, 0, pl.ds(tile_start(k), FAR_TILE), :],
                                                          preferred_element_type=F32)

    def stage(k, s_cur, s_nxt, p_cur, al_cur, p_prev, al_prev):
        scores(k + 1, s_nxt)
        pv_update(k - 1, p_prev, al_prev)
        m_old = m_ref[...]
        m_new = jnp.maximum(m_old, jnp.max(s_cur[...], axis=1, keepdims=True))
        al_cur[...] = jnp.exp(m_old - m_new)
        p_cur[...] = jnp.exp(s_cur[...] - m_new).astype(BF16)
        m_ref[...] = m_new

    s = jnp.dot(q_aug, kst_ref[0, 0, :, pl.ds(c_near, FAR_TILE)], preferred_element_type=F32) + stab_ref[0]
    scores(0, sa_ref)
    m_s = jnp.max(s, axis=1, keepdims=True)
    m_ref[...] = m_s
    pb_ref[...] = jnp.exp(s - m_s).astype(BF16)
    alb_ref[...] = jnp.ones_like(alb_ref)
    acc_ref[...] = jnp.zeros_like(acc_ref)

    n_full = jnp.maximum(n_far - 1, 0)

    def pair(j, _):
        stage(2 * j, sa_ref, sb_ref, pa_ref, ala_ref, pb_ref, alb_ref)

        @pl.when(2 * j + 1 < n_full)
        def _():
            stage(2 * j + 1, sb_ref, sa_ref, pb_ref, alb_ref, pa_ref, ala_ref)
        return 0

    lax.fori_loop(0, (n_full + 1) // 2, pair, 0)

    def last_stage(s_cur, p_cur, al_cur, p_prev, al_prev):
        pv_update(n_far - 2, p_prev, al_prev)
        m_old = m_ref[...]
        m_new = jnp.maximum(m_old, jnp.max(s_cur[...], axis=1, keepdims=True))
        al_cur[...] = jnp.exp(m_old - m_new)
        p_cur[...] = jnp.exp(s_cur[...] - m_new).astype(BF16)
        pv_update(n_far - 1, p_cur, al_cur)

    @pl.when(n_far == 0)
    def _():
        pv_update(-1, pb_ref, alb_ref)

    @pl.when((n_far > 0) & (n_far % 2 == 1))
    def _():
        last_stage(sa_ref, pa_ref, ala_ref, pb_ref, alb_ref)

    @pl.when((n_far > 0) & (n_far % 2 == 0))
    def _():
        last_stage(sb_ref, pb_ref, alb_ref, pa_ref, ala_ref)

    acc = acc_ref[...]
    o_s = acc[:, :hd] * (1.0 / acc[:, hd:hd + 1])

    s_w = sw_ref[...]
    m_w = jnp.max(s_w, axis=1, keepdims=True)
    acc_w = jnp.dot(jnp.exp(s_w - m_w).astype(BF16), vw_ref[0, 0, pl.ds(c_w, nw), :], preferred_element_type=F32)
    o_w = acc_w[:, :hd] * (1.0 / acc_w[:, hd:hd + 1])

    gt = gt_ref[0]
    for r in range(r_heads):
        sl = slice(r * Q_BLOCK, (r + 1) * Q_BLOCK)
        o = (gt[:, 3 * r:3 * r + 1] * o_c[sl, :hd] + gt[:, 3 * r + 1:3 * r + 2] * o_s[sl]
             + gt[:, 3 * r + 2:3 * r + 3] * o_w[sl])
        o_ref[:, r * hd:(r + 1) * hd] = o.astype(BF16)


def _nsa_attn_prompt(q_pad, gates_g, kct, vc, kst, vs, kwt, vw, wtab, stab, atab, poolt, bsz, seq):
    g, r_heads, hd = N_KV_HEADS, GQA_R, HEAD_DIM
    nq = seq // Q_BLOCK
    nb, nc = poolt.shape
    tp = PADL + seq
    kern = functools.partial(_nsa_attn_prompt_kernel, nb=nb, nc=nc)
    rows = r_heads * Q_BLOCK
    per_bg = lambda shape: pl.BlockSpec((1, 1) + shape, lambda b, gi, i: (b, gi, 0, 0))
    per_g = lambda shape: pl.BlockSpec((1,) + shape, lambda b, gi, i: (gi, 0, 0))
    return pl.pallas_call(
        kern,
        grid=(bsz, g, nq),
        in_specs=[pl.BlockSpec((Q_BLOCK, r_heads * 128), lambda b, gi, i: (b * nq + i, gi)),
                  pl.BlockSpec((1, Q_BLOCK, 128), lambda b, gi, i: (gi, b * nq + i, 0)),
                  per_bg((128, nc)), per_bg((nc, 128)),
                  per_bg((128 + nb, tp)), per_bg((tp, 128)),
                  per_bg((128, tp)), per_bg((tp, 128)),
                  per_g((r_heads * Q_BLOCK, WINDOW + Q_BLOCK)), per_g((r_heads * Q_BLOCK, FAR_TILE)),
                  per_g((r_heads * Q_BLOCK, 128)),
                  pl.BlockSpec((nb, nc), lambda b, gi, i: (0, 0))],
        out_specs=pl.BlockSpec((Q_BLOCK, r_heads * hd), lambda b, gi, i: (b * nq + i, gi)),
        out_shape=jax.ShapeDtypeStruct((bsz * seq, g * r_heads * hd), BF16),
        scratch_shapes=[pltpu.VMEM((128, nc), BF16), pltpu.VMEM((rows, 128), F32), pltpu.VMEM((Q_BLOCK, nb), BF16),
                        pltpu.VMEM((rows, WINDOW + Q_BLOCK), F32),
                        pltpu.VMEM((rows, FAR_TILE), F32), pltpu.VMEM((rows, FAR_TILE), F32),
                        pltpu.VMEM((rows, FAR_TILE), BF16), pltpu.VMEM((rows, FAR_TILE), BF16),
                        pltpu.VMEM((rows, 1), F32), pltpu.VMEM((rows, 1), F32),
                        pltpu.VMEM((rows, 1), F32), pltpu.VMEM((rows, 128), F32)],
        compiler_params=_cparams(3),
        name="nsa_attn_prompt",
    )(q_pad, gates_g, kct, vc, kst, vs, kwt, vw, wtab, stab, atab, poolt)


def _rel_bucket_np(dist):
    n = np.maximum(dist, 0)
    exact = REL_BUCKETS // 2
    nf = np.maximum(n, 1).astype(np.float32)
    large = exact + (np.log(nf / np.float32(exact)) / np.float32(math.log(REL_MAX_DIST / exact))
                     * np.float32(REL_BUCKETS - exact)).astype(np.int32)
    return np.where(n < exact, n, np.minimum(large, REL_BUCKETS - 1))


FAR_DIST = Q_BLOCK - CMP_STRIDE + 1
assert np.all(_rel_bucket_np(np.arange(FAR_DIST, 1 << 16)) == REL_BUCKETS - 1)


def _bias_table(rel_bias, dist, valid):
    relc = rel_bias.astype(F32) - rel_bias[REL_BUCKETS - 1].astype(F32)[None]
    onehot = np.eye(REL_BUCKETS, dtype=np.float32)[_rel_bucket_np(dist)]
    tab = jnp.einsum('...b,bh->...h', onehot, relc, precision=lax.Precision.HIGHEST)
    tab = jnp.where(valid[..., None], tab, NEG)
    return jnp.moveaxis(tab, -1, 0)


def _prompt_tables(rel_bias, nb, nc):
    g, r_heads = N_KV_HEADS, GQA_R
    qi = np.arange(Q_BLOCK)[:, None]
    nw = WINDOW + Q_BLOCK
    d_w = qi + WINDOW - np.arange(nw)[None, :]
    wtab = _bias_table(rel_bias, d_w, (d_w >= 0) & (d_w <= WINDOW)).reshape(g, r_heads * Q_BLOCK, nw)
    d_s = d_w[:, nw - FAR_TILE:]
    stab = _bias_table(rel_bias, d_s, d_s >= 0).reshape(g, r_heads * Q_BLOCK, FAR_TILE)
    d_c = qi - CMP_STRIDE * (np.arange(16)[None, :] - 8) - (CMP_LEN - 1)
    valid_c = d_c >= 0
    a = _bias_table(rel_bias, d_c, valid_c).reshape(g, r_heads * Q_BLOCK, 16)
    hi = a.astype(BF16)
    lo = jnp.where(jnp.asarray(np.tile(valid_c, (r_heads, 1)))[None], a - hi.astype(F32), 0.0).astype(BF16)
    fut = jnp.full((g, r_heads * Q_BLOCK, 1), NEG, BF16)
    atab = jnp.concatenate([jnp.zeros((g, r_heads * Q_BLOCK, ATAB_LANE), BF16), hi, lo, fut,
                            jnp.zeros((g, r_heads * Q_BLOCK, 15), BF16)], axis=-1)
    jb = np.arange(nb)[:, None]
    n = np.arange(nc)[None, :]
    poolt = jnp.asarray(((n >= CMP_PER_SEL * jb - 1) & (n <= CMP_PER_SEL * jb + CMP_PER_SEL - 1)).astype(np.float32),
                        BF16)
    return wtab, stab, atab, poolt


CMP_PAGES_PER_STEP = 16
SEL_PAGES_PER_STEP = 16


def _gather_cmp_kernel(pt_ref, *refs):
    del pt_ref
    page_refs = refs[:CMP_PAGES_PER_STEP]
    o_ref, t_ref = refs[CMP_PAGES_PER_STEP:]
    n_fb = t_ref.shape[0]
    for k in range(CMP_PAGES_PER_STEP):
        for fb in range(n_fb):
            t_ref[fb, k * PAGE_SIZE:(k + 1) * PAGE_SIZE, :] = page_refs[k][0, 0, fb * 128:(fb + 1) * 128, :].T
    rows = CMP_PAGES_PER_STEP * PAGE_SIZE // CMP_STRIDE
    for s in range(CMP_STRIDE):
        for fb in range(n_fb):
            c0 = (s * n_fb + fb) * 128
            o_ref[0, :, c0:c0 + 128] = t_ref[fb, pl.ds(s, rows, stride=CMP_STRIDE), :].astype(BF16)


def _gather_cmp(cache_t, layer, page_table):
    bsz, n_pages = page_table.shape
    half = cache_t.shape[2] // 2
    pp = CMP_PAGES_PER_STEP
    chunks = pp * PAGE_SIZE // CMP_STRIDE
    page_spec = lambda k: pl.BlockSpec((1, 1, half, PAGE_SIZE), lambda b, i, pt: (layer, pt[b, i * pp + k], 0, 0))
    return pl.pallas_call(
        _gather_cmp_kernel,
        grid_spec=pltpu.PrefetchScalarGridSpec(
            num_scalar_prefetch=1, grid=(bsz, n_pages // pp),
            in_specs=[page_spec(k) for k in range(pp)],
            out_specs=pl.BlockSpec((1, chunks, CMP_STRIDE * half), lambda b, i, pt: (b, i, 0)),
            scratch_shapes=[pltpu.VMEM((half // 128, pp * PAGE_SIZE, 128), F32)]),
        out_shape=jax.ShapeDtypeStruct((bsz, n_pages * PAGE_SIZE // CMP_STRIDE, CMP_STRIDE * half), BF16),
        compiler_params=_cparams(2),
        name="gather_cmp",
    )(page_table, *([cache_t] * pp))


def _nsa_attn_sample_kernel(pt_ref, q_ref, gt_ref, kct_ref, vc_ref, new_ref, win_ref,
                            tc_ref, tnear_ref, tnew_ref, tw_ref, pool_ref, e_ref, rsum_ref, *refs,
                            t, n_past_blk):
    del pt_ref
    pp = SEL_PAGES_PER_STEP
    k_refs, v_refs = refs[:pp], refs[pp:2 * pp]
    o_ref, m_ref, l_ref, acc_ref, selb_ref, oc_ref, ow_ref = refs[2 * pp:]
    g, r_heads, hd = N_KV_HEADS, GQA_R, HEAD_DIM
    gw = g * hd
    i = pl.program_id(1)
    n_steps = pl.num_programs(1)
    nt = (((1,), (1,)), ((), ()))
    q = q_ref[0]
    tile = pp * PAGE_SIZE

    @pl.when(i == 0)
    def _():
        new = new_ref[0]
        s_c = jnp.dot(q, kct_ref[0], preferred_element_type=F32) + tc_ref[...]
        m_c = jnp.max(s_c, axis=1, keepdims=True)
        e_c = jnp.exp(s_c - m_c)
        inv_c = jnp.where(m_c > NEG_VALID, 1.0 / jnp.maximum(jnp.sum(e_c, axis=1, keepdims=True), 1e-30), 0.0)
        p_c = e_c * inv_c
        oc_ref[...] = jnp.dot(p_c.astype(BF16), vc_ref[0], preferred_element_type=F32)
        imp0 = jnp.zeros((q.shape[0], n_past_blk), F32)
        for part in _split3(p_c):
            imp0 = imp0 + jnp.dot(part, pool_ref[...], preferred_element_type=F32)
        imp = jnp.zeros((q.shape[0], n_past_blk), F32)
        for part in _split3(imp0):
            imp = imp + jnp.dot(rsum_ref[...], part, preferred_element_type=F32)
        blk = lax.broadcasted_iota(jnp.int32, (n_past_blk, q.shape[0]), 0)
        forced = (blk == 0) | (blk == n_past_blk - 1)
        sel = _top_blocks(jnp.where(forced, FORCE_SCORE, imp.T), blk, min(TOP_N - 1, n_past_blk), n_past_blk)
        selb_ref[...] = jnp.where(sel, 0.0, NEG).T.astype(BF16)
        win = win_ref[0, 0].astype(BF16)
        s_w = jnp.dot(q, win[:gw], preferred_element_type=F32) + tw_ref[...]
        s_wn = lax.dot_general(q, new[:, 4 * gw:5 * gw], nt, preferred_element_type=F32) + tnew_ref[...]
        m_w = jnp.maximum(jnp.max(s_w, axis=1, keepdims=True), jnp.max(s_wn, axis=1, keepdims=True))
        p_w = jnp.exp(s_w - m_w)
        p_wn = jnp.exp(s_wn - m_w)
        l_w = jnp.sum(p_w, axis=1, keepdims=True) + jnp.sum(p_wn, axis=1, keepdims=True)
        o_w = (lax.dot_general(p_w.astype(BF16), win[gw:], nt, preferred_element_type=F32)
               + jnp.dot(p_wn.astype(BF16), new[:, 5 * gw:6 * gw], preferred_element_type=F32))
        ow_ref[...] = o_w / l_w
        s_n = lax.dot_general(q, new[:, 2 * gw:3 * gw], nt, preferred_element_type=F32) + tnew_ref[...]
        m_n = jnp.max(s_n, axis=1, keepdims=True)
        p_n = jnp.exp(s_n - m_n)
        m_ref[...] = m_n
        l_ref[...] = jnp.sum(p_n, axis=1, keepdims=True)
        acc_ref[...] = jnp.dot(p_n.astype(BF16), new[:, 3 * gw:4 * gw], preferred_element_type=F32)

    k_t = jnp.concatenate([r[0, 0] for r in k_refs], axis=1).astype(BF16)
    v_t = jnp.concatenate([r[0, 0] for r in v_refs], axis=1).astype(BF16)
    s = (jnp.dot(q, k_t, preferred_element_type=F32)
         + jnp.dot(selb_ref[...], e_ref[:, pl.ds(pl.multiple_of(i * tile, tile), tile)], preferred_element_type=F32))
    s = s + jnp.where(i == n_steps - 1, tnear_ref[...], 0.0)
    m_old = m_ref[...]
    m_new = jnp.maximum(m_old, jnp.max(s, axis=1, keepdims=True))
    alpha = jnp.exp(m_old - m_new)
    p = jnp.exp(s - m_new)
    m_ref[...] = m_new
    l_ref[...] = alpha * l_ref[...] + jnp.sum(p, axis=1, keepdims=True)
    acc_ref[...] = alpha * acc_ref[...] + lax.dot_general(p.astype(BF16), v_t, nt, preferred_element_type=F32)

    @pl.when(i == n_steps - 1)
    def _():
        gt = gt_ref[0]
        o = gt[:, 0:1] * oc_ref[...] + gt[:, 1:2] * (acc_ref[...] / l_ref[...]) + gt[:, 2:3] * ow_ref[...]
        for gi in range(g):
            for r in range(r_heads):
                h = gi * r_heads + r
                o_ref[0, :, h * hd:(h + 1) * hd] = o[h * t:(h + 1) * t, gi * hd:(gi + 1) * hd].astype(BF16)


def _nsa_attn_sample(q_bd, gates, kct, vc, new_pad, win_t, layer, cache_t, page_table, tabs, t):
    bsz, n_pages = page_table.shape
    g, r_heads, hd = N_KV_HEADS, GQA_R, HEAD_DIM
    gw = g * hd
    lanes = g * r_heads * t
    pp = SEL_PAGES_PER_STEP
    n_past_blk = n_pages * PAGE_SIZE // SEL_LEN
    kern = functools.partial(_nsa_attn_sample_kernel, t=t, n_past_blk=n_past_blk)
    per_b = lambda a: pl.BlockSpec((1,) + a.shape[1:], lambda b, i, pt: (b, 0, 0))
    full = lambda a: pl.BlockSpec(a.shape, lambda b, i, pt: (0, 0))
    page_spec = lambda fblk, k: pl.BlockSpec((1, 1, gw, PAGE_SIZE),
                                             lambda b, i, pt: (layer, pt[b, i * pp + k], fblk, 0))
    return pl.pallas_call(
        kern,
        grid_spec=pltpu.PrefetchScalarGridSpec(
            num_scalar_prefetch=1, grid=(bsz, n_pages // pp),
            in_specs=[per_b(q_bd), per_b(gates), per_b(kct), per_b(vc), per_b(new_pad),
                      pl.BlockSpec((1, 1) + win_t.shape[2:], lambda b, i, pt: (layer, b, 0, 0))]
                     + [full(a) for a in tabs]
                     + [page_spec(2, k) for k in range(pp)] + [page_spec(3, k) for k in range(pp)],
            out_specs=pl.BlockSpec((1, t, g * r_heads * hd), lambda b, i, pt: (b, 0, 0)),
            scratch_shapes=[pltpu.VMEM((lanes, 1), F32), pltpu.VMEM((lanes, 1), F32), pltpu.VMEM((lanes, gw), F32),
                            pltpu.VMEM((lanes, n_past_blk), BF16), pltpu.VMEM((lanes, gw), F32),
                            pltpu.VMEM((lanes, gw), F32)]),
        out_shape=jax.ShapeDtypeStruct((bsz, t, g * r_heads * hd), BF16),
        compiler_params=_cparams(2),
        name="nsa_attn_sample",
    )(page_table, q_bd, gates, kct, vc, new_pad, win_t, *tabs, *([cache_t] * (2 * pp)))


def _sample_tables(rel_bias, pos0, t, nc):
    g, r_heads = N_KV_HEADS, GQA_R
    lanes = g * r_heads * t
    tq = pos0 + np.arange(t)[:, None]

    def table(kpos, extra_valid=True):
        dist = tq - kpos[None, :]
        tab = _bias_table(rel_bias, dist, (dist >= 0) & extra_valid)
        return tab.reshape(lanes, kpos.shape[0])

    tc = table(CMP_STRIDE * np.arange(nc) + CMP_LEN - 1)
    tnear = table(pos0 - Q_BLOCK + np.arange(Q_BLOCK))
    tnear = jnp.pad(tnear, ((0, 0), (SEL_PAGES_PER_STEP * PAGE_SIZE - Q_BLOCK, 0)))
    new_pos = pos0 + np.arange(128)
    tnew = table(new_pos, (new_pos < pos0 + t)[None, :])
    kw = pos0 - WINDOW + np.arange(WINDOW)
    tw = table(kw, (tq - kw[None, :]) <= WINDOW)
    n_past_blk = pos0 // SEL_LEN
    jb = np.arange(n_past_blk)[None, :]
    n = np.arange(nc)[:, None]
    pool = jnp.asarray(((n >= CMP_PER_SEL * jb - 1) & (n <= CMP_PER_SEL * jb + CMP_PER_SEL - 1)).astype(np.float32),
                       BF16)
    e = jnp.asarray((np.arange(pos0)[None, :] // SEL_LEN == np.arange(n_past_blk)[:, None]).astype(np.float32), BF16)
    li = np.arange(lanes)
    same = (li[:, None] // (r_heads * t) == li[None, :] // (r_heads * t)) & (li[:, None] % t == li[None, :] % t)
    rsum = jnp.asarray(same.astype(np.float32), BF16)
    return tc, tnear, tnew, tw, pool, e, rsum


def _nsa_sample(xs2d, bsz, t, layer, cache_t, page_table, win_t, wts, tabs):
    g, r_heads, hd = N_KV_HEADS, GQA_R, HEAD_DIM
    gw = g * hd
    q, kv, kvb, gt = _nsa_proj(xs2d, wts["wq"], wts["wkv"], wts["wg"], tm=xs2d.shape[0])
    chunks = _gather_cmp(cache_t, layer, page_table)
    kct, vc = _nsa_cmp(chunks, wts["w1"], wts["pe_term"], wts["w2k"].T, wts["w2v"], transposed=False)
    q5 = q.reshape(bsz, t, g, r_heads, hd)
    q_bd = jnp.einsum('btgrd,gh->bgrthd', q5, jnp.eye(g, dtype=q.dtype)).reshape(bsz, g * r_heads * t, gw)
    gates = jnp.pad(gt[:, :3 * g * r_heads].reshape(bsz, t, g * r_heads, 3).transpose(0, 2, 1, 3)
                    .reshape(bsz, g * r_heads * t, 3), ((0, 0), (0, 0), (0, 128 - 3)))
    new_pad = jnp.pad(kvb.reshape(bsz, t, 6 * gw), ((0, 0), (0, 128 - t), (0, 0)))
    o = _nsa_attn_sample(q_bd, gates, kct, vc, new_pad, win_t, layer, cache_t, page_table, tabs, t)
    return o.reshape(bsz * t, g * r_heads * hd), kv


def _nsa_weights(w_in, cmp_pe, cmp_w1, cmp_b1, cmp_w2):
    d = w_in.shape[0]
    g, r_heads, hd = N_KV_HEADS, GQA_R, HEAD_DIM
    qd, kvd = g * r_heads * hd, 6 * g * hd
    wq = (w_in[:, :qd] * hd ** -0.5).astype(BF16)
    wq_pad = jnp.pad(wq.reshape(d, g * r_heads, hd), ((0, 0), (0, 0), (0, 128 - hd))).reshape(d, g * r_heads * 128)
    wkv = w_in[:, qd:qd + kvd].astype(BF16)
    wg = jnp.pad(w_in[:, qd + kvd:], ((0, 0), (0, 128 - 3 * g * r_heads))).astype(BF16)
    w1r = cmp_w1.reshape(2, CMP_LEN, hd, CMP_HID)
    w1 = jnp.concatenate([w1r[:, :CMP_STRIDE], w1r[:, CMP_STRIDE:]], axis=-1).astype(BF16)
    w1 = w1.reshape(2, CMP_STRIDE // CMP_PACK, CMP_PACK * hd, 2 * CMP_HID)
    pe_term = jnp.einsum('csd,csdh->ch', cmp_pe, w1r) + cmp_b1
    return dict(wq=wq, wq_pad=wq_pad, wkv=wkv, wg=wg, w1=w1, pe_term=pe_term,
                w2k=cmp_w2[0].astype(BF16), w2v=cmp_w2[1].astype(BF16))


def _nsa_prompt(x2d, bsz, seq, wts, tables):
    g, r_heads, hd = N_KV_HEADS, GQA_R, HEAD_DIM
    m = bsz * seq
    nc, nb = seq // CMP_STRIDE, seq // SEL_LEN
    q_pad, kv, kv_chunks, gt = _nsa_proj(x2d, wts["wq_pad"], wts["wkv"], wts["wg"], tm=ROW_TILE, chunk_seq=seq)
    w2k_t = jnp.pad(wts["w2k"].T, ((0, 128 - hd), (0, 0)))
    w2v_p = jnp.pad(wts["w2v"], ((0, 0), (0, 128 - hd)))
    kct, vc = _nsa_cmp(kv_chunks, wts["w1"], wts["pe_term"], w2k_t, w2v_p, transposed=True)
    kst, kwt, vs, vw = _nsa_prep(kv.reshape(bsz, seq, kv.shape[1]), nb)
    gates_g = jnp.pad(gt[:, :3 * g * r_heads].reshape(m, g, 3 * r_heads).transpose(1, 0, 2),
                      ((0, 0), (0, 0), (0, 128 - 3 * r_heads)))
    o = _nsa_attn_prompt(q_pad, gates_g, kct, vc, kst, vs, kwt, vw, *tables, bsz, seq)
    return o, kv


def kernel(x_prompt, x_sample, cache_nsa, state_nsa_win, state_pool, state_ffn, page_table, p_prompt, p_sample,
           rel_bias, nsa_w_in, nsa_cmp_pe, nsa_cmp_w1, nsa_cmp_b1, nsa_cmp_w2, nsa_w_out, pool_w, pool_b,
           pool_scale, ffn_w_up, ffn_conv_w, ffn_conv_b, ffn_w_down, ln_g, ln_b, ple_w_proj, ple_w_gate,
           ple_b_gate):
    bsz, seq, d = x_prompt.shape
    db, dt, _ = x_sample.shape
    depth = ffn_w_up.shape[0]
    alpha = (2.0 * depth) ** 0.25
    g, hd = N_KV_HEADS, HEAD_DIM
    gw = g * hd
    past_len = page_table.shape[1] * PAGE_SIZE
    mp, ms = bsz * seq, db * dt
    assert g * GQA_R * dt == 128 and past_len % SEL_LEN == 0 and seq % PADL == 0 and seq % ROW_TILE == 0
    assert page_table.shape[1] % SEL_PAGES_PER_STEP == 0 and page_table.shape[1] % CMP_PAGES_PER_STEP == 0
    assert state_nsa_win.shape[2] == WINDOW and seq >= WINDOW and state_pool.shape[2] == POOL_HIST

    xp = x_prompt.reshape(mp, d)
    xs = x_sample.reshape(ms, d)
    tables_p = _prompt_tables(rel_bias, seq // SEL_LEN, seq // CMP_STRIDE)
    tables_s = _sample_tables(rel_bias, past_len, dt, past_len // CMP_STRIDE)
    cache_t = jnp.transpose(cache_nsa, (0, 1, 3, 4, 5, 2)).reshape(cache_nsa.shape[0], cache_nsa.shape[1], 4 * gw,
                                                                   PAGE_SIZE)
    win_t = jnp.transpose(state_nsa_win, (0, 1, 3, 4, 5, 2)).reshape(state_nsa_win.shape[0], db, 2 * gw, WINDOW)
    nsa_rp, nsa_rs, nsa_wp, nsa_ws, pool_hp, pool_hs, ffn_hp, ffn_hs = ([] for _ in range(8))
    for i in range(depth):
        j = i // 2
        g0, b0, g1, b1 = ln_g[i, 0][None], ln_b[i, 0][None], ln_g[i, 1][None], ln_b[i, 1][None]
        if i % 2 == 0:
            wts = _nsa_weights(nsa_w_in[j], nsa_cmp_pe[j], nsa_cmp_w1[j], nsa_cmp_b1[j], nsa_cmp_w2[j])
            o_p, kv_p = _nsa_prompt(xp, bsz, seq, wts, tables_p)
            o_s, kv_s = _nsa_sample(xs, db, dt, j, cache_t, page_table, win_t, wts, tables_s)
            w_out = nsa_w_out[j].astype(BF16)
            xp = _linear_ln(o_p, w_out, xp, g0, b0, alpha, tm=ROW_TILE)
            xs = _linear_ln(o_s, w_out, xs, g0, b0, alpha, tm=ms)
            kv_p3 = kv_p.reshape(bsz, seq, 6, g, hd)
            kv_s3 = kv_s.reshape(db, dt, 6, g, hd)
            nsa_rp.append(kv_p3[:, :, :4])
            nsa_rs.append(kv_s3[:, :, :4])
            nsa_wp.append(kv_p3[:, seq - WINDOW:, 4:])
            nsa_ws.append(jnp.concatenate([state_nsa_win[j][:, dt:], kv_s3[:, :, 4:]], axis=1))
        else:
            xp3 = xp.reshape(bsz, seq, d)
            xe = jnp.concatenate([state_pool[j], xs.reshape(db, dt, d)], axis=1)
            pool_hp.append(xp3[:, seq - POOL_HIST:])
            pool_hs.append(xe[:, xe.shape[1] - POOL_HIST:])
            pw = pool_w[j].astype(BF16)
            xp = _pool_ln_prompt(xp3, pw, pool_b[j][None], pool_scale[j][None], g0, b0, alpha).reshape(mp, d)
            xs = _pool_ln_sample(xe, dt, past_len, pw, pool_b[j][None], pool_scale[j][None], g0, b0,
                                 alpha).reshape(ms, d)
        w_up = ffn_w_up[i].astype(BF16)
        w_down = ffn_w_down[i].astype(BF16)
        cw, cb = ffn_conv_w[i], ffn_conv_b[i][None]
        h_p = _ffn_up_prompt(xp, w_up, cw, cb, seq)
        tail = xp.reshape(bsz, seq, d)[:, seq - 8:].reshape(bsz * 8, d)
        u_tail = _matmul(tail, w_up, tm=bsz * 8).reshape(bsz, 8, -1)
        ffn_hp.append(u_tail[:, 8 - (CONV_W - 1):])
        u_s = _matmul(xs, w_up, tm=ms).reshape(db, dt, -1)
        ue = jnp.concatenate([state_ffn[i], u_s], axis=1)
        ffn_hs.append(ue[:, ue.shape[1] - (CONV_W - 1):])
        h_s = _ffn_gate_sample(ue, cw, cb, dt).reshape(ms, -1)
        ple = (ple_w_gate[i].astype(BF16), ple_b_gate[i][None], ple_w_proj[i].astype(BF16))
        xp = _ffn_down_ln_ple(h_p, w_down, xp, g1, b1, p_prompt[i].reshape(mp, -1), *ple, alpha, ROW_TILE)
        xs = _ffn_down_ln_ple(h_s, w_down, xs, g1, b1, p_sample[i].reshape(ms, -1), *ple, alpha, ms)
    return (xp.reshape(bsz, seq, d), xs.reshape(db, dt, d),
            jnp.stack(nsa_rp), jnp.stack(nsa_rs), jnp.stack(nsa_wp), jnp.stack(nsa_ws),
            jnp.stack(pool_hp), jnp.stack(pool_hs), jnp.stack(ffn_hp), jnp.stack(ffn_hs))
```

```python
import functools
import math

import numpy as np
import jax
import jax.numpy as jnp
from jax import lax
from jax.experimental import pallas as pl
from jax.experimental.pallas import tpu as pltpu

N_KV_HEADS = 4
GQA_R = 4
HEAD_DIM = 64
CMP_LEN = 32
CMP_STRIDE = 16
CMP_HID = 256
CMP_PACK = 4
SEL_LEN = 64
CMP_PER_SEL = SEL_LEN // CMP_STRIDE
TOP_N = 16
WINDOW = 512
Q_BLOCK = 128
FORCE_SCORE = 1.0e4
REL_BUCKETS = 32
REL_MAX_DIST = 128
POOL_WINDOWS = (2, 4, 8, 16)
POOL_MAX = max(POOL_WINDOWS)
POOL_HIST = POOL_MAX - 1
CONV_W = 3
LN_EPS = 1e-5
PAGE_SIZE = 128

NEG = -1.0e30
NEG_VALID = -1.0e29
BF16 = jnp.bfloat16
F32 = jnp.float32

VMEM_LIMIT_BYTES = 56 * 1024 * 1024
ROW_TILE = 512


def _cparams(n_grid):
    return pltpu.CompilerParams(dimension_semantics=("arbitrary",) * n_grid,
                                vmem_limit_bytes=VMEM_LIMIT_BYTES)


def _layer_norm(y, g, b):
    mu = jnp.mean(y, axis=-1, keepdims=True)
    yc = y - mu
    var = jnp.mean(yc * yc, axis=-1, keepdims=True)
    return yc * lax.rsqrt(var + LN_EPS) * g + b


FFN_TN = 256
HALO = 16


def _ffn_up_prompt_kernel(x_ref, xh_ref, w_ref, cw_ref, cb_ref, h_ref, xe_ref, ue_ref, *,
                          tm, d_ff, tiles_per_batch):
    i = pl.program_id(0)
    first = (i % tiles_per_batch) == 0
    halo = xh_ref[...]
    xe_ref[0:HALO, :] = jnp.where(first, jnp.zeros_like(halo), halo).astype(BF16)
    xe_ref[HALO:HALO + tm, :] = x_ref[...].astype(BF16)
    xe = xe_ref[...]
    for j in range(d_ff // FFN_TN):
        c_parts = []
        for half in range(2):
            c0 = half * d_ff + j * FFN_TN
            ue_ref[...] = jnp.dot(xe, w_ref[:, c0:c0 + FFN_TN], preferred_element_type=F32)
            cw = cw_ref[:, c0:c0 + FFN_TN]
            c = cb_ref[:, c0:c0 + FFN_TN]
            for k in range(CONV_W):
                off = HALO - (CONV_W - 1) + k
                c = c + cw[k:k + 1, :] * ue_ref[off:off + tm, :]
            c_parts.append(c)
        h_ref[:, j * FFN_TN:(j + 1) * FFN_TN] = (jax.nn.gelu(c_parts[1]) * c_parts[0]).astype(BF16)


def _ffn_up_prompt(x, w_up, conv_w, conv_b, seq, tm=ROW_TILE):
    m, d = x.shape
    d_ff = w_up.shape[1] // 2
    tpb = seq // tm
    kern = functools.partial(_ffn_up_prompt_kernel, tm=tm, d_ff=d_ff, tiles_per_batch=tpb)
    return pl.pallas_call(
        kern,
        grid=(m // tm,),
        in_specs=[
            pl.BlockSpec((tm, d), lambda i: (i, 0)),
            pl.BlockSpec((HALO, d), lambda i: (jnp.maximum(i * (tm // HALO) - 1, 0), 0)),
            pl.BlockSpec(w_up.shape, lambda i: (0, 0)),
            pl.BlockSpec(conv_w.shape, lambda i: (0, 0)),
            pl.BlockSpec(conv_b.shape, lambda i: (0, 0)),
        ],
        out_specs=pl.BlockSpec((tm, d_ff), lambda i: (i, 0)),
        out_shape=jax.ShapeDtypeStruct((m, d_ff), BF16),
        scratch_shapes=[pltpu.VMEM((HALO + tm, d), BF16), pltpu.VMEM((HALO + tm, FFN_TN), F32)],
        compiler_params=_cparams(1),
        name="ffn_up_prompt",
    )(x, x, w_up, conv_w, conv_b)


def _ffn_gate_sample_kernel(ue_ref, cw_ref, cb_ref, h_ref, *, t, d_ff):
    for j in range(d_ff // FFN_TN):
        c_parts = []
        for half in range(2):
            c0 = half * d_ff + j * FFN_TN
            cw = cw_ref[:, c0:c0 + FFN_TN]
            c = cb_ref[:, c0:c0 + FFN_TN][None]
            for k in range(CONV_W):
                c = c + cw[k:k + 1, :][None] * ue_ref[:, k:k + t, c0:c0 + FFN_TN]
            c_parts.append(c)
        h_ref[:, :, j * FFN_TN:(j + 1) * FFN_TN] = (jax.nn.gelu(c_parts[1]) * c_parts[0]).astype(BF16)


def _ffn_gate_sample(ue, conv_w, conv_b, t):
    b, te, n2 = ue.shape
    d_ff = n2 // 2
    bb = 8
    kern = functools.partial(_ffn_gate_sample_kernel, t=t, d_ff=d_ff)
    return pl.pallas_call(
        kern,
        grid=(b // bb,),
        in_specs=[
            pl.BlockSpec((bb, te, n2), lambda i: (i, 0, 0)),
            pl.BlockSpec(conv_w.shape, lambda i: (0, 0)),
            pl.BlockSpec(conv_b.shape, lambda i: (0, 0)),
        ],
        out_specs=pl.BlockSpec((bb, t, d_ff), lambda i: (i, 0, 0)),
        out_shape=jax.ShapeDtypeStruct((b, t, d_ff), BF16),
        compiler_params=_cparams(1),
        name="ffn_gate_sample",
    )(ue, conv_w, conv_b)


def _matmul_kernel(x_ref, w_ref, o_ref):
    o_ref[...] = jnp.dot(x_ref[...].astype(BF16), w_ref[...], preferred_element_type=F32).astype(o_ref.dtype)


def _matmul(x, w, tm, tn=None, out_dtype=F32):
    m, k = x.shape
    n = w.shape[1]
    tn = n if tn is None else tn
    return pl.pallas_call(
        _matmul_kernel,
        grid=(m // tm, n // tn),
        in_specs=[pl.BlockSpec((tm, k), lambda i, j: (i, 0)),
                  pl.BlockSpec((k, tn), lambda i, j: (0, j))],
        out_specs=pl.BlockSpec((tm, tn), lambda i, j: (i, j)),
        out_shape=jax.ShapeDtypeStruct((m, n), out_dtype),
        compiler_params=_cparams(2),
        name="matmul",
    )(x, w)


def _linear_ln_kernel(a_ref, w_ref, r_ref, g_ref, b_ref, o_ref, *, alpha):
    f = jnp.dot(a_ref[...].astype(BF16), w_ref[...], preferred_element_type=F32)
    o_ref[...] = _layer_norm(alpha * r_ref[...] + f, g_ref[...], b_ref[...])


def _linear_ln(a, w, resid, g, b, alpha, tm):
    m, k = a.shape
    d = w.shape[1]
    kern = functools.partial(_linear_ln_kernel, alpha=alpha)
    return pl.pallas_call(
        kern,
        grid=(m // tm,),
        in_specs=[pl.BlockSpec((tm, k), lambda i: (i, 0)),
                  pl.BlockSpec((k, d), lambda i: (0, 0)),
                  pl.BlockSpec((tm, d), lambda i: (i, 0)),
                  pl.BlockSpec((1, d), lambda i: (0, 0)),
                  pl.BlockSpec((1, d), lambda i: (0, 0))],
        out_specs=pl.BlockSpec((tm, d), lambda i: (i, 0)),
        out_shape=jax.ShapeDtypeStruct((m, d), F32),
        compiler_params=_cparams(1),
        name="linear_ln",
    )(a, w, resid, g, b)


def _ffn_down_kernel(h_ref, wd_ref, r_ref, g_ref, b_ref, p_ref, wg_ref, bg_ref, wp_ref, o_ref, *, alpha):
    f = jnp.dot(h_ref[...], wd_ref[...], preferred_element_type=F32)
    x2 = _layer_norm(alpha * r_ref[...] + f, g_ref[...], b_ref[...])
    gate = jax.nn.sigmoid(jnp.dot(x2.astype(BF16), wg_ref[...], preferred_element_type=F32) + bg_ref[...])
    inj = jnp.dot(p_ref[...].astype(BF16), wp_ref[...], preferred_element_type=F32)
    o_ref[...] = x2 + gate * inj


def _ffn_down_ln_ple(h, w_down, resid, g, b, p, w_gate, b_gate, w_proj, alpha, tm):
    m, k = h.shape
    d = w_down.shape[1]
    pd = p.shape[1]
    kern = functools.partial(_ffn_down_kernel, alpha=alpha)
    full = lambda shape: pl.BlockSpec(shape, lambda i: (0, 0))
    return pl.pallas_call(
        kern,
        grid=(m // tm,),
        in_specs=[pl.BlockSpec((tm, k), lambda i: (i, 0)), full((k, d)),
                  pl.BlockSpec((tm, d), lambda i: (i, 0)), full((1, d)), full((1, d)),
                  pl.BlockSpec((tm, pd), lambda i: (i, 0)), full((d, d)), full((1, d)), full((pd, d))],
        out_specs=pl.BlockSpec((tm, d), lambda i: (i, 0)),
        out_shape=jax.ShapeDtypeStruct((m, d), F32),
        compiler_params=_cparams(1),
        name="ffn_down_ln_ple",
    )(h, w_down, resid, g, b, p, w_gate, b_gate, w_proj)


def _pool_finish(x, diffs, w_ref, pb_ref, ps_ref, g_ref, b_ref, alpha):
    ys = [jnp.dot(dg.astype(BF16), w_ref[gi], preferred_element_type=F32) for gi, dg in enumerate(diffs)]
    y = jnp.concatenate(ys, axis=-1) + pb_ref[...]
    return _layer_norm(alpha * x + ps_ref[...] * y, g_ref[...], b_ref[...])


def _pool_prompt_kernel(x_ref, xh_ref, w_ref, pb_ref, ps_ref, g_ref, b_ref, o_ref, xe_ref, *,
                        tt, gw, alpha):
    i = pl.program_id(1)
    halo = xh_ref[0]
    xe_ref[0:POOL_MAX, :] = jnp.where(i == 0, jnp.zeros_like(halo), halo)
    xe_ref[POOL_MAX:POOL_MAX + tt, :] = x_ref[0]
    pos = i * tt + lax.broadcasted_iota(jnp.int32, (tt, 1), 0)
    x = x_ref[0]
    diffs = []
    for gi, win in enumerate(POOL_WINDOWS):
        lo = gi * gw
        tot = x[:, lo:lo + gw]
        for k in range(1, win):
            tot = tot + xe_ref[POOL_MAX - k:POOL_MAX - k + tt, lo:lo + gw]
        cnt = jnp.minimum(win, pos + 1).astype(F32)
        diffs.append(tot / cnt - x[:, lo:lo + gw])
    o_ref[0] = _pool_finish(x, diffs, w_ref, pb_ref, ps_ref, g_ref, b_ref, alpha)


def _pool_ln_prompt(x, w, pb, ps, g, b, alpha, tt=ROW_TILE):
    bsz, t, d = x.shape
    gw = d // len(POOL_WINDOWS)
    kern = functools.partial(_pool_prompt_kernel, tt=tt, gw=gw, alpha=alpha)
    vec = pl.BlockSpec((1, d), lambda bi, i: (0, 0))
    return pl.pallas_call(
        kern,
        grid=(bsz, t // tt),
        in_specs=[pl.BlockSpec((1, tt, d), lambda bi, i: (bi, i, 0)),
                  pl.BlockSpec((1, POOL_MAX, d), lambda bi, i: (bi, jnp.maximum(i * (tt // POOL_MAX) - 1, 0), 0)),
                  pl.BlockSpec(w.shape, lambda bi, i: (0, 0, 0)), vec, vec, vec, vec],
        out_specs=pl.BlockSpec((1, tt, d), lambda bi, i: (bi, i, 0)),
        out_shape=jax.ShapeDtypeStruct((bsz, t, d), F32),
        scratch_shapes=[pltpu.VMEM((POOL_MAX + tt, d), F32)],
        compiler_params=_cparams(2),
        name="pool_ln_prompt",
    )(x, x, w, pb, ps, g, b)


def _pool_sample_kernel(xe_ref, w_ref, pb_ref, ps_ref, g_ref, b_ref, o_ref, *, t, nh, gw, pos0, alpha):
    bb = xe_ref.shape[0]
    d = xe_ref.shape[2]
    x = xe_ref[:, nh:nh + t, :].reshape(bb * t, d)
    cnt_np = np.minimum(np.array(POOL_WINDOWS)[:, None], pos0 + np.arange(t)[None, :] + 1).astype(np.float32)
    diffs = []
    for gi, win in enumerate(POOL_WINDOWS):
        lo = gi * gw
        tot = xe_ref[:, nh:nh + t, lo:lo + gw]
        for k in range(1, win):
            tot = tot + xe_ref[:, nh - k:nh - k + t, lo:lo + gw]
        if np.all(cnt_np[gi] == win):
            mean = tot / float(win)
        else:
            t_idx = lax.broadcasted_iota(jnp.int32, (1, t, 1), 1)
            mean = tot / jnp.minimum(win, pos0 + t_idx + 1).astype(F32)
        diffs.append(mean.reshape(bb * t, gw) - x[:, lo:lo + gw])
    o_ref[...] = _pool_finish(x, diffs, w_ref, pb_ref, ps_ref, g_ref, b_ref, alpha).reshape(bb, t, d)


def _pool_ln_sample(xe, t, pos0, w, pb, ps, g, b, alpha, bb=8):
    bsz, te, d = xe.shape
    gw = d // len(POOL_WINDOWS)
    kern = functools.partial(_pool_sample_kernel, t=t, nh=te - t, gw=gw, pos0=pos0, alpha=alpha)
    vec = pl.BlockSpec((1, d), lambda i: (0, 0))
    return pl.pallas_call(
        kern,
        grid=(bsz // bb,),
        in_specs=[pl.BlockSpec((bb, te, d), lambda i: (i, 0, 0)),
                  pl.BlockSpec(w.shape, lambda i: (0, 0, 0)), vec, vec, vec, vec],
        out_specs=pl.BlockSpec((bb, t, d), lambda i: (i, 0, 0)),
        out_shape=jax.ShapeDtypeStruct((bsz, t, d), F32),
        compiler_params=_cparams(1),
        name="pool_ln_sample",
    )(xe, w, pb, ps, g, b)


def _nsa_proj_kernel(x_ref, wq_ref, wkv_ref, wg_ref, q_ref, kv_ref, kvb_ref, gt_ref, *scratch):
    xb = x_ref[...].astype(BF16)
    q_ref[...] = jnp.dot(xb, wq_ref[...], preferred_element_type=F32).astype(BF16)
    kv = jnp.dot(xb, wkv_ref[...], preferred_element_type=F32)
    kv_ref[...] = kv
    gt_ref[...] = jax.nn.sigmoid(jnp.dot(xb, wg_ref[...], preferred_element_type=F32))
    if not scratch:
        kvb_ref[...] = kv.astype(BF16)
        return
    (t_ref,) = scratch
    n_fb, tm = t_ref.shape[0], t_ref.shape[1]
    for fb in range(n_fb):
        t_ref[fb] = kv[:, fb * 128:(fb + 1) * 128]
    for s in range(CMP_STRIDE):
        for fb in range(n_fb):
            kvb_ref[0, s, :, fb * 128:(fb + 1) * 128] = (
                t_ref[fb, pl.ds(s, tm // CMP_STRIDE, stride=CMP_STRIDE), :].astype(BF16))


def _nsa_proj(x, wq, wkv, wg, tm, chunk_seq=None):
    m, d = x.shape
    nq, nkv, ng = wq.shape[1], wkv.shape[1], wg.shape[1]
    full = lambda w: pl.BlockSpec(w.shape, lambda i: (0, 0))
    row = lambda n: pl.BlockSpec((tm, n), lambda i: (i, 0))
    if chunk_seq is None:
        kvb_spec, kvb_shape, scratch = row(nkv), jax.ShapeDtypeStruct((m, nkv), BF16), []
    else:
        feat = 2 * N_KV_HEADS * HEAD_DIM
        tpb = chunk_seq // tm
        kvb_spec = pl.BlockSpec((1, CMP_STRIDE, tm // CMP_STRIDE, feat), lambda i: (i // tpb, 0, i % tpb, 0))
        kvb_shape = jax.ShapeDtypeStruct((m // chunk_seq, CMP_STRIDE, chunk_seq // CMP_STRIDE, feat), BF16)
        scratch = [pltpu.VMEM((feat // 128, tm, 128), F32)]
    return pl.pallas_call(
        _nsa_proj_kernel,
        grid=(m // tm,),
        in_specs=[row(d), full(wq), full(wkv), full(wg)],
        out_specs=[row(nq), row(nkv), kvb_spec, row(ng)],
        out_shape=[jax.ShapeDtypeStruct((m, nq), BF16), jax.ShapeDtypeStruct((m, nkv), F32),
                   kvb_shape, jax.ShapeDtypeStruct((m, ng), F32)],
        scratch_shapes=scratch,
        compiler_params=_cparams(1),
        name="nsa_proj",
    )(x, wq, wkv, wg)


def _nsa_cmp_kernel(*refs, nc, transposed):
    ns = CMP_STRIDE
    x_refs = refs[:ns]
    w1_ref, pe_ref, w2k_ref, w2v_ref, ko_ref, vo_ref = refs[ns:]
    hd = HEAD_DIM
    gw = N_KV_HEADS * hd
    last = lax.broadcasted_iota(jnp.int32, (nc, CMP_HID), 0) == nc - 1
    nt = (((1,), (1,)), ((), ()))
    for c in range(2):
        for g in range(N_KV_HEADS):
            acc = jnp.zeros((nc, 2 * CMP_HID), F32)
            lo = c * gw + g * hd
            for k in range(ns // CMP_PACK):
                xs = jnp.concatenate([x_refs[k * CMP_PACK + i][:, lo:lo + hd] for i in range(CMP_PACK)], axis=1)
                acc = acc + jnp.dot(xs, w1_ref[c, k], preferred_element_type=F32)
            h1 = jnp.where(last, 0.0, pltpu.roll(acc[:, CMP_HID:], nc - 1, 0))
            hid = jax.nn.gelu(acc[:, :CMP_HID] + h1 + pe_ref[c:c + 1, :]).astype(BF16)
            if c == 0 and transposed:
                ko_ref[0, g] = lax.dot_general(w2k_ref[...], hid, nt, preferred_element_type=F32).astype(BF16)
            elif c == 0:
                ko_ref[0, g * hd:(g + 1) * hd, :] = lax.dot_general(w2k_ref[...], hid, nt,
                                                                    preferred_element_type=F32).astype(BF16)
            elif transposed:
                vo_ref[0, g] = jnp.dot(hid, w2v_ref[...], preferred_element_type=F32).astype(BF16)
            else:
                vo_ref[0, :, g * hd:(g + 1) * hd] = jnp.dot(hid, w2v_ref[...], preferred_element_type=F32).astype(BF16)


def _nsa_cmp(xv, w1, pe_term, w2k, w2v, transposed):
    g, hd = N_KV_HEADS, HEAD_DIM
    feat = 2 * g * hd
    bsz = xv.shape[0]
    if xv.ndim == 4:
        nc = xv.shape[2]
        x_specs = [pl.BlockSpec((None, None, nc, feat), functools.partial(lambda b, s: (b, s, 0, 0), s=s))
                   for s in range(CMP_STRIDE)]
    else:
        nc = xv.shape[1]
        x_specs = [pl.BlockSpec((None, nc, feat), functools.partial(lambda b, s: (b, 0, s), s=s))
                   for s in range(CMP_STRIDE)]
    kern = functools.partial(_nsa_cmp_kernel, nc=nc, transposed=transposed)
    full = lambda w: pl.BlockSpec(w.shape, lambda b: (0,) * w.ndim)
    if transposed:
        out_specs = [pl.BlockSpec((1, g, 128, nc), lambda b: (b, 0, 0, 0)),
                     pl.BlockSpec((1, g, nc, 128), lambda b: (b, 0, 0, 0))]
        out_shape = [jax.ShapeDtypeStruct((bsz, g, 128, nc), BF16), jax.ShapeDtypeStruct((bsz, g, nc, 128), BF16)]
    else:
        out_specs = [pl.BlockSpec((1, g * hd, nc), lambda b: (b, 0, 0)), pl.BlockSpec((1, nc, g * hd), lambda b: (b, 0, 0))]
        out_shape = [jax.ShapeDtypeStruct((bsz, g * hd, nc), BF16), jax.ShapeDtypeStruct((bsz, nc, g * hd), BF16)]
    return pl.pallas_call(
        kern,
        grid=(bsz,),
        in_specs=x_specs + [full(w1), full(pe_term), full(w2k), full(w2v)],
        out_specs=out_specs,
        out_shape=out_shape,
        compiler_params=_cparams(1),
        name="nsa_cmp",
    )(*([xv] * CMP_STRIDE), w1, pe_term, w2k, w2v)


PADL = 512
FAR_TILE = 512
ATAB_LANE = 80


def _nsa_prep_kernel(kv_ref, kst_ref, kwt_ref, vs_ref, vw_ref, *, nb):
    j = pl.program_id(1)
    tl = PADL
    hd = HEAD_DIM
    gw = N_KV_HEADS * hd

    @pl.when(j == 0)
    def _():
        for g in range(N_KV_HEADS):
            kst_ref[0, g] = (lax.broadcasted_iota(jnp.int32, (128 + nb, tl), 0) == hd).astype(BF16)
            kwt_ref[0, g] = (lax.broadcasted_iota(jnp.int32, (128, tl), 0) == hd).astype(BF16)
            vs_ref[0, g] = jnp.zeros((tl, 128), BF16)
            vw_ref[0, g] = jnp.zeros((tl, 128), BF16)

    @pl.when(j > 0)
    def _():
        kst = kv_ref[0, :, 2 * gw:3 * gw].T
        kwt = kv_ref[0, :, 4 * gw:5 * gw].T
        pos = (j - 1) * tl + lax.broadcasted_iota(jnp.int32, (nb, tl), 1)
        onehot = (pos // SEL_LEN == lax.broadcasted_iota(jnp.int32, (nb, tl), 0)).astype(BF16)
        zeros = jnp.zeros((128 - hd, tl), BF16)
        lane = lax.broadcasted_iota(jnp.int32, (tl, 128), 1)
        tail = (lane == hd).astype(F32)
        for g in range(N_KV_HEADS):
            kst_ref[0, g, 0:hd, :] = kst[g * hd:(g + 1) * hd].astype(BF16)
            kst_ref[0, g, hd:128, :] = zeros
            kst_ref[0, g, 128:128 + nb, :] = onehot
            kwt_ref[0, g, 0:hd, :] = kwt[g * hd:(g + 1) * hd].astype(BF16)
            kwt_ref[0, g, hd:128, :] = zeros
            for stream, ref in ((3, vs_ref), (5, vw_ref)):
                a = kv_ref[0, :, stream * gw + (g // 2) * 128:stream * gw + (g // 2 + 1) * 128]
                if g % 2:
                    a = pltpu.roll(a, hd, 1)
                ref[0, g] = jnp.where(lane < hd, a, tail).astype(BF16)


def _nsa_prep(kv, nb):
    bsz, t, n = kv.shape
    g = N_KV_HEADS
    tl = PADL
    tp = PADL + t
    kern = functools.partial(_nsa_prep_kernel, nb=nb)
    kspec = lambda rows: pl.BlockSpec((1, g, rows, tl), lambda b, j: (b, 0, 0, j))
    vspec = pl.BlockSpec((1, g, tl, 128), lambda b, j: (b, 0, j, 0))
    return pl.pallas_call(
        kern,
        grid=(bsz, tp // tl),
        in_specs=[pl.BlockSpec((1, tl, n), lambda b, j: (b, jnp.maximum(j - 1, 0), 0))],
        out_specs=[kspec(128 + nb), kspec(128), vspec, vspec],
        out_shape=[jax.ShapeDtypeStruct((bsz, g, 128 + nb, tp), BF16), jax.ShapeDtypeStruct((bsz, g, 128, tp), BF16),
                   jax.ShapeDtypeStruct((bsz, g, tp, 128), BF16), jax.ShapeDtypeStruct((bsz, g, tp, 128), BF16)],
        compiler_params=_cparams(2),
        name="nsa_prep",
    )(kv)


def _split3(x):
    a = x.astype(BF16)
    r = x - a.astype(F32)
    b = r.astype(BF16)
    return a, b, (r - b.astype(F32)).astype(BF16)


def _top_blocks(score, blk, n_sel, n_rows):
    sel = jnp.zeros(score.shape, jnp.bool_)
    for _ in range(n_sel):
        m = jnp.max(score, axis=0, keepdims=True)
        first = jnp.min(jnp.where(score == m, blk, n_rows), axis=0, keepdims=True)
        pick = blk == first
        sel = sel | pick
        score = jnp.where(pick, -3.0e38, score)
    return sel


def _nsa_attn_prompt_kernel(q_ref, gt_ref, kct_ref, vc_ref, kst_ref, vs_ref, kwt_ref, vw_ref,
                            wtab_ref, stab_ref, atab_ref, poolt_ref, o_ref,
                            kc_ref, sw_ref, sa_ref, sb_ref, pa_ref, pb_ref, ala_ref, alb_ref, m_ref, acc_ref, *, nb, nc):
    qi = pl.program_id(2)
    t0 = qi * Q_BLOCK
    r_heads = GQA_R
    hd = HEAD_DIM
    rows = r_heads * Q_BLOCK
    nw = WINDOW + Q_BLOCK
    lane = lax.broadcasted_iota(jnp.int32, (Q_BLOCK, 128), 1)
    q = jnp.concatenate([jnp.where(lane == hd, NEG, q_ref[:, r * 128:(r + 1) * 128]) for r in range(r_heads)],
                        axis=0)

    n_idx = lax.broadcasted_iota(jnp.int32, (48, nc), 1)
    j_idx = lax.broadcasted_iota(jnp.int32, (48, nc), 0)
    rel = n_idx - (qi * (Q_BLOCK // CMP_STRIDE) - 8)
    shift = (((j_idx < 32) & (rel == jnp.where(j_idx < 16, j_idx, j_idx - 16)))
             | ((j_idx == 32) & (rel >= 16))).astype(F32).astype(BF16)
    kc_ref[0:ATAB_LANE, :] = kct_ref[0, 0, 0:ATAB_LANE, :]
    kc_ref[ATAB_LANE:128, :] = shift
    lane_r = lax.broadcasted_iota(jnp.int32, (rows, 128), 1)
    s_c = jnp.dot(jnp.where(lane_r >= ATAB_LANE, atab_ref[0], q), kc_ref[...], preferred_element_type=F32)
    m_c = jnp.max(s_c, axis=1, keepdims=True)
    e_c = jnp.exp(s_c - m_c)
    inv_c = jnp.where(m_c > NEG_VALID, 1.0 / jnp.maximum(jnp.sum(e_c, axis=1, keepdims=True), 1e-30), 0.0)
    p_c = e_c * inv_c
    o_c = jnp.dot(p_c.astype(BF16), vc_ref[0, 0], preferred_element_type=F32)

    grp = p_c[0:Q_BLOCK]
    for r in range(1, r_heads):
        grp = grp + p_c[r * Q_BLOCK:(r + 1) * Q_BLOCK]
    imp_t = jnp.zeros((nb, Q_BLOCK), F32)
    for part in _split3(grp):
        imp_t = imp_t + lax.dot_general(poolt_ref[...], part, (((1,), (1,)), ((), ())),
                                        preferred_element_type=F32)
    blk = lax.broadcasted_iota(jnp.int32, (nb, Q_BLOCK), 0)
    tq = t0 + lax.broadcasted_iota(jnp.int32, (nb, Q_BLOCK), 1)
    cur = tq // SEL_LEN
    forced = (blk == 0) | (blk == cur) | (blk == cur - 1)
    score = jnp.where(blk * SEL_LEN > tq, -1.0, jnp.where(forced, -3.0e38, imp_t))
    sel = forced | _top_blocks(score, blk, min(TOP_N, nb) - 3, nb)
    selb = jnp.where(sel, 0.0, NEG).T.astype(BF16)
    q_aug = jnp.concatenate([q, jnp.concatenate([selb] * r_heads, axis=0)], axis=1)

    c_w = pl.multiple_of(t0 + PADL - WINDOW, Q_BLOCK)
    sw_ref[...] = jnp.dot(q, kwt_ref[0, 0, :, pl.ds(c_w, nw)], preferred_element_type=F32) + wtab_ref[0]

    c_near = pl.multiple_of(t0 + PADL + Q_BLOCK - FAR_TILE, Q_BLOCK)
    n_far = jnp.maximum((c_near - PADL + FAR_TILE - 1) // FAR_TILE, 0)

    def tile_start(k):
        return pl.multiple_of(c_near - FAR_TILE * (jnp.clip(k, -1, n_far - 1) + 1), Q_BLOCK)

    def scores(k, s_out):
        s_out[...] = jnp.dot(q_aug, kst_ref[0, 0, :, pl.ds(tile_start(k), FAR_TILE)], preferred_element_type=F32)

    def pv_update(k, p_in, al_in):
        acc_ref[...] = al_in[...] * acc_ref[...] + jnp.dot(p_in[...], vs_ref[0, 0, pl.ds(tile_start(k), FAR_TILE), :],
                                                          preferred_element_type=F32)

    def stage(k, s_cur, s_nxt, p_cur, al_cur, p_prev, al_prev):
        scores(k + 1, s_nxt)
        pv_update(k - 1, p_prev, al_prev)
        m_old = m_ref[...]
        m_new = jnp.maximum(m_old, jnp.max(s_cur[...], axis=1, keepdims=True))
        al_cur[...] = jnp.exp(m_old - m_new)
        p_cur[...] = jnp.exp(s_cur[...] - m_new).astype(BF16)
        m_ref[...] = m_new

    s = jnp.dot(q_aug, kst_ref[0, 0, :, pl.ds(c_near, FAR_TILE)], preferred_element_type=F32) + stab_ref[0]
    scores(0, sa_ref)
    m_s = jnp.max(s, axis=1, keepdims=True)
    m_ref[...] = m_s
    pb_ref[...] = jnp.exp(s - m_s).astype(BF16)
    alb_ref[...] = jnp.ones_like(alb_ref)
    acc_ref[...] = jnp.zeros_like(acc_ref)

    n_full = jnp.maximum(n_far - 1, 0)

    def pair(j, _):
        stage(2 * j, sa_ref, sb_ref, pa_ref, ala_ref, pb_ref, alb_ref)

        @pl.when(2 * j + 1 < n_full)
        def _():
            stage(2 * j + 1, sb_ref, sa_ref, pb_ref, alb_ref, pa_ref, ala_ref)
        return 0

    lax.fori_loop(0, (n_full + 1) // 2, pair, 0)

    def last_stage(s_cur, p_cur, al_cur, p_prev, al_prev):
        pv_update(n_far - 2, p_prev, al_prev)
        m_old = m_ref[...]
        m_new = jnp.maximum(m_old, jnp.max(s_cur[...], axis=1, keepdims=True))
        al_cur[...] = jnp.exp(m_old - m_new)
        p_cur[...] = jnp.exp(s_cur[...] - m_new).astype(BF16)
        pv_update(n_far - 1, p_cur, al_cur)

    @pl.when(n_far == 0)
    def _():
        pv_update(-1, pb_ref, alb_ref)

    @pl.when((n_far > 0) & (n_far % 2 == 1))
    def _():
        last_stage(sa_ref, pa_ref, ala_ref, pb_ref, alb_ref)

    @pl.when((n_far > 0) & (n_far % 2 == 0))
    def _():
        last_stage(sb_ref, pb_ref, alb_ref, pa_ref, ala_ref)

    acc = acc_ref[...]
    o_s = acc[:, :hd] * (1.0 / acc[:, hd:hd + 1])

    s_w = sw_ref[...]
    m_w = jnp.max(s_w, axis=1, keepdims=True)
    acc_w = jnp.dot(jnp.exp(s_w - m_w).astype(BF16), vw_ref[0, 0, pl.ds(c_w, nw), :], preferred_element_type=F32)
    o_w = acc_w[:, :hd] * (1.0 / acc_w[:, hd:hd + 1])

    gt = gt_ref[0]
    for r in range(r_heads):
        sl = slice(r * Q_BLOCK, (r + 1) * Q_BLOCK)
        o = (gt[:, 3 * r:3 * r + 1] * o_c[sl, :hd] + gt[:, 3 * r + 1:3 * r + 2] * o_s[sl]
             + gt[:, 3 * r + 2:3 * r + 3] * o_w[sl])
        o_ref[:, r * hd:(r + 1) * hd] = o.astype(BF16)


def _nsa_attn_prompt(q_pad, gates_g, kct, vc, kst, vs, kwt, vw, wtab, stab, atab, poolt, bsz, seq):
    g, r_heads, hd = N_KV_HEADS, GQA_R, HEAD_DIM
    nq = seq // Q_BLOCK
    nb, nc = poolt.shape
    tp = PADL + seq
    kern = functools.partial(_nsa_attn_prompt_kernel, nb=nb, nc=nc)
    rows = r_heads * Q_BLOCK
    per_bg = lambda shape: pl.BlockSpec((1, 1) + shape, lambda b, gi, i: (b, gi, 0, 0))
    per_g = lambda shape: pl.BlockSpec((1,) + shape, lambda b, gi, i: (gi, 0, 0))
    return pl.pallas_call(
        kern,
        grid=(bsz, g, nq),
        in_specs=[pl.BlockSpec((Q_BLOCK, r_heads * 128), lambda b, gi, i: (b * nq + i, gi)),
                  pl.BlockSpec((1, Q_BLOCK, 128), lambda b, gi, i: (gi, b * nq + i, 0)),
                  per_bg((128, nc)), per_bg((nc, 128)),
                  per_bg((128 + nb, tp)), per_bg((tp, 128)),
                  per_bg((128, tp)), per_bg((tp, 128)),
                  per_g((r_heads * Q_BLOCK, WINDOW + Q_BLOCK)), per_g((r_heads * Q_BLOCK, FAR_TILE)),
                  per_g((r_heads * Q_BLOCK, 128)),
                  pl.BlockSpec((nb, nc), lambda b, gi, i: (0, 0))],
        out_specs=pl.BlockSpec((Q_BLOCK, r_heads * hd), lambda b, gi, i: (b * nq + i, gi)),
        out_shape=jax.ShapeDtypeStruct((bsz * seq, g * r_heads * hd), BF16),
        scratch_shapes=[pltpu.VMEM((128, nc), BF16), pltpu.VMEM((rows, WINDOW + Q_BLOCK), F32),
                        pltpu.VMEM((rows, FAR_TILE), F32), pltpu.VMEM((rows, FAR_TILE), F32),
                        pltpu.VMEM((rows, FAR_TILE), BF16), pltpu.VMEM((rows, FAR_TILE), BF16),
                        pltpu.VMEM((rows, 1), F32), pltpu.VMEM((rows, 1), F32),
                        pltpu.VMEM((rows, 1), F32), pltpu.VMEM((rows, 128), F32)],
        compiler_params=_cparams(3),
        name="nsa_attn_prompt",
    )(q_pad, gates_g, kct, vc, kst, vs, kwt, vw, wtab, stab, atab, poolt)


def _rel_bucket_np(dist):
    n = np.maximum(dist, 0)
    exact = REL_BUCKETS // 2
    nf = np.maximum(n, 1).astype(np.float32)
    large = exact + (np.log(nf / np.float32(exact)) / np.float32(math.log(REL_MAX_DIST / exact))
                     * np.float32(REL_BUCKETS - exact)).astype(np.int32)
    return np.where(n < exact, n, np.minimum(large, REL_BUCKETS - 1))


FAR_DIST = Q_BLOCK - CMP_STRIDE + 1
assert np.all(_rel_bucket_np(np.arange(FAR_DIST, 1 << 16)) == REL_BUCKETS - 1)


def _bias_table(rel_bias, dist, valid):
    relc = rel_bias.astype(F32) - rel_bias[REL_BUCKETS - 1].astype(F32)[None]
    onehot = np.eye(REL_BUCKETS, dtype=np.float32)[_rel_bucket_np(dist)]
    tab = jnp.einsum('...b,bh->...h', onehot, relc, precision=lax.Precision.HIGHEST)
    tab = jnp.where(valid[..., None], tab, NEG)
    return jnp.moveaxis(tab, -1, 0)


def _prompt_tables(rel_bias, nb, nc):
    g, r_heads = N_KV_HEADS, GQA_R
    qi = np.arange(Q_BLOCK)[:, None]
    nw = WINDOW + Q_BLOCK
    d_w = qi + WINDOW - np.arange(nw)[None, :]
    wtab = _bias_table(rel_bias, d_w, (d_w >= 0) & (d_w <= WINDOW)).reshape(g, r_heads * Q_BLOCK, nw)
    d_s = d_w[:, nw - FAR_TILE:]
    stab = _bias_table(rel_bias, d_s, d_s >= 0).reshape(g, r_heads * Q_BLOCK, FAR_TILE)
    d_c = qi - CMP_STRIDE * (np.arange(16)[None, :] - 8) - (CMP_LEN - 1)
    valid_c = d_c >= 0
    a = _bias_table(rel_bias, d_c, valid_c).reshape(g, r_heads * Q_BLOCK, 16)
    hi = a.astype(BF16)
    lo = jnp.where(jnp.asarray(np.tile(valid_c, (r_heads, 1)))[None], a - hi.astype(F32), 0.0).astype(BF16)
    fut = jnp.full((g, r_heads * Q_BLOCK, 1), NEG, BF16)
    atab = jnp.concatenate([jnp.zeros((g, r_heads * Q_BLOCK, ATAB_LANE), BF16), hi, lo, fut,
                            jnp.zeros((g, r_heads * Q_BLOCK, 15), BF16)], axis=-1)
    jb = np.arange(nb)[:, None]
    n = np.arange(nc)[None, :]
    poolt = jnp.asarray(((n >= CMP_PER_SEL * jb - 1) & (n <= CMP_PER_SEL * jb + CMP_PER_SEL - 1)).astype(np.float32),
                        BF16)
    return wtab, stab, atab, poolt


CMP_PAGES_PER_STEP = 32
SEL_PAGES_PER_STEP = 32


def _gather_cmp_kernel(pt_ref, *refs):
    del pt_ref
    page_refs = refs[:CMP_PAGES_PER_STEP]
    o_ref, t_ref = refs[CMP_PAGES_PER_STEP:]
    n_fb = t_ref.shape[0]
    for k in range(CMP_PAGES_PER_STEP):
        for fb in range(n_fb):
            t_ref[fb, k * PAGE_SIZE:(k + 1) * PAGE_SIZE, :] = page_refs[k][0, 0, fb * 128:(fb + 1) * 128, :].T
    rows = CMP_PAGES_PER_STEP * PAGE_SIZE // CMP_STRIDE
    for s in range(CMP_STRIDE):
        for fb in range(n_fb):
            c0 = (s * n_fb + fb) * 128
            o_ref[0, :, c0:c0 + 128] = t_ref[fb, pl.ds(s, rows, stride=CMP_STRIDE), :].astype(BF16)


def _gather_cmp(cache_t, layer, page_table):
    bsz, n_pages = page_table.shape
    half = cache_t.shape[2] // 2
    pp = CMP_PAGES_PER_STEP
    chunks = pp * PAGE_SIZE // CMP_STRIDE
    page_spec = lambda k: pl.BlockSpec((1, 1, half, PAGE_SIZE), lambda b, i, pt: (layer, pt[b, i * pp + k], 0, 0))
    return pl.pallas_call(
        _gather_cmp_kernel,
        grid_spec=pltpu.PrefetchScalarGridSpec(
            num_scalar_prefetch=1, grid=(bsz, n_pages // pp),
            in_specs=[page_spec(k) for k in range(pp)],
            out_specs=pl.BlockSpec((1, chunks, CMP_STRIDE * half), lambda b, i, pt: (b, i, 0)),
            scratch_shapes=[pltpu.VMEM((half // 128, pp * PAGE_SIZE, 128), F32)]),
        out_shape=jax.ShapeDtypeStruct((bsz, n_pages * PAGE_SIZE // CMP_STRIDE, CMP_STRIDE * half), BF16),
        compiler_params=_cparams(2),
        name="gather_cmp",
    )(page_table, *([cache_t] * pp))


def _nsa_attn_sample_kernel(pt_ref, q_ref, gt_ref, kct_ref, vc_ref, new_ref, win_ref,
                            tc_ref, tnear_ref, tnew_ref, tw_ref, pool_ref, e_ref, rsum_ref, *refs,
                            t, n_past_blk):
    del pt_ref
    pp = SEL_PAGES_PER_STEP
    k_refs, v_refs = refs[:pp], refs[pp:2 * pp]
    o_ref, m_ref, l_ref, acc_ref, selb_ref, oc_ref, ow_ref = refs[2 * pp:]
    g, r_heads, hd = N_KV_HEADS, GQA_R, HEAD_DIM
    gw = g * hd
    i = pl.program_id(1)
    n_steps = pl.num_programs(1)
    nt = (((1,), (1,)), ((), ()))
    q = q_ref[0]
    tile = pp * PAGE_SIZE

    @pl.when(i == 0)
    def _():
        new = new_ref[0]
        s_c = jnp.dot(q, kct_ref[0], preferred_element_type=F32) + tc_ref[...]
        m_c = jnp.max(s_c, axis=1, keepdims=True)
        e_c = jnp.exp(s_c - m_c)
        inv_c = jnp.where(m_c > NEG_VALID, 1.0 / jnp.maximum(jnp.sum(e_c, axis=1, keepdims=True), 1e-30), 0.0)
        p_c = e_c * inv_c
        oc_ref[...] = jnp.dot(p_c.astype(BF16), vc_ref[0], preferred_element_type=F32)
        imp0 = jnp.zeros((q.shape[0], n_past_blk), F32)
        for part in _split3(p_c):
            imp0 = imp0 + jnp.dot(part, pool_ref[...], preferred_element_type=F32)
        imp = jnp.zeros((q.shape[0], n_past_blk), F32)
        for part in _split3(imp0):
            imp = imp + jnp.dot(rsum_ref[...], part, preferred_element_type=F32)
        blk = lax.broadcasted_iota(jnp.int32, (n_past_blk, q.shape[0]), 0)
        forced = (blk == 0) | (blk == n_past_blk - 1)
        sel = _top_blocks(jnp.where(forced, FORCE_SCORE, imp.T), blk, min(TOP_N - 1, n_past_blk), n_past_blk)
        selb_ref[...] = jnp.where(sel, 0.0, NEG).T.astype(BF16)
        win = win_ref[0, 0].astype(BF16)
        s_w = jnp.dot(q, win[:gw], preferred_element_type=F32) + tw_ref[...]
        s_wn = lax.dot_general(q, new[:, 4 * gw:5 * gw], nt, preferred_element_type=F32) + tnew_ref[...]
        m_w = jnp.maximum(jnp.max(s_w, axis=1, keepdims=True), jnp.max(s_wn, axis=1, keepdims=True))
        p_w = jnp.exp(s_w - m_w)
        p_wn = jnp.exp(s_wn - m_w)
        l_w = jnp.sum(p_w, axis=1, keepdims=True) + jnp.sum(p_wn, axis=1, keepdims=True)
        o_w = (lax.dot_general(p_w.astype(BF16), win[gw:], nt, preferred_element_type=F32)
               + jnp.dot(p_wn.astype(BF16), new[:, 5 * gw:6 * gw], preferred_element_type=F32))
        ow_ref[...] = o_w / l_w
        s_n = lax.dot_general(q, new[:, 2 * gw:3 * gw], nt, preferred_element_type=F32) + tnew_ref[...]
        m_n = jnp.max(s_n, axis=1, keepdims=True)
        p_n = jnp.exp(s_n - m_n)
        m_ref[...] = m_n
        l_ref[...] = jnp.sum(p_n, axis=1, keepdims=True)
        acc_ref[...] = jnp.dot(p_n.astype(BF16), new[:, 3 * gw:4 * gw], preferred_element_type=F32)

    k_t = jnp.concatenate([r[0, 0] for r in k_refs], axis=1).astype(BF16)
    v_t = jnp.concatenate([r[0, 0] for r in v_refs], axis=1).astype(BF16)
    s = (jnp.dot(q, k_t, preferred_element_type=F32)
         + jnp.dot(selb_ref[...], e_ref[:, pl.ds(pl.multiple_of(i * tile, tile), tile)], preferred_element_type=F32))
    s = s + jnp.where(i == n_steps - 1, tnear_ref[...], 0.0)
    m_old = m_ref[...]
    m_new = jnp.maximum(m_old, jnp.max(s, axis=1, keepdims=True))
    alpha = jnp.exp(m_old - m_new)
    p = jnp.exp(s - m_new)
    m_ref[...] = m_new
    l_ref[...] = alpha * l_ref[...] + jnp.sum(p, axis=1, keepdims=True)
    acc_ref[...] = alpha * acc_ref[...] + lax.dot_general(p.astype(BF16), v_t, nt, preferred_element_type=F32)

    @pl.when(i == n_steps - 1)
    def _():
        gt = gt_ref[0]
        o = gt[:, 0:1] * oc_ref[...] + gt[:, 1:2] * (acc_ref[...] / l_ref[...]) + gt[:, 2:3] * ow_ref[...]
        for gi in range(g):
            for r in range(r_heads):
                h = gi * r_heads + r
                o_ref[0, :, h * hd:(h + 1) * hd] = o[h * t:(h + 1) * t, gi * hd:(gi + 1) * hd].astype(BF16)


def _nsa_attn_sample(q_bd, gates, kct, vc, new_pad, win_t, layer, cache_t, page_table, tabs, t):
    bsz, n_pages = page_table.shape
    g, r_heads, hd = N_KV_HEADS, GQA_R, HEAD_DIM
    gw = g * hd
    lanes = g * r_heads * t
    pp = SEL_PAGES_PER_STEP
    n_past_blk = n_pages * PAGE_SIZE // SEL_LEN
    kern = functools.partial(_nsa_attn_sample_kernel, t=t, n_past_blk=n_past_blk)
    per_b = lambda a: pl.BlockSpec((1,) + a.shape[1:], lambda b, i, pt: (b, 0, 0))
    full = lambda a: pl.BlockSpec(a.shape, lambda b, i, pt: (0, 0))
    page_spec = lambda fblk, k: pl.BlockSpec((1, 1, gw, PAGE_SIZE),
                                             lambda b, i, pt: (layer, pt[b, i * pp + k], fblk, 0))
    return pl.pallas_call(
        kern,
        grid_spec=pltpu.PrefetchScalarGridSpec(
            num_scalar_prefetch=1, grid=(bsz, n_pages // pp),
            in_specs=[per_b(q_bd), per_b(gates), per_b(kct), per_b(vc), per_b(new_pad),
                      pl.BlockSpec((1, 1) + win_t.shape[2:], lambda b, i, pt: (layer, b, 0, 0))]
                     + [full(a) for a in tabs]
                     + [page_spec(2, k) for k in range(pp)] + [page_spec(3, k) for k in range(pp)],
            out_specs=pl.BlockSpec((1, t, g * r_heads * hd), lambda b, i, pt: (b, 0, 0)),
            scratch_shapes=[pltpu.VMEM((lanes, 1), F32), pltpu.VMEM((lanes, 1), F32), pltpu.VMEM((lanes, gw), F32),
                            pltpu.VMEM((lanes, n_past_blk), BF16), pltpu.VMEM((lanes, gw), F32),
                            pltpu.VMEM((lanes, gw), F32)]),
        out_shape=jax.ShapeDtypeStruct((bsz, t, g * r_heads * hd), BF16),
        compiler_params=_cparams(2),
        name="nsa_attn_sample",
    )(page_table, q_bd, gates, kct, vc, new_pad, win_t, *tabs, *([cache_t] * (2 * pp)))


def _sample_tables(rel_bias, pos0, t, nc):
    g, r_heads = N_KV_HEADS, GQA_R
    lanes = g * r_heads * t
    tq = pos0 + np.arange(t)[:, None]

    def table(kpos, extra_valid=True):
        dist = tq - kpos[None, :]
        tab = _bias_table(rel_bias, dist, (dist >= 0) & extra_valid)
        return tab.reshape(lanes, kpos.shape[0])

    tc = table(CMP_STRIDE * np.arange(nc) + CMP_LEN - 1)
    tnear = table(pos0 - Q_BLOCK + np.arange(Q_BLOCK))
    tnear = jnp.pad(tnear, ((0, 0), (SEL_PAGES_PER_STEP * PAGE_SIZE - Q_BLOCK, 0)))
    new_pos = pos0 + np.arange(128)
    tnew = table(new_pos, (new_pos < pos0 + t)[None, :])
    kw = pos0 - WINDOW + np.arange(WINDOW)
    tw = table(kw, (tq - kw[None, :]) <= WINDOW)
    n_past_blk = pos0 // SEL_LEN
    jb = np.arange(n_past_blk)[None, :]
    n = np.arange(nc)[:, None]
    pool = jnp.asarray(((n >= CMP_PER_SEL * jb - 1) & (n <= CMP_PER_SEL * jb + CMP_PER_SEL - 1)).astype(np.float32),
                       BF16)
    e = jnp.asarray((np.arange(pos0)[None, :] // SEL_LEN == np.arange(n_past_blk)[:, None]).astype(np.float32), BF16)
    li = np.arange(lanes)
    same = (li[:, None] // (r_heads * t) == li[None, :] // (r_heads * t)) & (li[:, None] % t == li[None, :] % t)
    rsum = jnp.asarray(same.astype(np.float32), BF16)
    return tc, tnear, tnew, tw, pool, e, rsum


def _nsa_sample(xs2d, bsz, t, layer, cache_t, page_table, win_t, wts, tabs):
    g, r_heads, hd = N_KV_HEADS, GQA_R, HEAD_DIM
    gw = g * hd
    q, kv, kvb, gt = _nsa_proj(xs2d, wts["wq"], wts["wkv"], wts["wg"], tm=xs2d.shape[0])
    chunks = _gather_cmp(cache_t, layer, page_table)
    kct, vc = _nsa_cmp(chunks, wts["w1"], wts["pe_term"], wts["w2k"].T, wts["w2v"], transposed=False)
    q5 = q.reshape(bsz, t, g, r_heads, hd)
    q_bd = jnp.einsum('btgrd,gh->bgrthd', q5, jnp.eye(g, dtype=q.dtype)).reshape(bsz, g * r_heads * t, gw)
    gates = jnp.pad(gt[:, :3 * g * r_heads].reshape(bsz, t, g * r_heads, 3).transpose(0, 2, 1, 3)
                    .reshape(bsz, g * r_heads * t, 3), ((0, 0), (0, 0), (0, 128 - 3)))
    new_pad = jnp.pad(kvb.reshape(bsz, t, 6 * gw), ((0, 0), (0, 128 - t), (0, 0)))
    o = _nsa_attn_sample(q_bd, gates, kct, vc, new_pad, win_t, layer, cache_t, page_table, tabs, t)
    return o.reshape(bsz * t, g * r_heads * hd), kv


def _nsa_weights(w_in, cmp_pe, cmp_w1, cmp_b1, cmp_w2):
    d = w_in.shape[0]
    g, r_heads, hd = N_KV_HEADS, GQA_R, HEAD_DIM
    qd, kvd = g * r_heads * hd, 6 * g * hd
    wq = (w_in[:, :qd] * hd ** -0.5).astype(BF16)
    wq_pad = jnp.pad(wq.reshape(d, g * r_heads, hd), ((0, 0), (0, 0), (0, 128 - hd))).reshape(d, g * r_heads * 128)
    wkv = w_in[:, qd:qd + kvd].astype(BF16)
    wg = jnp.pad(w_in[:, qd + kvd:], ((0, 0), (0, 128 - 3 * g * r_heads))).astype(BF16)
    w1r = cmp_w1.reshape(2, CMP_LEN, hd, CMP_HID)
    w1 = jnp.concatenate([w1r[:, :CMP_STRIDE], w1r[:, CMP_STRIDE:]], axis=-1).astype(BF16)
    w1 = w1.reshape(2, CMP_STRIDE // CMP_PACK, CMP_PACK * hd, 2 * CMP_HID)
    pe_term = jnp.einsum('csd,csdh->ch', cmp_pe, w1r) + cmp_b1
    return dict(wq=wq, wq_pad=wq_pad, wkv=wkv, wg=wg, w1=w1, pe_term=pe_term,
                w2k=cmp_w2[0].astype(BF16), w2v=cmp_w2[1].astype(BF16))


def _nsa_prompt(x2d, bsz, seq, wts, tables):
    g, r_heads, hd = N_KV_HEADS, GQA_R, HEAD_DIM
    m = bsz * seq
    nc, nb = seq // CMP_STRIDE, seq // SEL_LEN
    q_pad, kv, kv_chunks, gt = _nsa_proj(x2d, wts["wq_pad"], wts["wkv"], wts["wg"], tm=ROW_TILE, chunk_seq=seq)
    w2k_t = jnp.pad(wts["w2k"].T, ((0, 128 - hd), (0, 0)))
    w2v_p = jnp.pad(wts["w2v"], ((0, 0), (0, 128 - hd)))
    kct, vc = _nsa_cmp(kv_chunks, wts["w1"], wts["pe_term"], w2k_t, w2v_p, transposed=True)
    kst, kwt, vs, vw = _nsa_prep(kv.reshape(bsz, seq, kv.shape[1]), nb)
    gates_g = jnp.pad(gt[:, :3 * g * r_heads].reshape(m, g, 3 * r_heads).transpose(1, 0, 2),
                      ((0, 0), (0, 0), (0, 128 - 3 * r_heads)))
    o = _nsa_attn_prompt(q_pad, gates_g, kct, vc, kst, vs, kwt, vw, *tables, bsz, seq)
    return o, kv


def kernel(x_prompt, x_sample, cache_nsa, state_nsa_win, state_pool, state_ffn, page_table, p_prompt, p_sample,
           rel_bias, nsa_w_in, nsa_cmp_pe, nsa_cmp_w1, nsa_cmp_b1, nsa_cmp_w2, nsa_w_out, pool_w, pool_b,
           pool_scale, ffn_w_up, ffn_conv_w, ffn_conv_b, ffn_w_down, ln_g, ln_b, ple_w_proj, ple_w_gate,
           ple_b_gate):
    bsz, seq, d = x_prompt.shape
    db, dt, _ = x_sample.shape
    depth = ffn_w_up.shape[0]
    alpha = (2.0 * depth) ** 0.25
    g, hd = N_KV_HEADS, HEAD_DIM
    gw = g * hd
    past_len = page_table.shape[1] * PAGE_SIZE
    mp, ms = bsz * seq, db * dt
    assert g * GQA_R * dt == 128 and past_len % SEL_LEN == 0 and seq % PADL == 0 and seq % ROW_TILE == 0
    assert page_table.shape[1] % SEL_PAGES_PER_STEP == 0 and page_table.shape[1] % CMP_PAGES_PER_STEP == 0
    assert state_nsa_win.shape[2] == WINDOW and seq >= WINDOW and state_pool.shape[2] == POOL_HIST

    xp = x_prompt.reshape(mp, d)
    xs = x_sample.reshape(ms, d)
    tables_p = _prompt_tables(rel_bias, seq // SEL_LEN, seq // CMP_STRIDE)
    tables_s = _sample_tables(rel_bias, past_len, dt, past_len // CMP_STRIDE)
    cache_t = jnp.transpose(cache_nsa, (0, 1, 3, 4, 5, 2)).reshape(cache_nsa.shape[0], cache_nsa.shape[1], 4 * gw,
                                                                   PAGE_SIZE)
    win_t = jnp.transpose(state_nsa_win, (0, 1, 3, 4, 5, 2)).reshape(state_nsa_win.shape[0], db, 2 * gw, WINDOW)
    nsa_rp, nsa_rs, nsa_wp, nsa_ws, pool_hp, pool_hs, ffn_hp, ffn_hs = ([] for _ in range(8))
    for i in range(depth):
        j = i // 2
        g0, b0, g1, b1 = ln_g[i, 0][None], ln_b[i, 0][None], ln_g[i, 1][None], ln_b[i, 1][None]
        if i % 2 == 0:
            wts = _nsa_weights(nsa_w_in[j], nsa_cmp_pe[j], nsa_cmp_w1[j], nsa_cmp_b1[j], nsa_cmp_w2[j])
            o_p, kv_p = _nsa_prompt(xp, bsz, seq, wts, tables_p)
            o_s, kv_s = _nsa_sample(xs, db, dt, j, cache_t, page_table, win_t, wts, tables_s)
            w_out = nsa_w_out[j].astype(BF16)
            xp = _linear_ln(o_p, w_out, xp, g0, b0, alpha, tm=ROW_TILE)
            xs = _linear_ln(o_s, w_out, xs, g0, b0, alpha, tm=ms)
            kv_p3 = kv_p.reshape(bsz, seq, 6, g, hd)
            kv_s3 = kv_s.reshape(db, dt, 6, g, hd)
            nsa_rp.append(kv_p3[:, :, :4])
            nsa_rs.append(kv_s3[:, :, :4])
            nsa_wp.append(kv_p3[:, seq - WINDOW:, 4:])
            nsa_ws.append(jnp.concatenate([state_nsa_win[j][:, dt:], kv_s3[:, :, 4:]], axis=1))
        else:
            xp3 = xp.reshape(bsz, seq, d)
            xe = jnp.concatenate([state_pool[j], xs.reshape(db, dt, d)], axis=1)
            pool_hp.append(xp3[:, seq - POOL_HIST:])
            pool_hs.append(xe[:, xe.shape[1] - POOL_HIST:])
            pw = pool_w[j].astype(BF16)
            xp = _pool_ln_prompt(xp3, pw, pool_b[j][None], pool_scale[j][None], g0, b0, alpha).reshape(mp, d)
            xs = _pool_ln_sample(xe, dt, past_len, pw, pool_b[j][None], pool_scale[j][None], g0, b0,
                                 alpha).reshape(ms, d)
        w_up = ffn_w_up[i].astype(BF16)
        w_down = ffn_w_down[i].astype(BF16)
        cw, cb = ffn_conv_w[i], ffn_conv_b[i][None]
        h_p = _ffn_up_prompt(xp, w_up, cw, cb, seq)
        tail = xp.reshape(bsz, seq, d)[:, seq - 8:].reshape(bsz * 8, d)
        u_tail = _matmul(tail, w_up, tm=bsz * 8).reshape(bsz, 8, -1)
        ffn_hp.append(u_tail[:, 8 - (CONV_W - 1):])
        u_s = _matmul(xs, w_up, tm=ms).reshape(db, dt, -1)
        ue = jnp.concatenate([state_ffn[i], u_s], axis=1)
        ffn_hs.append(ue[:, ue.shape[1] - (CONV_W - 1):])
        h_s = _ffn_gate_sample(ue, cw, cb, dt).reshape(ms, -1)
        ple = (ple_w_gate[i].astype(BF16), ple_b_gate[i][None], ple_w_proj[i].astype(BF16))
        xp = _ffn_down_ln_ple(h_p, w_down, xp, g1, b1, p_prompt[i].reshape(mp, -1), *ple, alpha, ROW_TILE)
        xs = _ffn_down_ln_ple(h_s, w_down, xs, g1, b1, p_sample[i].reshape(ms, -1), *ple, alpha, ms)
    return (xp.reshape(bsz, seq, d), xs.reshape(db, dt, d),
            jnp.stack(nsa_rp), jnp.stack(nsa_rs), jnp.stack(nsa_wp), jnp.stack(nsa_ws),
            jnp.stack(pool_hp), jnp.stack(pool_hs), jnp.stack(ffn_hp), jnp.stack(ffn_hs))
```
